```python
import math
import jax, jax.numpy as jnp
from jax import lax
import numpy as np

D_MODEL = 4096
BATCH = 8
SEQ = 4096
DEPTH = 2

EXPAND = 2
MIX_WIDTH = EXPAND * D_MODEL
S5_WIDTH = MIX_WIDTH // 4
S5_GROUP = 16
S5_GROUPS = S5_WIDTH // S5_GROUP
S5_STATE = 64
S5_EIG_CLIP = -1e-4
SSD_WIDTH = MIX_WIDTH - S5_WIDTH
SSD_HEAD_DIM = 64
SSD_HEADS = SSD_WIDTH // SSD_HEAD_DIM
SSD_GROUPS = 8
SSD_STATE = 128
SSD_CONV = 4
SSD_CHUNK = 128
SSD_XBC = SSD_WIDTH + 2 * SSD_GROUPS * SSD_STATE
FOX_HEAD_DIM = 128
FOX_HEADS = D_MODEL // FOX_HEAD_DIM
FOX_WIDTH = FOX_HEADS * FOX_HEAD_DIM
FOX_BLOCK = 128
NORM_EPS = 1e-5

EVEN_IN = 2 * S5_WIDTH + SSD_WIDTH + SSD_XBC + SSD_HEADS
ODD_IN = 4 * FOX_WIDTH + FOX_HEADS

kernel_name = "hybrid_s5_ssd_fox_trunk"

F32 = jnp.float32


def rms_norm(x, w):
    xf = x.astype(F32)
    y = xf * lax.rsqrt(jnp.mean(xf * xf, axis=-1, keepdims=True) + NORM_EPS)
    return (y * w.astype(F32)).astype(x.dtype)


def causal_depthwise_conv(x, w, b):
    k_width, ch = w.shape
    y = lax.conv_general_dilated(
        x, w.astype(F32)[:, None, :], window_strides=(1,),
        padding=((k_width - 1, 0),), dimension_numbers=("NWC", "WIO", "NWC"),
        feature_group_count=ch)
    return y + b.astype(F32)


def s5_mixer(u, lam_re, lam_im, log_step, b_re, b_im, c_re, c_im, d, w_glu, b_glu):
    bsz, seqlen, _ = u.shape
    u = u.reshape(bsz, seqlen, S5_GROUPS, S5_GROUP)
    lr = jnp.minimum(lam_re.astype(F32), S5_EIG_CLIP)
    li = lam_im.astype(F32)
    step = jnp.exp(log_step.astype(F32))[:, None]
    mag = jnp.exp(lr * step)
    ab_re = mag * jnp.cos(li * step)
    ab_im = mag * jnp.sin(li * step)
    denom = lr * lr + li * li
    nr = ab_re - 1.0
    ni = ab_im
    coef_re = (nr * lr + ni * li) / denom
    coef_im = (ni * lr - nr * li) / denom
    br = b_re.astype(F32)
    bi = b_im.astype(F32)
    bb_re = coef_re[..., None] * br - coef_im[..., None] * bi
    bb_im = coef_re[..., None] * bi + coef_im[..., None] * br
    bu_re = jnp.einsum('blgh,gph->blgp', u, bb_re)
    bu_im = jnp.einsum('blgh,gph->blgp', u, bb_im)
    a_re = jnp.broadcast_to(ab_re, bu_re.shape)
    a_im = jnp.broadcast_to(ab_im, bu_im.shape)

    def combine(e_i, e_j):
        ar_i, ai_i, br_i, bi_i = e_i
        ar_j, ai_j, br_j, bi_j = e_j
        return (ar_j * ar_i - ai_j * ai_i,
                ar_j * ai_i + ai_j * ar_i,
                ar_j * br_i - ai_j * bi_i + br_j,
                ar_j * bi_i + ai_j * br_i + bi_j)

    _, _, s_re, s_im = lax.associative_scan(combine, (a_re, a_im, bu_re, bu_im), axis=1)
    y = (jnp.einsum('blgp,ghp->blgh', s_re, c_re.astype(F32))
         - jnp.einsum('blgp,ghp->blgh', s_im, c_im.astype(F32))
         + d.astype(F32) * u)
    y = y.reshape(bsz, seqlen, S5_WIDTH)
    g = jax.nn.gelu(y)
    return g * jax.nn.sigmoid(g @ w_glu.astype(F32) + b_glu.astype(F32))


def ssd_chunked(x, dt, a_head, bm, cm):
    bsz, seqlen, nh, hd = x.shape
    ng, ns = bm.shape[2], bm.shape[3]
    r = nh // ng
    q = SSD_CHUNK
    nc = seqlen // q
    xc = (x * dt[..., None]).reshape(bsz, nc, q, ng, r, hd)
    la = (dt * a_head).reshape(bsz, nc, q, ng, r).transpose(0, 3, 4, 1, 2)
    la_cum = jnp.cumsum(la, axis=-1)
    bc = bm.reshape(bsz, nc, q, ng, ns)
    cc = cm.reshape(bsz, nc, q, ng, ns)
    causal = jnp.tril(jnp.ones((q, q), dtype=bool))
    seg = la_cum[..., :, None] - la_cum[..., None, :]
    decay_in = jnp.exp(jnp.where(causal, seg, -jnp.inf))
    scores = jnp.einsum('bcqgn,bckgn->bgcqk', cc, bc)
    w_in = scores[:, :, None] * decay_in
    y_diag = jnp.einsum('bgrcqk,bckgrp->bcqgrp', w_in, xc)
    decay_end = jnp.exp(la_cum[..., -1:] - la_cum).transpose(0, 3, 4, 1, 2)
    states = jnp.einsum('bckgn,bckgrp->bcgrpn', bc, xc * decay_end[..., None])
    chunk_decay = la_cum[..., -1]
    cs = jnp.cumsum(jnp.pad(chunk_decay, ((0, 0), (0, 0), (0, 0), (1, 0))), axis=-1)
    seg_c = cs[..., :, None] - cs[..., None, :]
    mask_c = jnp.tril(jnp.ones((nc + 1, nc + 1), dtype=bool))
    decay_c = jnp.exp(jnp.where(mask_c, seg_c, -jnp.inf))
    states_cat = jnp.concatenate([jnp.zeros_like(states[:, :1]), states], axis=1)
    states_in = jnp.einsum('bgrzc,bcgrpn->bzgrpn', decay_c[..., :nc, :], states_cat)
    decay_out = jnp.exp(la_cum).transpose(0, 3, 4, 1, 2)
    y_off = jnp.einsum('bcqgn,bcgrpn->bcqgrp', cc, states_in) * decay_out[..., None]
    return (y_diag + y_off).reshape(bsz, seqlen, nh, hd)


def ssd_mixer(z, xbc, dt_raw, conv_w, conv_b, dt_bias, a_log, d, norm_w):
    bsz, seqlen, _ = z.shape
    xbc = jax.nn.silu(causal_depthwise_conv(xbc, conv_w, conv_b))
    xs, bm, cm = jnp.split(xbc, [SSD_WIDTH, SSD_WIDTH + SSD_GROUPS * SSD_STATE], axis=-1)
    xs = xs.reshape(bsz, seqlen, SSD_HEADS, SSD_HEAD_DIM)
    bm = bm.reshape(bsz, seqlen, SSD_GROUPS, SSD_STATE)
    cm = cm.reshape(bsz, seqlen, SSD_GROUPS, SSD_STATE)
    dt = jax.nn.softplus(dt_raw + dt_bias.astype(F32))
    a_head = -jnp.exp(a_log.astype(F32))
    y = ssd_chunked(xs, dt, a_head, bm, cm) + d.astype(F32)[:, None] * xs
    y = y.reshape(bsz, seqlen, SSD_WIDTH) * jax.nn.silu(z)
    yg = y.reshape(bsz, seqlen, SSD_GROUPS, SSD_WIDTH // SSD_GROUPS)
    yg = yg * lax.rsqrt(jnp.mean(yg * yg, axis=-1, keepdims=True) + NORM_EPS)
    return yg.reshape(bsz, seqlen, SSD_WIDTH) * norm_w.astype(F32)


def fox_attention(q, k, v, f_logit, b_f):
    bsz, seqlen, nh, hd = q.shape
    log_f = jax.nn.log_sigmoid(f_logit + b_f.astype(F32))
    c = jnp.cumsum(log_f, axis=1).transpose(0, 2, 1)
    nb = seqlen // FOX_BLOCK
    qb = q.reshape(bsz, nb, FOX_BLOCK, nh, hd).transpose(1, 0, 2, 3, 4)
    cb = c.reshape(bsz, nh, nb, FOX_BLOCK).transpose(2, 0, 1, 3)
    kpos = jnp.arange(seqlen)
    scale = 1.0 / math.sqrt(FOX_HEAD_DIM)

    def block(args):
        qi, ci, i = args
        s = jnp.einsum('bqhd,bkhd->bhqk', qi, k) * scale + (ci[..., :, None] - c[:, :, None, :])
        qpos = i * FOX_BLOCK + jnp.arange(FOX_BLOCK)
        s = jnp.where(kpos[None, :] <= qpos[:, None], s, -jnp.inf)
        p = jax.nn.softmax(s, axis=-1)
        return jnp.einsum('bhqk,bkhd->bqhd', p, v)

    out = lax.map(block, (qb, cb, jnp.arange(nb)))
    return out.transpose(1, 0, 2, 3, 4).reshape(bsz, seqlen, nh * hd)


def ssm_layer(x, norm_w, w_in, lam_re, lam_im, log_step, b_re, b_im, c_re, c_im, s5_d,
              w_glu, b_glu, conv_w, conv_b, dt_bias, a_log, ssd_d, ssd_norm_w, w_out):
    h = rms_norm(x, norm_w)
    proj = (h @ w_in).astype(F32)
    s5_u, s5_gate, ssd_z, ssd_xbc, ssd_dt = jnp.split(
        proj, [S5_WIDTH, 2 * S5_WIDTH, 2 * S5_WIDTH + SSD_WIDTH,
               2 * S5_WIDTH + SSD_WIDTH + SSD_XBC], axis=-1)
    s5_out = s5_mixer(s5_u, lam_re, lam_im, log_step, b_re, b_im, c_re, c_im, s5_d,
                      w_glu, b_glu) * jax.nn.silu(s5_gate)
    ssd_out = ssd_mixer(ssd_z, ssd_xbc, ssd_dt, conv_w, conv_b, dt_bias, a_log, ssd_d, ssd_norm_w)
    mixed = jnp.concatenate([s5_out, ssd_out], axis=-1).astype(x.dtype)
    return mixed @ w_out


def fox_layer(x, norm_w, w_in, b_f, w_out):
    bsz, seqlen, _ = x.shape
    h = rms_norm(x, norm_w)
    proj = (h @ w_in).astype(F32)
    q, k, v, gate, f_logit = jnp.split(
        proj, [FOX_WIDTH, 2 * FOX_WIDTH, 3 * FOX_WIDTH, 4 * FOX_WIDTH], axis=-1)
    shp = (bsz, seqlen, FOX_HEADS, FOX_HEAD_DIM)
    att = fox_attention(q.reshape(shp), k.reshape(shp), v.reshape(shp), f_logit, b_f)
    out = (att * jax.nn.silu(gate)).astype(x.dtype)
    return out @ w_out


def _fwd_setup_inputs(seed: int = 0) -> dict:
    key = jax.random.key(seed)
    ks = jax.random.split(key, 32)
    nrm = lambda k, shp, s: jax.random.normal(k, shp, F32) * s
    x = nrm(ks[0], (BATCH, SEQ, D_MODEL), 1.0)
    l0_norm_w = 1.0 + nrm(ks[1], (D_MODEL,), 0.02)
    l0_w_in = nrm(ks[2], (D_MODEL, EVEN_IN), D_MODEL ** -0.5)
    l0_s5_lambda_re = -0.5 + nrm(ks[3], (S5_GROUPS, S5_STATE), 0.01)
    l0_s5_lambda_im = (jnp.pi * jnp.broadcast_to(jnp.arange(S5_STATE, dtype=F32), (S5_GROUPS, S5_STATE))
                       + nrm(ks[4], (S5_GROUPS, S5_STATE), 0.01))
    l0_s5_log_step = jax.random.uniform(ks[5], (S5_GROUPS,), F32, math.log(1e-3), math.log(1e-1))
    l0_s5_b_re = nrm(ks[6], (S5_GROUPS, S5_STATE, S5_GROUP), (2 * S5_GROUP) ** -0.5)
    l0_s5_b_im = nrm(ks[7], (S5_GROUPS, S5_STATE, S5_GROUP), (2 * S5_GROUP) ** -0.5)
    l0_s5_c_re = nrm(ks[8], (S5_GROUPS, S5_GROUP, S5_STATE), S5_STATE ** -0.5)
    l0_s5_c_im = nrm(ks[9], (S5_GROUPS, S5_GROUP, S5_STATE), S5_STATE ** -0.5)
    l0_s5_d = nrm(ks[10], (S5_GROUPS, S5_GROUP), 1.0)
    l0_s5_w_glu = nrm(ks[11], (S5_WIDTH, S5_WIDTH), S5_WIDTH ** -0.5)
    l0_s5_b_glu = nrm(ks[12], (S5_WIDTH,), 0.01)
    l0_ssd_conv_w = nrm(ks[13], (SSD_CONV, SSD_XBC), SSD_CONV ** -0.5)
    l0_ssd_conv_b = nrm(ks[14], (SSD_XBC,), 0.01)
    dt0 = jnp.exp(jax.random.uniform(ks[15], (SSD_HEADS,), F32, math.log(1e-3), math.log(1e-1)))
    l0_ssd_dt_bias = dt0 + jnp.log(-jnp.expm1(-dt0))
    l0_ssd_a_log = jnp.log(jax.random.uniform(ks[16], (SSD_HEADS,), F32, 1.0, 16.0))
    l0_ssd_d = 1.0 + nrm(ks[17], (SSD_HEADS,), 0.01)
    l0_ssd_norm_w = 1.0 + nrm(ks[18], (SSD_WIDTH,), 0.02)
    l0_w_out = nrm(ks[19], (MIX_WIDTH, D_MODEL), MIX_WIDTH ** -0.5)
    l1_norm_w = 1.0 + nrm(ks[20], (D_MODEL,), 0.02)
    l1_w_in = nrm(ks[21], (D_MODEL, ODD_IN), D_MODEL ** -0.5)
    l1_fox_b_f = jnp.log(jnp.exp(jax.random.uniform(ks[22], (FOX_HEADS,), F32, math.log(8.0), math.log(2048.0))))
    l1_w_out = nrm(ks[23], (FOX_WIDTH, D_MODEL), FOX_WIDTH ** -0.5)
    final_norm_w = 1.0 + nrm(ks[24], (D_MODEL,), 0.02)
    return {
        "x": x,
        "l0_norm_w": l0_norm_w, "l0_w_in": l0_w_in,
        "l0_s5_lambda_re": l0_s5_lambda_re, "l0_s5_lambda_im": l0_s5_lambda_im,
        "l0_s5_log_step": l0_s5_log_step,
        "l0_s5_b_re": l0_s5_b_re, "l0_s5_b_im": l0_s5_b_im,
        "l0_s5_c_re": l0_s5_c_re, "l0_s5_c_im": l0_s5_c_im,
        "l0_s5_d": l0_s5_d, "l0_s5_w_glu": l0_s5_w_glu, "l0_s5_b_glu": l0_s5_b_glu,
        "l0_ssd_conv_w": l0_ssd_conv_w, "l0_ssd_conv_b": l0_ssd_conv_b,
        "l0_ssd_dt_bias": l0_ssd_dt_bias, "l0_ssd_a_log": l0_ssd_a_log,
        "l0_ssd_d": l0_ssd_d, "l0_ssd_norm_w": l0_ssd_norm_w,
        "l0_w_out": l0_w_out,
        "l1_norm_w": l1_norm_w, "l1_w_in": l1_w_in, "l1_fox_b_f": l1_fox_b_f,
        "l1_w_out": l1_w_out,
        "final_norm_w": final_norm_w,
    }


def _fwd_reference(x, l0_norm_w, l0_w_in, l0_s5_lambda_re, l0_s5_lambda_im, l0_s5_log_step,
              l0_s5_b_re, l0_s5_b_im, l0_s5_c_re, l0_s5_c_im, l0_s5_d, l0_s5_w_glu,
              l0_s5_b_glu, l0_ssd_conv_w, l0_ssd_conv_b, l0_ssd_dt_bias, l0_ssd_a_log,
              l0_ssd_d, l0_ssd_norm_w, l0_w_out, l1_norm_w, l1_w_in, l1_fox_b_f, l1_w_out,
              final_norm_w):
    layers = [
        (l0_norm_w, l0_w_in, l0_s5_lambda_re, l0_s5_lambda_im, l0_s5_log_step,
         l0_s5_b_re, l0_s5_b_im, l0_s5_c_re, l0_s5_c_im, l0_s5_d, l0_s5_w_glu,
         l0_s5_b_glu, l0_ssd_conv_w, l0_ssd_conv_b, l0_ssd_dt_bias, l0_ssd_a_log,
         l0_ssd_d, l0_ssd_norm_w, l0_w_out),
        (l1_norm_w, l1_w_in, l1_fox_b_f, l1_w_out),
    ]
    for layer in range(DEPTH):
        if layer % 2 == 0:
            x = x + ssm_layer(x, *layers[layer])
        else:
            x = x + fox_layer(x, *layers[layer])
    return rms_norm(x, final_norm_w)


import jax as _jax
import jax.numpy as _jnp

TWIN_FORMAT = 'train_step'
FWD_PARAMS = ['x', 'l0_norm_w', 'l0_w_in', 'l0_s5_lambda_re', 'l0_s5_lambda_im', 'l0_s5_log_step', 'l0_s5_b_re', 'l0_s5_b_im', 'l0_s5_c_re', 'l0_s5_c_im', 'l0_s5_d', 'l0_s5_w_glu', 'l0_s5_b_glu', 'l0_ssd_conv_w', 'l0_ssd_conv_b', 'l0_ssd_dt_bias', 'l0_ssd_a_log', 'l0_ssd_d', 'l0_ssd_norm_w', 'l0_w_out', 'l1_norm_w', 'l1_w_in', 'l1_fox_b_f', 'l1_w_out', 'final_norm_w']
TWIN_WEIGHTS = ['l0_norm_w', 'l0_w_in', 'l0_s5_lambda_re', 'l0_s5_lambda_im', 'l0_s5_log_step', 'l0_s5_b_re', 'l0_s5_b_im', 'l0_s5_c_re', 'l0_s5_c_im', 'l0_s5_d', 'l0_s5_w_glu', 'l0_s5_b_glu', 'l0_ssd_conv_w', 'l0_ssd_conv_b', 'l0_ssd_dt_bias', 'l0_ssd_a_log', 'l0_ssd_d', 'l0_ssd_norm_w', 'l0_w_out', 'l1_norm_w', 'l1_w_in', 'l1_fox_b_f', 'l1_w_out', 'final_norm_w']
TWIN_DIFF_INPUT = 'x'
TWIN_INPUTS = ['x', 'l0_norm_w', 'l0_w_in', 'l0_s5_lambda_re', 'l0_s5_lambda_im', 'l0_s5_log_step', 'l0_s5_b_re', 'l0_s5_b_im', 'l0_s5_c_re', 'l0_s5_c_im', 'l0_s5_d', 'l0_s5_w_glu', 'l0_s5_b_glu', 'l0_ssd_conv_w', 'l0_ssd_conv_b', 'l0_ssd_dt_bias', 'l0_ssd_a_log', 'l0_ssd_d', 'l0_ssd_norm_w', 'l0_w_out', 'l1_norm_w', 'l1_w_in', 'l1_fox_b_f', 'l1_w_out', 'final_norm_w', 'loss_target', 'm_l0_norm_w', 'm_l0_w_in', 'm_l0_s5_lambda_re', 'm_l0_s5_lambda_im', 'm_l0_s5_log_step', 'm_l0_s5_b_re', 'm_l0_s5_b_im', 'm_l0_s5_c_re', 'm_l0_s5_c_im', 'm_l0_s5_d', 'm_l0_s5_w_glu', 'm_l0_s5_b_glu', 'm_l0_ssd_conv_w', 'm_l0_ssd_conv_b', 'm_l0_ssd_dt_bias', 'm_l0_ssd_a_log', 'm_l0_ssd_d', 'm_l0_ssd_norm_w', 'm_l0_w_out', 'm_l1_norm_w', 'm_l1_w_in', 'm_l1_fox_b_f', 'm_l1_w_out', 'm_final_norm_w', 'v_l0_norm_w', 'v_l0_w_in', 'v_l0_s5_lambda_re', 'v_l0_s5_lambda_im', 'v_l0_s5_log_step', 'v_l0_s5_b_re', 'v_l0_s5_b_im', 'v_l0_s5_c_re', 'v_l0_s5_c_im', 'v_l0_s5_d', 'v_l0_s5_w_glu', 'v_l0_s5_b_glu', 'v_l0_ssd_conv_w', 'v_l0_ssd_conv_b', 'v_l0_ssd_dt_bias', 'v_l0_ssd_a_log', 'v_l0_ssd_d', 'v_l0_ssd_norm_w', 'v_l0_w_out', 'v_l1_norm_w', 'v_l1_w_in', 'v_l1_fox_b_f', 'v_l1_w_out', 'v_final_norm_w']
TWIN_OUTPUTS = ['loss', 'grad_x', 'grad_l0_norm_w', 'grad_l0_w_in', 'grad_l0_s5_lambda_re', 'grad_l0_s5_lambda_im', 'grad_l0_s5_log_step', 'grad_l0_s5_b_re', 'grad_l0_s5_b_im', 'grad_l0_s5_c_re', 'grad_l0_s5_c_im', 'grad_l0_s5_d', 'grad_l0_s5_w_glu', 'grad_l0_s5_b_glu', 'grad_l0_ssd_conv_w', 'grad_l0_ssd_conv_b', 'grad_l0_ssd_dt_bias', 'grad_l0_ssd_a_log', 'grad_l0_ssd_d', 'grad_l0_ssd_norm_w', 'grad_l0_w_out', 'grad_l1_norm_w', 'grad_l1_w_in', 'grad_l1_fox_b_f', 'grad_l1_w_out', 'grad_final_norm_w', 'delta_l0_norm_w', 'delta_l0_w_in', 'delta_l0_s5_lambda_re', 'delta_l0_s5_lambda_im', 'delta_l0_s5_log_step', 'delta_l0_s5_b_re', 'delta_l0_s5_b_im', 'delta_l0_s5_c_re', 'delta_l0_s5_c_im', 'delta_l0_s5_d', 'delta_l0_s5_w_glu', 'delta_l0_s5_b_glu', 'delta_l0_ssd_conv_w', 'delta_l0_ssd_conv_b', 'delta_l0_ssd_dt_bias', 'delta_l0_ssd_a_log', 'delta_l0_ssd_d', 'delta_l0_ssd_norm_w', 'delta_l0_w_out', 'delta_l1_norm_w', 'delta_l1_w_in', 'delta_l1_fox_b_f', 'delta_l1_w_out', 'delta_final_norm_w', 'new_m_l0_norm_w', 'new_m_l0_w_in', 'new_m_l0_s5_lambda_re', 'new_m_l0_s5_lambda_im', 'new_m_l0_s5_log_step', 'new_m_l0_s5_b_re', 'new_m_l0_s5_b_im', 'new_m_l0_s5_c_re', 'new_m_l0_s5_c_im', 'new_m_l0_s5_d', 'new_m_l0_s5_w_glu', 'new_m_l0_s5_b_glu', 'new_m_l0_ssd_conv_w', 'new_m_l0_ssd_conv_b', 'new_m_l0_ssd_dt_bias', 'new_m_l0_ssd_a_log', 'new_m_l0_ssd_d', 'new_m_l0_ssd_norm_w', 'new_m_l0_w_out', 'new_m_l1_norm_w', 'new_m_l1_w_in', 'new_m_l1_fox_b_f', 'new_m_l1_w_out', 'new_m_final_norm_w', 'new_v_l0_norm_w', 'new_v_l0_w_in', 'new_v_l0_s5_lambda_re', 'new_v_l0_s5_lambda_im', 'new_v_l0_s5_log_step', 'new_v_l0_s5_b_re', 'new_v_l0_s5_b_im', 'new_v_l0_s5_c_re', 'new_v_l0_s5_c_im', 'new_v_l0_s5_d', 'new_v_l0_s5_w_glu', 'new_v_l0_s5_b_glu', 'new_v_l0_ssd_conv_w', 'new_v_l0_ssd_conv_b', 'new_v_l0_ssd_dt_bias', 'new_v_l0_ssd_a_log', 'new_v_l0_ssd_d', 'new_v_l0_ssd_norm_w', 'new_v_l0_w_out', 'new_v_l1_norm_w', 'new_v_l1_w_in', 'new_v_l1_fox_b_f', 'new_v_l1_w_out', 'new_v_final_norm_w']
TWIN_LEAF_KINDS = {'loss': 'loss', 'grad_x': 'grad_x', 'grad_l0_norm_w': 'grad_w', 'grad_l0_w_in': 'grad_w', 'grad_l0_s5_lambda_re': 'grad_w', 'grad_l0_s5_lambda_im': 'grad_w', 'grad_l0_s5_log_step': 'grad_w', 'grad_l0_s5_b_re': 'grad_w', 'grad_l0_s5_b_im': 'grad_w', 'grad_l0_s5_c_re': 'grad_w', 'grad_l0_s5_c_im': 'grad_w', 'grad_l0_s5_d': 'grad_w', 'grad_l0_s5_w_glu': 'grad_w', 'grad_l0_s5_b_glu': 'grad_w', 'grad_l0_ssd_conv_w': 'grad_w', 'grad_l0_ssd_conv_b': 'grad_w', 'grad_l0_ssd_dt_bias': 'grad_w', 'grad_l0_ssd_a_log': 'grad_w', 'grad_l0_ssd_d': 'grad_w', 'grad_l0_ssd_norm_w': 'grad_w', 'grad_l0_w_out': 'grad_w', 'grad_l1_norm_w': 'grad_w', 'grad_l1_w_in': 'grad_w', 'grad_l1_fox_b_f': 'grad_w', 'grad_l1_w_out': 'grad_w', 'grad_final_norm_w': 'grad_w', 'delta_l0_norm_w': 'delta_w', 'delta_l0_w_in': 'delta_w', 'delta_l0_s5_lambda_re': 'delta_w', 'delta_l0_s5_lambda_im': 'delta_w', 'delta_l0_s5_log_step': 'delta_w', 'delta_l0_s5_b_re': 'delta_w', 'delta_l0_s5_b_im': 'delta_w', 'delta_l0_s5_c_re': 'delta_w', 'delta_l0_s5_c_im': 'delta_w', 'delta_l0_s5_d': 'delta_w', 'delta_l0_s5_w_glu': 'delta_w', 'delta_l0_s5_b_glu': 'delta_w', 'delta_l0_ssd_conv_w': 'delta_w', 'delta_l0_ssd_conv_b': 'delta_w', 'delta_l0_ssd_dt_bias': 'delta_w', 'delta_l0_ssd_a_log': 'delta_w', 'delta_l0_ssd_d': 'delta_w', 'delta_l0_ssd_norm_w': 'delta_w', 'delta_l0_w_out': 'delta_w', 'delta_l1_norm_w': 'delta_w', 'delta_l1_w_in': 'delta_w', 'delta_l1_fox_b_f': 'delta_w', 'delta_l1_w_out': 'delta_w', 'delta_final_norm_w': 'delta_w', 'new_m_l0_norm_w': 'new_m', 'new_m_l0_w_in': 'new_m', 'new_m_l0_s5_lambda_re': 'new_m', 'new_m_l0_s5_lambda_im': 'new_m', 'new_m_l0_s5_log_step': 'new_m', 'new_m_l0_s5_b_re': 'new_m', 'new_m_l0_s5_b_im': 'new_m', 'new_m_l0_s5_c_re': 'new_m', 'new_m_l0_s5_c_im': 'new_m', 'new_m_l0_s5_d': 'new_m', 'new_m_l0_s5_w_glu': 'new_m', 'new_m_l0_s5_b_glu': 'new_m', 'new_m_l0_ssd_conv_w': 'new_m', 'new_m_l0_ssd_conv_b': 'new_m', 'new_m_l0_ssd_dt_bias': 'new_m', 'new_m_l0_ssd_a_log': 'new_m', 'new_m_l0_ssd_d': 'new_m', 'new_m_l0_ssd_norm_w': 'new_m', 'new_m_l0_w_out': 'new_m', 'new_m_l1_norm_w': 'new_m', 'new_m_l1_w_in': 'new_m', 'new_m_l1_fox_b_f': 'new_m', 'new_m_l1_w_out': 'new_m', 'new_m_final_norm_w': 'new_m', 'new_v_l0_norm_w': 'new_v', 'new_v_l0_w_in': 'new_v', 'new_v_l0_s5_lambda_re': 'new_v', 'new_v_l0_s5_lambda_im': 'new_v', 'new_v_l0_s5_log_step': 'new_v', 'new_v_l0_s5_b_re': 'new_v', 'new_v_l0_s5_b_im': 'new_v', 'new_v_l0_s5_c_re': 'new_v', 'new_v_l0_s5_c_im': 'new_v', 'new_v_l0_s5_d': 'new_v', 'new_v_l0_s5_w_glu': 'new_v', 'new_v_l0_s5_b_glu': 'new_v', 'new_v_l0_ssd_conv_w': 'new_v', 'new_v_l0_ssd_conv_b': 'new_v', 'new_v_l0_ssd_dt_bias': 'new_v', 'new_v_l0_ssd_a_log': 'new_v', 'new_v_l0_ssd_d': 'new_v', 'new_v_l0_ssd_norm_w': 'new_v', 'new_v_l0_w_out': 'new_v', 'new_v_l1_norm_w': 'new_v', 'new_v_l1_w_in': 'new_v', 'new_v_l1_fox_b_f': 'new_v', 'new_v_l1_w_out': 'new_v', 'new_v_final_norm_w': 'new_v'}


def _forward(args):
    return _fwd_reference(*[args[k] for k in FWD_PARAMS])


def _output_shape():
    out = _jax.eval_shape(lambda: _forward(_fwd_setup_inputs(0)))
    return out.shape, out.dtype

N_MICROBATCH = 1
ADAM_LR = 0.001
ADAM_B1 = 0.9
ADAM_B2 = 0.999
ADAM_EPS = 1e-08
ADAM_WD = 0.01
ADAM_STEP = 10
PER_EXAMPLE_BATCH_AXIS = {'x': 0, 'loss_target': 0}
SHARED_INPUTS = []
_WEIGHT_DTYPES = {'l0_norm_w': _jnp.float32, 'l0_w_in': _jnp.float32, 'l0_s5_lambda_re': _jnp.float32, 'l0_s5_lambda_im': _jnp.float32, 'l0_s5_log_step': _jnp.float32, 'l0_s5_b_re': _jnp.float32, 'l0_s5_b_im': _jnp.float32, 'l0_s5_c_re': _jnp.float32, 'l0_s5_c_im': _jnp.float32, 'l0_s5_d': _jnp.float32, 'l0_s5_w_glu': _jnp.float32, 'l0_s5_b_glu': _jnp.float32, 'l0_ssd_conv_w': _jnp.float32, 'l0_ssd_conv_b': _jnp.float32, 'l0_ssd_dt_bias': _jnp.float32, 'l0_ssd_a_log': _jnp.float32, 'l0_ssd_d': _jnp.float32, 'l0_ssd_norm_w': _jnp.float32, 'l0_w_out': _jnp.float32, 'l1_norm_w': _jnp.float32, 'l1_w_in': _jnp.float32, 'l1_fox_b_f': _jnp.float32, 'l1_w_out': _jnp.float32, 'final_norm_w': _jnp.float32}
MOMENT_SCALE = {'l0_norm_w': 4.439655e-02, 'l0_w_in': 2.085712e-02, 'l0_s5_lambda_re': 3.870776e-04, 'l0_s5_lambda_im': 4.083226e-04, 'l0_s5_log_step': 2.359733e-01, 'l0_s5_b_re': 2.494462e-04, 'l0_s5_b_im': 2.490425e-04, 'l0_s5_c_re': 3.489535e-04, 'l0_s5_c_im': 3.552781e-04, 'l0_s5_d': 5.656204e-03, 'l0_s5_w_glu': 1.508729e-03, 'l0_s5_b_glu': 2.392377e-03, 'l0_ssd_conv_w': 2.193208e-02, 'l0_ssd_conv_b': 2.930162e-02, 'l0_ssd_dt_bias': 5.396244e-02, 'l0_ssd_a_log': 7.194254e-02, 'l0_ssd_d': 1.390367e-01, 'l0_ssd_norm_w': 2.413211e-02, 'l0_w_out': 2.971570e-02, 'l1_norm_w': 1.017247e-02, 'l1_w_in': 5.120948e-03, 'l1_fox_b_f': 2.687681e-02, 'l1_w_out': 5.752018e-03, 'final_norm_w': 7.991635e+00}


def _to_microbatches(a, axis):
    t = _jnp.moveaxis(a, axis, 0)
    t = t.reshape((N_MICROBATCH, t.shape[0] // N_MICROBATCH) + t.shape[1:])
    return _jnp.moveaxis(t, 1, axis + 1)


def setup_inputs(seed: int = 0) -> dict:
    inp = _fwd_setup_inputs(seed)
    key = _jax.random.fold_in(_jax.random.key(seed), 7919)
    shape, _ = _output_shape()
    out = dict(inp)
    out["loss_target"] = _jax.random.normal(_jax.random.fold_in(key, 0), shape, _jnp.float32)
    for i, name in enumerate(TWIN_WEIGHTS):
        w = inp[name].astype(_jnp.float32)
        if MOMENT_SCALE is None:
            s = _jnp.sqrt(_jnp.mean(_jnp.square(w)) + 1e-30)
        else:
            s = MOMENT_SCALE[name]
        km, kv = _jax.random.split(_jax.random.fold_in(key, i + 1))
        out[name] = w
        out["m_" + name] = s * _jax.random.normal(km, w.shape, _jnp.float32)
        out["v_" + name] = (s * s) * _jax.random.uniform(kv, w.shape, _jnp.float32, 0.5, 1.5)
    if N_MICROBATCH > 1:
        for name, axis in PER_EXAMPLE_BATCH_AXIS.items():
            out[name] = _to_microbatches(out[name], axis)
    return {'x': out['x'], 'l0_norm_w': out['l0_norm_w'], 'l0_w_in': out['l0_w_in'], 'l0_s5_lambda_re': out['l0_s5_lambda_re'], 'l0_s5_lambda_im': out['l0_s5_lambda_im'], 'l0_s5_log_step': out['l0_s5_log_step'], 'l0_s5_b_re': out['l0_s5_b_re'], 'l0_s5_b_im': out['l0_s5_b_im'], 'l0_s5_c_re': out['l0_s5_c_re'], 'l0_s5_c_im': out['l0_s5_c_im'], 'l0_s5_d': out['l0_s5_d'], 'l0_s5_w_glu': out['l0_s5_w_glu'], 'l0_s5_b_glu': out['l0_s5_b_glu'], 'l0_ssd_conv_w': out['l0_ssd_conv_w'], 'l0_ssd_conv_b': out['l0_ssd_conv_b'], 'l0_ssd_dt_bias': out['l0_ssd_dt_bias'], 'l0_ssd_a_log': out['l0_ssd_a_log'], 'l0_ssd_d': out['l0_ssd_d'], 'l0_ssd_norm_w': out['l0_ssd_norm_w'], 'l0_w_out': out['l0_w_out'], 'l1_norm_w': out['l1_norm_w'], 'l1_w_in': out['l1_w_in'], 'l1_fox_b_f': out['l1_fox_b_f'], 'l1_w_out': out['l1_w_out'], 'final_norm_w': out['final_norm_w'], 'loss_target': out['loss_target'], 'm_l0_norm_w': out['m_l0_norm_w'], 'm_l0_w_in': out['m_l0_w_in'], 'm_l0_s5_lambda_re': out['m_l0_s5_lambda_re'], 'm_l0_s5_lambda_im': out['m_l0_s5_lambda_im'], 'm_l0_s5_log_step': out['m_l0_s5_log_step'], 'm_l0_s5_b_re': out['m_l0_s5_b_re'], 'm_l0_s5_b_im': out['m_l0_s5_b_im'], 'm_l0_s5_c_re': out['m_l0_s5_c_re'], 'm_l0_s5_c_im': out['m_l0_s5_c_im'], 'm_l0_s5_d': out['m_l0_s5_d'], 'm_l0_s5_w_glu': out['m_l0_s5_w_glu'], 'm_l0_s5_b_glu': out['m_l0_s5_b_glu'], 'm_l0_ssd_conv_w': out['m_l0_ssd_conv_w'], 'm_l0_ssd_conv_b': out['m_l0_ssd_conv_b'], 'm_l0_ssd_dt_bias': out['m_l0_ssd_dt_bias'], 'm_l0_ssd_a_log': out['m_l0_ssd_a_log'], 'm_l0_ssd_d': out['m_l0_ssd_d'], 'm_l0_ssd_norm_w': out['m_l0_ssd_norm_w'], 'm_l0_w_out': out['m_l0_w_out'], 'm_l1_norm_w': out['m_l1_norm_w'], 'm_l1_w_in': out['m_l1_w_in'], 'm_l1_fox_b_f': out['m_l1_fox_b_f'], 'm_l1_w_out': out['m_l1_w_out'], 'm_final_norm_w': out['m_final_norm_w'], 'v_l0_norm_w': out['v_l0_norm_w'], 'v_l0_w_in': out['v_l0_w_in'], 'v_l0_s5_lambda_re': out['v_l0_s5_lambda_re'], 'v_l0_s5_lambda_im': out['v_l0_s5_lambda_im'], 'v_l0_s5_log_step': out['v_l0_s5_log_step'], 'v_l0_s5_b_re': out['v_l0_s5_b_re'], 'v_l0_s5_b_im': out['v_l0_s5_b_im'], 'v_l0_s5_c_re': out['v_l0_s5_c_re'], 'v_l0_s5_c_im': out['v_l0_s5_c_im'], 'v_l0_s5_d': out['v_l0_s5_d'], 'v_l0_s5_w_glu': out['v_l0_s5_w_glu'], 'v_l0_s5_b_glu': out['v_l0_s5_b_glu'], 'v_l0_ssd_conv_w': out['v_l0_ssd_conv_w'], 'v_l0_ssd_conv_b': out['v_l0_ssd_conv_b'], 'v_l0_ssd_dt_bias': out['v_l0_ssd_dt_bias'], 'v_l0_ssd_a_log': out['v_l0_ssd_a_log'], 'v_l0_ssd_d': out['v_l0_ssd_d'], 'v_l0_ssd_norm_w': out['v_l0_ssd_norm_w'], 'v_l0_w_out': out['v_l0_w_out'], 'v_l1_norm_w': out['v_l1_norm_w'], 'v_l1_w_in': out['v_l1_w_in'], 'v_l1_fox_b_f': out['v_l1_fox_b_f'], 'v_l1_w_out': out['v_l1_w_out'], 'v_final_norm_w': out['v_final_norm_w']}


def _loss(weights, diff, rest, loss_target):
    with _jax.named_scope("forward"):
        args = {**rest, TWIN_DIFF_INPUT: diff, **{k: w.astype(_WEIGHT_DTYPES[k]) for k, w in weights.items()}}
        y = _forward(args)
    with _jax.named_scope("loss_head"):
        err = _jnp.square(y.astype(_jnp.float32) - loss_target)
        return 0.5 * _jnp.sum(_jnp.mean(err, axis=-1)) if err.ndim else 0.5 * err


def _adamw(w, g, m, v):
    m = ADAM_B1 * m + (1.0 - ADAM_B1) * g
    v = ADAM_B2 * v + (1.0 - ADAM_B2) * _jnp.square(g)
    m_hat = m / (1.0 - ADAM_B1 ** ADAM_STEP)
    v_hat = v / (1.0 - ADAM_B2 ** ADAM_STEP)
    delta = -ADAM_LR * (m_hat / (_jnp.sqrt(v_hat) + ADAM_EPS) + ADAM_WD * w)
    return delta, m, v


def reference(x, l0_norm_w, l0_w_in, l0_s5_lambda_re, l0_s5_lambda_im, l0_s5_log_step, l0_s5_b_re, l0_s5_b_im, l0_s5_c_re, l0_s5_c_im, l0_s5_d, l0_s5_w_glu, l0_s5_b_glu, l0_ssd_conv_w, l0_ssd_conv_b, l0_ssd_dt_bias, l0_ssd_a_log, l0_ssd_d, l0_ssd_norm_w, l0_w_out, l1_norm_w, l1_w_in, l1_fox_b_f, l1_w_out, final_norm_w, loss_target, m_l0_norm_w, m_l0_w_in, m_l0_s5_lambda_re, m_l0_s5_lambda_im, m_l0_s5_log_step, m_l0_s5_b_re, m_l0_s5_b_im, m_l0_s5_c_re, m_l0_s5_c_im, m_l0_s5_d, m_l0_s5_w_glu, m_l0_s5_b_glu, m_l0_ssd_conv_w, m_l0_ssd_conv_b, m_l0_ssd_dt_bias, m_l0_ssd_a_log, m_l0_ssd_d, m_l0_ssd_norm_w, m_l0_w_out, m_l1_norm_w, m_l1_w_in, m_l1_fox_b_f, m_l1_w_out, m_final_norm_w, v_l0_norm_w, v_l0_w_in, v_l0_s5_lambda_re, v_l0_s5_lambda_im, v_l0_s5_log_step, v_l0_s5_b_re, v_l0_s5_b_im, v_l0_s5_c_re, v_l0_s5_c_im, v_l0_s5_d, v_l0_s5_w_glu, v_l0_s5_b_glu, v_l0_ssd_conv_w, v_l0_ssd_conv_b, v_l0_ssd_dt_bias, v_l0_ssd_a_log, v_l0_ssd_d, v_l0_ssd_norm_w, v_l0_w_out, v_l1_norm_w, v_l1_w_in, v_l1_fox_b_f, v_l1_w_out, v_final_norm_w):
    given = dict(x=x, l0_norm_w=l0_norm_w, l0_w_in=l0_w_in, l0_s5_lambda_re=l0_s5_lambda_re, l0_s5_lambda_im=l0_s5_lambda_im, l0_s5_log_step=l0_s5_log_step, l0_s5_b_re=l0_s5_b_re, l0_s5_b_im=l0_s5_b_im, l0_s5_c_re=l0_s5_c_re, l0_s5_c_im=l0_s5_c_im, l0_s5_d=l0_s5_d, l0_s5_w_glu=l0_s5_w_glu, l0_s5_b_glu=l0_s5_b_glu, l0_ssd_conv_w=l0_ssd_conv_w, l0_ssd_conv_b=l0_ssd_conv_b, l0_ssd_dt_bias=l0_ssd_dt_bias, l0_ssd_a_log=l0_ssd_a_log, l0_ssd_d=l0_ssd_d, l0_ssd_norm_w=l0_ssd_norm_w, l0_w_out=l0_w_out, l1_norm_w=l1_norm_w, l1_w_in=l1_w_in, l1_fox_b_f=l1_fox_b_f, l1_w_out=l1_w_out, final_norm_w=final_norm_w, loss_target=loss_target, m_l0_norm_w=m_l0_norm_w, m_l0_w_in=m_l0_w_in, m_l0_s5_lambda_re=m_l0_s5_lambda_re, m_l0_s5_lambda_im=m_l0_s5_lambda_im, m_l0_s5_log_step=m_l0_s5_log_step, m_l0_s5_b_re=m_l0_s5_b_re, m_l0_s5_b_im=m_l0_s5_b_im, m_l0_s5_c_re=m_l0_s5_c_re, m_l0_s5_c_im=m_l0_s5_c_im, m_l0_s5_d=m_l0_s5_d, m_l0_s5_w_glu=m_l0_s5_w_glu, m_l0_s5_b_glu=m_l0_s5_b_glu, m_l0_ssd_conv_w=m_l0_ssd_conv_w, m_l0_ssd_conv_b=m_l0_ssd_conv_b, m_l0_ssd_dt_bias=m_l0_ssd_dt_bias, m_l0_ssd_a_log=m_l0_ssd_a_log, m_l0_ssd_d=m_l0_ssd_d, m_l0_ssd_norm_w=m_l0_ssd_norm_w, m_l0_w_out=m_l0_w_out, m_l1_norm_w=m_l1_norm_w, m_l1_w_in=m_l1_w_in, m_l1_fox_b_f=m_l1_fox_b_f, m_l1_w_out=m_l1_w_out, m_final_norm_w=m_final_norm_w, v_l0_norm_w=v_l0_norm_w, v_l0_w_in=v_l0_w_in, v_l0_s5_lambda_re=v_l0_s5_lambda_re, v_l0_s5_lambda_im=v_l0_s5_lambda_im, v_l0_s5_log_step=v_l0_s5_log_step, v_l0_s5_b_re=v_l0_s5_b_re, v_l0_s5_b_im=v_l0_s5_b_im, v_l0_s5_c_re=v_l0_s5_c_re, v_l0_s5_c_im=v_l0_s5_c_im, v_l0_s5_d=v_l0_s5_d, v_l0_s5_w_glu=v_l0_s5_w_glu, v_l0_s5_b_glu=v_l0_s5_b_glu, v_l0_ssd_conv_w=v_l0_ssd_conv_w, v_l0_ssd_conv_b=v_l0_ssd_conv_b, v_l0_ssd_dt_bias=v_l0_ssd_dt_bias, v_l0_ssd_a_log=v_l0_ssd_a_log, v_l0_ssd_d=v_l0_ssd_d, v_l0_ssd_norm_w=v_l0_ssd_norm_w, v_l0_w_out=v_l0_w_out, v_l1_norm_w=v_l1_norm_w, v_l1_w_in=v_l1_w_in, v_l1_fox_b_f=v_l1_fox_b_f, v_l1_w_out=v_l1_w_out, v_final_norm_w=v_final_norm_w)
    weights = {n: given[n] for n in TWIN_WEIGHTS}
    shared = {n: given[n] for n in SHARED_INPUTS}
    per_example = {n: given[n] for n in ['x']}
    grad_fn = _jax.value_and_grad(_loss, argnums=(0, 1))

    def one_microbatch(ex, loss_target):
        ex = dict(ex)
        diff = ex.pop(TWIN_DIFF_INPUT)
        return grad_fn(weights, diff, {**shared, **ex}, loss_target)

    if N_MICROBATCH == 1:
        loss, (grad_w, grad_x) = one_microbatch(per_example, given["loss_target"])
    else:
        def body(carry, xs):
            loss_sum, grad_sum = carry
            l_k, (gw_k, gx_k) = one_microbatch(xs[0], xs[1])
            with _jax.named_scope("update"):
                return (loss_sum + l_k, _jax.tree.map(_jnp.add, grad_sum, gw_k)), gx_k

        init = (_jnp.zeros((), _jnp.float32), _jax.tree.map(_jnp.zeros_like, weights))
        (loss, grad_w), grad_x = _jax.lax.scan(body, init, (per_example, given["loss_target"]))
    with _jax.named_scope("update"):
        delta_w, new_m, new_v = {}, {}, {}
        for n in TWIN_WEIGHTS:
            delta_w[n], new_m[n], new_v[n] = _adamw(weights[n], grad_w[n], given["m_" + n], given["v_" + n])
    return (loss, grad_x, *[grad_w[n] for n in TWIN_WEIGHTS], *[delta_w[n] for n in TWIN_WEIGHTS],
            *[new_m[n] for n in TWIN_WEIGHTS], *[new_v[n] for n in TWIN_WEIGHTS])
```

```python
import functools
import math

import jax
import jax.numpy as jnp
from jax import lax
from jax.experimental import pallas as pl
from jax.experimental.pallas import tpu as pltpu

F32 = jnp.float32
BF16 = jnp.bfloat16

S5_GROUP = 16
S5_STATE = 64
S5_EIG_CLIP = -1e-4
SSD_HEAD_DIM = 64
SSD_GROUPS = 8
SSD_STATE = 128
SSD_CONV = 4
SSD_CHUNK = 128
FOX_HEAD_DIM = 128
NORM_EPS = 1e-5
ADAM_LR = 0.001
ADAM_B1 = 0.9
ADAM_B2 = 0.999
ADAM_EPS = 1e-08
ADAM_WD = 0.01
ADAM_STEP = 10

N_SHARD = 4
N_DEV = 8
LANES = 128
VMEM_LIMIT = 56 * 1024 * 1024
MESH = pl.DeviceIdType.MESH


def _pick(dim, prefs, offs=()):
    for p in prefs:
        if dim % p == 0 and all(o % p == 0 for o in offs):
            return p
    return dim


def _params(sem=None, vmem=VMEM_LIMIT):
    return pltpu.CompilerParams(dimension_semantics=sem, vmem_limit_bytes=vmem)


def _matmul(a, b, *, mode="nn", dims=None, a_off=(0, 0), b_off=(0, 0), addend=None,
            out_dtype=F32, name):
    if dims is None:
        if mode == "nn":
            dims = (a.shape[0], b.shape[1], a.shape[1])
        elif mode == "nt":
            dims = (a.shape[0], b.shape[0], a.shape[1])
        else:
            dims = (a.shape[1], b.shape[1], a.shape[0])
    m, n, k = dims
    if mode == "nn":
        om, on, ok = (a_off[0],), (b_off[1],), (a_off[1], b_off[0])
    elif mode == "nt":
        om, on, ok = (a_off[0],), (b_off[0],), (a_off[1], b_off[1])
    else:
        om, on, ok = (a_off[1],), (b_off[1],), (a_off[0], b_off[0])
    tm = _pick(m, (1024, 512, 256, 128), om)
    tn = _pick(n, (1024, 768, 512, 384, 256, 128), on)
    tk = _pick(k, (512, 256, 128), ok)
    nk = k // tk
    if mode == "nn":
        a_blk, a_div = (tm, tk), (tm, tk)
        b_blk, b_div = (tk, tn), (tk, tn)
        a_map = lambda i, j, kk: (i + a_off[0] // tm, kk + a_off[1] // tk)
        b_map = lambda i, j, kk: (kk + b_off[0] // tk, j + b_off[1] // tn)
        dn = (((1,), (0,)), ((), ()))
    elif mode == "nt":
        a_blk, a_div = (tm, tk), (tm, tk)
        b_blk, b_div = (tn, tk), (tn, tk)
        a_map = lambda i, j, kk: (i + a_off[0] // tm, kk + a_off[1] // tk)
        b_map = lambda i, j, kk: (j + b_off[0] // tn, kk + b_off[1] // tk)
        dn = (((1,), (1,)), ((), ()))
    else:
        a_blk, a_div = (tk, tm), (tk, tm)
        b_blk, b_div = (tk, tn), (tk, tn)
        a_map = lambda i, j, kk: (kk + a_off[0] // tk, i + a_off[1] // tm)
        b_map = lambda i, j, kk: (kk + b_off[0] // tk, j + b_off[1] // tn)
        dn = (((0,), (0,)), ((), ()))
    assert a_off[0] % a_div[0] == 0 and a_off[1] % a_div[1] == 0, (name, a_off, a_div)
    assert b_off[0] % b_div[0] == 0 and b_off[1] % b_div[1] == 0, (name, b_off, b_div)
    has_add = addend is not None

    def body(*refs):
        if has_add:
            a_ref, b_ref, c_ref, o_ref, acc_ref = refs
        else:
            a_ref, b_ref, o_ref, acc_ref = refs
        kk = pl.program_id(2)

        @pl.when(kk == 0)
        def _():
            acc_ref[...] = jnp.zeros_like(acc_ref)

        acc_ref[...] += lax.dot_general(a_ref[...].astype(BF16), b_ref[...].astype(BF16), dn,
                                        preferred_element_type=F32)

        @pl.when(kk == nk - 1)
        def _():
            r = acc_ref[...]
            if has_add:
                r = r + c_ref[...].astype(F32)
            o_ref[...] = r.astype(o_ref.dtype)

    in_specs = [pl.BlockSpec(a_blk, a_map), pl.BlockSpec(b_blk, b_map)]
    args = [a, b]
    if has_add:
        in_specs.append(pl.BlockSpec((tm, tn), lambda i, j, kk: (i, j)))
        args.append(addend)
    return pl.pallas_call(
        body, name=name, grid=(m // tm, n // tn, nk),
        in_specs=in_specs, out_specs=pl.BlockSpec((tm, tn), lambda i, j, kk: (i, j)),
        out_shape=jax.ShapeDtypeStruct((m, n), out_dtype),
        scratch_shapes=[pltpu.VMEM((tm, tn), F32)],
        compiler_params=_params(("parallel", "parallel", "arbitrary")),
    )(*args)


def _rowwise(fn, rows, params, out_rows, out_accs=(), *, tl, ncol=1, name):
    rows = [r if isinstance(r, tuple) else (r, r.shape[1] // ncol, 0) for r in rows]
    n_rows, n_par, n_or, n_oa = len(rows), len(params), len(out_rows), len(out_accs)
    length = rows[0][0].shape[0]
    tl = _pick(length, [t for t in (1024, 512, 256, 128, 64, 32, 16, 8) if t <= tl])

    def body(*refs):
        row_refs = refs[:n_rows]
        par_refs = refs[n_rows:n_rows + n_par]
        or_refs = refs[n_rows + n_par:n_rows + n_par + n_or]
        oa_refs = refs[n_rows + n_par + n_or:]
        outs = fn(*[r[...] for r in row_refs], *[p[...] for p in par_refs])
        if not isinstance(outs, (tuple, list)):
            outs = (outs,)
        for r, v in zip(or_refs, outs[:n_or]):
            r[...] = v.astype(r.dtype)
        if n_oa:
            @pl.when(pl.program_id(1) == 0)
            def _():
                for r in oa_refs:
                    r[...] = jnp.zeros_like(r)

            for r, v in zip(oa_refs, outs[n_or:]):
                r[...] += v.astype(F32)

    in_specs = [pl.BlockSpec((tl, w), functools.partial(lambda j, i, b0: (i, b0 + j), b0=b0))
                for (_, w, b0) in rows]
    in_specs += [pl.BlockSpec((p.shape[0], p.shape[1] // ncol), lambda j, i: (0, j)) for p in params]
    out_specs = [pl.BlockSpec((tl, w), lambda j, i: (i, j)) for (w, _) in out_rows]
    out_specs += [pl.BlockSpec((1, w), lambda j, i: (0, j)) for w in out_accs]
    out_shape = [jax.ShapeDtypeStruct((length, ncol * w), dt) for (w, dt) in out_rows]
    out_shape += [jax.ShapeDtypeStruct((1, ncol * w), F32) for w in out_accs]
    res = pl.pallas_call(
        body, name=name, grid=(ncol, length // tl),
        in_specs=in_specs, out_specs=out_specs, out_shape=out_shape,
        compiler_params=_params(("parallel", "arbitrary" if n_oa else "parallel")),
    )(*[r[0] for r in rows], *params)
    return res


def _bdmm(xs, ws, *, addend=None, out_dtype=F32, name):
    nj, kin, kout = ws[0].shape
    xs = [x if isinstance(x, tuple) else (x, 0) for x in xs]
    length = xs[0][0].shape[0]
    tl = _pick(length, (512, 256, 128))
    n_x = len(xs)
    has_add = addend is not None

    def body(*refs):
        x_refs = refs[:n_x]
        w_refs = refs[n_x:2 * n_x]
        o_ref = refs[-1]
        acc = None
        for xr, wr in zip(x_refs, w_refs):
            t = jnp.dot(xr[...].astype(BF16), wr[0], preferred_element_type=F32)
            acc = t if acc is None else acc + t
        if has_add:
            acc = acc + refs[2 * n_x][...].astype(F32)
        o_ref[...] = acc.astype(o_ref.dtype)

    in_specs = [pl.BlockSpec((tl, kin), functools.partial(lambda i, j, b0: (i, b0 + j), b0=b0)) for (_, b0) in xs]
    in_specs += [pl.BlockSpec((1, kin, kout), lambda i, j: (j, 0, 0)) for _ in ws]
    args = [x[0] for x in xs] + list(ws)
    if has_add:
        in_specs.append(pl.BlockSpec((tl, kout), lambda i, j: (i, j)))
        args.append(addend)
    return pl.pallas_call(
        body, name=name, grid=(length // tl, nj),
        in_specs=in_specs, out_specs=pl.BlockSpec((tl, kout), lambda i, j: (i, j)),
        out_shape=jax.ShapeDtypeStruct((length, nj * kout), out_dtype),
        compiler_params=_params(("parallel", "parallel")),
    )(*args)


def _bdmm_tn_sized(x, g, nj, kin, kout, x_first, name):
    length = x.shape[0]
    tl = _pick(length, (512, 256, 128))
    nt = length // tl

    def body(x_ref, g_ref, o_ref):
        @pl.when(pl.program_id(1) == 0)
        def _():
            o_ref[...] = jnp.zeros_like(o_ref)

        o_ref[0] += lax.dot_general(x_ref[...].astype(BF16), g_ref[...].astype(BF16),
                                    (((0,), (0,)), ((), ())), preferred_element_type=F32)

    return pl.pallas_call(
        body, name=name, grid=(nj, nt),
        in_specs=[pl.BlockSpec((tl, kin), lambda j, t: (t, x_first + j)),
                  pl.BlockSpec((tl, kout), lambda j, t: (t, j))],
        out_specs=pl.BlockSpec((1, kin, kout), lambda j, t: (j, 0, 0)),
        out_shape=jax.ShapeDtypeStruct((nj, kin, kout), F32),
        compiler_params=_params(("parallel", "arbitrary")),
    )(x, g)


def _f_norm(x, w):
    return x * lax.rsqrt(jnp.mean(x * x, axis=-1, keepdims=True) + NORM_EPS) * w


def _gelu(y):
    return 0.5 * y * (1.0 + jnp.tanh(math.sqrt(2.0 / math.pi) * (y + 0.044715 * (y * y * y))))


def _sigmoid(x):
    return 1.0 / (1.0 + jnp.exp(-x))


def _silu(x):
    return x * _sigmoid(x)


def _softplus(x):
    return jnp.maximum(x, 0.0) + jnp.log(1.0 + jnp.exp(-jnp.abs(x)))


def _f_s5_gelu(yc, u, dvec):
    return _gelu(yc + dvec * u)


def _f_s5_out(yc, u, t, gate, dvec, bglu):
    gl = _gelu(yc + dvec * u)
    return gl * _sigmoid(t + bglu) * _silu(gate)


def _f_ssd_out(y, xs, z, dpar, nw):
    v = (y + dpar * xs) * _silu(z)
    return v * lax.rsqrt(jnp.mean(v * v, axis=-1, keepdims=True) + NORM_EPS) * nw


def _f_fox_out(att, gate):
    return att * _silu(gate)


def _norm_fwd(x, w, name):
    return _rowwise(lambda xt, wt: _f_norm(xt, wt), [x], [w], [(x.shape[1], BF16)], tl=256, name=name)[0]


def _norm_bwd(x, dh, dres, w, name):
    d = x.shape[1]

    def fn(xt, dht, drt, wt):
        _, vjp = jax.vjp(_f_norm, xt, wt)
        dx, dw = vjp(dht)
        dx = dx + drt
        return dx, dx, dw

    return _rowwise(fn, [x, dh, dres], [w], [(d, F32), (d, BF16)], [d], tl=128, name=name)


def _adamw_math(w, g, m, v):
    m = ADAM_B1 * m + (1.0 - ADAM_B1) * g
    v = ADAM_B2 * v + (1.0 - ADAM_B2) * jnp.square(g)
    m_hat = m / (1.0 - ADAM_B1 ** ADAM_STEP)
    v_hat = v / (1.0 - ADAM_B2 ** ADAM_STEP)
    delta = -ADAM_LR * (m_hat / (jnp.sqrt(v_hat) + ADAM_EPS) + ADAM_WD * w)
    return delta, m, v


def _adamw(w, g, m, v, name):
    c = w.shape[1]
    tl = max(8, min(256, (2 * 1024 * 1024 // (4 * c)) // 8 * 8))
    return _rowwise(_adamw_math, [w, g, m, v], [], [(c, F32)] * 3, tl=tl, name=name)


def _s5_scan_fwd(bu_re, bu_im, a_re, a_im, name):
    length, rows, _ = bu_re.shape
    rb = _pick(rows, (32, 16, 8))
    tl = _pick(length, (64, 32, 16, 8))

    def body(bur_ref, bui_ref, ar_ref, ai_ref, sr_ref, si_ref, st_ref):
        @pl.when(pl.program_id(1) == 0)
        def _():
            st_ref[...] = jnp.zeros_like(st_ref)

        ar = ar_ref[...]
        ai = ai_ref[...]

        def step(l, carry):
            sr, si = carry
            nr = ar * sr - ai * si + bur_ref[l]
            ni = ar * si + ai * sr + bui_ref[l]
            sr_ref[l] = nr
            si_ref[l] = ni
            return nr, ni

        sr, si = lax.fori_loop(0, tl, step, (st_ref[0], st_ref[1]))
        st_ref[0] = sr
        st_ref[1] = si

    blk = pl.BlockSpec((tl, rb, LANES), lambda cb, t: (t, cb, 0))
    ablk = pl.BlockSpec((rb, LANES), lambda cb, t: (cb, 0))
    return pl.pallas_call(
        body, name=name, grid=(rows // rb, length // tl),
        in_specs=[blk, blk, ablk, ablk], out_specs=[blk, blk],
        out_shape=[jax.ShapeDtypeStruct(bu_re.shape, F32)] * 2,
        scratch_shapes=[pltpu.VMEM((2, rb, LANES), F32)],
        compiler_params=_params(("parallel", "arbitrary")),
    )(bu_re, bu_im, a_re, a_im)


def _s5_scan_bwd(ds_re, ds_im, s_re, s_im, a_re, a_im, name):
    length, rows, _ = ds_re.shape
    rb = _pick(rows, (32, 16, 8))
    tl = _pick(length, (64, 32, 16, 8))
    nt = length // tl

    def body(dsr_ref, dsi_ref, sr_ref, si_ref, pr_ref, pi_ref, ar_ref, ai_ref,
             gr_ref, gi_ref, dar_ref, dai_ref, st_ref):
        t = pl.program_id(1)

        @pl.when(t == 0)
        def _():
            st_ref[...] = jnp.zeros_like(st_ref)
            dar_ref[...] = jnp.zeros_like(dar_ref)
            dai_ref[...] = jnp.zeros_like(dai_ref)

        ar = ar_ref[...]
        ai = ai_ref[...]

        def adj(l, gr, gi):
            ngr = dsr_ref[l] + ar * gr + ai * gi
            ngi = dsi_ref[l] + ar * gi - ai * gr
            gr_ref[l] = ngr
            gi_ref[l] = ngi
            return ngr, ngi

        def step(idx, carry):
            gr, gi, dar, dai = carry
            l = tl - 1 - idx
            gr, gi = adj(l, gr, gi)
            pr = sr_ref[l - 1]
            pi = si_ref[l - 1]
            dar = dar + gr * pr + gi * pi
            dai = dai + gi * pr - gr * pi
            return gr, gi, dar, dai

        zero = jnp.zeros((rb, LANES), F32)
        gr, gi, dar, dai = lax.fori_loop(0, tl - 1, step, (st_ref[0], st_ref[1], zero, zero))
        gr, gi = adj(0, gr, gi)
        first = (t == nt - 1)
        pr = jnp.where(first, 0.0, pr_ref[0])
        pi = jnp.where(first, 0.0, pi_ref[0])
        dar = dar + gr * pr + gi * pi
        dai = dai + gi * pr - gr * pi
        st_ref[0] = gr
        st_ref[1] = gi
        dar_ref[...] += dar
        dai_ref[...] += dai

    blk = pl.BlockSpec((tl, rb, LANES), lambda cb, t: (nt - 1 - t, cb, 0))
    prev = pl.BlockSpec((1, rb, LANES), lambda cb, t: (jnp.maximum((nt - 1 - t) * tl - 1, 0), cb, 0))
    ablk = pl.BlockSpec((rb, LANES), lambda cb, t: (cb, 0))
    return pl.pallas_call(
        body, name=name, grid=(rows // rb, nt),
        in_specs=[blk, blk, blk, blk, prev, prev, ablk, ablk],
        out_specs=[blk, blk, ablk, ablk],
        out_shape=[jax.ShapeDtypeStruct(ds_re.shape, F32)] * 2 + [jax.ShapeDtypeStruct(a_re.shape, F32)] * 2,
        scratch_shapes=[pltpu.VMEM((2, rb, LANES), F32)],
        compiler_params=_params(("parallel", "arbitrary")),
    )(ds_re, ds_im, s_re, s_im, s_re, s_im, a_re, a_im)


def _s5_prepare(lam_re, lam_im, log_step, b_re, b_im, c_re, c_im):
    groups, state = lam_re.shape
    lr = jnp.minimum(lam_re, S5_EIG_CLIP)
    li = lam_im
    step = jnp.exp(log_step)[:, None]
    mag = jnp.exp(lr * step)
    ab_re = mag * jnp.cos(li * step)
    ab_im = mag * jnp.sin(li * step)
    denom = lr * lr + li * li
    nr = ab_re - 1.0
    ni = ab_im
    coef_re = (nr * lr + ni * li) / denom
    coef_im = (ni * lr - nr * li) / denom
    bb_re = coef_re[..., None] * b_re - coef_im[..., None] * b_im
    bb_im = coef_re[..., None] * b_im + coef_im[..., None] * b_re
    per = LANES // S5_GROUP
    nj = groups // per
    eye = jnp.eye(per, dtype=F32)

    def in_map(bb):
        return jnp.einsum('jgph,gk->jghkp', bb.reshape(nj, per, state, S5_GROUP), eye).reshape(
            nj, per * S5_GROUP, per * state)

    def out_map(cc):
        return jnp.einsum('jghp,gk->jgpkh', cc.reshape(nj, per, S5_GROUP, state), eye).reshape(
            nj, per * state, per * S5_GROUP)

    shape2 = (groups * state // LANES, LANES)
    return (ab_re.reshape(shape2), ab_im.reshape(shape2), in_map(bb_re), in_map(bb_im),
            out_map(c_re), -out_map(c_im))


def _shift_down(cur, prev8, j):
    rolled = pltpu.roll(cur, j, 0)
    pr = pltpu.roll(prev8, j, 0)
    row = lax.broadcasted_iota(jnp.int32, cur.shape, 0)
    return jnp.where(row < j, jnp.tile(pr, (cur.shape[0] // 8, 1)), rolled)


def _shift_up(cur, next8, j):
    tl = cur.shape[0]
    rolled = pltpu.roll(cur, tl - j, 0)
    nx = pltpu.roll(next8, 8 - j, 0)
    row = lax.broadcasted_iota(jnp.int32, cur.shape, 0)
    return jnp.where(row >= tl - j, jnp.tile(nx, (tl // 8, 1)), rolled)


def _conv_tiles(length, ch):
    return _pick(length, (256, 128, 64, 32, 16, 8)), _pick(ch, (1024, 512, 256, 128))


def _conv_fwd(xbc, w, b, name):
    length, ch = xbc.shape
    tl, tc = _conv_tiles(length, ch)

    def body(x_ref, p_ref, w_ref, b_ref, o_ref):
        cur = x_ref[...]
        prev8 = jnp.where(pl.program_id(1) == 0, 0.0, p_ref[...])
        pre = b_ref[...] + w_ref[SSD_CONV - 1:SSD_CONV, :] * cur
        for j in range(1, SSD_CONV):
            pre = pre + w_ref[SSD_CONV - 1 - j:SSD_CONV - j, :] * _shift_down(cur, prev8, j)
        o_ref[...] = _silu(pre)

    return pl.pallas_call(
        body, name=name, grid=(ch // tc, length // tl),
        in_specs=[pl.BlockSpec((tl, tc), lambda c, i: (i, c)),
                  pl.BlockSpec((8, tc), lambda c, i: (jnp.maximum(i * (tl // 8) - 1, 0), c)),
                  pl.BlockSpec((SSD_CONV, tc), lambda c, i: (0, c)),
                  pl.BlockSpec((1, tc), lambda c, i: (0, c))],
        out_specs=pl.BlockSpec((tl, tc), lambda c, i: (i, c)),
        out_shape=jax.ShapeDtypeStruct((length, ch), F32),
        compiler_params=_params(("parallel", "parallel")),
    )(xbc, xbc, w, b)


def _conv_bwd_pre(dxc, xbc, w, b, name):
    length, ch = xbc.shape
    tl, tc = _conv_tiles(length, ch)

    def body(d_ref, x_ref, p_ref, w_ref, b_ref, o_ref, dw_ref, db_ref):
        @pl.when(pl.program_id(1) == 0)
        def _():
            dw_ref[...] = jnp.zeros_like(dw_ref)
            db_ref[...] = jnp.zeros_like(db_ref)

        cur = x_ref[...]
        prev8 = jnp.where(pl.program_id(1) == 0, 0.0, p_ref[...])
        shifted = [cur] + [_shift_down(cur, prev8, j) for j in range(1, SSD_CONV)]
        pre = b_ref[...]
        for j in range(SSD_CONV):
            pre = pre + w_ref[SSD_CONV - 1 - j:SSD_CONV - j, :] * shifted[j]
        sg = _sigmoid(pre)
        dpre = d_ref[...] * (sg * (1.0 + pre * (1.0 - sg)))
        o_ref[...] = dpre
        db_ref[...] += jnp.sum(dpre, axis=0, keepdims=True)
        row = lax.broadcasted_iota(jnp.int32, (SSD_CONV, tc), 0)
        dw = jnp.zeros((SSD_CONV, tc), F32)
        for j in range(SSD_CONV):
            dw = dw + jnp.where(row == SSD_CONV - 1 - j, jnp.sum(dpre * shifted[j], axis=0, keepdims=True), 0.0)
        dw_ref[...] += dw

    return pl.pallas_call(
        body, name=name, grid=(ch // tc, length // tl),
        in_specs=[pl.BlockSpec((tl, tc), lambda c, i: (i, c)),
                  pl.BlockSpec((tl, tc), lambda c, i: (i, c)),
                  pl.BlockSpec((8, tc), lambda c, i: (jnp.maximum(i * (tl // 8) - 1, 0), c)),
                  pl.BlockSpec((SSD_CONV, tc), lambda c, i: (0, c)),
                  pl.BlockSpec((1, tc), lambda c, i: (0, c))],
        out_specs=[pl.BlockSpec((tl, tc), lambda c, i: (i, c)),
                   pl.BlockSpec((SSD_CONV, tc), lambda c, i: (0, c)),
                   pl.BlockSpec((1, tc), lambda c, i: (0, c))],
        out_shape=[jax.ShapeDtypeStruct((length, ch), F32), jax.ShapeDtypeStruct((SSD_CONV, ch), F32),
                   jax.ShapeDtypeStruct((1, ch), F32)],
        compiler_params=_params(("parallel", "arbitrary")),
    )(dxc, xbc, xbc, w, b)


def _conv_bwd_in(dpre, w, name):
    length, ch = dpre.shape
    tl, tc = _conv_tiles(length, ch)
    nt = length // tl

    def body(d_ref, n_ref, w_ref, o_ref):
        cur = d_ref[...]
        next8 = jnp.where(pl.program_id(1) == nt - 1, 0.0, n_ref[...])
        acc = w_ref[SSD_CONV - 1:SSD_CONV, :] * cur
        for j in range(1, SSD_CONV):
            acc = acc + w_ref[SSD_CONV - 1 - j:SSD_CONV - j, :] * _shift_up(cur, next8, j)
        o_ref[...] = acc.astype(o_ref.dtype)

    return pl.pallas_call(
        body, name=name, grid=(ch // tc, nt),
        in_specs=[pl.BlockSpec((tl, tc), lambda c, i: (i, c)),
                  pl.BlockSpec((8, tc), lambda c, i: (jnp.minimum((i + 1) * (tl // 8), length // 8 - 1), c)),
                  pl.BlockSpec((SSD_CONV, tc), lambda c, i: (0, c))],
        out_specs=pl.BlockSpec((tl, tc), lambda c, i: (i, c)),
        out_shape=jax.ShapeDtypeStruct((length, ch), BF16),
        compiler_params=_params(("parallel", "parallel")),
    )(dpre, dpre, w)


def _split3(x):
    h = x.astype(BF16)
    r = x - h.astype(F32)
    m = r.astype(BF16)
    lo = (r - m.astype(F32)).astype(BF16)
    return h, m, lo


def _dot(a, b, dn=(((1,), (0,)), ((), ()))):
    return lax.dot_general(a, b, dn, preferred_element_type=F32)


_NT = (((1,), (1,)), ((), ()))
_TN = (((0,), (0,)), ((), ()))


def _dot3(x, sel, dn=(((1,), (0,)), ((), ()))):
    h, m, lo = _split3(x)
    return _dot(h, sel, dn) + _dot(m, sel, dn) + _dot(lo, sel, dn)


def _dot3r(sel, x, dn=(((1,), (0,)), ((), ()))):
    h, m, lo = _split3(x)
    return _dot(sel, h, dn) + _dot(sel, m, dn) + _dot(sel, lo, dn)


def _iota2(shape, axis):
    return lax.broadcasted_iota(jnp.int32, shape, axis)


def _ssd_masks():
    q = SSD_CHUNK
    r, c = _iota2((q, q), 0), _iota2((q, q), 1)
    tril = (c <= r)
    return r, c, tril


def _pair_sel(i):
    r, c, _ = _ssd_masks()
    return (r == 2 * i + c // SSD_HEAD_DIM).astype(BF16)


def _pair_sel_t(i):
    r, c, _ = _ssd_masks()
    return (c == 2 * i + r // SSD_HEAD_DIM).astype(F32)


def _ssd_common(dtraw, dtb, ap, b_t, c_t):
    r, c, tril = _ssd_masks()
    dt = _softplus(dtraw + dtb)
    la = dt * ap
    tril_b = tril.astype(BF16)
    cum = _dot3r(tril_b, la)
    rem = _dot3r((c > r).astype(BF16), la)
    total = _dot3(la, jnp.ones((SSD_CHUNK, SSD_CHUNK), BF16), _TN)
    scores = _dot(c_t, b_t, _NT)
    return dt, la, cum, rem, total, scores, tril


def _head_decay(cum, h, tril):
    r, c, _ = _ssd_masks()
    cq = _dot3(cum, (r == h).astype(BF16))
    ck = _dot3r((c == h).astype(BF16), cum, _NT)
    return jnp.exp(jnp.where(tril, cq - ck, -jnp.inf))


def _ssd_tiles(xc, n_heads):
    hpg = n_heads // SSD_GROUPS
    wg = hpg * SSD_HEAD_DIM
    xw = n_heads * SSD_HEAD_DIM
    return hpg, wg, xw // wg, xw // SSD_STATE


def _ssd_fwd(xc, dtraw, dtb, ap, n_heads, name):
    length = xc.shape[0]
    q = SSD_CHUNK
    nc = length // q
    hpg, wg, _, b_blk0 = _ssd_tiles(xc, n_heads)
    c_blk0 = b_blk0 + SSD_GROUPS
    npair = hpg // 2

    def body(x_ref, b_ref, c_ref, dt_ref, dtb_ref, ap_ref, y_ref, st_ref, s_ref):
        @pl.when(pl.program_id(1) == 0)
        def _():
            s_ref[...] = jnp.zeros_like(s_ref)

        st_ref[0, 0] = s_ref[...]
        b_t = b_ref[...].astype(BF16)
        c_t = c_ref[...].astype(BF16)
        dt, la, cum, rem, total, scores, tril = _ssd_common(dt_ref[...], dtb_ref[...], ap_ref[...], b_t, c_t)
        lane = _iota2((q, q), 1)
        for i in range(npair):
            sel = _pair_sel(i)
            xp = x_ref[:, i * LANES:(i + 1) * LANES]
            xd = xp * _dot3(dt, sel)
            xd_b = xd.astype(BF16)
            s_prev = s_ref[i * LANES:(i + 1) * LANES, :]
            y = _dot(c_t, s_prev.astype(BF16), _NT) * jnp.exp(_dot3(cum, sel))
            for hh in range(2):
                own = (lane // SSD_HEAD_DIM) == hh
                wm = scores * _head_decay(cum, 2 * i + hh, tril)
                y = y + _dot(wm.astype(BF16), jnp.where(own, xd_b, 0))
            y_ref[:, i * LANES:(i + 1) * LANES] = y
            xw_b = (xd * jnp.exp(_dot3(rem, sel))).astype(BF16)
            grow = jnp.exp(_dot3r(sel, total, _TN))
            s_ref[i * LANES:(i + 1) * LANES, :] = grow * s_prev + _dot(xw_b, b_t, _TN)

    return pl.pallas_call(
        body, name=name, grid=(SSD_GROUPS, nc),
        in_specs=[pl.BlockSpec((q, wg), lambda g, c: (c, g)),
                  pl.BlockSpec((q, SSD_STATE), lambda g, c: (c, b_blk0 + g)),
                  pl.BlockSpec((q, SSD_STATE), lambda g, c: (c, c_blk0 + g)),
                  pl.BlockSpec((q, LANES), lambda g, c: (c, g)),
                  pl.BlockSpec((1, LANES), lambda g, c: (0, g)),
                  pl.BlockSpec((1, LANES), lambda g, c: (0, g))],
        out_specs=[pl.BlockSpec((q, wg), lambda g, c: (c, g)),
                   pl.BlockSpec((1, 1, wg, SSD_STATE), lambda g, c: (c, g, 0, 0))],
        out_shape=[jax.ShapeDtypeStruct((length, SSD_GROUPS * wg), F32),
                   jax.ShapeDtypeStruct((nc, SSD_GROUPS, wg, SSD_STATE), F32)],
        scratch_shapes=[pltpu.VMEM((wg, SSD_STATE), F32)],
        compiler_params=_params(("parallel", "arbitrary")),
    )(xc, xc, xc, dtraw, dtb, ap)


def _ssd_bwd(dy, dxa, xc, dtraw, states, dtb, ap, n_heads, name):
    length = xc.shape[0]
    q = SSD_CHUNK
    nc = length // q
    hpg, wg, _, b_blk0 = _ssd_tiles(xc, n_heads)
    c_blk0 = b_blk0 + SSD_GROUPS
    npair = hpg // 2

    def body(dy_ref, dxa_ref, x_ref, b_ref, c_ref, dt_ref, st_ref, dtb_ref, ap_ref,
             dx_ref, db_ref, dc_ref, ddt_ref, ddtb_ref, dap_ref, ds_ref):
        @pl.when(pl.program_id(1) == 0)
        def _():
            ds_ref[...] = jnp.zeros_like(ds_ref)
            ddtb_ref[...] = jnp.zeros_like(ddtb_ref)
            dap_ref[...] = jnp.zeros_like(dap_ref)

        b_f = b_ref[...]
        c_f = c_ref[...]
        b_t = b_f.astype(BF16)
        c_t = c_f.astype(BF16)
        dtraw_t = dt_ref[...]
        dt, la, cum, rem, total, scores, tril = _ssd_common(dtraw_t, dtb_ref[...], ap_ref[...], b_t, c_t)
        r, c, _ = _ssd_masks()
        lane = c
        ones_b = jnp.ones((q, q), BF16)
        dcum = jnp.zeros((q, q), F32)
        drem = jnp.zeros((q, q), F32)
        dtot = jnp.zeros((q, q), F32)
        ddt = jnp.zeros((q, q), F32)
        dscores = jnp.zeros((q, q), F32)
        db_acc = jnp.zeros((q, SSD_STATE), F32)
        dc_acc = jnp.zeros((q, SSD_STATE), F32)
        for i in range(npair):
            sel = _pair_sel(i)
            xp = x_ref[:, i * LANES:(i + 1) * LANES]
            dyp = dy_ref[:, i * LANES:(i + 1) * LANES]
            dyp_b = dyp.astype(BF16)
            dtp = _dot3(dt, sel)
            ecum = jnp.exp(_dot3(cum, sel))
            wrem = jnp.exp(_dot3(rem, sel))
            xd = xp * dtp
            xd_b = xd.astype(BF16)
            s_prev = s_prev_f = st_ref[0, 0, i * LANES:(i + 1) * LANES, :]
            ds1 = ds_ref[i * LANES:(i + 1) * LANES, :]
            ds1_b = ds1.astype(BF16)
            dxd = jnp.zeros((q, LANES), F32)
            for hh in range(2):
                h = 2 * i + hh
                own = (lane // SSD_HEAD_DIM) == hh
                decay = _head_decay(cum, h, tril)
                wm = scores * decay
                dwm = _dot(jnp.where(own, dyp_b, 0), xd_b, _NT)
                dxd = dxd + jnp.where(own, _dot(wm.astype(BF16), dyp_b, _TN), 0.0)
                dscores = dscores + dwm * decay
                e = (dwm * wm).astype(BF16)
                put = (c == h).astype(BF16)
                dcum = dcum + _dot(e, put) - _dot(e, put, _TN)
            t_mat = _dot(c_t, s_prev.astype(BF16), _NT)
            d_t = (dyp * ecum).astype(BF16)
            dc_acc = dc_acc + _dot(d_t, s_prev.astype(BF16))
            ds_prev = _dot(d_t, c_t, _TN)
            dcum = dcum + _dot3(dyp * t_mat * ecum, sel, _NT)
            grow = jnp.exp(_dot3r(sel, total, _TN))
            ds_prev = ds_prev + grow * ds1
            zs = jnp.sum(ds1 * s_prev_f * grow, axis=1, keepdims=True)
            dtot = dtot + _dot3r(ones_b, zs * _pair_sel_t(i))
            xw = xd * wrem
            dxw = _dot(b_t, ds1_b, _NT)
            db_acc = db_acc + _dot(xw.astype(BF16), ds1_b)
            dxd = dxd + dxw * wrem
            drem = drem + _dot3(dxw * xw, sel, _NT)
            dx_ref[:, i * LANES:(i + 1) * LANES] = dxd * dtp + dxa_ref[:, i * LANES:(i + 1) * LANES]
            ddt = ddt + _dot3(dxd * xp, sel, _NT)
            ds_ref[i * LANES:(i + 1) * LANES, :] = ds_prev
        ds_b = dscores.astype(BF16)
        dc_ref[...] = dc_acc + _dot(ds_b, b_t)
        db_ref[...] = db_acc + _dot(ds_b, c_t, _TN)
        dla = (_dot3r(tril.astype(BF16), dcum, _TN) + _dot3r((c > r).astype(BF16), drem, _TN) + dtot)
        ddt = ddt + dla * ap_ref[...]
        dap_ref[...] += jnp.sum(dla * dt, axis=0, keepdims=True)
        ddtraw = ddt * _sigmoid(dtraw_t + dtb_ref[...])
        ddt_ref[...] = ddtraw.astype(ddt_ref.dtype)
        ddtb_ref[...] += jnp.sum(ddtraw, axis=0, keepdims=True)

    rev = lambda g, c: (nc - 1 - c, g)
    return pl.pallas_call(
        body, name=name, grid=(SSD_GROUPS, nc),
        in_specs=[pl.BlockSpec((q, wg), rev),
                  pl.BlockSpec((q, wg), rev),
                  pl.BlockSpec((q, wg), rev),
                  pl.BlockSpec((q, SSD_STATE), lambda g, c: (nc - 1 - c, b_blk0 + g)),
                  pl.BlockSpec((q, SSD_STATE), lambda g, c: (nc - 1 - c, c_blk0 + g)),
                  pl.BlockSpec((q, LANES), rev),
                  pl.BlockSpec((1, 1, wg, SSD_STATE), lambda g, c: (nc - 1 - c, g, 0, 0)),
                  pl.BlockSpec((1, LANES), lambda g, c: (0, g)),
                  pl.BlockSpec((1, LANES), lambda g, c: (0, g))],
        out_specs=[pl.BlockSpec((q, wg), rev),
                   pl.BlockSpec((q, SSD_STATE), rev),
                   pl.BlockSpec((q, SSD_STATE), rev),
                   pl.BlockSpec((q, LANES), rev),
                   pl.BlockSpec((1, LANES), lambda g, c: (0, g)),
                   pl.BlockSpec((1, LANES), lambda g, c: (0, g))],
        out_shape=[jax.ShapeDtypeStruct((length, SSD_GROUPS * wg), F32),
                   jax.ShapeDtypeStruct((length, SSD_GROUPS * SSD_STATE), F32),
                   jax.ShapeDtypeStruct((length, SSD_GROUPS * SSD_STATE), F32),
                   jax.ShapeDtypeStruct((length, SSD_GROUPS * LANES), BF16),
                   jax.ShapeDtypeStruct((1, SSD_GROUPS * LANES), F32),
                   jax.ShapeDtypeStruct((1, SSD_GROUPS * LANES), F32)],
        scratch_shapes=[pltpu.VMEM((wg, SSD_STATE), F32)],
        compiler_params=_params(("parallel", "arbitrary")),
    )(dy, dxa, xc, xc, xc, dtraw, states, dtb, ap)


def _fox_cumsum(fraw, bf, name):
    length = fraw.shape[0]
    q = 128

    def body(f_ref, b_ref, o_ref, carry_ref):
        @pl.when(pl.program_id(0) == 0)
        def _():
            carry_ref[...] = jnp.zeros_like(carry_ref)

        lf = -_softplus(-(f_ref[...] + b_ref[...]))
        r, c = _iota2((q, q), 0), _iota2((q, q), 1)
        o_ref[...] = _dot3r((c <= r).astype(BF16), lf) + carry_ref[...]
        carry_ref[...] += jnp.sum(lf, axis=0, keepdims=True)

    return pl.pallas_call(
        body, name=name, grid=(length // q,),
        in_specs=[pl.BlockSpec((q, LANES), lambda i: (i, 0)), pl.BlockSpec((1, LANES), lambda i: (0, 0))],
        out_specs=pl.BlockSpec((q, LANES), lambda i: (i, 0)),
        out_shape=jax.ShapeDtypeStruct((length, LANES), F32),
        scratch_shapes=[pltpu.VMEM((1, LANES), F32)],
        compiler_params=_params(("arbitrary",)),
    )(fraw, bf)


def _fox_cumsum_bwd(dc, fraw, bf, name):
    length = fraw.shape[0]
    q = 128
    nt = length // q

    def body(d_ref, f_ref, b_ref, o_ref, db_ref, carry_ref):
        @pl.when(pl.program_id(0) == 0)
        def _():
            carry_ref[...] = jnp.zeros_like(carry_ref)
            db_ref[...] = jnp.zeros_like(db_ref)

        d = d_ref[...]
        r, c = _iota2((q, q), 0), _iota2((q, q), 1)
        dlf = _dot3r((c >= r).astype(BF16), d) + carry_ref[...]
        carry_ref[...] += jnp.sum(d, axis=0, keepdims=True)
        df = dlf * _sigmoid(-(f_ref[...] + b_ref[...]))
        o_ref[...] = df.astype(o_ref.dtype)
        db_ref[...] += jnp.sum(df, axis=0, keepdims=True)

    rev = lambda i: (nt - 1 - i, 0)
    return pl.pallas_call(
        body, name=name, grid=(nt,),
        in_specs=[pl.BlockSpec((q, LANES), rev), pl.BlockSpec((q, LANES), rev),
                  pl.BlockSpec((1, LANES), lambda i: (0, 0))],
        out_specs=[pl.BlockSpec((q, LANES), rev), pl.BlockSpec((1, LANES), lambda i: (0, 0))],
        out_shape=[jax.ShapeDtypeStruct((length, LANES), BF16), jax.ShapeDtypeStruct((1, LANES), F32)],
        scratch_shapes=[pltpu.VMEM((1, LANES), F32)],
        compiler_params=_params(("arbitrary",)),
    )(dc, fraw, bf)


def _fox_scores(q_ref, k_ref, cq_ref, ck_ref, i, j, tq):
    scale = 1.0 / math.sqrt(FOX_HEAD_DIM)
    s = _dot(q_ref[...].astype(BF16), k_ref[...].astype(BF16), _NT) * scale + (cq_ref[0] - ck_ref[0])
    qpos = i * tq + _iota2(s.shape, 0)
    kpos = j * tq + _iota2(s.shape, 1)
    return jnp.where(kpos <= qpos, s, -jnp.inf)


def _fox_fwd(qkvg, c_col, c_row, n_heads, name):
    length = qkvg.shape[0]
    hd = FOX_HEAD_DIM
    tq = _pick(length, (512, 256, 128))
    nq = length // tq

    def body(q_ref, k_ref, v_ref, cq_ref, ck_ref, o_ref, lse_ref, m_ref, l_ref, acc_ref):
        i, j = pl.program_id(1), pl.program_id(2)

        @pl.when(j == 0)
        def _():
            m_ref[...] = jnp.full_like(m_ref, -jnp.inf)
            l_ref[...] = jnp.zeros_like(l_ref)
            acc_ref[...] = jnp.zeros_like(acc_ref)

        @pl.when(j <= i)
        def _():
            s = _fox_scores(q_ref, k_ref, cq_ref, ck_ref, i, j, tq)
            m_new = jnp.maximum(m_ref[...], jnp.max(s, axis=1, keepdims=True))
            alpha = jnp.exp(m_ref[...] - m_new)
            p = jnp.exp(s - m_new)
            l_ref[...] = alpha * l_ref[...] + jnp.sum(p, axis=1, keepdims=True)
            acc_ref[...] = alpha * acc_ref[...] + _dot(p.astype(BF16), v_ref[...].astype(BF16))
            m_ref[...] = m_new

        @pl.when(j == nq - 1)
        def _():
            o_ref[...] = acc_ref[...] / l_ref[...]
            lse_ref[0] = m_ref[...] + jnp.log(l_ref[...])

    kmap = lambda off: (lambda h, i, j: (jnp.minimum(j, i), off * n_heads + h))
    return pl.pallas_call(
        body, name=name, grid=(n_heads, nq, nq),
        in_specs=[pl.BlockSpec((tq, hd), lambda h, i, j: (i, h)),
                  pl.BlockSpec((tq, hd), kmap(1)),
                  pl.BlockSpec((tq, hd), kmap(2)),
                  pl.BlockSpec((1, tq, 1), lambda h, i, j: (h, i, 0)),
                  pl.BlockSpec((1, 1, tq), lambda h, i, j: (h, 0, jnp.minimum(j, i)))],
        out_specs=[pl.BlockSpec((tq, hd), lambda h, i, j: (i, h)),
                   pl.BlockSpec((1, tq, 1), lambda h, i, j: (h, i, 0))],
        out_shape=[jax.ShapeDtypeStruct((length, n_heads * hd), F32),
                   jax.ShapeDtypeStruct((n_heads, length, 1), F32)],
        scratch_shapes=[pltpu.VMEM((tq, 1), F32), pltpu.VMEM((tq, 1), F32), pltpu.VMEM((tq, hd), F32)],
        compiler_params=_params(("parallel", "parallel", "arbitrary")),
    )(qkvg, qkvg, qkvg, c_col, c_row)


def _fox_bwd_q(qkvg, datt, lse, c_col, c_row, n_heads, name):
    length = qkvg.shape[0]
    hd = FOX_HEAD_DIM
    tq = _pick(length, (512, 256, 128))
    nq = length // tq
    scale = 1.0 / math.sqrt(hd)

    def body(q_ref, k_ref, v_ref, do_ref, lse_ref, cq_ref, ck_ref, dq_ref, dsum_ref, acc_ref, d_ref):
        i, jj = pl.program_id(1), pl.program_id(2)
        j = lax.rem(jj, nq)

        @pl.when(jj == 0)
        def _():
            acc_ref[...] = jnp.zeros_like(acc_ref)
            d_ref[...] = jnp.zeros_like(d_ref)

        def p_dp():
            s = _fox_scores(q_ref, k_ref, cq_ref, ck_ref, i, j, tq)
            p = jnp.exp(s - lse_ref[0])
            return p, _dot(do_ref[...].astype(BF16), v_ref[...].astype(BF16), _NT)

        @pl.when(jnp.logical_and(j <= i, jj < nq))
        def _():
            p, dp = p_dp()
            d_ref[...] += jnp.sum(p * dp, axis=1, keepdims=True)

        @pl.when(jnp.logical_and(j <= i, jj >= nq))
        def _():
            p, dp = p_dp()
            ds = p * (dp - d_ref[...])
            acc_ref[...] += _dot(ds.astype(BF16), k_ref[...].astype(BF16))

        @pl.when(jj == 2 * nq - 1)
        def _():
            dq_ref[...] = (acc_ref[...] * scale).astype(dq_ref.dtype)
            dsum_ref[0] = d_ref[...]

    kblk = lambda i, jj: jnp.minimum(lax.rem(jj, nq), i)
    kmap = lambda off: (lambda h, i, jj: (kblk(i, jj), off * n_heads + h))
    qmap = lambda h, i, jj: (i, h)
    col = pl.BlockSpec((1, tq, 1), lambda h, i, jj: (h, i, 0))
    return pl.pallas_call(
        body, name=name, grid=(n_heads, nq, 2 * nq),
        in_specs=[pl.BlockSpec((tq, hd), qmap), pl.BlockSpec((tq, hd), kmap(1)), pl.BlockSpec((tq, hd), kmap(2)),
                  pl.BlockSpec((tq, hd), qmap), col, col,
                  pl.BlockSpec((1, 1, tq), lambda h, i, jj: (h, 0, kblk(i, jj)))],
        out_specs=[pl.BlockSpec((tq, hd), qmap), col],
        out_shape=[jax.ShapeDtypeStruct((length, n_heads * hd), BF16),
                   jax.ShapeDtypeStruct((n_heads, length, 1), F32)],
        scratch_shapes=[pltpu.VMEM((tq, hd), F32), pltpu.VMEM((tq, 1), F32)],
        compiler_params=_params(("parallel", "parallel", "arbitrary")),
    )(qkvg, qkvg, qkvg, datt, lse, c_col, c_row)


def _fox_bwd_kv(qkvg, datt, lse, dsum, c_col, c_row, n_heads, name):
    length = qkvg.shape[0]
    hd = FOX_HEAD_DIM
    tq = _pick(length, (512, 256, 128))
    nq = length // tq
    scale = 1.0 / math.sqrt(hd)

    def body(q_ref, k_ref, v_ref, do_ref, lse_ref, dsum_ref, cq_ref, ck_ref, dk_ref, dv_ref, dck_ref,
             dk_acc, dv_acc, dc_acc):
        j, i = pl.program_id(1), pl.program_id(2)

        @pl.when(i == 0)
        def _():
            dk_acc[...] = jnp.zeros_like(dk_acc)
            dv_acc[...] = jnp.zeros_like(dv_acc)
            dc_acc[...] = jnp.zeros_like(dc_acc)

        @pl.when(i >= j)
        def _():
            s = _fox_scores(q_ref, k_ref, cq_ref, ck_ref, i, j, tq)
            p = jnp.exp(s - lse_ref[0])
            do_b = do_ref[...].astype(BF16)
            dv_acc[...] += _dot(p.astype(BF16), do_b, _TN)
            dp = _dot(do_b, v_ref[...].astype(BF16), _NT)
            ds = p * (dp - dsum_ref[0])
            dk_acc[...] += _dot(ds.astype(BF16), q_ref[...].astype(BF16), _TN)
            dc_acc[...] -= jnp.sum(ds, axis=0, keepdims=True)

        @pl.when(i == nq - 1)
        def _():
            dk_ref[...] = (dk_acc[...] * scale).astype(dk_ref.dtype)
            dv_ref[...] = dv_acc[...].astype(dv_ref.dtype)
            dck_ref[0] = dc_acc[...]

    qmap = lambda h, j, i: (jnp.maximum(i, j), h)
    kmap = lambda off: (lambda h, j, i: (j, off * n_heads + h))
    col = pl.BlockSpec((1, tq, 1), lambda h, j, i: (h, jnp.maximum(i, j), 0))
    return pl.pallas_call(
        body, name=name, grid=(n_heads, nq, nq),
        in_specs=[pl.BlockSpec((tq, hd), qmap), pl.BlockSpec((tq, hd), kmap(1)), pl.BlockSpec((tq, hd), kmap(2)),
                  pl.BlockSpec((tq, hd), qmap), col, col, col,
                  pl.BlockSpec((1, 1, tq), lambda h, j, i: (h, 0, j))],
        out_specs=[pl.BlockSpec((tq, hd), lambda h, j, i: (j, h)), pl.BlockSpec((tq, hd), lambda h, j, i: (j, h)),
                   pl.BlockSpec((1, 1, tq), lambda h, j, i: (h, 0, j))],
        out_shape=[jax.ShapeDtypeStruct((length, n_heads * hd), BF16)] * 2
        + [jax.ShapeDtypeStruct((n_heads, 1, length), F32)],
        scratch_shapes=[pltpu.VMEM((tq, hd), F32), pltpu.VMEM((tq, hd), F32), pltpu.VMEM((1, tq), F32)],
        compiler_params=_params(("parallel", "parallel", "arbitrary")),
    )(qkvg, qkvg, qkvg, datt, lse, dsum, c_col, c_row)


def _row(v):
    return v.reshape(1, -1).astype(F32)


def _pad_heads(v, per_group):
    lead = v.shape[:-1]
    v = v.reshape(lead + (SSD_GROUPS, per_group))
    v = jnp.pad(v, [(0, 0)] * len(lead) + [(0, 0), (0, LANES - per_group)])
    return v.reshape(lead + (SSD_GROUPS * LANES,))


def _unpad_heads(v, per_group):
    lead = v.shape[:-1]
    return v.reshape(lead + (SSD_GROUPS, LANES))[..., :per_group].reshape(lead + (SSD_GROUPS * per_group,))


def _local_step(x, tgt, wb, sm):
    length, d = x.shape
    s5w = wb["w0_ug"].shape[1] // 2
    ssdw = wb["w0_z"].shape[1]
    xbcw = wb["w0_xbc"].shape[1]
    n_ssd = ssdw // SSD_HEAD_DIM
    hpg = n_ssd // SSD_GROUPS
    fw = wb["w1_out"].shape[0]
    n_fox = fw // FOX_HEAD_DIM
    s5g = s5w // S5_GROUP
    s5s = s5g * S5_STATE
    grads = {}

    s5_in = (sm["l0_s5_lambda_re"], sm["l0_s5_lambda_im"], sm["l0_s5_log_step"], sm["l0_s5_b_re"],
             sm["l0_s5_b_im"], sm["l0_s5_c_re"], sm["l0_s5_c_im"])
    (a_re, a_im, bd_re, bd_im, cd_re, cd_imn), s5_vjp = jax.vjp(_s5_prepare, *s5_in)
    nj = bd_re.shape[0]
    bd_re_b, bd_im_b, cd_re_b, cd_imn_b = (t.astype(BF16) for t in (bd_re, bd_im, cd_re, cd_imn))
    tr = lambda t: jnp.swapaxes(t, 1, 2)
    dvec = _row(sm["l0_s5_d"])
    bglu = _row(sm["l0_s5_b_glu"])
    conv_w = sm["l0_ssd_conv_w"]
    conv_b = _row(sm["l0_ssd_conv_b"])

    def ssd_prepare(dt_bias, a_log, dd):
        return (_pad_heads(_row(dt_bias), hpg), _pad_heads(_row(-jnp.exp(a_log)), hpg),
                jnp.repeat(_row(dd), SSD_HEAD_DIM, axis=1))

    (dtb, ap, dpar), ssd_vjp = jax.vjp(ssd_prepare, sm["l0_ssd_dt_bias"], sm["l0_ssd_a_log"], sm["l0_ssd_d"])
    ssd_nw = _row(sm["l0_ssd_norm_w"])
    nw0, nw1, fnw = _row(sm["l0_norm_w"]), _row(sm["l1_norm_w"]), _row(sm["final_norm_w"])
    bf = jnp.pad(_row(sm["l1_fox_b_f"]), ((0, 0), (0, LANES - n_fox)))

    h0 = _norm_fwd(x, nw0, "l0_norm")
    ug = _matmul(h0, wb["w0_ug"], name="l0_in_ug")
    z = _matmul(h0, wb["w0_z"], name="l0_in_z")
    xbc = _matmul(h0, wb["w0_xbc"], name="l0_in_xbc")
    dtraw = _matmul(h0, wb["w0_dt"], name="l0_in_dt")
    u_win, gate_win = (ug, s5w, 0), (ug, s5w, 1)

    shape3 = (length, s5s // LANES, LANES)
    bu_re = _bdmm([(ug, 0)], [bd_re_b], name="s5_bu_re").reshape(shape3)
    bu_im = _bdmm([(ug, 0)], [bd_im_b], name="s5_bu_im").reshape(shape3)
    s_re3, s_im3 = _s5_scan_fwd(bu_re, bu_im, a_re, a_im, "s5_scan")
    s_re, s_im = s_re3.reshape(length, s5s), s_im3.reshape(length, s5s)
    yc = _bdmm([s_re, s_im], [cd_re_b, cd_imn_b], name="s5_y")
    gl = _rowwise(_f_s5_gelu, [yc, u_win], [dvec], [(s5w, BF16)], tl=256, name="s5_gelu")[0]
    t_glu = _matmul(gl, wb["w_glu"], name="s5_glu")
    s5o = _rowwise(_f_s5_out, [yc, u_win, t_glu, gate_win], [dvec, bglu], [(s5w, BF16)], tl=256,
                   name="s5_out")[0]

    xc = _conv_fwd(xbc, conv_w, conv_b, "ssd_conv")
    y_ssd, states = _ssd_fwd(xc, dtraw, dtb, ap, n_ssd, "ssd_scan")
    wg = ssdw // SSD_GROUPS
    ssdo = _rowwise(_f_ssd_out, [y_ssd, (xc, wg, 0), z], [dpar, ssd_nw], [(wg, BF16)], tl=256,
                    ncol=SSD_GROUPS, name="ssd_out")[0]
    x1 = _matmul(s5o, wb["w0_out"], dims=(length, d, s5w), addend=x, name="l0_out_s5")
    x1 = _matmul(ssdo, wb["w0_out"], dims=(length, d, ssdw), b_off=(s5w, 0), addend=x1, name="l0_out_ssd")

    h1 = _norm_fwd(x1, nw1, "l1_norm")
    qkvg = _matmul(h1, wb["w1_main"], name="l1_in")
    fraw = _matmul(h1, wb["w1_f"], name="l1_in_f")
    cc = _fox_cumsum(fraw, bf, "fox_cumsum")
    c_t = cc[:, :n_fox].T
    c_col, c_row = c_t[:, :, None], c_t[:, None, :]
    att, lse = _fox_fwd(qkvg, c_col, c_row, n_fox, "fox_fwd")
    gate1_win = (qkvg, fw, 3)
    o1 = _rowwise(_f_fox_out, [att, gate1_win], [], [(fw, BF16)], tl=256, name="fox_out")[0]
    x2 = _matmul(o1, wb["w1_out"], addend=x1, name="l1_out")

    def loss_fn(xt, tt, wt):
        def f(xx, ww):
            err = _f_norm(xx, ww) - tt
            return (0.5 / d) * err * err
        lanes, vjp = jax.vjp(f, xt, wt)
        dx, dw = vjp(jnp.ones_like(lanes))
        return dx, dx, jnp.sum(lanes, axis=0, keepdims=True), dw

    dx2, dx2b, loss_lanes, g_fnw = _rowwise(loss_fn, [x2, tgt], [fnw], [(d, F32), (d, BF16)], [d, d],
                                            tl=128, name="loss_head")
    grads["final_norm_w"] = g_fnw

    grads["l1_w_out"] = _matmul(o1, dx2b, mode="tn", name="l1_out_dw")
    do1 = _matmul(dx2b, wb["w1_out"], mode="nt", name="l1_out_dx")

    def fox_out_bwd(at, gt, dt_):
        _, vjp = jax.vjp(_f_fox_out, at, gt)
        return vjp(dt_)

    datt, dgate1 = _rowwise(fox_out_bwd, [att, gate1_win, do1], [], [(fw, F32), (fw, BF16)], tl=256,
                            name="fox_out_bwd")
    dq, dsum = _fox_bwd_q(qkvg, datt, lse, c_col, c_row, n_fox, "fox_bwd_q")
    dk, dv, dck = _fox_bwd_kv(qkvg, datt, lse, dsum, c_col, c_row, n_fox, "fox_bwd_kv")
    dcc = jnp.pad(dck[:, 0, :].T, ((0, 0), (0, LANES - n_fox)))
    dfraw, g_bf = _fox_cumsum_bwd(dcc, fraw, bf, "fox_cumsum_bwd")
    grads["l1_fox_b_f"] = g_bf[:, :n_fox]
    dsegs = [dq, dk, dv, dgate1]
    grads["l1_w_in"] = jnp.concatenate(
        [_matmul(h1, s, mode="tn", name=f"l1_in_dw{i}") for i, s in enumerate(dsegs)]
        + [_matmul(h1, dfraw, mode="tn", name="l1_in_dwf")[:, :n_fox]], axis=1)
    dh1 = _matmul(dfraw, wb["w1_f"], mode="nt", name="l1_in_dxf")
    for i, s in enumerate(dsegs):
        dh1 = _matmul(s, wb["w1_main"], mode="nt", dims=(length, d, fw), b_off=(0, i * fw), addend=dh1,
                      name=f"l1_in_dx{i}")
    dx1, dx1b, grads["l1_norm_w"] = _norm_bwd(x1, dh1, dx2, nw1, "l1_norm_bwd")

    grads["l0_w_out"] = jnp.concatenate([_matmul(s5o, dx1b, mode="tn", name="l0_out_dw_s5"),
                                         _matmul(ssdo, dx1b, mode="tn", name="l0_out_dw_ssd")], axis=0)
    ds5o = _matmul(dx1b, wb["w0_out"], mode="nt", dims=(length, s5w, d), name="l0_out_dx_s5")
    dssdo = _matmul(dx1b, wb["w0_out"], mode="nt", dims=(length, ssdw, d), b_off=(s5w, 0), name="l0_out_dx_ssd")

    def ssd_out_bwd(yt, xt, zt, dt_, dp, nw):
        _, vjp = jax.vjp(_f_ssd_out, yt, xt, zt, dp, nw)
        return vjp(dt_)

    dy_ssd, dxa, dz, g_dpar, g_ssd_nw = _rowwise(
        ssd_out_bwd, [y_ssd, (xc, wg, 0), z, dssdo], [dpar, ssd_nw],
        [(wg, F32), (wg, F32), (wg, BF16)], [wg, wg], tl=128, ncol=SSD_GROUPS, name="ssd_out_bwd")
    grads["l0_ssd_norm_w"] = g_ssd_nw
    dxs, db_ssd, dc_ssd, ddtraw, g_dtb, g_ap = _ssd_bwd(dy_ssd, dxa, xc, dtraw, states, dtb, ap, n_ssd,
                                                        "ssd_scan_bwd")
    g_dt_bias, g_a_log, g_ssd_d = ssd_vjp((g_dtb, g_ap, g_dpar))
    grads["l0_ssd_dt_bias"], grads["l0_ssd_a_log"], grads["l0_ssd_d"] = g_dt_bias, g_a_log, g_ssd_d
    dxc = jnp.concatenate([dxs, db_ssd, dc_ssd], axis=1)
    dpre, grads["l0_ssd_conv_w"], grads["l0_ssd_conv_b"] = _conv_bwd_pre(dxc, xbc, conv_w, conv_b, "ssd_conv_bwd_pre")
    dxbc = _conv_bwd_in(dpre, conv_w, "ssd_conv_bwd_in")

    def s5_out_bwd(yt, ut, tt, gt, dt_, dv_, bg):
        _, vjp = jax.vjp(_f_s5_out, yt, ut, tt, gt, dv_, bg)
        return vjp(dt_)

    dyc_a, du_a, dt_glu, dgate, g_dvec_a, g_bglu = _rowwise(
        s5_out_bwd, [yc, u_win, t_glu, gate_win, ds5o], [dvec, bglu],
        [(s5w, F32), (s5w, F32), (s5w, BF16), (s5w, BF16)], [s5w, s5w], tl=128, name="s5_out_bwd")
    grads["l0_s5_b_glu"] = g_bglu
    grads["l0_s5_w_glu"] = _matmul(gl, dt_glu, mode="tn", name="s5_glu_dw")
    dgl = _matmul(dt_glu, wb["w_glu"], mode="nt", name="s5_glu_dx")

    def s5_gelu_bwd(yt, ut, dg, dya, dua, dv_):
        _, vjp = jax.vjp(_f_s5_gelu, yt, ut, dv_)
        dy_, du_, ddv = vjp(dg)
        return dy_ + dya, du_ + dua, ddv

    dyc, du_ab, g_dvec_b = _rowwise(s5_gelu_bwd, [yc, u_win, dgl, dyc_a, du_a], [dvec],
                                    [(s5w, F32), (s5w, F32)], [s5w], tl=128, name="s5_gelu_bwd")
    ds_re = _bdmm([dyc], [tr(cd_re_b)], name="s5_ds_re").reshape(shape3)
    ds_im = _bdmm([dyc], [tr(cd_imn_b)], name="s5_ds_im").reshape(shape3)
    kin_s, kin_u = s5s // nj, s5w // nj
    g_cd_re = _bdmm_tn_sized(s_re, dyc, nj, kin_s, kin_u, 0, "s5_dcd_re")
    g_cd_imn = _bdmm_tn_sized(s_im, dyc, nj, kin_s, kin_u, 0, "s5_dcd_im")
    g_re3, g_im3, g_a_re, g_a_im = _s5_scan_bwd(ds_re, ds_im, s_re3, s_im3, a_re, a_im, "s5_scan_bwd")
    g_re, g_im = g_re3.reshape(length, s5s), g_im3.reshape(length, s5s)
    du = _bdmm([g_re, g_im], [tr(bd_re_b), tr(bd_im_b)], addend=du_ab, out_dtype=BF16, name="s5_du")
    g_bd_re = _bdmm_tn_sized(ug, g_re, nj, kin_u, kin_s, 0, "s5_dbd_re")
    g_bd_im = _bdmm_tn_sized(ug, g_im, nj, kin_u, kin_s, 0, "s5_dbd_im")
    s5_g = s5_vjp((g_a_re, g_a_im, g_bd_re, g_bd_im, g_cd_re, g_cd_imn))
    for nm, g in zip(("lambda_re", "lambda_im", "log_step", "b_re", "b_im", "c_re", "c_im"), s5_g):
        grads["l0_s5_" + nm] = g
    grads["l0_s5_d"] = (g_dvec_a + g_dvec_b).reshape(sm["l0_s5_d"].shape)

    grads["l0_w_in"] = jnp.concatenate(
        [_matmul(h0, du, mode="tn", name="l0_in_dw_u"), _matmul(h0, dgate, mode="tn", name="l0_in_dw_g"),
         _matmul(h0, dz, mode="tn", name="l0_in_dw_z"), _matmul(h0, dxbc, mode="tn", name="l0_in_dw_xbc"),
         _unpad_heads(_matmul(h0, ddtraw, mode="tn", name="l0_in_dw_dt"), hpg)], axis=1)
    dh0 = _matmul(du, wb["w0_ug"], mode="nt", dims=(length, d, s5w), name="l0_in_dx_u")
    dh0 = _matmul(dgate, wb["w0_ug"], mode="nt", dims=(length, d, s5w), b_off=(0, s5w), addend=dh0,
                  name="l0_in_dx_g")
    dh0 = _matmul(dz, wb["w0_z"], mode="nt", addend=dh0, name="l0_in_dx_z")
    dh0 = _matmul(dxbc, wb["w0_xbc"], mode="nt", addend=dh0, name="l0_in_dx_xbc")
    dh0 = _matmul(ddtraw, wb["w0_dt"], mode="nt", addend=dh0, name="l0_in_dx_dt")
    dx, _, grads["l0_norm_w"] = _norm_bwd(x, dh0, dx1, nw0, "l0_norm_bwd")
    return loss_lanes, dx, grads


_ANY = pl.BlockSpec(memory_space=pl.ANY)


def _place():
    x, y, c = lax.axis_index("x"), lax.axis_index("y"), lax.axis_index("c")
    return x, y, c, [(1 - x, y), (x, 1 - y), (1 - x, 1 - y)]


def _remote(src, dst, send_sem, recv_sem, to):
    return pltpu.make_async_remote_copy(src_ref=src, dst_ref=dst, send_sem=send_sem, recv_sem=recv_sem,
                                        device_id=to, device_id_type=MESH)


def _comm_call(body, n_in, out_shape, n_sems, name):
    return pl.pallas_call(
        body, name=name, in_specs=[_ANY] * n_in, out_specs=[_ANY] * len(out_shape), out_shape=out_shape,
        scratch_shapes=[pltpu.SemaphoreType.DMA((k,)) for k in n_sems],
        compiler_params=pltpu.CompilerParams(has_side_effects=True),
    )


def _gather_weights(shards, name):
    n = len(shards)

    def body(*refs):
        ins, outs = refs[:n], refs[n:2 * n]
        send, recv, fsend, frecv, lsem = refs[2 * n:]
        x, y, c, chips = _place()
        me = 2 * x + y
        local = [pltpu.make_async_copy(ins[a], outs[a].at[me], lsem.at[a]) for a in range(n)]
        for cp in local:
            cp.start()
        first, passed = [], []
        for a in range(n):
            for k, (px, py) in enumerate(chips):
                cp = _remote(ins[a].at[c], outs[a].at[me, c], send.at[3 * a + k], recv.at[3 * a + k], (px, py, c))
                cp.start()
                first.append(cp)
        for a in range(n):
            for k, (px, py) in enumerate(chips):
                got = outs[a].at[2 * px + py, c]
                _remote(got, got, send.at[3 * a + k], recv.at[3 * a + k], (px, py, c)).wait_recv()
                cp = _remote(got, got, fsend.at[3 * a + k], frecv.at[3 * a + k], (x, y, 1 - c))
                cp.start()
                passed.append(cp)
        for a in range(n):
            for k, (px, py) in enumerate(chips):
                got = outs[a].at[2 * px + py, 1 - c]
                _remote(got, got, fsend.at[3 * a + k], frecv.at[3 * a + k], (x, y, 1 - c)).wait_recv()
        for cp in first + passed:
            cp.wait_send()
        for cp in local:
            cp.wait()

    out_shape = [jax.ShapeDtypeStruct((N_SHARD,) + s.shape, s.dtype) for s in shards]
    return _comm_call(body, n, out_shape, [3 * n, 3 * n, 3 * n, 3 * n, n], name)(*shards)


def _sibling_halves(grads, name):
    n = len(grads)

    def body(*refs):
        ins, outs = refs[:n], refs[n:2 * n]
        send, recv = refs[2 * n:]
        x, y, c, _ = _place()
        copies = []
        for a in range(n):
            for j in range(N_SHARD):
                cp = _remote(ins[a].at[j, 1 - c], outs[a].at[j], send.at[N_SHARD * a + j],
                             recv.at[N_SHARD * a + j], (x, y, 1 - c))
                cp.start()
                copies.append(cp)
        for cp in copies:
            cp.wait()

    out_shape = [jax.ShapeDtypeStruct((N_SHARD,) + g.shape[2:], g.dtype) for g in grads]
    return _comm_call(body, n, out_shape, [N_SHARD * n, N_SHARD * n], name)(*grads)


def _chip_partials(parts, name):
    n = len(parts)

    def body(*refs):
        ins, outs = refs[:n], refs[n:2 * n]
        send, recv = refs[2 * n:]
        x, y, c, chips = _place()
        copies = []
        for a in range(n):
            for k, (px, py) in enumerate(chips):
                cp = _remote(ins[a].at[2 * px + py], outs[a].at[k], send.at[3 * a + k], recv.at[3 * a + k],
                             (px, py, c))
                cp.start()
                copies.append(cp)
        for cp in copies:
            cp.wait()

    out_shape = [jax.ShapeDtypeStruct((3,) + p.shape[1:], p.dtype) for p in parts]
    return _comm_call(body, n, out_shape, [3 * n, 3 * n], name)(*parts)


def _join_halves(halves, name):
    n = len(halves)

    def body(*refs):
        ins, outs = refs[:n], refs[n:2 * n]
        send, recv, lsem = refs[2 * n:]
        x, y, c, _ = _place()
        local = [pltpu.make_async_copy(ins[a], outs[a].at[c], lsem.at[a]) for a in range(n)]
        copies = [_remote(ins[a], outs[a].at[c], send.at[a], recv.at[a], (x, y, 1 - c)) for a in range(n)]
        for cp in local + copies:
            cp.start()
        for a in range(n):
            copies[a].wait_send()
            got = outs[a].at[1 - c]
            _remote(got, got, send.at[a], recv.at[a], (x, y, 1 - c)).wait_recv()
            local[a].wait()

    out_shape = [jax.ShapeDtypeStruct((2,) + h.shape, h.dtype) for h in halves]
    return _comm_call(body, n, out_shape, [n, n, n], name)(*halves)


def _gather_all(buf, name):
    def body(in_ref, out_ref, send, recv, lsem):
        x, y, c, _ = _place()
        me = 4 * x + 2 * y + c
        local = pltpu.make_async_copy(in_ref, out_ref.at[me], lsem.at[0])
        local.start()
        copies = []
        for k in range(1, N_DEV):
            fx, fy, fc = (k >> 2) & 1, (k >> 1) & 1, k & 1
            peer = (x + fx - 2 * x * fx, y + fy - 2 * y * fy, c + fc - 2 * c * fc)
            cp = _remote(in_ref, out_ref.at[me], send.at[k - 1], recv.at[k - 1], peer)
            cp.start()
            copies.append((cp, 4 * peer[0] + 2 * peer[1] + peer[2]))
        for k, (cp, slot) in enumerate(copies):
            cp.wait_send()
            got = out_ref.at[slot]
            _remote(got, got, send.at[k], recv.at[k], (x, y, c)).wait_recv()
        local.wait()

    out_shape = [jax.ShapeDtypeStruct((N_DEV,) + buf.shape, buf.dtype)]
    return _comm_call(body, 1, out_shape, [N_DEV - 1, N_DEV - 1, 1], name)(buf)[0]


def _sum_slots(buf, name):
    slots, rows, _ = buf.shape
    tr = _pick(rows, (512, 256, 128, 64, 32, 16, 8))

    def body(b_ref, o_ref):
        acc = b_ref[0]
        for s in range(1, slots):
            acc = acc + b_ref[s]
        o_ref[...] = acc

    return pl.pallas_call(
        body, name=name, grid=(rows // tr,),
        in_specs=[pl.BlockSpec((slots, tr, LANES), lambda i: (0, i, 0))],
        out_specs=pl.BlockSpec((tr, LANES), lambda i: (i, 0)),
        out_shape=jax.ShapeDtypeStruct((rows, LANES), F32),
        compiler_params=_params(("parallel",)),
    )(buf)


def _row_tile(cols, n_bufs):
    return max(8, min(512, (24 * 1024 * 1024 // (4 * cols * n_bufs)) // 8 * 8))


def _presum(grad, sib, place, name):
    ns, _, rh, cols = grad.shape
    tr = _pick(rh, [t for t in (512, 256, 128, 64, 32, 16) if t <= _row_tile(cols, 6)])

    def body(s_ref, g_ref, r_ref, o_ref):
        o_ref[0] = (g_ref[0, 0] + r_ref[0]).astype(o_ref.dtype)

    return pl.pallas_call(
        body, name=name,
        grid_spec=pltpu.PrefetchScalarGridSpec(
            num_scalar_prefetch=1, grid=(ns, rh // tr),
            in_specs=[pl.BlockSpec((1, 1, tr, cols), lambda j, i, s: (j, s[0], i, 0)),
                      pl.BlockSpec((1, tr, cols), lambda j, i, s: (j, i, 0))],
            out_specs=pl.BlockSpec((1, tr, cols), lambda j, i, s: (j, i, 0))),
        out_shape=jax.ShapeDtypeStruct((ns, rh, cols), BF16),
        compiler_params=_params(("parallel", "parallel")),
    )(place, grad, sib)


def _finish_half(grad, sib, others, place, name):
    _, _, rh, cols = grad.shape
    tr = _pick(rh, [t for t in (512, 256, 128, 64, 32, 16) if t <= _row_tile(cols, 10)])

    def body(s_ref, g_ref, r_ref, q_ref, o_ref):
        acc = g_ref[0, 0] + r_ref[0]
        for k in range(3):
            acc = acc + q_ref[k].astype(F32)
        o_ref[...] = acc

    return pl.pallas_call(
        body, name=name,
        grid_spec=pltpu.PrefetchScalarGridSpec(
            num_scalar_prefetch=1, grid=(rh // tr,),
            in_specs=[pl.BlockSpec((1, 1, tr, cols), lambda i, s: (s[1], s[0], i, 0)),
                      pl.BlockSpec((1, tr, cols), lambda i, s: (s[1], i, 0)),
                      pl.BlockSpec((3, tr, cols), lambda i, s: (0, i, 0))],
            out_specs=pl.BlockSpec((tr, cols), lambda i, s: (i, 0))),
        out_shape=jax.ShapeDtypeStruct((rh, cols), F32),
        compiler_params=_params(("parallel",)),
    )(place, grad, sib, others)


def _cast_bf16(w, name):
    cols = w.shape[1]
    return _rowwise(lambda t: t, [w], [], [(cols, BF16)], tl=_row_tile(cols, 4), name=name)[0]


_WEIGHTS = ("l0_norm_w", "l0_w_in", "l0_s5_lambda_re", "l0_s5_lambda_im", "l0_s5_log_step", "l0_s5_b_re",
            "l0_s5_b_im", "l0_s5_c_re", "l0_s5_c_im", "l0_s5_d", "l0_s5_w_glu", "l0_s5_b_glu", "l0_ssd_conv_w",
            "l0_ssd_conv_b", "l0_ssd_dt_bias", "l0_ssd_a_log", "l0_ssd_d", "l0_ssd_norm_w", "l0_w_out",
            "l1_norm_w", "l1_w_in", "l1_fox_b_f", "l1_w_out", "final_norm_w")
_COL_SHARDED = ("l0_w_in", "l1_w_in")
_ROW_SHARDED = ("l0_s5_w_glu", "l0_w_out", "l1_w_out")
_BIG = ("l0_w_in", "l0_s5_w_glu", "l0_w_out", "l1_w_in", "l1_w_out")
_CONV = "l0_ssd_conv_w"
_SMALL = tuple(n for n in _WEIGHTS if n not in _BIG and n != _CONV)


def _pack(arrays):
    flat = jnp.concatenate([a.reshape(-1).astype(F32) for a in arrays])
    size = flat.shape[0]
    padded = -(-size // (8 * LANES)) * (8 * LANES)
    return jnp.pad(flat, (0, padded - size)).reshape(-1, LANES)


def _unpack(buf, like):
    flat = buf.reshape(-1)
    out, pos = [], 0
    for a in like:
        out.append(flat[pos:pos + a.size].reshape(a.shape))
        pos += a.size
    return out


def _step(p):
    x, tgt = p["x"][0], p["loss_target"][0]
    d = x.shape[1]
    mx, my, mc = lax.axis_index("x"), lax.axis_index("y"), lax.axis_index("c")
    chip = 2 * mx + my
    place = jnp.stack([mc, chip]).astype(jnp.int32)

    halves = lambda w: w.reshape((2, w.shape[0] // 2) + w.shape[1:])
    shards = [halves(_cast_bf16(p[n], "cast_" + n)) for n in _BIG] + [halves(p[_CONV])]
    gathered = dict(zip(_BIG + (_CONV,), _gather_weights(shards, "gather_weights")))

    def whole(n):
        g = gathered[n]
        rows, cols = 2 * g.shape[2], g.shape[3]
        if n in _ROW_SHARDED:
            return g.reshape(N_SHARD * rows, cols)
        return g.reshape(N_SHARD, rows, cols).transpose(1, 0, 2).reshape(rows, N_SHARD * cols)

    w0, w1 = whole("l0_w_in"), whole("l1_w_in")
    mix = 2 * d
    s5w = mix // 4
    ssdw = mix - s5w
    n_ssd = ssdw // SSD_HEAD_DIM
    xbcw = ssdw + 2 * SSD_GROUPS * SSD_STATE
    fw = p["l1_w_out"].shape[0] * N_SHARD
    n_fox = fw // FOX_HEAD_DIM
    o1, o2, o3 = 2 * s5w, 2 * s5w + ssdw, 2 * s5w + ssdw + xbcw
    wb = {
        "w0_ug": w0[:, :o1], "w0_z": w0[:, o1:o2], "w0_xbc": w0[:, o2:o3],
        "w0_dt": _pad_heads(w0[:, o3:], n_ssd // SSD_GROUPS),
        "w_glu": whole("l0_s5_w_glu"), "w0_out": whole("l0_w_out"),
        "w1_main": w1[:, :4 * fw], "w1_f": jnp.pad(w1[:, 4 * fw:], ((0, 0), (0, LANES - n_fox))),
        "w1_out": whole("l1_w_out"),
    }
    sm = {n: p[n] for n in _SMALL}
    sm[_CONV] = whole(_CONV)

    loss_lanes, dx, grads = _local_step(x, tgt, wb, sm)

    small_like = [p[n] for n in _SMALL] + [sm[_CONV], jnp.zeros((1,), F32)]
    small_sum = _sum_slots(_gather_all(_pack([grads[n] for n in _SMALL] + [grads[_CONV], jnp.sum(loss_lanes)]),
                                       "gather_small"), "sum_small")
    *small_grads, conv_grad, loss = _unpack(small_sum, small_like)
    taps, ccols = p[_CONV].shape
    conv_grad = lax.dynamic_slice(conv_grad, (0, chip * ccols), (taps, ccols))
    final = dict(zip(_SMALL, small_grads))
    final[_CONV] = conv_grad

    def by_shard(n):
        g = grads[n]
        if n in _ROW_SHARDED:
            return g.reshape(N_SHARD, 2, g.shape[0] // (2 * N_SHARD), g.shape[1])
        rows, cols = g.shape[0], g.shape[1] // N_SHARD
        return g.reshape(2, rows // 2, N_SHARD, cols).transpose(2, 0, 1, 3)

    big = [by_shard(n) for n in _BIG]
    sib = _sibling_halves(big, "reduce_sibling")
    parts = [_presum(g, s, place, "presum_" + n) for n, g, s in zip(_BIG, big, sib)]
    others = _chip_partials(parts, "reduce_chips")
    done = [_finish_half(g, s, q, place, "finish_" + n) for n, g, s, q in zip(_BIG, big, sib, others)]
    for n, full in zip(_BIG, _join_halves(done, "join_halves")):
        final[n] = full.reshape(p[n].shape)

    delta, new_m, new_v = {}, {}, {}
    for n in _BIG:
        delta[n], new_m[n], new_v[n] = _adamw(p[n], final[n], p["m_" + n], p["v_" + n], "adamw_" + n)
    rest = _SMALL + (_CONV,)
    packed = [_pack([t[n] for n in rest]) for t in
              ({n: p[n] for n in rest}, final, {n: p["m_" + n] for n in rest}, {n: p["v_" + n] for n in rest})]
    for dst, buf in zip((delta, new_m, new_v), _adamw(*packed, "adamw_small")):
        dst.update(zip(rest, _unpack(buf, [p[n] for n in rest])))

    outs = [loss.reshape(()), dx[None]]
    for group in (final, delta, new_m, new_v):
        outs += [group[n].reshape(p[n].shape) for n in _WEIGHTS]
    return tuple(outs)


_INPUTS = ("x",) + _WEIGHTS + ("loss_target",) + tuple("m_" + n for n in _WEIGHTS) + tuple("v_" + n for n in _WEIGHTS)


def kernel(x, l0_norm_w, l0_w_in, l0_s5_lambda_re, l0_s5_lambda_im, l0_s5_log_step, l0_s5_b_re, l0_s5_b_im, l0_s5_c_re,
           l0_s5_c_im, l0_s5_d, l0_s5_w_glu, l0_s5_b_glu, l0_ssd_conv_w, l0_ssd_conv_b, l0_ssd_dt_bias,
           l0_ssd_a_log, l0_ssd_d, l0_ssd_norm_w, l0_w_out, l1_norm_w, l1_w_in, l1_fox_b_f, l1_w_out,
           final_norm_w, loss_target, m_l0_norm_w, m_l0_w_in, m_l0_s5_lambda_re, m_l0_s5_lambda_im,
           m_l0_s5_log_step, m_l0_s5_b_re, m_l0_s5_b_im, m_l0_s5_c_re, m_l0_s5_c_im, m_l0_s5_d,
           m_l0_s5_w_glu, m_l0_s5_b_glu, m_l0_ssd_conv_w, m_l0_ssd_conv_b, m_l0_ssd_dt_bias, m_l0_ssd_a_log,
           m_l0_ssd_d, m_l0_ssd_norm_w, m_l0_w_out, m_l1_norm_w, m_l1_w_in, m_l1_fox_b_f, m_l1_w_out,
           m_final_norm_w, v_l0_norm_w, v_l0_w_in, v_l0_s5_lambda_re, v_l0_s5_lambda_im, v_l0_s5_log_step,
           v_l0_s5_b_re, v_l0_s5_b_im, v_l0_s5_c_re, v_l0_s5_c_im, v_l0_s5_d, v_l0_s5_w_glu, v_l0_s5_b_glu,
           v_l0_ssd_conv_w, v_l0_ssd_conv_b, v_l0_ssd_dt_bias, v_l0_ssd_a_log, v_l0_ssd_d, v_l0_ssd_norm_w,
           v_l0_w_out, v_l1_norm_w, v_l1_w_in, v_l1_fox_b_f, v_l1_w_out, v_final_norm_w):
    values = (x, l0_norm_w, l0_w_in, l0_s5_lambda_re, l0_s5_lambda_im, l0_s5_log_step, l0_s5_b_re, l0_s5_b_im,
              l0_s5_c_re, l0_s5_c_im, l0_s5_d, l0_s5_w_glu, l0_s5_b_glu, l0_ssd_conv_w, l0_ssd_conv_b,
              l0_ssd_dt_bias, l0_ssd_a_log, l0_ssd_d, l0_ssd_norm_w, l0_w_out, l1_norm_w, l1_w_in,
              l1_fox_b_f, l1_w_out, final_norm_w, loss_target, m_l0_norm_w, m_l0_w_in,
              m_l0_s5_lambda_re, m_l0_s5_lambda_im, m_l0_s5_log_step, m_l0_s5_b_re, m_l0_s5_b_im,
              m_l0_s5_c_re, m_l0_s5_c_im, m_l0_s5_d, m_l0_s5_w_glu, m_l0_s5_b_glu, m_l0_ssd_conv_w,
              m_l0_ssd_conv_b, m_l0_ssd_dt_bias, m_l0_ssd_a_log, m_l0_ssd_d, m_l0_ssd_norm_w,
              m_l0_w_out, m_l1_norm_w, m_l1_w_in, m_l1_fox_b_f, m_l1_w_out, m_final_norm_w, v_l0_norm_w,
              v_l0_w_in, v_l0_s5_lambda_re, v_l0_s5_lambda_im, v_l0_s5_log_step, v_l0_s5_b_re,
              v_l0_s5_b_im, v_l0_s5_c_re, v_l0_s5_c_im, v_l0_s5_d, v_l0_s5_w_glu, v_l0_s5_b_glu,
              v_l0_ssd_conv_w, v_l0_ssd_conv_b, v_l0_ssd_dt_bias, v_l0_ssd_a_log, v_l0_ssd_d,
              v_l0_ssd_norm_w, v_l0_w_out, v_l1_norm_w, v_l1_w_in, v_l1_fox_b_f, v_l1_w_out,
              v_final_norm_w)
    return _step(dict(zip(_INPUTS, values)))
```

```python
import functools
import math

import jax
import jax.numpy as jnp
from jax import lax
from jax.experimental import pallas as pl
from jax.experimental.pallas import tpu as pltpu

F32 = jnp.float32
BF16 = jnp.bfloat16

S5_GROUP = 16
S5_STATE = 64
S5_EIG_CLIP = -1e-4
SSD_HEAD_DIM = 64
SSD_GROUPS = 8
SSD_STATE = 128
SSD_CONV = 4
SSD_CHUNK = 128
FOX_HEAD_DIM = 128
NORM_EPS = 1e-5
ADAM_LR = 0.001
ADAM_B1 = 0.9
ADAM_B2 = 0.999
ADAM_EPS = 1e-08
ADAM_WD = 0.01
ADAM_STEP = 10

N_SHARD = 4
N_DEV = 8
LANES = 128
VMEM_LIMIT = 56 * 1024 * 1024
MESH = pl.DeviceIdType.MESH


def _pick(dim, prefs, offs=()):
    for p in prefs:
        if dim % p == 0 and all(o % p == 0 for o in offs):
            return p
    return dim


def _params(sem=None, vmem=VMEM_LIMIT):
    return pltpu.CompilerParams(dimension_semantics=sem, vmem_limit_bytes=vmem)


def _matmul(a, b, *, mode="nn", dims=None, a_off=(0, 0), b_off=(0, 0), addend=None,
            out_dtype=F32, name):
    if dims is None:
        if mode == "nn":
            dims = (a.shape[0], b.shape[1], a.shape[1])
        elif mode == "nt":
            dims = (a.shape[0], b.shape[0], a.shape[1])
        else:
            dims = (a.shape[1], b.shape[1], a.shape[0])
    m, n, k = dims
    if mode == "nn":
        om, on, ok = (a_off[0],), (b_off[1],), (a_off[1], b_off[0])
    elif mode == "nt":
        om, on, ok = (a_off[0],), (b_off[0],), (a_off[1], b_off[1])
    else:
        om, on, ok = (a_off[1],), (b_off[1],), (a_off[0], b_off[0])
    tm = _pick(m, (1024, 512, 256, 128), om)
    tn = _pick(n, (1024, 768, 512, 384, 256, 128), on)
    tk = _pick(k, (512, 256, 128), ok)
    nk = k // tk
    if mode == "nn":
        a_blk, a_div = (tm, tk), (tm, tk)
        b_blk, b_div = (tk, tn), (tk, tn)
        a_map = lambda i, j, kk: (i + a_off[0] // tm, kk + a_off[1] // tk)
        b_map = lambda i, j, kk: (kk + b_off[0] // tk, j + b_off[1] // tn)
        dn = (((1,), (0,)), ((), ()))
    elif mode == "nt":
        a_blk, a_div = (tm, tk), (tm, tk)
        b_blk, b_div = (tn, tk), (tn, tk)
        a_map = lambda i, j, kk: (i + a_off[0] // tm, kk + a_off[1] // tk)
        b_map = lambda i, j, kk: (j + b_off[0] // tn, kk + b_off[1] // tk)
        dn = (((1,), (1,)), ((), ()))
    else:
        a_blk, a_div = (tk, tm), (tk, tm)
        b_blk, b_div = (tk, tn), (tk, tn)
        a_map = lambda i, j, kk: (kk + a_off[0] // tk, i + a_off[1] // tm)
        b_map = lambda i, j, kk: (kk + b_off[0] // tk, j + b_off[1] // tn)
        dn = (((0,), (0,)), ((), ()))
    assert a_off[0] % a_div[0] == 0 and a_off[1] % a_div[1] == 0, (name, a_off, a_div)
    assert b_off[0] % b_div[0] == 0 and b_off[1] % b_div[1] == 0, (name, b_off, b_div)
    has_add = addend is not None

    def body(*refs):
        if has_add:
            a_ref, b_ref, c_ref, o_ref, acc_ref = refs
        else:
            a_ref, b_ref, o_ref, acc_ref = refs
        kk = pl.program_id(2)

        @pl.when(kk == 0)
        def _():
            acc_ref[...] = jnp.zeros_like(acc_ref)

        acc_ref[...] += lax.dot_general(a_ref[...].astype(BF16), b_ref[...].astype(BF16), dn,
                                        preferred_element_type=F32)

        @pl.when(kk == nk - 1)
        def _():
            r = acc_ref[...]
            if has_add:
                r = r + c_ref[...].astype(F32)
            o_ref[...] = r.astype(o_ref.dtype)

    in_specs = [pl.BlockSpec(a_blk, a_map), pl.BlockSpec(b_blk, b_map)]
    args = [a, b]
    if has_add:
        in_specs.append(pl.BlockSpec((tm, tn), lambda i, j, kk: (i, j)))
        args.append(addend)
    return pl.pallas_call(
        body, name=name, grid=(m // tm, n // tn, nk),
        in_specs=in_specs, out_specs=pl.BlockSpec((tm, tn), lambda i, j, kk: (i, j)),
        out_shape=jax.ShapeDtypeStruct((m, n), out_dtype),
        scratch_shapes=[pltpu.VMEM((tm, tn), F32)],
        compiler_params=_params(("parallel", "parallel", "arbitrary")),
    )(*args)


def _rowwise(fn, rows, params, out_rows, out_accs=(), *, tl, ncol=1, name):
    rows = [r if isinstance(r, tuple) else (r, r.shape[1] // ncol, 0) for r in rows]
    n_rows, n_par, n_or, n_oa = len(rows), len(params), len(out_rows), len(out_accs)
    length = rows[0][0].shape[0]
    tl = _pick(length, [t for t in (1024, 512, 256, 128, 64, 32, 16, 8) if t <= tl])

    def body(*refs):
        row_refs = refs[:n_rows]
        par_refs = refs[n_rows:n_rows + n_par]
        or_refs = refs[n_rows + n_par:n_rows + n_par + n_or]
        oa_refs = refs[n_rows + n_par + n_or:]
        outs = fn(*[r[...] for r in row_refs], *[p[...] for p in par_refs])
        if not isinstance(outs, (tuple, list)):
            outs = (outs,)
        for r, v in zip(or_refs, outs[:n_or]):
            r[...] = v.astype(r.dtype)
        if n_oa:
            @pl.when(pl.program_id(1) == 0)
            def _():
                for r in oa_refs:
                    r[...] = jnp.zeros_like(r)

            for r, v in zip(oa_refs, outs[n_or:]):
                r[...] += v.astype(F32)

    in_specs = [pl.BlockSpec((tl, w), functools.partial(lambda j, i, b0: (i, b0 + j), b0=b0))
                for (_, w, b0) in rows]
    in_specs += [pl.BlockSpec((p.shape[0], p.shape[1] // ncol), lambda j, i: (0, j)) for p in params]
    out_specs = [pl.BlockSpec((tl, w), lambda j, i: (i, j)) for (w, _) in out_rows]
    out_specs += [pl.BlockSpec((1, w), lambda j, i: (0, j)) for w in out_accs]
    out_shape = [jax.ShapeDtypeStruct((length, ncol * w), dt) for (w, dt) in out_rows]
    out_shape += [jax.ShapeDtypeStruct((1, ncol * w), F32) for w in out_accs]
    res = pl.pallas_call(
        body, name=name, grid=(ncol, length // tl),
        in_specs=in_specs, out_specs=out_specs, out_shape=out_shape,
        compiler_params=_params(("parallel", "arbitrary" if n_oa else "parallel")),
    )(*[r[0] for r in rows], *params)
    return res


def _bdmm(xs, ws, *, addend=None, out_dtype=F32, name):
    nj, kin, kout = ws[0].shape
    xs = [x if isinstance(x, tuple) else (x, 0) for x in xs]
    length = xs[0][0].shape[0]
    tl = _pick(length, (512, 256, 128))
    n_x = len(xs)
    has_add = addend is not None

    def body(*refs):
        x_refs = refs[:n_x]
        w_refs = refs[n_x:2 * n_x]
        o_ref = refs[-1]
        acc = None
        for xr, wr in zip(x_refs, w_refs):
            t = jnp.dot(xr[...].astype(BF16), wr[0], preferred_element_type=F32)
            acc = t if acc is None else acc + t
        if has_add:
            acc = acc + refs[2 * n_x][...].astype(F32)
        o_ref[...] = acc.astype(o_ref.dtype)

    in_specs = [pl.BlockSpec((tl, kin), functools.partial(lambda i, j, b0: (i, b0 + j), b0=b0)) for (_, b0) in xs]
    in_specs += [pl.BlockSpec((1, kin, kout), lambda i, j: (j, 0, 0)) for _ in ws]
    args = [x[0] for x in xs] + list(ws)
    if has_add:
        in_specs.append(pl.BlockSpec((tl, kout), lambda i, j: (i, j)))
        args.append(addend)
    return pl.pallas_call(
        body, name=name, grid=(length // tl, nj),
        in_specs=in_specs, out_specs=pl.BlockSpec((tl, kout), lambda i, j: (i, j)),
        out_shape=jax.ShapeDtypeStruct((length, nj * kout), out_dtype),
        compiler_params=_params(("parallel", "parallel")),
    )(*args)


def _bdmm_tn_sized(x, g, nj, kin, kout, x_first, name):
    length = x.shape[0]
    tl = _pick(length, (512, 256, 128))
    nt = length // tl

    def body(x_ref, g_ref, o_ref):
        @pl.when(pl.program_id(1) == 0)
        def _():
            o_ref[...] = jnp.zeros_like(o_ref)

        o_ref[0] += lax.dot_general(x_ref[...].astype(BF16), g_ref[...].astype(BF16),
                                    (((0,), (0,)), ((), ())), preferred_element_type=F32)

    return pl.pallas_call(
        body, name=name, grid=(nj, nt),
        in_specs=[pl.BlockSpec((tl, kin), lambda j, t: (t, x_first + j)),
                  pl.BlockSpec((tl, kout), lambda j, t: (t, j))],
        out_specs=pl.BlockSpec((1, kin, kout), lambda j, t: (j, 0, 0)),
        out_shape=jax.ShapeDtypeStruct((nj, kin, kout), F32),
        compiler_params=_params(("parallel", "arbitrary")),
    )(x, g)


def _f_norm(x, w):
    return x * lax.rsqrt(jnp.mean(x * x, axis=-1, keepdims=True) + NORM_EPS) * w


def _gelu(y):
    return 0.5 * y * (1.0 + jnp.tanh(math.sqrt(2.0 / math.pi) * (y + 0.044715 * (y * y * y))))


def _sigmoid(x):
    return 1.0 / (1.0 + jnp.exp(-x))


def _silu(x):
    return x * _sigmoid(x)


def _softplus(x):
    return jnp.maximum(x, 0.0) + jnp.log(1.0 + jnp.exp(-jnp.abs(x)))


def _f_s5_gelu(yc, u, dvec):
    return _gelu(yc + dvec * u)


def _f_s5_out(yc, u, t, gate, dvec, bglu):
    gl = _gelu(yc + dvec * u)
    return gl * _sigmoid(t + bglu) * _silu(gate)


def _f_ssd_out(y, xs, z, dpar, nw):
    v = (y + dpar * xs) * _silu(z)
    return v * lax.rsqrt(jnp.mean(v * v, axis=-1, keepdims=True) + NORM_EPS) * nw


def _f_fox_out(att, gate):
    return att * _silu(gate)


def _norm_fwd(x, w, name):
    return _rowwise(lambda xt, wt: _f_norm(xt, wt), [x], [w], [(x.shape[1], BF16)], tl=256, name=name)[0]


def _norm_bwd(x, dh, dres, w, name):
    d = x.shape[1]

    def fn(xt, dht, drt, wt):
        _, vjp = jax.vjp(_f_norm, xt, wt)
        dx, dw = vjp(dht)
        dx = dx + drt
        return dx, dx, dw

    return _rowwise(fn, [x, dh, dres], [w], [(d, F32), (d, BF16)], [d], tl=128, name=name)


def _adamw_math(w, g, m, v):
    m = ADAM_B1 * m + (1.0 - ADAM_B1) * g
    v = ADAM_B2 * v + (1.0 - ADAM_B2) * jnp.square(g)
    m_hat = m / (1.0 - ADAM_B1 ** ADAM_STEP)
    v_hat = v / (1.0 - ADAM_B2 ** ADAM_STEP)
    delta = -ADAM_LR * (m_hat / (jnp.sqrt(v_hat) + ADAM_EPS) + ADAM_WD * w)
    return delta, m, v


def _adamw(w, g, m, v, name):
    c = w.shape[1]
    tl = max(8, min(256, (2 * 1024 * 1024 // (4 * c)) // 8 * 8))
    return _rowwise(_adamw_math, [w, g, m, v], [], [(c, F32)] * 3, tl=tl, name=name)


def _s5_scan_fwd(bu_re, bu_im, a_re, a_im, name):
    length, rows, _ = bu_re.shape
    rb = _pick(rows, (32, 16, 8))
    tl = _pick(length, (64, 32, 16, 8))

    def body(bur_ref, bui_ref, ar_ref, ai_ref, sr_ref, si_ref, st_ref):
        @pl.when(pl.program_id(1) == 0)
        def _():
            st_ref[...] = jnp.zeros_like(st_ref)

        ar = ar_ref[...]
        ai = ai_ref[...]

        def step(l, carry):
            sr, si = carry
            nr = ar * sr - ai * si + bur_ref[l]
            ni = ar * si + ai * sr + bui_ref[l]
            sr_ref[l] = nr
            si_ref[l] = ni
            return nr, ni

        sr, si = lax.fori_loop(0, tl, step, (st_ref[0], st_ref[1]))
        st_ref[0] = sr
        st_ref[1] = si

    blk = pl.BlockSpec((tl, rb, LANES), lambda cb, t: (t, cb, 0))
    ablk = pl.BlockSpec((rb, LANES), lambda cb, t: (cb, 0))
    return pl.pallas_call(
        body, name=name, grid=(rows // rb, length // tl),
        in_specs=[blk, blk, ablk, ablk], out_specs=[blk, blk],
        out_shape=[jax.ShapeDtypeStruct(bu_re.shape, F32)] * 2,
        scratch_shapes=[pltpu.VMEM((2, rb, LANES), F32)],
        compiler_params=_params(("parallel", "arbitrary")),
    )(bu_re, bu_im, a_re, a_im)


def _s5_scan_bwd(ds_re, ds_im, s_re, s_im, a_re, a_im, name):
    length, rows, _ = ds_re.shape
    rb = _pick(rows, (32, 16, 8))
    tl = _pick(length, (64, 32, 16, 8))
    nt = length // tl

    def body(dsr_ref, dsi_ref, sr_ref, si_ref, pr_ref, pi_ref, ar_ref, ai_ref,
             gr_ref, gi_ref, dar_ref, dai_ref, st_ref):
        t = pl.program_id(1)

        @pl.when(t == 0)
        def _():
            st_ref[...] = jnp.zeros_like(st_ref)
            dar_ref[...] = jnp.zeros_like(dar_ref)
            dai_ref[...] = jnp.zeros_like(dai_ref)

        ar = ar_ref[...]
        ai = ai_ref[...]

        def adj(l, gr, gi):
            ngr = dsr_ref[l] + ar * gr + ai * gi
            ngi = dsi_ref[l] + ar * gi - ai * gr
            gr_ref[l] = ngr
            gi_ref[l] = ngi
            return ngr, ngi

        def step(idx, carry):
            gr, gi, dar, dai = carry
            l = tl - 1 - idx
            gr, gi = adj(l, gr, gi)
            pr = sr_ref[l - 1]
            pi = si_ref[l - 1]
            dar = dar + gr * pr + gi * pi
            dai = dai + gi * pr - gr * pi
            return gr, gi, dar, dai

        zero = jnp.zeros((rb, LANES), F32)
        gr, gi, dar, dai = lax.fori_loop(0, tl - 1, step, (st_ref[0], st_ref[1], zero, zero))
        gr, gi = adj(0, gr, gi)
        first = (t == nt - 1)
        pr = jnp.where(first, 0.0, pr_ref[0])
        pi = jnp.where(first, 0.0, pi_ref[0])
        dar = dar + gr * pr + gi * pi
        dai = dai + gi * pr - gr * pi
        st_ref[0] = gr
        st_ref[1] = gi
        dar_ref[...] += dar
        dai_ref[...] += dai

    blk = pl.BlockSpec((tl, rb, LANES), lambda cb, t: (nt - 1 - t, cb, 0))
    prev = pl.BlockSpec((1, rb, LANES), lambda cb, t: (jnp.maximum((nt - 1 - t) * tl - 1, 0), cb, 0))
    ablk = pl.BlockSpec((rb, LANES), lambda cb, t: (cb, 0))
    return pl.pallas_call(
        body, name=name, grid=(rows // rb, nt),
        in_specs=[blk, blk, blk, blk, prev, prev, ablk, ablk],
        out_specs=[blk, blk, ablk, ablk],
        out_shape=[jax.ShapeDtypeStruct(ds_re.shape, F32)] * 2 + [jax.ShapeDtypeStruct(a_re.shape, F32)] * 2,
        scratch_shapes=[pltpu.VMEM((2, rb, LANES), F32)],
        compiler_params=_params(("parallel", "arbitrary")),
    )(ds_re, ds_im, s_re, s_im, s_re, s_im, a_re, a_im)


def _s5_prepare(lam_re, lam_im, log_step, b_re, b_im, c_re, c_im):
    groups, state = lam_re.shape
    lr = jnp.minimum(lam_re, S5_EIG_CLIP)
    li = lam_im
    step = jnp.exp(log_step)[:, None]
    mag = jnp.exp(lr * step)
    ab_re = mag * jnp.cos(li * step)
    ab_im = mag * jnp.sin(li * step)
    denom = lr * lr + li * li
    nr = ab_re - 1.0
    ni = ab_im
    coef_re = (nr * lr + ni * li) / denom
    coef_im = (ni * lr - nr * li) / denom
    bb_re = coef_re[..., None] * b_re - coef_im[..., None] * b_im
    bb_im = coef_re[..., None] * b_im + coef_im[..., None] * b_re
    per = LANES // S5_GROUP
    nj = groups // per
    eye = jnp.eye(per, dtype=F32)

    def in_map(bb):
        return jnp.einsum('jgph,gk->jghkp', bb.reshape(nj, per, state, S5_GROUP), eye).reshape(
            nj, per * S5_GROUP, per * state)

    def out_map(cc):
        return jnp.einsum('jghp,gk->jgpkh', cc.reshape(nj, per, S5_GROUP, state), eye).reshape(
            nj, per * state, per * S5_GROUP)

    shape2 = (groups * state // LANES, LANES)
    return (ab_re.reshape(shape2), ab_im.reshape(shape2), in_map(bb_re), in_map(bb_im),
            out_map(c_re), -out_map(c_im))


def _shift_down(cur, prev8, j):
    rolled = pltpu.roll(cur, j, 0)
    pr = pltpu.roll(prev8, j, 0)
    row = lax.broadcasted_iota(jnp.int32, cur.shape, 0)
    return jnp.where(row < j, jnp.tile(pr, (cur.shape[0] // 8, 1)), rolled)


def _shift_up(cur, next8, j):
    tl = cur.shape[0]
    rolled = pltpu.roll(cur, tl - j, 0)
    nx = pltpu.roll(next8, 8 - j, 0)
    row = lax.broadcasted_iota(jnp.int32, cur.shape, 0)
    return jnp.where(row >= tl - j, jnp.tile(nx, (tl // 8, 1)), rolled)


def _conv_tiles(length, ch):
    return _pick(length, (256, 128, 64, 32, 16, 8)), _pick(ch, (1024, 512, 256, 128))


def _conv_fwd(xbc, w, b, name):
    length, ch = xbc.shape
    tl, tc = _conv_tiles(length, ch)

    def body(x_ref, p_ref, w_ref, b_ref, o_ref):
        cur = x_ref[...]
        prev8 = jnp.where(pl.program_id(1) == 0, 0.0, p_ref[...])
        pre = b_ref[...] + w_ref[SSD_CONV - 1:SSD_CONV, :] * cur
        for j in range(1, SSD_CONV):
            pre = pre + w_ref[SSD_CONV - 1 - j:SSD_CONV - j, :] * _shift_down(cur, prev8, j)
        o_ref[...] = _silu(pre)

    return pl.pallas_call(
        body, name=name, grid=(ch // tc, length // tl),
        in_specs=[pl.BlockSpec((tl, tc), lambda c, i: (i, c)),
                  pl.BlockSpec((8, tc), lambda c, i: (jnp.maximum(i * (tl // 8) - 1, 0), c)),
                  pl.BlockSpec((SSD_CONV, tc), lambda c, i: (0, c)),
                  pl.BlockSpec((1, tc), lambda c, i: (0, c))],
        out_specs=pl.BlockSpec((tl, tc), lambda c, i: (i, c)),
        out_shape=jax.ShapeDtypeStruct((length, ch), F32),
        compiler_params=_params(("parallel", "parallel")),
    )(xbc, xbc, w, b)


def _conv_bwd_pre(dxc, xbc, w, b, name):
    length, ch = xbc.shape
    tl, tc = _conv_tiles(length, ch)

    def body(d_ref, x_ref, p_ref, w_ref, b_ref, o_ref, dw_ref, db_ref):
        @pl.when(pl.program_id(1) == 0)
        def _():
            dw_ref[...] = jnp.zeros_like(dw_ref)
            db_ref[...] = jnp.zeros_like(db_ref)

        cur = x_ref[...]
        prev8 = jnp.where(pl.program_id(1) == 0, 0.0, p_ref[...])
        shifted = [cur] + [_shift_down(cur, prev8, j) for j in range(1, SSD_CONV)]
        pre = b_ref[...]
        for j in range(SSD_CONV):
            pre = pre + w_ref[SSD_CONV - 1 - j:SSD_CONV - j, :] * shifted[j]
        sg = _sigmoid(pre)
        dpre = d_ref[...] * (sg * (1.0 + pre * (1.0 - sg)))
        o_ref[...] = dpre
        db_ref[...] += jnp.sum(dpre, axis=0, keepdims=True)
        row = lax.broadcasted_iota(jnp.int32, (SSD_CONV, tc), 0)
        dw = jnp.zeros((SSD_CONV, tc), F32)
        for j in range(SSD_CONV):
            dw = dw + jnp.where(row == SSD_CONV - 1 - j, jnp.sum(dpre * shifted[j], axis=0, keepdims=True), 0.0)
        dw_ref[...] += dw

    return pl.pallas_call(
        body, name=name, grid=(ch // tc, length // tl),
        in_specs=[pl.BlockSpec((tl, tc), lambda c, i: (i, c)),
                  pl.BlockSpec((tl, tc), lambda c, i: (i, c)),
                  pl.BlockSpec((8, tc), lambda c, i: (jnp.maximum(i * (tl // 8) - 1, 0), c)),
                  pl.BlockSpec((SSD_CONV, tc), lambda c, i: (0, c)),
                  pl.BlockSpec((1, tc), lambda c, i: (0, c))],
        out_specs=[pl.BlockSpec((tl, tc), lambda c, i: (i, c)),
                   pl.BlockSpec((SSD_CONV, tc), lambda c, i: (0, c)),
                   pl.BlockSpec((1, tc), lambda c, i: (0, c))],
        out_shape=[jax.ShapeDtypeStruct((length, ch), F32), jax.ShapeDtypeStruct((SSD_CONV, ch), F32),
                   jax.ShapeDtypeStruct((1, ch), F32)],
        compiler_params=_params(("parallel", "arbitrary")),
    )(dxc, xbc, xbc, w, b)


def _conv_bwd_in(dpre, w, name):
    length, ch = dpre.shape
    tl, tc = _conv_tiles(length, ch)
    nt = length // tl

    def body(d_ref, n_ref, w_ref, o_ref):
        cur = d_ref[...]
        next8 = jnp.where(pl.program_id(1) == nt - 1, 0.0, n_ref[...])
        acc = w_ref[SSD_CONV - 1:SSD_CONV, :] * cur
        for j in range(1, SSD_CONV):
            acc = acc + w_ref[SSD_CONV - 1 - j:SSD_CONV - j, :] * _shift_up(cur, next8, j)
        o_ref[...] = acc.astype(o_ref.dtype)

    return pl.pallas_call(
        body, name=name, grid=(ch // tc, nt),
        in_specs=[pl.BlockSpec((tl, tc), lambda c, i: (i, c)),
                  pl.BlockSpec((8, tc), lambda c, i: (jnp.minimum((i + 1) * (tl // 8), length // 8 - 1), c)),
                  pl.BlockSpec((SSD_CONV, tc), lambda c, i: (0, c))],
        out_specs=pl.BlockSpec((tl, tc), lambda c, i: (i, c)),
        out_shape=jax.ShapeDtypeStruct((length, ch), BF16),
        compiler_params=_params(("parallel", "parallel")),
    )(dpre, dpre, w)


def _split(x, terms):
    parts = []
    for _ in range(terms):
        part = x.astype(BF16)
        parts.append(part)
        x = x - part.astype(F32)
    return parts


def _dot(a, b, dn=(((1,), (0,)), ((), ()))):
    return lax.dot_general(a, b, dn, preferred_element_type=F32)


_NN = (((1,), (0,)), ((), ()))
_NT = (((1,), (1,)), ((), ()))
_TN = (((0,), (0,)), ((), ()))


def _pdot(parts, sel, dn=_NN):
    return functools.reduce(lambda a, b: a + b, [_dot(part, sel, dn) for part in parts])


def _pdotr(sel, parts, dn=_NN):
    return functools.reduce(lambda a, b: a + b, [_dot(sel, part, dn) for part in parts])


def _dot3(x, sel, dn=_NN):
    return _pdot(_split(x, 3), sel, dn)


def _dot3r(sel, x, dn=_NN):
    return _pdotr(sel, _split(x, 3), dn)


def _dot2(x, sel, dn=_NN):
    return _pdot(_split(x, 2), sel, dn)


def _iota2(shape, axis):
    return lax.broadcasted_iota(jnp.int32, shape, axis)


def _ssd_masks():
    q = SSD_CHUNK
    r, c = _iota2((q, q), 0), _iota2((q, q), 1)
    tril = (c <= r)
    return r, c, tril


def _pair_sel(i):
    r, c, _ = _ssd_masks()
    return (r == 2 * i + c // SSD_HEAD_DIM).astype(BF16)


def _pair_sel_t(i):
    r, c, _ = _ssd_masks()
    return (c == 2 * i + r // SSD_HEAD_DIM).astype(F32)


def _ssd_common(dtraw, dtb, ap, b_t, c_t):
    r, c, tril = _ssd_masks()
    dt = _softplus(dtraw + dtb)
    la = dt * ap
    tril_b = tril.astype(BF16)
    cum = _dot3r(tril_b, la)
    rem = _dot3r((c > r).astype(BF16), la)
    total = _dot3(la, jnp.ones((SSD_CHUNK, SSD_CHUNK), BF16), _TN)
    scores = _dot(c_t, b_t, _NT)
    picks = {"dt": _split(dt, 2), "cum": _split(cum, 2), "rem": _split(rem, 2), "total": _split(total, 2)}
    return dt, la, cum, picks, scores, tril


def _head_decay(cum_parts, h, tril):
    r, c, _ = _ssd_masks()
    cq = _pdot(cum_parts, (r == h).astype(BF16))
    ck = _pdotr((c == h).astype(BF16), cum_parts, _NT)
    return jnp.exp(jnp.where(tril, cq - ck, -jnp.inf))


def _ssd_tiles(xc, n_heads):
    hpg = n_heads // SSD_GROUPS
    wg = hpg * SSD_HEAD_DIM
    xw = n_heads * SSD_HEAD_DIM
    return hpg, wg, xw // wg, xw // SSD_STATE


def _ssd_fwd(xc, dtraw, dtb, ap, n_heads, name):
    length = xc.shape[0]
    q = SSD_CHUNK
    nc = length // q
    hpg, wg, _, b_blk0 = _ssd_tiles(xc, n_heads)
    c_blk0 = b_blk0 + SSD_GROUPS
    npair = hpg // 2

    def body(x_ref, b_ref, c_ref, dt_ref, dtb_ref, ap_ref, y_ref, st_ref, s_ref):
        @pl.when(pl.program_id(1) == 0)
        def _():
            s_ref[...] = jnp.zeros_like(s_ref)

        st_ref[0, 0] = s_ref[...]
        b_t = b_ref[...].astype(BF16)
        c_t = c_ref[...].astype(BF16)
        dt, la, cum, picks, scores, tril = _ssd_common(dt_ref[...], dtb_ref[...], ap_ref[...], b_t, c_t)
        lane = _iota2((q, q), 1)
        for i in range(npair):
            sel = _pair_sel(i)
            xp = x_ref[:, i * LANES:(i + 1) * LANES]
            xd = xp * _pdot(picks["dt"], sel)
            xd_b = xd.astype(BF16)
            s_prev = s_ref[i * LANES:(i + 1) * LANES, :]
            y = _dot(c_t, s_prev.astype(BF16), _NT) * jnp.exp(_pdot(picks["cum"], sel))
            for hh in range(2):
                own = (lane // SSD_HEAD_DIM) == hh
                wm = scores * _head_decay(picks["cum"], 2 * i + hh, tril)
                y = y + _dot(wm.astype(BF16), jnp.where(own, xd_b, 0))
            y_ref[:, i * LANES:(i + 1) * LANES] = y
            xw_b = (xd * jnp.exp(_pdot(picks["rem"], sel))).astype(BF16)
            grow = jnp.exp(_pdotr(sel, picks["total"], _TN))
            s_ref[i * LANES:(i + 1) * LANES, :] = grow * s_prev + _dot(xw_b, b_t, _TN)

    return pl.pallas_call(
        body, name=name, grid=(SSD_GROUPS, nc),
        in_specs=[pl.BlockSpec((q, wg), lambda g, c: (c, g)),
                  pl.BlockSpec((q, SSD_STATE), lambda g, c: (c, b_blk0 + g)),
                  pl.BlockSpec((q, SSD_STATE), lambda g, c: (c, c_blk0 + g)),
                  pl.BlockSpec((q, LANES), lambda g, c: (c, g)),
                  pl.BlockSpec((1, LANES), lambda g, c: (0, g)),
                  pl.BlockSpec((1, LANES), lambda g, c: (0, g))],
        out_specs=[pl.BlockSpec((q, wg), lambda g, c: (c, g)),
                   pl.BlockSpec((1, 1, wg, SSD_STATE), lambda g, c: (c, g, 0, 0))],
        out_shape=[jax.ShapeDtypeStruct((length, SSD_GROUPS * wg), F32),
                   jax.ShapeDtypeStruct((nc, SSD_GROUPS, wg, SSD_STATE), F32)],
        scratch_shapes=[pltpu.VMEM((wg, SSD_STATE), F32)],
        compiler_params=_params(("parallel", "arbitrary")),
    )(xc, xc, xc, dtraw, dtb, ap)


def _ssd_bwd(dy, dxa, xc, dtraw, states, dtb, ap, n_heads, name):
    length = xc.shape[0]
    q = SSD_CHUNK
    nc = length // q
    hpg, wg, _, b_blk0 = _ssd_tiles(xc, n_heads)
    c_blk0 = b_blk0 + SSD_GROUPS
    npair = hpg // 2

    def body(dy_ref, dxa_ref, x_ref, b_ref, c_ref, dt_ref, st_ref, dtb_ref, ap_ref,
             dx_ref, db_ref, dc_ref, ddt_ref, ddtb_ref, dap_ref, ds_ref):
        @pl.when(pl.program_id(1) == 0)
        def _():
            ds_ref[...] = jnp.zeros_like(ds_ref)
            ddtb_ref[...] = jnp.zeros_like(ddtb_ref)
            dap_ref[...] = jnp.zeros_like(dap_ref)

        b_f = b_ref[...]
        c_f = c_ref[...]
        b_t = b_f.astype(BF16)
        c_t = c_f.astype(BF16)
        dtraw_t = dt_ref[...]
        dt, la, cum, picks, scores, tril = _ssd_common(dtraw_t, dtb_ref[...], ap_ref[...], b_t, c_t)
        r, c, _ = _ssd_masks()
        lane = c
        ones_b = jnp.ones((q, q), BF16)
        dcum = jnp.zeros((q, q), F32)
        drem = jnp.zeros((q, q), F32)
        dtot = jnp.zeros((q, q), F32)
        ddt = jnp.zeros((q, q), F32)
        dscores = jnp.zeros((q, q), F32)
        db_acc = jnp.zeros((q, SSD_STATE), F32)
        dc_acc = jnp.zeros((q, SSD_STATE), F32)
        for i in range(npair):
            sel = _pair_sel(i)
            xp = x_ref[:, i * LANES:(i + 1) * LANES]
            dyp = dy_ref[:, i * LANES:(i + 1) * LANES]
            dyp_b = dyp.astype(BF16)
            dtp = _pdot(picks["dt"], sel)
            ecum = jnp.exp(_pdot(picks["cum"], sel))
            wrem = jnp.exp(_pdot(picks["rem"], sel))
            xd = xp * dtp
            xd_b = xd.astype(BF16)
            s_prev = s_prev_f = st_ref[0, 0, i * LANES:(i + 1) * LANES, :]
            ds1 = ds_ref[i * LANES:(i + 1) * LANES, :]
            ds1_b = ds1.astype(BF16)
            dxd = jnp.zeros((q, LANES), F32)
            for hh in range(2):
                h = 2 * i + hh
                own = (lane // SSD_HEAD_DIM) == hh
                decay = _head_decay(picks["cum"], h, tril)
                wm = scores * decay
                dwm = _dot(jnp.where(own, dyp_b, 0), xd_b, _NT)
                dxd = dxd + jnp.where(own, _dot(wm.astype(BF16), dyp_b, _TN), 0.0)
                dscores = dscores + dwm * decay
                e = (dwm * wm).astype(BF16)
                put = (c == h).astype(BF16)
                dcum = dcum + _dot(e, put) - _dot(e, put, _TN)
            t_mat = _dot(c_t, s_prev.astype(BF16), _NT)
            d_t = (dyp * ecum).astype(BF16)
            dc_acc = dc_acc + _dot(d_t, s_prev.astype(BF16))
            ds_prev = _dot(d_t, c_t, _TN)
            dcum = dcum + _dot2(dyp * t_mat * ecum, sel, _NT)
            grow = jnp.exp(_pdotr(sel, picks["total"], _TN))
            ds_prev = ds_prev + grow * ds1
            zs = jnp.sum(ds1 * s_prev_f * grow, axis=1, keepdims=True)
            dtot = dtot + _pdotr(ones_b, _split(zs * _pair_sel_t(i), 2))
            xw = xd * wrem
            dxw = _dot(b_t, ds1_b, _NT)
            db_acc = db_acc + _dot(xw.astype(BF16), ds1_b)
            dxd = dxd + dxw * wrem
            drem = drem + _dot2(dxw * xw, sel, _NT)
            dx_ref[:, i * LANES:(i + 1) * LANES] = dxd * dtp + dxa_ref[:, i * LANES:(i + 1) * LANES]
            ddt = ddt + _dot2(dxd * xp, sel, _NT)
            ds_ref[i * LANES:(i + 1) * LANES, :] = ds_prev
        ds_b = dscores.astype(BF16)
        dc_ref[...] = dc_acc + _dot(ds_b, b_t)
        db_ref[...] = db_acc + _dot(ds_b, c_t, _TN)
        dla = (_dot3r(tril.astype(BF16), dcum, _TN) + _dot3r((c > r).astype(BF16), drem, _TN) + dtot)
        ddt = ddt + dla * ap_ref[...]
        dap_ref[...] += jnp.sum(dla * dt, axis=0, keepdims=True)
        ddtraw = ddt * _sigmoid(dtraw_t + dtb_ref[...])
        ddt_ref[...] = ddtraw.astype(ddt_ref.dtype)
        ddtb_ref[...] += jnp.sum(ddtraw, axis=0, keepdims=True)

    rev = lambda g, c: (nc - 1 - c, g)
    return pl.pallas_call(
        body, name=name, grid=(SSD_GROUPS, nc),
        in_specs=[pl.BlockSpec((q, wg), rev),
                  pl.BlockSpec((q, wg), rev),
                  pl.BlockSpec((q, wg), rev),
                  pl.BlockSpec((q, SSD_STATE), lambda g, c: (nc - 1 - c, b_blk0 + g)),
                  pl.BlockSpec((q, SSD_STATE), lambda g, c: (nc - 1 - c, c_blk0 + g)),
                  pl.BlockSpec((q, LANES), rev),
                  pl.BlockSpec((1, 1, wg, SSD_STATE), lambda g, c: (nc - 1 - c, g, 0, 0)),
                  pl.BlockSpec((1, LANES), lambda g, c: (0, g)),
                  pl.BlockSpec((1, LANES), lambda g, c: (0, g))],
        out_specs=[pl.BlockSpec((q, wg), rev),
                   pl.BlockSpec((q, SSD_STATE), rev),
                   pl.BlockSpec((q, SSD_STATE), rev),
                   pl.BlockSpec((q, LANES), rev),
                   pl.BlockSpec((1, LANES), lambda g, c: (0, g)),
                   pl.BlockSpec((1, LANES), lambda g, c: (0, g))],
        out_shape=[jax.ShapeDtypeStruct((length, SSD_GROUPS * wg), F32),
                   jax.ShapeDtypeStruct((length, SSD_GROUPS * SSD_STATE), F32),
                   jax.ShapeDtypeStruct((length, SSD_GROUPS * SSD_STATE), F32),
                   jax.ShapeDtypeStruct((length, SSD_GROUPS * LANES), BF16),
                   jax.ShapeDtypeStruct((1, SSD_GROUPS * LANES), F32),
                   jax.ShapeDtypeStruct((1, SSD_GROUPS * LANES), F32)],
        scratch_shapes=[pltpu.VMEM((wg, SSD_STATE), F32)],
        compiler_params=_params(("parallel", "arbitrary")),
    )(dy, dxa, xc, xc, xc, dtraw, states, dtb, ap)


def _fox_cumsum(fraw, bf, name):
    length = fraw.shape[0]
    q = 128

    def body(f_ref, b_ref, o_ref, carry_ref):
        @pl.when(pl.program_id(0) == 0)
        def _():
            carry_ref[...] = jnp.zeros_like(carry_ref)

        lf = -_softplus(-(f_ref[...] + b_ref[...]))
        r, c = _iota2((q, q), 0), _iota2((q, q), 1)
        o_ref[...] = _dot3r((c <= r).astype(BF16), lf) + carry_ref[...]
        carry_ref[...] += jnp.sum(lf, axis=0, keepdims=True)

    return pl.pallas_call(
        body, name=name, grid=(length // q,),
        in_specs=[pl.BlockSpec((q, LANES), lambda i: (i, 0)), pl.BlockSpec((1, LANES), lambda i: (0, 0))],
        out_specs=pl.BlockSpec((q, LANES), lambda i: (i, 0)),
        out_shape=jax.ShapeDtypeStruct((length, LANES), F32),
        scratch_shapes=[pltpu.VMEM((1, LANES), F32)],
        compiler_params=_params(("arbitrary",)),
    )(fraw, bf)


def _fox_cumsum_bwd(dc, fraw, bf, name):
    length = fraw.shape[0]
    q = 128
    nt = length // q

    def body(d_ref, f_ref, b_ref, o_ref, db_ref, carry_ref):
        @pl.when(pl.program_id(0) == 0)
        def _():
            carry_ref[...] = jnp.zeros_like(carry_ref)
            db_ref[...] = jnp.zeros_like(db_ref)

        d = d_ref[...]
        r, c = _iota2((q, q), 0), _iota2((q, q), 1)
        dlf = _dot3r((c >= r).astype(BF16), d) + carry_ref[...]
        carry_ref[...] += jnp.sum(d, axis=0, keepdims=True)
        df = dlf * _sigmoid(-(f_ref[...] + b_ref[...]))
        o_ref[...] = df.astype(o_ref.dtype)
        db_ref[...] += jnp.sum(df, axis=0, keepdims=True)

    rev = lambda i: (nt - 1 - i, 0)
    return pl.pallas_call(
        body, name=name, grid=(nt,),
        in_specs=[pl.BlockSpec((q, LANES), rev), pl.BlockSpec((q, LANES), rev),
                  pl.BlockSpec((1, LANES), lambda i: (0, 0))],
        out_specs=[pl.BlockSpec((q, LANES), rev), pl.BlockSpec((1, LANES), lambda i: (0, 0))],
        out_shape=[jax.ShapeDtypeStruct((length, LANES), BF16), jax.ShapeDtypeStruct((1, LANES), F32)],
        scratch_shapes=[pltpu.VMEM((1, LANES), F32)],
        compiler_params=_params(("arbitrary",)),
    )(dc, fraw, bf)


def _fox_scores(q_ref, k_ref, cq_ref, ck_ref, diagonal):
    scale = 1.0 / math.sqrt(FOX_HEAD_DIM)
    s = _dot(q_ref[...].astype(BF16), k_ref[...].astype(BF16), _NT) * scale + (cq_ref[0] - ck_ref[0])
    if diagonal:
        s = jnp.where(_iota2(s.shape, 1) <= _iota2(s.shape, 0), s, -jnp.inf)
    return s


def _fox_tiles(i, j, step):
    @pl.when(j < i)
    def _():
        step(False)

    @pl.when(j == i)
    def _():
        step(True)


def _fox_fwd(qkvg, c_col, c_row, n_heads, name):
    length = qkvg.shape[0]
    hd = FOX_HEAD_DIM
    tq = _pick(length, (512, 256, 128))
    nq = length // tq

    def body(q_ref, k_ref, v_ref, cq_ref, ck_ref, o_ref, lse_ref, m_ref, l_ref, acc_ref):
        i, j = pl.program_id(1), pl.program_id(2)

        @pl.when(j == 0)
        def _():
            m_ref[...] = jnp.full_like(m_ref, -jnp.inf)
            l_ref[...] = jnp.zeros_like(l_ref)
            acc_ref[...] = jnp.zeros_like(acc_ref)

        def step(diagonal):
            s = _fox_scores(q_ref, k_ref, cq_ref, ck_ref, diagonal)
            m_new = jnp.maximum(m_ref[...], jnp.max(s, axis=1, keepdims=True))
            alpha = jnp.exp(m_ref[...] - m_new)
            p = jnp.exp(s - m_new)
            l_ref[...] = alpha * l_ref[...] + jnp.sum(p, axis=1, keepdims=True)
            acc_ref[...] = alpha * acc_ref[...] + _dot(p.astype(BF16), v_ref[...].astype(BF16))
            m_ref[...] = m_new

        _fox_tiles(i, j, step)

        @pl.when(j == nq - 1)
        def _():
            o_ref[...] = acc_ref[...] / l_ref[...]
            lse_ref[0] = m_ref[...] + jnp.log(l_ref[...])

    kmap = lambda off: (lambda h, i, j: (jnp.minimum(j, i), off * n_heads + h))
    return pl.pallas_call(
        body, name=name, grid=(n_heads, nq, nq),
        in_specs=[pl.BlockSpec((tq, hd), lambda h, i, j: (i, h)),
                  pl.BlockSpec((tq, hd), kmap(1)),
                  pl.BlockSpec((tq, hd), kmap(2)),
                  pl.BlockSpec((1, tq, 1), lambda h, i, j: (h, i, 0)),
                  pl.BlockSpec((1, 1, tq), lambda h, i, j: (h, 0, jnp.minimum(j, i)))],
        out_specs=[pl.BlockSpec((tq, hd), lambda h, i, j: (i, h)),
                   pl.BlockSpec((1, tq, 1), lambda h, i, j: (h, i, 0))],
        out_shape=[jax.ShapeDtypeStruct((length, n_heads * hd), F32),
                   jax.ShapeDtypeStruct((n_heads, length, 1), F32)],
        scratch_shapes=[pltpu.VMEM((tq, 1), F32), pltpu.VMEM((tq, 1), F32), pltpu.VMEM((tq, hd), F32)],
        compiler_params=_params(("parallel", "parallel", "arbitrary")),
    )(qkvg, qkvg, qkvg, c_col, c_row)


def _fox_bwd_q(qkvg, datt, lse, c_col, c_row, n_heads, name):
    length = qkvg.shape[0]
    hd = FOX_HEAD_DIM
    tq = _pick(length, (512, 256, 128))
    nq = length // tq
    scale = 1.0 / math.sqrt(hd)

    def body(q_ref, k_ref, v_ref, do_ref, lse_ref, cq_ref, ck_ref, dq_ref, dsum_ref, a1_ref, a2_ref, d_ref):
        i, j = pl.program_id(1), pl.program_id(2)

        @pl.when(j == 0)
        def _():
            a1_ref[...] = jnp.zeros_like(a1_ref)
            a2_ref[...] = jnp.zeros_like(a2_ref)
            d_ref[...] = jnp.zeros_like(d_ref)

        def step(diagonal):
            s = _fox_scores(q_ref, k_ref, cq_ref, ck_ref, diagonal)
            p = jnp.exp(s - lse_ref[0])
            pdp = p * _dot(do_ref[...].astype(BF16), v_ref[...].astype(BF16), _NT)
            d_ref[...] += jnp.sum(pdp, axis=1, keepdims=True)
            k_b = k_ref[...].astype(BF16)
            a1_ref[...] += _dot(pdp.astype(BF16), k_b)
            a2_ref[...] += _dot(p.astype(BF16), k_b)

        _fox_tiles(i, j, step)

        @pl.when(j == nq - 1)
        def _():
            dq_ref[...] = ((a1_ref[...] - d_ref[...] * a2_ref[...]) * scale).astype(dq_ref.dtype)
            dsum_ref[0] = d_ref[...]

    kmap = lambda off: (lambda h, i, j: (jnp.minimum(j, i), off * n_heads + h))
    qmap = lambda h, i, j: (i, h)
    col = pl.BlockSpec((1, tq, 1), lambda h, i, j: (h, i, 0))
    return pl.pallas_call(
        body, name=name, grid=(n_heads, nq, nq),
        in_specs=[pl.BlockSpec((tq, hd), qmap), pl.BlockSpec((tq, hd), kmap(1)), pl.BlockSpec((tq, hd), kmap(2)),
                  pl.BlockSpec((tq, hd), qmap), col, col,
                  pl.BlockSpec((1, 1, tq), lambda h, i, j: (h, 0, jnp.minimum(j, i)))],
        out_specs=[pl.BlockSpec((tq, hd), qmap), col],
        out_shape=[jax.ShapeDtypeStruct((length, n_heads * hd), BF16),
                   jax.ShapeDtypeStruct((n_heads, length, 1), F32)],
        scratch_shapes=[pltpu.VMEM((tq, hd), F32), pltpu.VMEM((tq, hd), F32), pltpu.VMEM((tq, 1), F32)],
        compiler_params=_params(("parallel", "parallel", "arbitrary")),
    )(qkvg, qkvg, qkvg, datt, lse, c_col, c_row)


def _fox_bwd_kv(qkvg, datt, lse, dsum, c_col, c_row, n_heads, name):
    length = qkvg.shape[0]
    hd = FOX_HEAD_DIM
    tq = _pick(length, (512, 256, 128))
    nq = length // tq
    scale = 1.0 / math.sqrt(hd)

    def body(q_ref, k_ref, v_ref, do_ref, lse_ref, dsum_ref, cq_ref, ck_ref, dk_ref, dv_ref, dck_ref,
             dk_acc, dv_acc, dc_acc):
        j, i = pl.program_id(1), pl.program_id(2)

        @pl.when(i == 0)
        def _():
            dk_acc[...] = jnp.zeros_like(dk_acc)
            dv_acc[...] = jnp.zeros_like(dv_acc)
            dc_acc[...] = jnp.zeros_like(dc_acc)

        def step(diagonal):
            s = _fox_scores(q_ref, k_ref, cq_ref, ck_ref, diagonal)
            p = jnp.exp(s - lse_ref[0])
            do_b = do_ref[...].astype(BF16)
            dv_acc[...] += _dot(p.astype(BF16), do_b, _TN)
            dp = _dot(do_b, v_ref[...].astype(BF16), _NT)
            ds = p * (dp - dsum_ref[0])
            dk_acc[...] += _dot(ds.astype(BF16), q_ref[...].astype(BF16), _TN)
            dc_acc[...] -= jnp.sum(ds, axis=0, keepdims=True)

        _fox_tiles(i, j, step)

        @pl.when(i == nq - 1)
        def _():
            dk_ref[...] = (dk_acc[...] * scale).astype(dk_ref.dtype)
            dv_ref[...] = dv_acc[...].astype(dv_ref.dtype)
            dck_ref[0] = dc_acc[...]

    qmap = lambda h, j, i: (jnp.maximum(i, j), h)
    kmap = lambda off: (lambda h, j, i: (j, off * n_heads + h))
    col = pl.BlockSpec((1, tq, 1), lambda h, j, i: (h, jnp.maximum(i, j), 0))
    return pl.pallas_call(
        body, name=name, grid=(n_heads, nq, nq),
        in_specs=[pl.BlockSpec((tq, hd), qmap), pl.BlockSpec((tq, hd), kmap(1)), pl.BlockSpec((tq, hd), kmap(2)),
                  pl.BlockSpec((tq, hd), qmap), col, col, col,
                  pl.BlockSpec((1, 1, tq), lambda h, j, i: (h, 0, j))],
        out_specs=[pl.BlockSpec((tq, hd), lambda h, j, i: (j, h)), pl.BlockSpec((tq, hd), lambda h, j, i: (j, h)),
                   pl.BlockSpec((1, 1, tq), lambda h, j, i: (h, 0, j))],
        out_shape=[jax.ShapeDtypeStruct((length, n_heads * hd), BF16)] * 2
        + [jax.ShapeDtypeStruct((n_heads, 1, length), F32)],
        scratch_shapes=[pltpu.VMEM((tq, hd), F32), pltpu.VMEM((tq, hd), F32), pltpu.VMEM((1, tq), F32)],
        compiler_params=_params(("parallel", "parallel", "arbitrary")),
    )(qkvg, qkvg, qkvg, datt, lse, dsum, c_col, c_row)


def _row(v):
    return v.reshape(1, -1).astype(F32)


def _pad_heads(v, per_group):
    lead = v.shape[:-1]
    v = v.reshape(lead + (SSD_GROUPS, per_group))
    v = jnp.pad(v, [(0, 0)] * len(lead) + [(0, 0), (0, LANES - per_group)])
    return v.reshape(lead + (SSD_GROUPS * LANES,))


def _unpad_heads(v, per_group):
    lead = v.shape[:-1]
    return v.reshape(lead + (SSD_GROUPS, LANES))[..., :per_group].reshape(lead + (SSD_GROUPS * per_group,))


def _local_step(x, tgt, wb, sm):
    length, d = x.shape
    s5w = wb["w0_ug"].shape[1] // 2
    ssdw = wb["w0_z"].shape[1]
    xbcw = wb["w0_xbc"].shape[1]
    n_ssd = ssdw // SSD_HEAD_DIM
    hpg = n_ssd // SSD_GROUPS
    fw = wb["w1_out"].shape[0]
    n_fox = fw // FOX_HEAD_DIM
    s5g = s5w // S5_GROUP
    s5s = s5g * S5_STATE
    grads = {}

    s5_in = (sm["l0_s5_lambda_re"], sm["l0_s5_lambda_im"], sm["l0_s5_log_step"], sm["l0_s5_b_re"],
             sm["l0_s5_b_im"], sm["l0_s5_c_re"], sm["l0_s5_c_im"])
    (a_re, a_im, bd_re, bd_im, cd_re, cd_imn), s5_vjp = jax.vjp(_s5_prepare, *s5_in)
    nj = bd_re.shape[0]
    bd_re_b, bd_im_b, cd_re_b, cd_imn_b = (t.astype(BF16) for t in (bd_re, bd_im, cd_re, cd_imn))
    tr = lambda t: jnp.swapaxes(t, 1, 2)
    dvec = _row(sm["l0_s5_d"])
    bglu = _row(sm["l0_s5_b_glu"])
    conv_w = sm["l0_ssd_conv_w"]
    conv_b = _row(sm["l0_ssd_conv_b"])

    def ssd_prepare(dt_bias, a_log, dd):
        return (_pad_heads(_row(dt_bias), hpg), _pad_heads(_row(-jnp.exp(a_log)), hpg),
                jnp.repeat(_row(dd), SSD_HEAD_DIM, axis=1))

    (dtb, ap, dpar), ssd_vjp = jax.vjp(ssd_prepare, sm["l0_ssd_dt_bias"], sm["l0_ssd_a_log"], sm["l0_ssd_d"])
    ssd_nw = _row(sm["l0_ssd_norm_w"])
    nw0, nw1, fnw = _row(sm["l0_norm_w"]), _row(sm["l1_norm_w"]), _row(sm["final_norm_w"])
    bf = jnp.pad(_row(sm["l1_fox_b_f"]), ((0, 0), (0, LANES - n_fox)))

    h0 = _norm_fwd(x, nw0, "l0_norm")
    ug = _matmul(h0, wb["w0_ug"], name="l0_in_ug")
    z = _matmul(h0, wb["w0_z"], name="l0_in_z")
    xbc = _matmul(h0, wb["w0_xbc"], name="l0_in_xbc")
    dtraw = _matmul(h0, wb["w0_dt"], name="l0_in_dt")
    u_win, gate_win = (ug, s5w, 0), (ug, s5w, 1)

    shape3 = (length, s5s // LANES, LANES)
    bu_re = _bdmm([(ug, 0)], [bd_re_b], name="s5_bu_re").reshape(shape3)
    bu_im = _bdmm([(ug, 0)], [bd_im_b], name="s5_bu_im").reshape(shape3)
    s_re3, s_im3 = _s5_scan_fwd(bu_re, bu_im, a_re, a_im, "s5_scan")
    s_re, s_im = s_re3.reshape(length, s5s), s_im3.reshape(length, s5s)
    yc = _bdmm([s_re, s_im], [cd_re_b, cd_imn_b], name="s5_y")
    gl = _rowwise(_f_s5_gelu, [yc, u_win], [dvec], [(s5w, BF16)], tl=256, name="s5_gelu")[0]
    t_glu = _matmul(gl, wb["w_glu"], name="s5_glu")
    s5o = _rowwise(_f_s5_out, [yc, u_win, t_glu, gate_win], [dvec, bglu], [(s5w, BF16)], tl=256,
                   name="s5_out")[0]

    xc = _conv_fwd(xbc, conv_w, conv_b, "ssd_conv")
    y_ssd, states = _ssd_fwd(xc, dtraw, dtb, ap, n_ssd, "ssd_scan")
    wg = ssdw // SSD_GROUPS
    ssdo = _rowwise(_f_ssd_out, [y_ssd, (xc, wg, 0), z], [dpar, ssd_nw], [(wg, BF16)], tl=256,
                    ncol=SSD_GROUPS, name="ssd_out")[0]
    x1 = _matmul(s5o, wb["w0_out"], dims=(length, d, s5w), addend=x, name="l0_out_s5")
    x1 = _matmul(ssdo, wb["w0_out"], dims=(length, d, ssdw), b_off=(s5w, 0), addend=x1, name="l0_out_ssd")

    h1 = _norm_fwd(x1, nw1, "l1_norm")
    qkvg = _matmul(h1, wb["w1_main"], name="l1_in")
    fraw = _matmul(h1, wb["w1_f"], name="l1_in_f")
    cc = _fox_cumsum(fraw, bf, "fox_cumsum")
    c_t = cc[:, :n_fox].T
    c_col, c_row = c_t[:, :, None], c_t[:, None, :]
    att, lse = _fox_fwd(qkvg, c_col, c_row, n_fox, "fox_fwd")
    gate1_win = (qkvg, fw, 3)
    o1 = _rowwise(_f_fox_out, [att, gate1_win], [], [(fw, BF16)], tl=256, name="fox_out")[0]
    x2 = _matmul(o1, wb["w1_out"], addend=x1, name="l1_out")

    def loss_fn(xt, tt, wt):
        def f(xx, ww):
            err = _f_norm(xx, ww) - tt
            return (0.5 / d) * err * err
        lanes, vjp = jax.vjp(f, xt, wt)
        dx, dw = vjp(jnp.ones_like(lanes))
        return dx, dx, jnp.sum(lanes, axis=0, keepdims=True), dw

    dx2, dx2b, loss_lanes, g_fnw = _rowwise(loss_fn, [x2, tgt], [fnw], [(d, F32), (d, BF16)], [d, d],
                                            tl=128, name="loss_head")
    grads["final_norm_w"] = g_fnw

    grads["l1_w_out"] = _matmul(o1, dx2b, mode="tn", name="l1_out_dw")
    do1 = _matmul(dx2b, wb["w1_out"], mode="nt", name="l1_out_dx")

    def fox_out_bwd(at, gt, dt_):
        _, vjp = jax.vjp(_f_fox_out, at, gt)
        return vjp(dt_)

    datt, dgate1 = _rowwise(fox_out_bwd, [att, gate1_win, do1], [], [(fw, F32), (fw, BF16)], tl=256,
                            name="fox_out_bwd")
    dq, dsum = _fox_bwd_q(qkvg, datt, lse, c_col, c_row, n_fox, "fox_bwd_q")
    dk, dv, dck = _fox_bwd_kv(qkvg, datt, lse, dsum, c_col, c_row, n_fox, "fox_bwd_kv")
    dcc = jnp.pad(dck[:, 0, :].T, ((0, 0), (0, LANES - n_fox)))
    dfraw, g_bf = _fox_cumsum_bwd(dcc, fraw, bf, "fox_cumsum_bwd")
    grads["l1_fox_b_f"] = g_bf[:, :n_fox]
    dsegs = [dq, dk, dv, dgate1]
    grads["l1_w_in"] = jnp.concatenate(
        [_matmul(h1, s, mode="tn", name=f"l1_in_dw{i}") for i, s in enumerate(dsegs)]
        + [_matmul(h1, dfraw, mode="tn", name="l1_in_dwf")[:, :n_fox]], axis=1)
    dh1 = _matmul(dfraw, wb["w1_f"], mode="nt", name="l1_in_dxf")
    for i, s in enumerate(dsegs):
        dh1 = _matmul(s, wb["w1_main"], mode="nt", dims=(length, d, fw), b_off=(0, i * fw), addend=dh1,
                      name=f"l1_in_dx{i}")
    dx1, dx1b, grads["l1_norm_w"] = _norm_bwd(x1, dh1, dx2, nw1, "l1_norm_bwd")

    grads["l0_w_out"] = jnp.concatenate([_matmul(s5o, dx1b, mode="tn", name="l0_out_dw_s5"),
                                         _matmul(ssdo, dx1b, mode="tn", name="l0_out_dw_ssd")], axis=0)
    ds5o = _matmul(dx1b, wb["w0_out"], mode="nt", dims=(length, s5w, d), name="l0_out_dx_s5")
    dssdo = _matmul(dx1b, wb["w0_out"], mode="nt", dims=(length, ssdw, d), b_off=(s5w, 0), name="l0_out_dx_ssd")

    def ssd_out_bwd(yt, xt, zt, dt_, dp, nw):
        _, vjp = jax.vjp(_f_ssd_out, yt, xt, zt, dp, nw)
        return vjp(dt_)

    dy_ssd, dxa, dz, g_dpar, g_ssd_nw = _rowwise(
        ssd_out_bwd, [y_ssd, (xc, wg, 0), z, dssdo], [dpar, ssd_nw],
        [(wg, F32), (wg, F32), (wg, BF16)], [wg, wg], tl=128, ncol=SSD_GROUPS, name="ssd_out_bwd")
    grads["l0_ssd_norm_w"] = g_ssd_nw
    dxs, db_ssd, dc_ssd, ddtraw, g_dtb, g_ap = _ssd_bwd(dy_ssd, dxa, xc, dtraw, states, dtb, ap, n_ssd,
                                                        "ssd_scan_bwd")
    g_dt_bias, g_a_log, g_ssd_d = ssd_vjp((g_dtb, g_ap, g_dpar))
    grads["l0_ssd_dt_bias"], grads["l0_ssd_a_log"], grads["l0_ssd_d"] = g_dt_bias, g_a_log, g_ssd_d
    dxc = jnp.concatenate([dxs, db_ssd, dc_ssd], axis=1)
    dpre, grads["l0_ssd_conv_w"], grads["l0_ssd_conv_b"] = _conv_bwd_pre(dxc, xbc, conv_w, conv_b, "ssd_conv_bwd_pre")
    dxbc = _conv_bwd_in(dpre, conv_w, "ssd_conv_bwd_in")

    def s5_out_bwd(yt, ut, tt, gt, dt_, dv_, bg):
        _, vjp = jax.vjp(_f_s5_out, yt, ut, tt, gt, dv_, bg)
        return vjp(dt_)

    dyc_a, du_a, dt_glu, dgate, g_dvec_a, g_bglu = _rowwise(
        s5_out_bwd, [yc, u_win, t_glu, gate_win, ds5o], [dvec, bglu],
        [(s5w, F32), (s5w, F32), (s5w, BF16), (s5w, BF16)], [s5w, s5w], tl=128, name="s5_out_bwd")
    grads["l0_s5_b_glu"] = g_bglu
    grads["l0_s5_w_glu"] = _matmul(gl, dt_glu, mode="tn", name="s5_glu_dw")
    dgl = _matmul(dt_glu, wb["w_glu"], mode="nt", name="s5_glu_dx")

    def s5_gelu_bwd(yt, ut, dg, dya, dua, dv_):
        _, vjp = jax.vjp(_f_s5_gelu, yt, ut, dv_)
        dy_, du_, ddv = vjp(dg)
        return dy_ + dya, du_ + dua, ddv

    dyc, du_ab, g_dvec_b = _rowwise(s5_gelu_bwd, [yc, u_win, dgl, dyc_a, du_a], [dvec],
                                    [(s5w, F32), (s5w, F32)], [s5w], tl=128, name="s5_gelu_bwd")
    ds_re = _bdmm([dyc], [tr(cd_re_b)], name="s5_ds_re").reshape(shape3)
    ds_im = _bdmm([dyc], [tr(cd_imn_b)], name="s5_ds_im").reshape(shape3)
    kin_s, kin_u = s5s // nj, s5w // nj
    g_cd_re = _bdmm_tn_sized(s_re, dyc, nj, kin_s, kin_u, 0, "s5_dcd_re")
    g_cd_imn = _bdmm_tn_sized(s_im, dyc, nj, kin_s, kin_u, 0, "s5_dcd_im")
    g_re3, g_im3, g_a_re, g_a_im = _s5_scan_bwd(ds_re, ds_im, s_re3, s_im3, a_re, a_im, "s5_scan_bwd")
    g_re, g_im = g_re3.reshape(length, s5s), g_im3.reshape(length, s5s)
    du = _bdmm([g_re, g_im], [tr(bd_re_b), tr(bd_im_b)], addend=du_ab, out_dtype=BF16, name="s5_du")
    g_bd_re = _bdmm_tn_sized(ug, g_re, nj, kin_u, kin_s, 0, "s5_dbd_re")
    g_bd_im = _bdmm_tn_sized(ug, g_im, nj, kin_u, kin_s, 0, "s5_dbd_im")
    s5_g = s5_vjp((g_a_re, g_a_im, g_bd_re, g_bd_im, g_cd_re, g_cd_imn))
    for nm, g in zip(("lambda_re", "lambda_im", "log_step", "b_re", "b_im", "c_re", "c_im"), s5_g):
        grads["l0_s5_" + nm] = g
    grads["l0_s5_d"] = (g_dvec_a + g_dvec_b).reshape(sm["l0_s5_d"].shape)

    grads["l0_w_in"] = jnp.concatenate(
        [_matmul(h0, du, mode="tn", name="l0_in_dw_u"), _matmul(h0, dgate, mode="tn", name="l0_in_dw_g"),
         _matmul(h0, dz, mode="tn", name="l0_in_dw_z"), _matmul(h0, dxbc, mode="tn", name="l0_in_dw_xbc"),
         _unpad_heads(_matmul(h0, ddtraw, mode="tn", name="l0_in_dw_dt"), hpg)], axis=1)
    dh0 = _matmul(du, wb["w0_ug"], mode="nt", dims=(length, d, s5w), name="l0_in_dx_u")
    dh0 = _matmul(dgate, wb["w0_ug"], mode="nt", dims=(length, d, s5w), b_off=(0, s5w), addend=dh0,
                  name="l0_in_dx_g")
    dh0 = _matmul(dz, wb["w0_z"], mode="nt", addend=dh0, name="l0_in_dx_z")
    dh0 = _matmul(dxbc, wb["w0_xbc"], mode="nt", addend=dh0, name="l0_in_dx_xbc")
    dh0 = _matmul(ddtraw, wb["w0_dt"], mode="nt", addend=dh0, name="l0_in_dx_dt")
    dx, _, grads["l0_norm_w"] = _norm_bwd(x, dh0, dx1, nw0, "l0_norm_bwd")
    return loss_lanes, dx, grads


_ANY = pl.BlockSpec(memory_space=pl.ANY)


def _place():
    x, y, c = lax.axis_index("x"), lax.axis_index("y"), lax.axis_index("c")
    return x, y, c, [(1 - x, y), (x, 1 - y), (1 - x, 1 - y)]


def _remote(src, dst, send_sem, recv_sem, to):
    return pltpu.make_async_remote_copy(src_ref=src, dst_ref=dst, send_sem=send_sem, recv_sem=recv_sem,
                                        device_id=to, device_id_type=MESH)


def _comm_call(body, n_in, out_shape, n_sems, name):
    return pl.pallas_call(
        body, name=name, in_specs=[_ANY] * n_in, out_specs=[_ANY] * len(out_shape), out_shape=out_shape,
        scratch_shapes=[pltpu.SemaphoreType.DMA((k,)) for k in n_sems],
        compiler_params=pltpu.CompilerParams(has_side_effects=True),
    )


def _gather_weights(shards, name):
    n = len(shards)

    def body(*refs):
        ins, outs = refs[:n], refs[n:2 * n]
        send, recv, fsend, frecv = refs[2 * n:]
        x, y, c, chips = _place()
        me = 2 * x + y
        first, passed = [], []
        for a in range(n):
            for k, (px, py) in enumerate(chips):
                cp = _remote(ins[a].at[c], outs[a].at[me, c], send.at[3 * a + k], recv.at[3 * a + k], (px, py, c))
                cp.start()
                first.append(cp)
        for a in range(n):
            for k, (px, py) in enumerate(chips):
                got = outs[a].at[2 * px + py, c]
                _remote(got, got, send.at[3 * a + k], recv.at[3 * a + k], (px, py, c)).wait_recv()
                cp = _remote(got, got, fsend.at[3 * a + k], frecv.at[3 * a + k], (x, y, 1 - c))
                cp.start()
                passed.append(cp)
        for a in range(n):
            for k, (px, py) in enumerate(chips):
                got = outs[a].at[2 * px + py, 1 - c]
                _remote(got, got, fsend.at[3 * a + k], frecv.at[3 * a + k], (x, y, 1 - c)).wait_recv()
        for cp in first + passed:
            cp.wait_send()

    out_shape = [jax.ShapeDtypeStruct((N_SHARD,) + s.shape, s.dtype) for s in shards]
    return _comm_call(body, n, out_shape, [3 * n, 3 * n, 3 * n, 3 * n], name)(*shards)


def _sibling_halves(grads, name):
    n = len(grads)

    def body(*refs):
        ins, outs = refs[:n], refs[n:2 * n]
        send, recv = refs[2 * n:]
        x, y, c, _ = _place()
        copies = []
        for a in range(n):
            for j in range(N_SHARD):
                cp = _remote(ins[a].at[j, 1 - c], outs[a].at[j], send.at[N_SHARD * a + j],
                             recv.at[N_SHARD * a + j], (x, y, 1 - c))
                cp.start()
                copies.append(cp)
        for cp in copies:
            cp.wait()

    out_shape = [jax.ShapeDtypeStruct((N_SHARD,) + g.shape[2:], g.dtype) for g in grads]
    return _comm_call(body, n, out_shape, [N_SHARD * n, N_SHARD * n], name)(*grads)


def _chip_partials(parts, name):
    n = len(parts)

    def body(*refs):
        ins, outs = refs[:n], refs[n:2 * n]
        send, recv = refs[2 * n:]
        x, y, c, chips = _place()
        copies = []
        for a in range(n):
            for k, (px, py) in enumerate(chips):
                cp = _remote(ins[a].at[2 * px + py], outs[a].at[k], send.at[3 * a + k], recv.at[3 * a + k],
                             (px, py, c))
                cp.start()
                copies.append(cp)
        for cp in copies:
            cp.wait()

    out_shape = [jax.ShapeDtypeStruct((3,) + p.shape[1:], p.dtype) for p in parts]
    return _comm_call(body, n, out_shape, [3 * n, 3 * n], name)(*parts)


def _join_halves(halves, name):
    n = len(halves)

    def body(*refs):
        outs = refs[n:2 * n]
        send, recv = refs[2 * n:]
        x, y, c, _ = _place()
        copies = [_remote(outs[a].at[c], outs[a].at[c], send.at[a], recv.at[a], (x, y, 1 - c)) for a in range(n)]
        for cp in copies:
            cp.start()
        for a in range(n):
            copies[a].wait_send()
            got = outs[a].at[1 - c]
            _remote(got, got, send.at[a], recv.at[a], (x, y, 1 - c)).wait_recv()

    return pl.pallas_call(
        body, name=name, in_specs=[_ANY] * n, out_specs=[_ANY] * n,
        out_shape=[jax.ShapeDtypeStruct(h.shape, h.dtype) for h in halves],
        input_output_aliases={a: a for a in range(n)},
        scratch_shapes=[pltpu.SemaphoreType.DMA((n,)), pltpu.SemaphoreType.DMA((n,))],
        compiler_params=pltpu.CompilerParams(has_side_effects=True),
    )(*halves)


def _gather_all(buf, name):
    def body(in_ref, out_ref, send, recv, lsem):
        x, y, c, _ = _place()
        me = 4 * x + 2 * y + c
        local = pltpu.make_async_copy(in_ref, out_ref.at[me], lsem.at[0])
        local.start()
        copies = []
        for k in range(1, N_DEV):
            fx, fy, fc = (k >> 2) & 1, (k >> 1) & 1, k & 1
            peer = (x + fx - 2 * x * fx, y + fy - 2 * y * fy, c + fc - 2 * c * fc)
            cp = _remote(in_ref, out_ref.at[me], send.at[k - 1], recv.at[k - 1], peer)
            cp.start()
            copies.append((cp, 4 * peer[0] + 2 * peer[1] + peer[2]))
        for k, (cp, slot) in enumerate(copies):
            cp.wait_send()
            got = out_ref.at[slot]
            _remote(got, got, send.at[k], recv.at[k], (x, y, c)).wait_recv()
        local.wait()

    out_shape = [jax.ShapeDtypeStruct((N_DEV,) + buf.shape, buf.dtype)]
    return _comm_call(body, 1, out_shape, [N_DEV - 1, N_DEV - 1, 1], name)(buf)[0]


def _sum_slots(buf, name):
    slots, rows, _ = buf.shape
    tr = _pick(rows, (512, 256, 128, 64, 32, 16, 8))

    def body(b_ref, o_ref):
        acc = b_ref[0]
        for s in range(1, slots):
            acc = acc + b_ref[s]
        o_ref[...] = acc

    return pl.pallas_call(
        body, name=name, grid=(rows // tr,),
        in_specs=[pl.BlockSpec((slots, tr, LANES), lambda i: (0, i, 0))],
        out_specs=pl.BlockSpec((tr, LANES), lambda i: (i, 0)),
        out_shape=jax.ShapeDtypeStruct((rows, LANES), F32),
        compiler_params=_params(("parallel",)),
    )(buf)


def _row_tile(cols, n_bufs):
    return max(8, min(512, (24 * 1024 * 1024 // (4 * cols * n_bufs)) // 8 * 8))


def _presum(grad, sib, name):
    ns, _, rh, cols = grad.shape
    tr = _pick(rh, [t for t in (512, 256, 128, 64, 32, 16) if t <= _row_tile(cols, 6)])

    def body(g_ref, r_ref, o_ref):
        o_ref[0] = (g_ref[0, 0] + r_ref[0]).astype(o_ref.dtype)

    return pl.pallas_call(
        body, name=name, grid=(ns, rh // tr),
        in_specs=[pl.BlockSpec((1, 1, tr, cols), lambda j, i: (j, lax.axis_index("c"), i, 0)),
                  pl.BlockSpec((1, tr, cols), lambda j, i: (j, i, 0))],
        out_specs=pl.BlockSpec((1, tr, cols), lambda j, i: (j, i, 0)),
        out_shape=jax.ShapeDtypeStruct((ns, rh, cols), BF16),
        compiler_params=_params(("parallel", "parallel")),
    )(grad, sib)


def _finish_half(grad, sib, others, name):
    _, _, rh, cols = grad.shape
    tr = _pick(rh, [t for t in (512, 256, 128, 64, 32, 16) if t <= _row_tile(cols, 10)])

    def body(g_ref, r_ref, q_ref, o_ref):
        acc = g_ref[0, 0] + r_ref[0]
        for k in range(3):
            acc = acc + q_ref[k].astype(F32)
        o_ref[0] = acc

    core = lambda: lax.axis_index("c")
    chip = lambda: 2 * lax.axis_index("x") + lax.axis_index("y")
    return pl.pallas_call(
        body, name=name, grid=(rh // tr,),
        in_specs=[pl.BlockSpec((1, 1, tr, cols), lambda i: (chip(), core(), i, 0)),
                  pl.BlockSpec((1, tr, cols), lambda i: (chip(), i, 0)),
                  pl.BlockSpec((3, tr, cols), lambda i: (0, i, 0))],
        out_specs=pl.BlockSpec((1, tr, cols), lambda i: (core(), i, 0)),
        out_shape=jax.ShapeDtypeStruct((2, rh, cols), F32),
        compiler_params=_params(("parallel",)),
    )(grad, sib, others)


def _cast_bf16(w, name):
    cols = w.shape[1]
    return _rowwise(lambda t: t, [w], [], [(cols, BF16)], tl=_row_tile(cols, 4), name=name)[0]


_WEIGHTS = ("l0_norm_w", "l0_w_in", "l0_s5_lambda_re", "l0_s5_lambda_im", "l0_s5_log_step", "l0_s5_b_re",
            "l0_s5_b_im", "l0_s5_c_re", "l0_s5_c_im", "l0_s5_d", "l0_s5_w_glu", "l0_s5_b_glu", "l0_ssd_conv_w",
            "l0_ssd_conv_b", "l0_ssd_dt_bias", "l0_ssd_a_log", "l0_ssd_d", "l0_ssd_norm_w", "l0_w_out",
            "l1_norm_w", "l1_w_in", "l1_fox_b_f", "l1_w_out", "final_norm_w")
_COL_SHARDED = ("l0_w_in", "l1_w_in")
_ROW_SHARDED = ("l0_s5_w_glu", "l0_w_out", "l1_w_out")
_BIG = ("l0_w_in", "l0_s5_w_glu", "l0_w_out", "l1_w_in", "l1_w_out")
_CONV = "l0_ssd_conv_w"
_SMALL = tuple(n for n in _WEIGHTS if n not in _BIG and n != _CONV)


def _pack(arrays):
    flat = jnp.concatenate([a.reshape(-1).astype(F32) for a in arrays])
    size = flat.shape[0]
    padded = -(-size // (512 * LANES)) * (512 * LANES)
    return jnp.pad(flat, (0, padded - size)).reshape(-1, LANES)


def _unpack(buf, like):
    flat = buf.reshape(-1)
    out, pos = [], 0
    for a in like:
        out.append(flat[pos:pos + a.size].reshape(a.shape))
        pos += a.size
    return out


def _step(p):
    x, tgt = p["x"][0], p["loss_target"][0]
    d = x.shape[1]
    chip = 2 * lax.axis_index("x") + lax.axis_index("y")

    halves = lambda w: w.reshape((2, w.shape[0] // 2) + w.shape[1:])
    shards = [halves(_cast_bf16(p[n], "cast_" + n)) for n in _BIG] + [halves(p[_CONV])]
    gathered = {n: lax.dynamic_update_index_in_dim(g, s[None], chip, 0)
                for n, g, s in zip(_BIG + (_CONV,), _gather_weights(shards, "gather_weights"), shards)}

    def whole(n):
        g = gathered[n]
        rows, cols = 2 * g.shape[2], g.shape[3]
        if n in _ROW_SHARDED:
            return g.reshape(N_SHARD * rows, cols)
        return g.reshape(N_SHARD, rows, cols).transpose(1, 0, 2).reshape(rows, N_SHARD * cols)

    w0, w1 = whole("l0_w_in"), whole("l1_w_in")
    mix = 2 * d
    s5w = mix // 4
    ssdw = mix - s5w
    n_ssd = ssdw // SSD_HEAD_DIM
    xbcw = ssdw + 2 * SSD_GROUPS * SSD_STATE
    fw = p["l1_w_out"].shape[0] * N_SHARD
    n_fox = fw // FOX_HEAD_DIM
    o1, o2, o3 = 2 * s5w, 2 * s5w + ssdw, 2 * s5w + ssdw + xbcw
    wb = {
        "w0_ug": w0[:, :o1], "w0_z": w0[:, o1:o2], "w0_xbc": w0[:, o2:o3],
        "w0_dt": _pad_heads(w0[:, o3:], n_ssd // SSD_GROUPS),
        "w_glu": whole("l0_s5_w_glu"), "w0_out": whole("l0_w_out"),
        "w1_main": w1[:, :4 * fw], "w1_f": jnp.pad(w1[:, 4 * fw:], ((0, 0), (0, LANES - n_fox))),
        "w1_out": whole("l1_w_out"),
    }
    sm = {n: p[n] for n in _SMALL}
    sm[_CONV] = whole(_CONV)

    loss_lanes, dx, grads = _local_step(x, tgt, wb, sm)

    small_like = [p[n] for n in _SMALL] + [sm[_CONV], jnp.zeros((1,), F32)]
    small_sum = _sum_slots(_gather_all(_pack([grads[n] for n in _SMALL] + [grads[_CONV], jnp.sum(loss_lanes)]),
                                       "gather_small"), "sum_small")
    *small_grads, conv_grad, loss = _unpack(small_sum, small_like)
    taps, ccols = p[_CONV].shape
    conv_grad = lax.dynamic_slice(conv_grad, (0, chip * ccols), (taps, ccols))
    final = dict(zip(_SMALL, small_grads))
    final[_CONV] = conv_grad

    def by_shard(n):
        g = grads[n]
        if n in _ROW_SHARDED:
            return g.reshape(N_SHARD, 2, g.shape[0] // (2 * N_SHARD), g.shape[1])
        rows, cols = g.shape[0], g.shape[1] // N_SHARD
        return g.reshape(2, rows // 2, N_SHARD, cols).transpose(2, 0, 1, 3)

    big = [by_shard(n) for n in _BIG]
    sib = _sibling_halves(big, "reduce_sibling")
    parts = [_presum(g, s, "presum_" + n) for n, g, s in zip(_BIG, big, sib)]
    others = _chip_partials(parts, "reduce_chips")
    done = [_finish_half(g, s, q, "finish_" + n) for n, g, s, q in zip(_BIG, big, sib, others)]
    for n, full in zip(_BIG, _join_halves(done, "join_halves")):
        final[n] = full.reshape(p[n].shape)

    delta, new_m, new_v = {}, {}, {}
    for n in _BIG:
        delta[n], new_m[n], new_v[n] = _adamw(p[n], final[n], p["m_" + n], p["v_" + n], "adamw_" + n)
    rest = _SMALL + (_CONV,)
    packed = [_pack([t[n] for n in rest]) for t in
              ({n: p[n] for n in rest}, final, {n: p["m_" + n] for n in rest}, {n: p["v_" + n] for n in rest})]
    for dst, buf in zip((delta, new_m, new_v), _adamw(*packed, "adamw_small")):
        dst.update(zip(rest, _unpack(buf, [p[n] for n in rest])))

    outs = [loss.reshape(()), dx[None]]
    for group in (final, delta, new_m, new_v):
        outs += [group[n].reshape(p[n].shape) for n in _WEIGHTS]
    return tuple(outs)


_INPUTS = ("x",) + _WEIGHTS + ("loss_target",) + tuple("m_" + n for n in _WEIGHTS) + tuple("v_" + n for n in _WEIGHTS)


def kernel(x, l0_norm_w, l0_w_in, l0_s5_lambda_re, l0_s5_lambda_im, l0_s5_log_step, l0_s5_b_re, l0_s5_b_im, l0_s5_c_re,
           l0_s5_c_im, l0_s5_d, l0_s5_w_glu, l0_s5_b_glu, l0_ssd_conv_w, l0_ssd_conv_b, l0_ssd_dt_bias,
           l0_ssd_a_log, l0_ssd_d, l0_ssd_norm_w, l0_w_out, l1_norm_w, l1_w_in, l1_fox_b_f, l1_w_out,
           final_norm_w, loss_target, m_l0_norm_w, m_l0_w_in, m_l0_s5_lambda_re, m_l0_s5_lambda_im,
           m_l0_s5_log_step, m_l0_s5_b_re, m_l0_s5_b_im, m_l0_s5_c_re, m_l0_s5_c_im, m_l0_s5_d,
           m_l0_s5_w_glu, m_l0_s5_b_glu, m_l0_ssd_conv_w, m_l0_ssd_conv_b, m_l0_ssd_dt_bias, m_l0_ssd_a_log,
           m_l0_ssd_d, m_l0_ssd_norm_w, m_l0_w_out, m_l1_norm_w, m_l1_w_in, m_l1_fox_b_f, m_l1_w_out,
           m_final_norm_w, v_l0_norm_w, v_l0_w_in, v_l0_s5_lambda_re, v_l0_s5_lambda_im, v_l0_s5_log_step,
           v_l0_s5_b_re, v_l0_s5_b_im, v_l0_s5_c_re, v_l0_s5_c_im, v_l0_s5_d, v_l0_s5_w_glu, v_l0_s5_b_glu,
           v_l0_ssd_conv_w, v_l0_ssd_conv_b, v_l0_ssd_dt_bias, v_l0_ssd_a_log, v_l0_ssd_d, v_l0_ssd_norm_w,
           v_l0_w_out, v_l1_norm_w, v_l1_w_in, v_l1_fox_b_f, v_l1_w_out, v_final_norm_w):
    values = (x, l0_norm_w, l0_w_in, l0_s5_lambda_re, l0_s5_lambda_im, l0_s5_log_step, l0_s5_b_re, l0_s5_b_im,
              l0_s5_c_re, l0_s5_c_im, l0_s5_d, l0_s5_w_glu, l0_s5_b_glu, l0_ssd_conv_w, l0_ssd_conv_b,
              l0_ssd_dt_bias, l0_ssd_a_log, l0_ssd_d, l0_ssd_norm_w, l0_w_out, l1_norm_w, l1_w_in,
              l1_fox_b_f, l1_w_out, final_norm_w, loss_target, m_l0_norm_w, m_l0_w_in,
              m_l0_s5_lambda_re, m_l0_s5_lambda_im, m_l0_s5_log_step, m_l0_s5_b_re, m_l0_s5_b_im,
              m_l0_s5_c_re, m_l0_s5_c_im, m_l0_s5_d, m_l0_s5_w_glu, m_l0_s5_b_glu, m_l0_ssd_conv_w,
              m_l0_ssd_conv_b, m_l0_ssd_dt_bias, m_l0_ssd_a_log, m_l0_ssd_d, m_l0_ssd_norm_w,
              m_l0_w_out, m_l1_norm_w, m_l1_w_in, m_l1_fox_b_f, m_l1_w_out, m_final_norm_w, v_l0_norm_w,
              v_l0_w_in, v_l0_s5_lambda_re, v_l0_s5_lambda_im, v_l0_s5_log_step, v_l0_s5_b_re,
              v_l0_s5_b_im, v_l0_s5_c_re, v_l0_s5_c_im, v_l0_s5_d, v_l0_s5_w_glu, v_l0_s5_b_glu,
              v_l0_ssd_conv_w, v_l0_ssd_conv_b, v_l0_ssd_dt_bias, v_l0_ssd_a_log, v_l0_ssd_d,
              v_l0_ssd_norm_w, v_l0_w_out, v_l1_norm_w, v_l1_w_in, v_l1_fox_b_f, v_l1_w_out,
              v_final_norm_w)
    return _step(dict(zip(_INPUTS, values)))
```

```python
import functools
import math

import jax
import jax.numpy as jnp
from jax import lax
from jax.experimental import pallas as pl
from jax.experimental.pallas import tpu as pltpu

F32 = jnp.float32
BF16 = jnp.bfloat16

S5_GROUP = 16
S5_STATE = 64
S5_EIG_CLIP = -1e-4
SSD_HEAD_DIM = 64
SSD_GROUPS = 8
SSD_STATE = 128
SSD_CONV = 4
SSD_CHUNK = 128
FOX_HEAD_DIM = 128
FOX_TILE = 1024
NORM_EPS = 1e-5
ADAM_LR = 0.001
ADAM_B1 = 0.9
ADAM_B2 = 0.999
ADAM_EPS = 1e-08
ADAM_WD = 0.01
ADAM_STEP = 10

N_SHARD = 4
N_DEV = 8
LANES = 128
VMEM_LIMIT = 56 * 1024 * 1024
MESH = pl.DeviceIdType.MESH


def _pick(dim, prefs, offs=()):
    for p in prefs:
        if dim % p == 0 and all(o % p == 0 for o in offs):
            return p
    return dim


def _params(sem=None, vmem=VMEM_LIMIT):
    return pltpu.CompilerParams(dimension_semantics=sem, vmem_limit_bytes=vmem)


class _Rider:
    def __init__(self, inputs, out_shape, sems, start, finish):
        self.inputs, self.out_shape, self.sems = list(inputs), list(out_shape), list(sems)
        self.start, self.finish = start, finish


def _hosted_call(body, *, name, grid, in_specs, out_specs, out_shape, scratch_shapes, sem, args, rider=None):
    single = not isinstance(out_shape, (list, tuple))
    out_specs = [out_specs] if single else list(out_specs)
    out_shape = [out_shape] if single else list(out_shape)
    if rider is None:
        res = pl.pallas_call(body, name=name, grid=grid, in_specs=in_specs, out_specs=out_specs,
                             out_shape=out_shape, scratch_shapes=scratch_shapes,
                             compiler_params=_params(sem))(*args)
        return (res[0] if single else res), []
    n_in, n_out, n_scr = len(in_specs), len(out_shape), len(scratch_shapes)
    n_rin, n_rout = len(rider.inputs), len(rider.out_shape)

    def carried(*refs):
        ins, refs = refs[:n_in], refs[n_in:]
        rin, refs = refs[:n_rin], refs[n_rin:]
        outs, refs = refs[:n_out], refs[n_out:]
        rout, refs = refs[:n_rout], refs[n_rout:]
        scr, rsem = refs[:n_scr], refs[n_scr:]
        ids = [pl.program_id(k) for k in range(len(grid))]
        first = functools.reduce(jnp.logical_and, [i == 0 for i in ids])
        last = functools.reduce(jnp.logical_and, [i == g - 1 for i, g in zip(ids, grid)])

        @pl.when(first)
        def _():
            rider.start(rin, rout, rsem)

        body(*ins, *outs, *scr)

        @pl.when(last)
        def _():
            rider.finish(rin, rout, rsem)

    res = pl.pallas_call(
        carried, name=name, grid=grid,
        in_specs=list(in_specs) + [_ANY] * n_rin, out_specs=out_specs + [_ANY] * n_rout,
        out_shape=out_shape + rider.out_shape,
        scratch_shapes=list(scratch_shapes) + [pltpu.SemaphoreType.DMA((k,)) for k in rider.sems],
        compiler_params=pltpu.CompilerParams(dimension_semantics=("arbitrary",) * len(grid),
                                             vmem_limit_bytes=VMEM_LIMIT, has_side_effects=True),
    )(*args, *rider.inputs)
    outs = res[:n_out]
    return (outs[0] if single else outs), list(res[n_out:])


def _matmul(a, b, *, mode="nn", dims=None, a_off=(0, 0), b_off=(0, 0), addend=None,
            out_dtype=F32, rider=None, name):
    if dims is None:
        if mode == "nn":
            dims = (a.shape[0], b.shape[1], a.shape[1])
        elif mode == "nt":
            dims = (a.shape[0], b.shape[0], a.shape[1])
        else:
            dims = (a.shape[1], b.shape[1], a.shape[0])
    m, n, k = dims
    if mode == "nn":
        om, on, ok = (a_off[0],), (b_off[1],), (a_off[1], b_off[0])
    elif mode == "nt":
        om, on, ok = (a_off[0],), (b_off[0],), (a_off[1], b_off[1])
    else:
        om, on, ok = (a_off[1],), (b_off[1],), (a_off[0], b_off[0])
    tm = _pick(m, (1024, 512, 256, 128), om)
    tn = _pick(n, (1024, 768, 512, 384, 256, 128), on)
    tk = _pick(k, (1024, 512, 256, 128), ok)
    nk = k // tk
    if mode == "nn":
        a_blk, a_div = (tm, tk), (tm, tk)
        b_blk, b_div = (tk, tn), (tk, tn)
        a_map = lambda i, j, kk: (i + a_off[0] // tm, kk + a_off[1] // tk)
        b_map = lambda i, j, kk: (kk + b_off[0] // tk, j + b_off[1] // tn)
        dn = (((1,), (0,)), ((), ()))
    elif mode == "nt":
        a_blk, a_div = (tm, tk), (tm, tk)
        b_blk, b_div = (tn, tk), (tn, tk)
        a_map = lambda i, j, kk: (i + a_off[0] // tm, kk + a_off[1] // tk)
        b_map = lambda i, j, kk: (j + b_off[0] // tn, kk + b_off[1] // tk)
        dn = (((1,), (1,)), ((), ()))
    else:
        a_blk, a_div = (tk, tm), (tk, tm)
        b_blk, b_div = (tk, tn), (tk, tn)
        a_map = lambda i, j, kk: (kk + a_off[0] // tk, i + a_off[1] // tm)
        b_map = lambda i, j, kk: (kk + b_off[0] // tk, j + b_off[1] // tn)
        dn = (((0,), (0,)), ((), ()))
    assert a_off[0] % a_div[0] == 0 and a_off[1] % a_div[1] == 0, (name, a_off, a_div)
    assert b_off[0] % b_div[0] == 0 and b_off[1] % b_div[1] == 0, (name, b_off, b_div)
    has_add = addend is not None

    def body(*refs):
        if has_add:
            a_ref, b_ref, c_ref, o_ref, acc_ref = refs
        else:
            a_ref, b_ref, o_ref, acc_ref = refs
        kk = pl.program_id(2)

        @pl.when(kk == 0)
        def _():
            acc_ref[...] = jnp.zeros_like(acc_ref)

        acc_ref[...] += lax.dot_general(a_ref[...].astype(BF16), b_ref[...].astype(BF16), dn,
                                        preferred_element_type=F32)

        @pl.when(kk == nk - 1)
        def _():
            r = acc_ref[...]
            if has_add:
                r = r + c_ref[...].astype(F32)
            o_ref[...] = r.astype(o_ref.dtype)

    in_specs = [pl.BlockSpec(a_blk, a_map), pl.BlockSpec(b_blk, b_map)]
    args = [a, b]
    if has_add:
        in_specs.append(pl.BlockSpec((tm, tn), lambda i, j, kk: (i, j)))
        args.append(addend)
    out, carried = _hosted_call(
        body, name=name, grid=(m // tm, n // tn, nk),
        in_specs=in_specs, out_specs=pl.BlockSpec((tm, tn), lambda i, j, kk: (i, j)),
        out_shape=jax.ShapeDtypeStruct((m, n), out_dtype),
        scratch_shapes=[pltpu.VMEM((tm, tn), F32)],
        sem=("parallel", "parallel", "arbitrary"), args=args, rider=rider)
    return out if rider is None else (out, carried)


def _rowwise(fn, rows, params, out_rows, out_accs=(), *, tl, ncol=1, name):
    rows = [r if isinstance(r, tuple) else (r, r.shape[1] // ncol, 0) for r in rows]
    n_rows, n_par, n_or, n_oa = len(rows), len(params), len(out_rows), len(out_accs)
    length = rows[0][0].shape[0]
    tl = _pick(length, [t for t in (1024, 512, 256, 128, 64, 32, 16, 8) if t <= tl])

    def body(*refs):
        row_refs = refs[:n_rows]
        par_refs = refs[n_rows:n_rows + n_par]
        or_refs = refs[n_rows + n_par:n_rows + n_par + n_or]
        oa_refs = refs[n_rows + n_par + n_or:]
        outs = fn(*[r[...] for r in row_refs], *[p[...] for p in par_refs])
        if not isinstance(outs, (tuple, list)):
            outs = (outs,)
        for r, v in zip(or_refs, outs[:n_or]):
            r[...] = v.astype(r.dtype)
        if n_oa:
            @pl.when(pl.program_id(1) == 0)
            def _():
                for r in oa_refs:
                    r[...] = jnp.zeros_like(r)

            for r, v in zip(oa_refs, outs[n_or:]):
                r[...] += v.astype(F32)

    in_specs = [pl.BlockSpec((tl, w), functools.partial(lambda j, i, b0: (i, b0 + j), b0=b0))
                for (_, w, b0) in rows]
    in_specs += [pl.BlockSpec((p.shape[0], p.shape[1] // ncol), lambda j, i: (0, j)) for p in params]
    out_specs = [pl.BlockSpec((tl, w), lambda j, i: (i, j)) for (w, _) in out_rows]
    out_specs += [pl.BlockSpec((1, w), lambda j, i: (0, j)) for w in out_accs]
    out_shape = [jax.ShapeDtypeStruct((length, ncol * w), dt) for (w, dt) in out_rows]
    out_shape += [jax.ShapeDtypeStruct((1, ncol * w), F32) for w in out_accs]
    res = pl.pallas_call(
        body, name=name, grid=(ncol, length // tl),
        in_specs=in_specs, out_specs=out_specs, out_shape=out_shape,
        compiler_params=_params(("parallel", "arbitrary" if n_oa else "parallel")),
    )(*[r[0] for r in rows], *params)
    return res


def _bdmm(xs, ws, *, addend=None, out_dtype=F32, name):
    nj, kin, kout = ws[0].shape
    xs = [x if isinstance(x, tuple) else (x, 0) for x in xs]
    length = xs[0][0].shape[0]
    tl = _pick(length, (512, 256, 128))
    n_x = len(xs)
    has_add = addend is not None

    def body(*refs):
        x_refs = refs[:n_x]
        w_refs = refs[n_x:2 * n_x]
        o_ref = refs[-1]
        acc = None
        for xr, wr in zip(x_refs, w_refs):
            t = jnp.dot(xr[...].astype(BF16), wr[0], preferred_element_type=F32)
            acc = t if acc is None else acc + t
        if has_add:
            acc = acc + refs[2 * n_x][...].astype(F32)
        o_ref[...] = acc.astype(o_ref.dtype)

    in_specs = [pl.BlockSpec((tl, kin), functools.partial(lambda i, j, b0: (i, b0 + j), b0=b0)) for (_, b0) in xs]
    in_specs += [pl.BlockSpec((1, kin, kout), lambda i, j: (j, 0, 0)) for _ in ws]
    args = [x[0] for x in xs] + list(ws)
    if has_add:
        in_specs.append(pl.BlockSpec((tl, kout), lambda i, j: (i, j)))
        args.append(addend)
    return pl.pallas_call(
        body, name=name, grid=(length // tl, nj),
        in_specs=in_specs, out_specs=pl.BlockSpec((tl, kout), lambda i, j: (i, j)),
        out_shape=jax.ShapeDtypeStruct((length, nj * kout), out_dtype),
        compiler_params=_params(("parallel", "parallel")),
    )(*args)


def _bdmm_tn_sized(x, g, nj, kin, kout, x_first, name):
    length = x.shape[0]
    tl = _pick(length, (512, 256, 128))
    nt = length // tl

    def body(x_ref, g_ref, o_ref):
        @pl.when(pl.program_id(1) == 0)
        def _():
            o_ref[...] = jnp.zeros_like(o_ref)

        o_ref[0] += lax.dot_general(x_ref[...].astype(BF16), g_ref[...].astype(BF16),
                                    (((0,), (0,)), ((), ())), preferred_element_type=F32)

    return pl.pallas_call(
        body, name=name, grid=(nj, nt),
        in_specs=[pl.BlockSpec((tl, kin), lambda j, t: (t, x_first + j)),
                  pl.BlockSpec((tl, kout), lambda j, t: (t, j))],
        out_specs=pl.BlockSpec((1, kin, kout), lambda j, t: (j, 0, 0)),
        out_shape=jax.ShapeDtypeStruct((nj, kin, kout), F32),
        compiler_params=_params(("parallel", "arbitrary")),
    )(x, g)


def _f_norm(x, w):
    return x * lax.rsqrt(jnp.mean(x * x, axis=-1, keepdims=True) + NORM_EPS) * w


def _gelu(y):
    return 0.5 * y * (1.0 + jnp.tanh(math.sqrt(2.0 / math.pi) * (y + 0.044715 * (y * y * y))))


def _sigmoid(x):
    return 1.0 / (1.0 + jnp.exp(-x))


def _silu(x):
    return x * _sigmoid(x)


def _softplus(x):
    return jnp.maximum(x, 0.0) + jnp.log(1.0 + jnp.exp(-jnp.abs(x)))


def _f_s5_gelu(yc, u, dvec):
    return _gelu(yc + dvec * u)


def _f_s5_out(yc, u, t, gate, dvec, bglu):
    gl = _gelu(yc + dvec * u)
    return gl * _sigmoid(t + bglu) * _silu(gate)


def _f_ssd_out(y, xs, z, dpar, nw):
    v = (y + dpar * xs) * _silu(z)
    return v * lax.rsqrt(jnp.mean(v * v, axis=-1, keepdims=True) + NORM_EPS) * nw


def _f_fox_out(att, gate):
    return att * _silu(gate)


def _norm_fwd(x, w, name):
    return _rowwise(lambda xt, wt: _f_norm(xt, wt), [x], [w], [(x.shape[1], BF16)], tl=256, name=name)[0]


def _norm_bwd(x, dh, dres, w, name):
    d = x.shape[1]

    def fn(xt, dht, drt, wt):
        _, vjp = jax.vjp(_f_norm, xt, wt)
        dx, dw = vjp(dht)
        dx = dx + drt
        return dx, dx, dw

    return _rowwise(fn, [x, dh, dres], [w], [(d, F32), (d, BF16)], [d], tl=128, name=name)


def _adamw_math(w, g, m, v):
    m = ADAM_B1 * m + (1.0 - ADAM_B1) * g
    v = ADAM_B2 * v + (1.0 - ADAM_B2) * jnp.square(g)
    m_hat = m / (1.0 - ADAM_B1 ** ADAM_STEP)
    v_hat = v / (1.0 - ADAM_B2 ** ADAM_STEP)
    delta = -ADAM_LR * (m_hat / (jnp.sqrt(v_hat) + ADAM_EPS) + ADAM_WD * w)
    return delta, m, v


def _adamw(w, g, m, v, name):
    c = w.shape[1]
    tl = max(8, min(256, (2 * 1024 * 1024 // (4 * c)) // 8 * 8))
    return _rowwise(_adamw_math, [w, g, m, v], [], [(c, F32)] * 3, tl=tl, name=name)


def _s5_scan_fwd(bu_re, bu_im, a_re, a_im, name):
    length, rows, _ = bu_re.shape
    rb = _pick(rows, (32, 16, 8))
    tl = _pick(length, (64, 32, 16, 8))

    def body(bur_ref, bui_ref, ar_ref, ai_ref, sr_ref, si_ref, st_ref):
        @pl.when(pl.program_id(1) == 0)
        def _():
            st_ref[...] = jnp.zeros_like(st_ref)

        ar = ar_ref[...]
        ai = ai_ref[...]

        def step(l, carry):
            sr, si = carry
            nr = ar * sr - ai * si + bur_ref[l]
            ni = ar * si + ai * sr + bui_ref[l]
            sr_ref[l] = nr
            si_ref[l] = ni
            return nr, ni

        sr, si = lax.fori_loop(0, tl, step, (st_ref[0], st_ref[1]))
        st_ref[0] = sr
        st_ref[1] = si

    blk = pl.BlockSpec((tl, rb, LANES), lambda cb, t: (t, cb, 0))
    ablk = pl.BlockSpec((rb, LANES), lambda cb, t: (cb, 0))
    return pl.pallas_call(
        body, name=name, grid=(rows // rb, length // tl),
        in_specs=[blk, blk, ablk, ablk], out_specs=[blk, blk],
        out_shape=[jax.ShapeDtypeStruct(bu_re.shape, F32)] * 2,
        scratch_shapes=[pltpu.VMEM((2, rb, LANES), F32)],
        compiler_params=_params(("parallel", "arbitrary")),
    )(bu_re, bu_im, a_re, a_im)


def _s5_scan_bwd(ds_re, ds_im, s_re, s_im, a_re, a_im, name):
    length, rows, _ = ds_re.shape
    rb = _pick(rows, (32, 16, 8))
    tl = _pick(length, (64, 32, 16, 8))
    nt = length // tl

    def body(dsr_ref, dsi_ref, sr_ref, si_ref, pr_ref, pi_ref, ar_ref, ai_ref,
             gr_ref, gi_ref, dar_ref, dai_ref, st_ref):
        t = pl.program_id(1)

        @pl.when(t == 0)
        def _():
            st_ref[...] = jnp.zeros_like(st_ref)
            dar_ref[...] = jnp.zeros_like(dar_ref)
            dai_ref[...] = jnp.zeros_like(dai_ref)

        ar = ar_ref[...]
        ai = ai_ref[...]

        def adj(l, gr, gi):
            ngr = dsr_ref[l] + ar * gr + ai * gi
            ngi = dsi_ref[l] + ar * gi - ai * gr
            gr_ref[l] = ngr
            gi_ref[l] = ngi
            return ngr, ngi

        def step(idx, carry):
            gr, gi, dar, dai = carry
            l = tl - 1 - idx
            gr, gi = adj(l, gr, gi)
            pr = sr_ref[l - 1]
            pi = si_ref[l - 1]
            dar = dar + gr * pr + gi * pi
            dai = dai + gi * pr - gr * pi
            return gr, gi, dar, dai

        zero = jnp.zeros((rb, LANES), F32)
        gr, gi, dar, dai = lax.fori_loop(0, tl - 1, step, (st_ref[0], st_ref[1], zero, zero))
        gr, gi = adj(0, gr, gi)
        first = (t == nt - 1)
        pr = jnp.where(first, 0.0, pr_ref[0])
        pi = jnp.where(first, 0.0, pi_ref[0])
        dar = dar + gr * pr + gi * pi
        dai = dai + gi * pr - gr * pi
        st_ref[0] = gr
        st_ref[1] = gi
        dar_ref[...] += dar
        dai_ref[...] += dai

    blk = pl.BlockSpec((tl, rb, LANES), lambda cb, t: (nt - 1 - t, cb, 0))
    prev = pl.BlockSpec((1, rb, LANES), lambda cb, t: (jnp.maximum((nt - 1 - t) * tl - 1, 0), cb, 0))
    ablk = pl.BlockSpec((rb, LANES), lambda cb, t: (cb, 0))
    return pl.pallas_call(
        body, name=name, grid=(rows // rb, nt),
        in_specs=[blk, blk, blk, blk, prev, prev, ablk, ablk],
        out_specs=[blk, blk, ablk, ablk],
        out_shape=[jax.ShapeDtypeStruct(ds_re.shape, F32)] * 2 + [jax.ShapeDtypeStruct(a_re.shape, F32)] * 2,
        scratch_shapes=[pltpu.VMEM((2, rb, LANES), F32)],
        compiler_params=_params(("parallel", "arbitrary")),
    )(ds_re, ds_im, s_re, s_im, s_re, s_im, a_re, a_im)


def _s5_prepare(lam_re, lam_im, log_step, b_re, b_im, c_re, c_im):
    groups, state = lam_re.shape
    lr = jnp.minimum(lam_re, S5_EIG_CLIP)
    li = lam_im
    step = jnp.exp(log_step)[:, None]
    mag = jnp.exp(lr * step)
    ab_re = mag * jnp.cos(li * step)
    ab_im = mag * jnp.sin(li * step)
    denom = lr * lr + li * li
    nr = ab_re - 1.0
    ni = ab_im
    coef_re = (nr * lr + ni * li) / denom
    coef_im = (ni * lr - nr * li) / denom
    bb_re = coef_re[..., None] * b_re - coef_im[..., None] * b_im
    bb_im = coef_re[..., None] * b_im + coef_im[..., None] * b_re
    per = LANES // S5_GROUP
    nj = groups // per
    eye = jnp.eye(per, dtype=F32)

    def in_map(bb):
        return jnp.einsum('jgph,gk->jghkp', bb.reshape(nj, per, state, S5_GROUP), eye).reshape(
            nj, per * S5_GROUP, per * state)

    def out_map(cc):
        return jnp.einsum('jghp,gk->jgpkh', cc.reshape(nj, per, S5_GROUP, state), eye).reshape(
            nj, per * state, per * S5_GROUP)

    shape2 = (groups * state // LANES, LANES)
    return (ab_re.reshape(shape2), ab_im.reshape(shape2), in_map(bb_re), in_map(bb_im),
            out_map(c_re), -out_map(c_im))


def _shift_down(cur, prev8, j):
    rolled = pltpu.roll(cur, j, 0)
    pr = pltpu.roll(prev8, j, 0)
    row = lax.broadcasted_iota(jnp.int32, cur.shape, 0)
    return jnp.where(row < j, jnp.tile(pr, (cur.shape[0] // 8, 1)), rolled)


def _shift_up(cur, next8, j):
    tl = cur.shape[0]
    rolled = pltpu.roll(cur, tl - j, 0)
    nx = pltpu.roll(next8, 8 - j, 0)
    row = lax.broadcasted_iota(jnp.int32, cur.shape, 0)
    return jnp.where(row >= tl - j, jnp.tile(nx, (tl // 8, 1)), rolled)


def _conv_tiles(length, ch):
    return _pick(length, (256, 128, 64, 32, 16, 8)), _pick(ch, (1024, 512, 256, 128))


def _conv_fwd(xbc, w, b, name):
    length, ch = xbc.shape
    tl, tc = _conv_tiles(length, ch)

    def body(x_ref, p_ref, w_ref, b_ref, o_ref):
        cur = x_ref[...]
        prev8 = jnp.where(pl.program_id(1) == 0, 0.0, p_ref[...])
        pre = b_ref[...] + w_ref[SSD_CONV - 1:SSD_CONV, :] * cur
        for j in range(1, SSD_CONV):
            pre = pre + w_ref[SSD_CONV - 1 - j:SSD_CONV - j, :] * _shift_down(cur, prev8, j)
        o_ref[...] = _silu(pre)

    return pl.pallas_call(
        body, name=name, grid=(ch // tc, length // tl),
        in_specs=[pl.BlockSpec((tl, tc), lambda c, i: (i, c)),
                  pl.BlockSpec((8, tc), lambda c, i: (jnp.maximum(i * (tl // 8) - 1, 0), c)),
                  pl.BlockSpec((SSD_CONV, tc), lambda c, i: (0, c)),
                  pl.BlockSpec((1, tc), lambda c, i: (0, c))],
        out_specs=pl.BlockSpec((tl, tc), lambda c, i: (i, c)),
        out_shape=jax.ShapeDtypeStruct((length, ch), F32),
        compiler_params=_params(("parallel", "parallel")),
    )(xbc, xbc, w, b)


def _conv_bwd_pre(dxc, xbc, w, b, name):
    length, ch = xbc.shape
    tl, tc = _conv_tiles(length, ch)

    def body(d_ref, x_ref, p_ref, w_ref, b_ref, o_ref, dw_ref, db_ref):
        @pl.when(pl.program_id(1) == 0)
        def _():
            dw_ref[...] = jnp.zeros_like(dw_ref)
            db_ref[...] = jnp.zeros_like(db_ref)

        cur = x_ref[...]
        prev8 = jnp.where(pl.program_id(1) == 0, 0.0, p_ref[...])
        shifted = [cur] + [_shift_down(cur, prev8, j) for j in range(1, SSD_CONV)]
        pre = b_ref[...]
        for j in range(SSD_CONV):
            pre = pre + w_ref[SSD_CONV - 1 - j:SSD_CONV - j, :] * shifted[j]
        sg = _sigmoid(pre)
        dpre = d_ref[...] * (sg * (1.0 + pre * (1.0 - sg)))
        o_ref[...] = dpre
        db_ref[...] += jnp.sum(dpre, axis=0, keepdims=True)
        row = lax.broadcasted_iota(jnp.int32, (SSD_CONV, tc), 0)
        dw = jnp.zeros((SSD_CONV, tc), F32)
        for j in range(SSD_CONV):
            dw = dw + jnp.where(row == SSD_CONV - 1 - j, jnp.sum(dpre * shifted[j], axis=0, keepdims=True), 0.0)
        dw_ref[...] += dw

    return pl.pallas_call(
        body, name=name, grid=(ch // tc, length // tl),
        in_specs=[pl.BlockSpec((tl, tc), lambda c, i: (i, c)),
                  pl.BlockSpec((tl, tc), lambda c, i: (i, c)),
                  pl.BlockSpec((8, tc), lambda c, i: (jnp.maximum(i * (tl // 8) - 1, 0), c)),
                  pl.BlockSpec((SSD_CONV, tc), lambda c, i: (0, c)),
                  pl.BlockSpec((1, tc), lambda c, i: (0, c))],
        out_specs=[pl.BlockSpec((tl, tc), lambda c, i: (i, c)),
                   pl.BlockSpec((SSD_CONV, tc), lambda c, i: (0, c)),
                   pl.BlockSpec((1, tc), lambda c, i: (0, c))],
        out_shape=[jax.ShapeDtypeStruct((length, ch), F32), jax.ShapeDtypeStruct((SSD_CONV, ch), F32),
                   jax.ShapeDtypeStruct((1, ch), F32)],
        compiler_params=_params(("parallel", "arbitrary")),
    )(dxc, xbc, xbc, w, b)


def _conv_bwd_in(dpre, w, name):
    length, ch = dpre.shape
    tl, tc = _conv_tiles(length, ch)
    nt = length // tl

    def body(d_ref, n_ref, w_ref, o_ref):
        cur = d_ref[...]
        next8 = jnp.where(pl.program_id(1) == nt - 1, 0.0, n_ref[...])
        acc = w_ref[SSD_CONV - 1:SSD_CONV, :] * cur
        for j in range(1, SSD_CONV):
            acc = acc + w_ref[SSD_CONV - 1 - j:SSD_CONV - j, :] * _shift_up(cur, next8, j)
        o_ref[...] = acc.astype(o_ref.dtype)

    return pl.pallas_call(
        body, name=name, grid=(ch // tc, nt),
        in_specs=[pl.BlockSpec((tl, tc), lambda c, i: (i, c)),
                  pl.BlockSpec((8, tc), lambda c, i: (jnp.minimum((i + 1) * (tl // 8), length // 8 - 1), c)),
                  pl.BlockSpec((SSD_CONV, tc), lambda c, i: (0, c))],
        out_specs=pl.BlockSpec((tl, tc), lambda c, i: (i, c)),
        out_shape=jax.ShapeDtypeStruct((length, ch), BF16),
        compiler_params=_params(("parallel", "parallel")),
    )(dpre, dpre, w)


def _split(x, terms):
    parts = []
    for _ in range(terms):
        part = x.astype(BF16)
        parts.append(part)
        x = x - part.astype(F32)
    return parts


def _dot(a, b, dn=(((1,), (0,)), ((), ()))):
    return lax.dot_general(a, b, dn, preferred_element_type=F32)


_NN = (((1,), (0,)), ((), ()))
_NT = (((1,), (1,)), ((), ()))
_TN = (((0,), (0,)), ((), ()))


def _pdot(parts, sel, dn=_NN):
    return functools.reduce(lambda a, b: a + b, [_dot(part, sel, dn) for part in parts])


def _pdotr(sel, parts, dn=_NN):
    return functools.reduce(lambda a, b: a + b, [_dot(sel, part, dn) for part in parts])


def _dot3(x, sel, dn=_NN):
    return _pdot(_split(x, 3), sel, dn)


def _dot3r(sel, x, dn=_NN):
    return _pdotr(sel, _split(x, 3), dn)


def _dot2(x, sel, dn=_NN):
    return _pdot(_split(x, 2), sel, dn)


def _iota2(shape, axis):
    return lax.broadcasted_iota(jnp.int32, shape, axis)


def _ssd_masks():
    q = SSD_CHUNK
    r, c = _iota2((q, q), 0), _iota2((q, q), 1)
    tril = (c <= r)
    return r, c, tril


def _pair_sel(i):
    r, c, _ = _ssd_masks()
    return (r == 2 * i + c // SSD_HEAD_DIM).astype(BF16)


def _pair_sel_t(i):
    r, c, _ = _ssd_masks()
    return (c == 2 * i + r // SSD_HEAD_DIM).astype(F32)


def _ssd_common(dtraw, dtb, ap, b_t, c_t):
    r, c, tril = _ssd_masks()
    dt = _softplus(dtraw + dtb)
    la = dt * ap
    tril_b = tril.astype(BF16)
    cum = _dot3r(tril_b, la)
    rem = _dot3r((c > r).astype(BF16), la)
    total = _dot3(la, jnp.ones((SSD_CHUNK, SSD_CHUNK), BF16), _TN)
    scores = _dot(c_t, b_t, _NT)
    picks = {"dt": _split(dt, 2), "cum": _split(cum, 2), "rem": _split(rem, 2), "total": _split(total, 2)}
    return dt, la, cum, picks, scores, tril


def _head_decay(cum_parts, h, tril):
    r, c, _ = _ssd_masks()
    cq = _pdot(cum_parts, (r == h).astype(BF16))
    ck = _pdotr((c == h).astype(BF16), cum_parts, _NT)
    return jnp.exp(jnp.where(tril, cq - ck, -jnp.inf))


def _ssd_tiles(xc, n_heads):
    hpg = n_heads // SSD_GROUPS
    wg = hpg * SSD_HEAD_DIM
    xw = n_heads * SSD_HEAD_DIM
    return hpg, wg, xw // wg, xw // SSD_STATE


def _ssd_fwd(xc, dtraw, dtb, ap, n_heads, name, rider=None):
    length = xc.shape[0]
    q = SSD_CHUNK
    nc = length // q
    hpg, wg, _, b_blk0 = _ssd_tiles(xc, n_heads)
    c_blk0 = b_blk0 + SSD_GROUPS
    npair = hpg // 2

    def body(x_ref, b_ref, c_ref, dt_ref, dtb_ref, ap_ref, y_ref, st_ref, s_ref):
        @pl.when(pl.program_id(1) == 0)
        def _():
            s_ref[...] = jnp.zeros_like(s_ref)

        st_ref[0, 0] = s_ref[...]
        b_t = b_ref[...].astype(BF16)
        c_t = c_ref[...].astype(BF16)
        dt, la, cum, picks, scores, tril = _ssd_common(dt_ref[...], dtb_ref[...], ap_ref[...], b_t, c_t)
        lane = _iota2((q, q), 1)
        for i in range(npair):
            sel = _pair_sel(i)
            xp = x_ref[:, i * LANES:(i + 1) * LANES]
            xd = xp * _pdot(picks["dt"], sel)
            xd_b = xd.astype(BF16)
            s_prev = s_ref[i * LANES:(i + 1) * LANES, :]
            y = _dot(c_t, s_prev.astype(BF16), _NT) * jnp.exp(_pdot(picks["cum"], sel))
            for hh in range(2):
                own = (lane // SSD_HEAD_DIM) == hh
                wm = scores * _head_decay(picks["cum"], 2 * i + hh, tril)
                y = y + _dot(wm.astype(BF16), jnp.where(own, xd_b, 0))
            y_ref[:, i * LANES:(i + 1) * LANES] = y
            xw_b = (xd * jnp.exp(_pdot(picks["rem"], sel))).astype(BF16)
            grow = jnp.exp(_pdotr(sel, picks["total"], _TN))
            s_ref[i * LANES:(i + 1) * LANES, :] = grow * s_prev + _dot(xw_b, b_t, _TN)

    outs, carried = _hosted_call(
        body, name=name, grid=(SSD_GROUPS, nc),
        in_specs=[pl.BlockSpec((q, wg), lambda g, c: (c, g)),
                  pl.BlockSpec((q, SSD_STATE), lambda g, c: (c, b_blk0 + g)),
                  pl.BlockSpec((q, SSD_STATE), lambda g, c: (c, c_blk0 + g)),
                  pl.BlockSpec((q, LANES), lambda g, c: (c, g)),
                  pl.BlockSpec((1, LANES), lambda g, c: (0, g)),
                  pl.BlockSpec((1, LANES), lambda g, c: (0, g))],
        out_specs=[pl.BlockSpec((q, wg), lambda g, c: (c, g)),
                   pl.BlockSpec((1, 1, wg, SSD_STATE), lambda g, c: (c, g, 0, 0))],
        out_shape=[jax.ShapeDtypeStruct((length, SSD_GROUPS * wg), F32),
                   jax.ShapeDtypeStruct((nc, SSD_GROUPS, wg, SSD_STATE), F32)],
        scratch_shapes=[pltpu.VMEM((wg, SSD_STATE), F32)],
        sem=("parallel", "arbitrary"), args=(xc, xc, xc, dtraw, dtb, ap), rider=rider)
    return outs[0], outs[1], carried


def _ssd_bwd(dy, dxa, xc, dtraw, states, dtb, ap, n_heads, name, rider=None):
    length = xc.shape[0]
    q = SSD_CHUNK
    nc = length // q
    hpg, wg, _, b_blk0 = _ssd_tiles(xc, n_heads)
    c_blk0 = b_blk0 + SSD_GROUPS
    npair = hpg // 2

    def body(dy_ref, dxa_ref, x_ref, b_ref, c_ref, dt_ref, st_ref, dtb_ref, ap_ref,
             dx_ref, db_ref, dc_ref, ddt_ref, ddtb_ref, dap_ref, ds_ref):
        @pl.when(pl.program_id(1) == 0)
        def _():
            ds_ref[...] = jnp.zeros_like(ds_ref)
            ddtb_ref[...] = jnp.zeros_like(ddtb_ref)
            dap_ref[...] = jnp.zeros_like(dap_ref)

        b_f = b_ref[...]
        c_f = c_ref[...]
        b_t = b_f.astype(BF16)
        c_t = c_f.astype(BF16)
        dtraw_t = dt_ref[...]
        dt, la, cum, picks, scores, tril = _ssd_common(dtraw_t, dtb_ref[...], ap_ref[...], b_t, c_t)
        r, c, _ = _ssd_masks()
        lane = c
        ones_b = jnp.ones((q, q), BF16)
        dcum = jnp.zeros((q, q), F32)
        drem = jnp.zeros((q, q), F32)
        dtot = jnp.zeros((q, q), F32)
        ddt = jnp.zeros((q, q), F32)
        dscores = jnp.zeros((q, q), F32)
        db_acc = jnp.zeros((q, SSD_STATE), F32)
        dc_acc = jnp.zeros((q, SSD_STATE), F32)
        for i in range(npair):
            sel = _pair_sel(i)
            xp = x_ref[:, i * LANES:(i + 1) * LANES]
            dyp = dy_ref[:, i * LANES:(i + 1) * LANES]
            dyp_b = dyp.astype(BF16)
            dtp = _pdot(picks["dt"], sel)
            ecum = jnp.exp(_pdot(picks["cum"], sel))
            wrem = jnp.exp(_pdot(picks["rem"], sel))
            xd = xp * dtp
            xd_b = xd.astype(BF16)
            s_prev = s_prev_f = st_ref[0, 0, i * LANES:(i + 1) * LANES, :]
            ds1 = ds_ref[i * LANES:(i + 1) * LANES, :]
            ds1_b = ds1.astype(BF16)
            dxd = jnp.zeros((q, LANES), F32)
            for hh in range(2):
                h = 2 * i + hh
                own = (lane // SSD_HEAD_DIM) == hh
                decay = _head_decay(picks["cum"], h, tril)
                wm = scores * decay
                dwm = _dot(jnp.where(own, dyp_b, 0), xd_b, _NT)
                dxd = dxd + jnp.where(own, _dot(wm.astype(BF16), dyp_b, _TN), 0.0)
                dscores = dscores + dwm * decay
                e = (dwm * wm).astype(BF16)
                put = (c == h).astype(BF16)
                dcum = dcum + _dot(e, put) - _dot(e, put, _TN)
            t_mat = _dot(c_t, s_prev.astype(BF16), _NT)
            d_t = (dyp * ecum).astype(BF16)
            dc_acc = dc_acc + _dot(d_t, s_prev.astype(BF16))
            ds_prev = _dot(d_t, c_t, _TN)
            dcum = dcum + _dot2(dyp * t_mat * ecum, sel, _NT)
            grow = jnp.exp(_pdotr(sel, picks["total"], _TN))
            ds_prev = ds_prev + grow * ds1
            zs = jnp.sum(ds1 * s_prev_f * grow, axis=1, keepdims=True)
            dtot = dtot + _pdotr(ones_b, _split(zs * _pair_sel_t(i), 2))
            xw = xd * wrem
            dxw = _dot(b_t, ds1_b, _NT)
            db_acc = db_acc + _dot(xw.astype(BF16), ds1_b)
            dxd = dxd + dxw * wrem
            drem = drem + _dot2(dxw * xw, sel, _NT)
            dx_ref[:, i * LANES:(i + 1) * LANES] = dxd * dtp + dxa_ref[:, i * LANES:(i + 1) * LANES]
            ddt = ddt + _dot2(dxd * xp, sel, _NT)
            ds_ref[i * LANES:(i + 1) * LANES, :] = ds_prev
        ds_b = dscores.astype(BF16)
        dc_ref[...] = dc_acc + _dot(ds_b, b_t)
        db_ref[...] = db_acc + _dot(ds_b, c_t, _TN)
        dla = (_dot3r(tril.astype(BF16), dcum, _TN) + _dot3r((c > r).astype(BF16), drem, _TN) + dtot)
        ddt = ddt + dla * ap_ref[...]
        dap_ref[...] += jnp.sum(dla * dt, axis=0, keepdims=True)
        ddtraw = ddt * _sigmoid(dtraw_t + dtb_ref[...])
        ddt_ref[...] = ddtraw.astype(ddt_ref.dtype)
        ddtb_ref[...] += jnp.sum(ddtraw, axis=0, keepdims=True)

    rev = lambda g, c: (nc - 1 - c, g)
    outs, carried = _hosted_call(
        body, name=name, grid=(SSD_GROUPS, nc),
        in_specs=[pl.BlockSpec((q, wg), rev),
                  pl.BlockSpec((q, wg), rev),
                  pl.BlockSpec((q, wg), rev),
                  pl.BlockSpec((q, SSD_STATE), lambda g, c: (nc - 1 - c, b_blk0 + g)),
                  pl.BlockSpec((q, SSD_STATE), lambda g, c: (nc - 1 - c, c_blk0 + g)),
                  pl.BlockSpec((q, LANES), rev),
                  pl.BlockSpec((1, 1, wg, SSD_STATE), lambda g, c: (nc - 1 - c, g, 0, 0)),
                  pl.BlockSpec((1, LANES), lambda g, c: (0, g)),
                  pl.BlockSpec((1, LANES), lambda g, c: (0, g))],
        out_specs=[pl.BlockSpec((q, wg), rev),
                   pl.BlockSpec((q, SSD_STATE), rev),
                   pl.BlockSpec((q, SSD_STATE), rev),
                   pl.BlockSpec((q, LANES), rev),
                   pl.BlockSpec((1, LANES), lambda g, c: (0, g)),
                   pl.BlockSpec((1, LANES), lambda g, c: (0, g))],
        out_shape=[jax.ShapeDtypeStruct((length, SSD_GROUPS * wg), F32),
                   jax.ShapeDtypeStruct((length, SSD_GROUPS * SSD_STATE), F32),
                   jax.ShapeDtypeStruct((length, SSD_GROUPS * SSD_STATE), F32),
                   jax.ShapeDtypeStruct((length, SSD_GROUPS * LANES), BF16),
                   jax.ShapeDtypeStruct((1, SSD_GROUPS * LANES), F32),
                   jax.ShapeDtypeStruct((1, SSD_GROUPS * LANES), F32)],
        scratch_shapes=[pltpu.VMEM((wg, SSD_STATE), F32)],
        sem=("parallel", "arbitrary"), args=(dy, dxa, xc, xc, xc, dtraw, states, dtb, ap), rider=rider)
    return tuple(outs) + (carried,)


def _fox_cumsum(fraw, bf, name):
    length = fraw.shape[0]
    q = 128

    def body(f_ref, b_ref, o_ref, carry_ref):
        @pl.when(pl.program_id(0) == 0)
        def _():
            carry_ref[...] = jnp.zeros_like(carry_ref)

        lf = -_softplus(-(f_ref[...] + b_ref[...]))
        r, c = _iota2((q, q), 0), _iota2((q, q), 1)
        o_ref[...] = _dot3r((c <= r).astype(BF16), lf) + carry_ref[...]
        carry_ref[...] += jnp.sum(lf, axis=0, keepdims=True)

    return pl.pallas_call(
        body, name=name, grid=(length // q,),
        in_specs=[pl.BlockSpec((q, LANES), lambda i: (i, 0)), pl.BlockSpec((1, LANES), lambda i: (0, 0))],
        out_specs=pl.BlockSpec((q, LANES), lambda i: (i, 0)),
        out_shape=jax.ShapeDtypeStruct((length, LANES), F32),
        scratch_shapes=[pltpu.VMEM((1, LANES), F32)],
        compiler_params=_params(("arbitrary",)),
    )(fraw, bf)


def _fox_cumsum_bwd(dc, fraw, bf, name):
    length = fraw.shape[0]
    q = 128
    nt = length // q

    def body(d_ref, f_ref, b_ref, o_ref, db_ref, carry_ref):
        @pl.when(pl.program_id(0) == 0)
        def _():
            carry_ref[...] = jnp.zeros_like(carry_ref)
            db_ref[...] = jnp.zeros_like(db_ref)

        d = d_ref[...]
        r, c = _iota2((q, q), 0), _iota2((q, q), 1)
        dlf = _dot3r((c >= r).astype(BF16), d) + carry_ref[...]
        carry_ref[...] += jnp.sum(d, axis=0, keepdims=True)
        df = dlf * _sigmoid(-(f_ref[...] + b_ref[...]))
        o_ref[...] = df.astype(o_ref.dtype)
        db_ref[...] += jnp.sum(df, axis=0, keepdims=True)

    rev = lambda i: (nt - 1 - i, 0)
    return pl.pallas_call(
        body, name=name, grid=(nt,),
        in_specs=[pl.BlockSpec((q, LANES), rev), pl.BlockSpec((q, LANES), rev),
                  pl.BlockSpec((1, LANES), lambda i: (0, 0))],
        out_specs=[pl.BlockSpec((q, LANES), rev), pl.BlockSpec((1, LANES), lambda i: (0, 0))],
        out_shape=[jax.ShapeDtypeStruct((length, LANES), BF16), jax.ShapeDtypeStruct((1, LANES), F32)],
        scratch_shapes=[pltpu.VMEM((1, LANES), F32)],
        compiler_params=_params(("arbitrary",)),
    )(dc, fraw, bf)


def _fox_scores(q_ref, k_ref, cq_ref, ck_ref, diagonal):
    scale = 1.0 / math.sqrt(FOX_HEAD_DIM)
    s = _dot(q_ref[...].astype(BF16), k_ref[...].astype(BF16), _NT) * scale + (cq_ref[0] - ck_ref[0])
    if diagonal:
        s = jnp.where(_iota2(s.shape, 1) <= _iota2(s.shape, 0), s, -jnp.inf)
    return s


def _fox_tiles(i, j, step):
    @pl.when(j < i)
    def _():
        step(False)

    @pl.when(j == i)
    def _():
        step(True)


def _fox_fwd(qkvg, c_col, c_row, n_heads, name):
    length = qkvg.shape[0]
    hd = FOX_HEAD_DIM
    tq = _pick(length, (FOX_TILE, 512, 256, 128))
    nq = length // tq

    def body(q_ref, k_ref, v_ref, cq_ref, ck_ref, o_ref, lse_ref, m_ref, l_ref, acc_ref):
        i, j = pl.program_id(1), pl.program_id(2)

        @pl.when(j == 0)
        def _():
            m_ref[...] = jnp.full_like(m_ref, -jnp.inf)
            l_ref[...] = jnp.zeros_like(l_ref)
            acc_ref[...] = jnp.zeros_like(acc_ref)

        def step(diagonal):
            s = _fox_scores(q_ref, k_ref, cq_ref, ck_ref, diagonal)
            m_new = jnp.maximum(m_ref[...], jnp.max(s, axis=1, keepdims=True))
            alpha = jnp.exp(m_ref[...] - m_new)
            p = jnp.exp(s - m_new)
            l_ref[...] = alpha * l_ref[...] + jnp.sum(p, axis=1, keepdims=True)
            acc_ref[...] = alpha * acc_ref[...] + _dot(p.astype(BF16), v_ref[...].astype(BF16))
            m_ref[...] = m_new

        _fox_tiles(i, j, step)

        @pl.when(j == nq - 1)
        def _():
            o_ref[...] = acc_ref[...] / l_ref[...]
            lse_ref[0] = m_ref[...] + jnp.log(l_ref[...])

    kmap = lambda off: (lambda h, i, j: (jnp.minimum(j, i), off * n_heads + h))
    return pl.pallas_call(
        body, name=name, grid=(n_heads, nq, nq),
        in_specs=[pl.BlockSpec((tq, hd), lambda h, i, j: (i, h)),
                  pl.BlockSpec((tq, hd), kmap(1)),
                  pl.BlockSpec((tq, hd), kmap(2)),
                  pl.BlockSpec((1, tq, 1), lambda h, i, j: (h, i, 0)),
                  pl.BlockSpec((1, 1, tq), lambda h, i, j: (h, 0, jnp.minimum(j, i)))],
        out_specs=[pl.BlockSpec((tq, hd), lambda h, i, j: (i, h)),
                   pl.BlockSpec((1, tq, 1), lambda h, i, j: (h, i, 0))],
        out_shape=[jax.ShapeDtypeStruct((length, n_heads * hd), F32),
                   jax.ShapeDtypeStruct((n_heads, length, 1), F32)],
        scratch_shapes=[pltpu.VMEM((tq, 1), F32), pltpu.VMEM((tq, 1), F32), pltpu.VMEM((tq, hd), F32)],
        compiler_params=_params(("parallel", "parallel", "arbitrary")),
    )(qkvg, qkvg, qkvg, c_col, c_row)


def _fox_bwd_q(qkvg, datt, lse, c_col, c_row, n_heads, name):
    length = qkvg.shape[0]
    hd = FOX_HEAD_DIM
    tq = _pick(length, (FOX_TILE, 512, 256, 128))
    nq = length // tq
    scale = 1.0 / math.sqrt(hd)

    def body(q_ref, k_ref, v_ref, do_ref, lse_ref, cq_ref, ck_ref, dq_ref, dsum_ref, a1_ref, a2_ref, d_ref):
        i, j = pl.program_id(1), pl.program_id(2)

        @pl.when(j == 0)
        def _():
            a1_ref[...] = jnp.zeros_like(a1_ref)
            a2_ref[...] = jnp.zeros_like(a2_ref)
            d_ref[...] = jnp.zeros_like(d_ref)

        def step(diagonal):
            s = _fox_scores(q_ref, k_ref, cq_ref, ck_ref, diagonal)
            p = jnp.exp(s - lse_ref[0])
            pdp = p * _dot(do_ref[...].astype(BF16), v_ref[...].astype(BF16), _NT)
            d_ref[...] += jnp.sum(pdp, axis=1, keepdims=True)
            k_b = k_ref[...].astype(BF16)
            a1_ref[...] += _dot(pdp.astype(BF16), k_b)
            a2_ref[...] += _dot(p.astype(BF16), k_b)

        _fox_tiles(i, j, step)

        @pl.when(j == nq - 1)
        def _():
            dq_ref[...] = ((a1_ref[...] - d_ref[...] * a2_ref[...]) * scale).astype(dq_ref.dtype)
            dsum_ref[0] = d_ref[...]

    kmap = lambda off: (lambda h, i, j: (jnp.minimum(j, i), off * n_heads + h))
    qmap = lambda h, i, j: (i, h)
    col = pl.BlockSpec((1, tq, 1), lambda h, i, j: (h, i, 0))
    return pl.pallas_call(
        body, name=name, grid=(n_heads, nq, nq),
        in_specs=[pl.BlockSpec((tq, hd), qmap), pl.BlockSpec((tq, hd), kmap(1)), pl.BlockSpec((tq, hd), kmap(2)),
                  pl.BlockSpec((tq, hd), qmap), col, col,
                  pl.BlockSpec((1, 1, tq), lambda h, i, j: (h, 0, jnp.minimum(j, i)))],
        out_specs=[pl.BlockSpec((tq, hd), qmap), col],
        out_shape=[jax.ShapeDtypeStruct((length, n_heads * hd), BF16),
                   jax.ShapeDtypeStruct((n_heads, length, 1), F32)],
        scratch_shapes=[pltpu.VMEM((tq, hd), F32), pltpu.VMEM((tq, hd), F32), pltpu.VMEM((tq, 1), F32)],
        compiler_params=_params(("parallel", "parallel", "arbitrary")),
    )(qkvg, qkvg, qkvg, datt, lse, c_col, c_row)


def _fox_bwd_kv(qkvg, datt, lse, dsum, c_col, c_row, n_heads, name):
    length = qkvg.shape[0]
    hd = FOX_HEAD_DIM
    tq = _pick(length, (FOX_TILE, 512, 256, 128))
    nq = length // tq
    scale = 1.0 / math.sqrt(hd)

    def body(q_ref, k_ref, v_ref, do_ref, lse_ref, dsum_ref, cq_ref, ck_ref, dk_ref, dv_ref, dck_ref,
             dk_acc, dv_acc, dc_acc):
        j, i = pl.program_id(1), pl.program_id(2)

        @pl.when(i == 0)
        def _():
            dk_acc[...] = jnp.zeros_like(dk_acc)
            dv_acc[...] = jnp.zeros_like(dv_acc)
            dc_acc[...] = jnp.zeros_like(dc_acc)

        def step(diagonal):
            s = _fox_scores(q_ref, k_ref, cq_ref, ck_ref, diagonal)
            p = jnp.exp(s - lse_ref[0])
            do_b = do_ref[...].astype(BF16)
            dv_acc[...] += _dot(p.astype(BF16), do_b, _TN)
            dp = _dot(do_b, v_ref[...].astype(BF16), _NT)
            ds = p * (dp - dsum_ref[0])
            dk_acc[...] += _dot(ds.astype(BF16), q_ref[...].astype(BF16), _TN)
            dc_acc[...] -= jnp.sum(ds, axis=0, keepdims=True)

        _fox_tiles(i, j, step)

        @pl.when(i == nq - 1)
        def _():
            dk_ref[...] = (dk_acc[...] * scale).astype(dk_ref.dtype)
            dv_ref[...] = dv_acc[...].astype(dv_ref.dtype)
            dck_ref[0] = dc_acc[...]

    qmap = lambda h, j, i: (jnp.maximum(i, j), h)
    kmap = lambda off: (lambda h, j, i: (j, off * n_heads + h))
    col = pl.BlockSpec((1, tq, 1), lambda h, j, i: (h, jnp.maximum(i, j), 0))
    return pl.pallas_call(
        body, name=name, grid=(n_heads, nq, nq),
        in_specs=[pl.BlockSpec((tq, hd), qmap), pl.BlockSpec((tq, hd), kmap(1)), pl.BlockSpec((tq, hd), kmap(2)),
                  pl.BlockSpec((tq, hd), qmap), col, col, col,
                  pl.BlockSpec((1, 1, tq), lambda h, j, i: (h, 0, j))],
        out_specs=[pl.BlockSpec((tq, hd), lambda h, j, i: (j, h)), pl.BlockSpec((tq, hd), lambda h, j, i: (j, h)),
                   pl.BlockSpec((1, 1, tq), lambda h, j, i: (h, 0, j))],
        out_shape=[jax.ShapeDtypeStruct((length, n_heads * hd), BF16)] * 2
        + [jax.ShapeDtypeStruct((n_heads, 1, length), F32)],
        scratch_shapes=[pltpu.VMEM((tq, hd), F32), pltpu.VMEM((tq, hd), F32), pltpu.VMEM((1, tq), F32)],
        compiler_params=_params(("parallel", "parallel", "arbitrary")),
    )(qkvg, qkvg, qkvg, datt, lse, dsum, c_col, c_row)


def _row(v):
    return v.reshape(1, -1).astype(F32)


def _pad_heads(v, per_group):
    lead = v.shape[:-1]
    v = v.reshape(lead + (SSD_GROUPS, per_group))
    v = jnp.pad(v, [(0, 0)] * len(lead) + [(0, 0), (0, LANES - per_group)])
    return v.reshape(lead + (SSD_GROUPS * LANES,))


def _unpad_heads(v, per_group):
    lead = v.shape[:-1]
    return v.reshape(lead + (SSD_GROUPS, LANES))[..., :per_group].reshape(lead + (SSD_GROUPS * per_group,))


class _NoOverlap:
    def gather_rider(self, host):
        return None

    def gathered(self, host, carried):
        return {}

    def reduce_rider(self, grads):
        return None

    def reduced(self, carried):
        pass


def _local_step(x, tgt, wb, sm, plan=None):
    plan = plan or _NoOverlap()
    wb = dict(wb)
    length, d = x.shape
    s5w = wb["w0_ug"].shape[1] // 2
    ssdw = wb["w0_z"].shape[1]
    xbcw = wb["w0_xbc"].shape[1]
    n_ssd = ssdw // SSD_HEAD_DIM
    hpg = n_ssd // SSD_GROUPS
    fw = d
    n_fox = fw // FOX_HEAD_DIM
    s5g = s5w // S5_GROUP
    s5s = s5g * S5_STATE
    grads = {}

    s5_in = (sm["l0_s5_lambda_re"], sm["l0_s5_lambda_im"], sm["l0_s5_log_step"], sm["l0_s5_b_re"],
             sm["l0_s5_b_im"], sm["l0_s5_c_re"], sm["l0_s5_c_im"])
    (a_re, a_im, bd_re, bd_im, cd_re, cd_imn), s5_vjp = jax.vjp(_s5_prepare, *s5_in)
    nj = bd_re.shape[0]
    bd_re_b, bd_im_b, cd_re_b, cd_imn_b = (t.astype(BF16) for t in (bd_re, bd_im, cd_re, cd_imn))
    tr = lambda t: jnp.swapaxes(t, 1, 2)
    dvec = _row(sm["l0_s5_d"])
    bglu = _row(sm["l0_s5_b_glu"])
    conv_w = sm["l0_ssd_conv_w"]
    conv_b = _row(sm["l0_ssd_conv_b"])

    def ssd_prepare(dt_bias, a_log, dd):
        return (_pad_heads(_row(dt_bias), hpg), _pad_heads(_row(-jnp.exp(a_log)), hpg),
                jnp.repeat(_row(dd), SSD_HEAD_DIM, axis=1))

    (dtb, ap, dpar), ssd_vjp = jax.vjp(ssd_prepare, sm["l0_ssd_dt_bias"], sm["l0_ssd_a_log"], sm["l0_ssd_d"])
    ssd_nw = _row(sm["l0_ssd_norm_w"])
    nw0, nw1, fnw = _row(sm["l0_norm_w"]), _row(sm["l1_norm_w"]), _row(sm["final_norm_w"])
    bf = jnp.pad(_row(sm["l1_fox_b_f"]), ((0, 0), (0, LANES - n_fox)))

    h0 = _norm_fwd(x, nw0, "l0_norm")
    ug = _matmul(h0, wb["w0_ug"], name="l0_in_ug")
    z = _matmul(h0, wb["w0_z"], name="l0_in_z")
    rider = plan.gather_rider("l0_in_xbc")
    xbc = _matmul(h0, wb["w0_xbc"], name="l0_in_xbc", rider=rider)
    if rider is not None:
        xbc, carried = xbc
        wb.update(plan.gathered("l0_in_xbc", carried))
    dtraw = _matmul(h0, wb["w0_dt"], name="l0_in_dt")
    u_win, gate_win = (ug, s5w, 0), (ug, s5w, 1)

    shape3 = (length, s5s // LANES, LANES)
    bu_re = _bdmm([(ug, 0)], [bd_re_b], name="s5_bu_re").reshape(shape3)
    bu_im = _bdmm([(ug, 0)], [bd_im_b], name="s5_bu_im").reshape(shape3)
    s_re3, s_im3 = _s5_scan_fwd(bu_re, bu_im, a_re, a_im, "s5_scan")
    s_re, s_im = s_re3.reshape(length, s5s), s_im3.reshape(length, s5s)
    yc = _bdmm([s_re, s_im], [cd_re_b, cd_imn_b], name="s5_y")
    gl = _rowwise(_f_s5_gelu, [yc, u_win], [dvec], [(s5w, BF16)], tl=256, name="s5_gelu")[0]
    t_glu = _matmul(gl, wb["w_glu"], name="s5_glu")
    s5o = _rowwise(_f_s5_out, [yc, u_win, t_glu, gate_win], [dvec, bglu], [(s5w, BF16)], tl=256,
                   name="s5_out")[0]

    xc = _conv_fwd(xbc, conv_w, conv_b, "ssd_conv")
    y_ssd, states, carried = _ssd_fwd(xc, dtraw, dtb, ap, n_ssd, "ssd_scan", rider=plan.gather_rider("ssd_scan"))
    wb.update(plan.gathered("ssd_scan", carried))
    wg = ssdw // SSD_GROUPS
    ssdo = _rowwise(_f_ssd_out, [y_ssd, (xc, wg, 0), z], [dpar, ssd_nw], [(wg, BF16)], tl=256,
                    ncol=SSD_GROUPS, name="ssd_out")[0]
    x1 = _matmul(s5o, wb["w0_out"], dims=(length, d, s5w), addend=x, name="l0_out_s5")
    x1 = _matmul(ssdo, wb["w0_out"], dims=(length, d, ssdw), b_off=(s5w, 0), addend=x1, name="l0_out_ssd")

    h1 = _norm_fwd(x1, nw1, "l1_norm")
    qkvg = _matmul(h1, wb["w1_main"], name="l1_in")
    fraw = _matmul(h1, wb["w1_f"], name="l1_in_f")
    cc = _fox_cumsum(fraw, bf, "fox_cumsum")
    c_t = cc[:, :n_fox].T
    c_col, c_row = c_t[:, :, None], c_t[:, None, :]
    att, lse = _fox_fwd(qkvg, c_col, c_row, n_fox, "fox_fwd")
    gate1_win = (qkvg, fw, 3)
    o1 = _rowwise(_f_fox_out, [att, gate1_win], [], [(fw, BF16)], tl=256, name="fox_out")[0]
    x2 = _matmul(o1, wb["w1_out"], addend=x1, name="l1_out")

    def loss_fn(xt, tt, wt):
        def f(xx, ww):
            err = _f_norm(xx, ww) - tt
            return (0.5 / d) * err * err
        lanes, vjp = jax.vjp(f, xt, wt)
        dx, dw = vjp(jnp.ones_like(lanes))
        return dx, dx, jnp.sum(lanes, axis=0, keepdims=True), dw

    dx2, dx2b, loss_lanes, g_fnw = _rowwise(loss_fn, [x2, tgt], [fnw], [(d, F32), (d, BF16)], [d, d],
                                            tl=128, name="loss_head")
    grads["final_norm_w"] = g_fnw

    grads["l1_w_out"] = _matmul(o1, dx2b, mode="tn", name="l1_out_dw")
    do1 = _matmul(dx2b, wb["w1_out"], mode="nt", name="l1_out_dx")

    def fox_out_bwd(at, gt, dt_):
        _, vjp = jax.vjp(_f_fox_out, at, gt)
        return vjp(dt_)

    datt, dgate1 = _rowwise(fox_out_bwd, [att, gate1_win, do1], [], [(fw, F32), (fw, BF16)], tl=256,
                            name="fox_out_bwd")
    dq, dsum = _fox_bwd_q(qkvg, datt, lse, c_col, c_row, n_fox, "fox_bwd_q")
    dk, dv, dck = _fox_bwd_kv(qkvg, datt, lse, dsum, c_col, c_row, n_fox, "fox_bwd_kv")
    dcc = jnp.pad(dck[:, 0, :].T, ((0, 0), (0, LANES - n_fox)))
    dfraw, g_bf = _fox_cumsum_bwd(dcc, fraw, bf, "fox_cumsum_bwd")
    grads["l1_fox_b_f"] = g_bf[:, :n_fox]
    dsegs = [dq, dk, dv, dgate1]
    grads["l1_w_in"] = jnp.concatenate(
        [_matmul(h1, s, mode="tn", name=f"l1_in_dw{i}") for i, s in enumerate(dsegs)]
        + [_matmul(h1, dfraw, mode="tn", name="l1_in_dwf")[:, :n_fox]], axis=1)
    dh1 = _matmul(dfraw, wb["w1_f"], mode="nt", name="l1_in_dxf")
    for i, s in enumerate(dsegs):
        dh1 = _matmul(s, wb["w1_main"], mode="nt", dims=(length, d, fw), b_off=(0, i * fw), addend=dh1,
                      name=f"l1_in_dx{i}")
    dx1, dx1b, grads["l1_norm_w"] = _norm_bwd(x1, dh1, dx2, nw1, "l1_norm_bwd")

    grads["l0_w_out"] = jnp.concatenate([_matmul(s5o, dx1b, mode="tn", name="l0_out_dw_s5"),
                                         _matmul(ssdo, dx1b, mode="tn", name="l0_out_dw_ssd")], axis=0)
    ds5o = _matmul(dx1b, wb["w0_out"], mode="nt", dims=(length, s5w, d), name="l0_out_dx_s5")
    dssdo = _matmul(dx1b, wb["w0_out"], mode="nt", dims=(length, ssdw, d), b_off=(s5w, 0), name="l0_out_dx_ssd")

    def ssd_out_bwd(yt, xt, zt, dt_, dp, nw):
        _, vjp = jax.vjp(_f_ssd_out, yt, xt, zt, dp, nw)
        return vjp(dt_)

    dy_ssd, dxa, dz, g_dpar, g_ssd_nw = _rowwise(
        ssd_out_bwd, [y_ssd, (xc, wg, 0), z, dssdo], [dpar, ssd_nw],
        [(wg, F32), (wg, F32), (wg, BF16)], [wg, wg], tl=128, ncol=SSD_GROUPS, name="ssd_out_bwd")
    grads["l0_ssd_norm_w"] = g_ssd_nw
    dxs, db_ssd, dc_ssd, ddtraw, g_dtb, g_ap, carried = _ssd_bwd(
        dy_ssd, dxa, xc, dtraw, states, dtb, ap, n_ssd, "ssd_scan_bwd", rider=plan.reduce_rider(grads))
    plan.reduced(carried)
    g_dt_bias, g_a_log, g_ssd_d = ssd_vjp((g_dtb, g_ap, g_dpar))
    grads["l0_ssd_dt_bias"], grads["l0_ssd_a_log"], grads["l0_ssd_d"] = g_dt_bias, g_a_log, g_ssd_d
    dxc = jnp.concatenate([dxs, db_ssd, dc_ssd], axis=1)
    dpre, grads["l0_ssd_conv_w"], grads["l0_ssd_conv_b"] = _conv_bwd_pre(dxc, xbc, conv_w, conv_b, "ssd_conv_bwd_pre")
    dxbc = _conv_bwd_in(dpre, conv_w, "ssd_conv_bwd_in")

    def s5_out_bwd(yt, ut, tt, gt, dt_, dv_, bg):
        _, vjp = jax.vjp(_f_s5_out, yt, ut, tt, gt, dv_, bg)
        return vjp(dt_)

    dyc_a, du_a, dt_glu, dgate, g_dvec_a, g_bglu = _rowwise(
        s5_out_bwd, [yc, u_win, t_glu, gate_win, ds5o], [dvec, bglu],
        [(s5w, F32), (s5w, F32), (s5w, BF16), (s5w, BF16)], [s5w, s5w], tl=128, name="s5_out_bwd")
    grads["l0_s5_b_glu"] = g_bglu
    grads["l0_s5_w_glu"] = _matmul(gl, dt_glu, mode="tn", name="s5_glu_dw")
    dgl = _matmul(dt_glu, wb["w_glu"], mode="nt", name="s5_glu_dx")

    def s5_gelu_bwd(yt, ut, dg, dya, dua, dv_):
        _, vjp = jax.vjp(_f_s5_gelu, yt, ut, dv_)
        dy_, du_, ddv = vjp(dg)
        return dy_ + dya, du_ + dua, ddv

    dyc, du_ab, g_dvec_b = _rowwise(s5_gelu_bwd, [yc, u_win, dgl, dyc_a, du_a], [dvec],
                                    [(s5w, F32), (s5w, F32)], [s5w], tl=128, name="s5_gelu_bwd")
    ds_re = _bdmm([dyc], [tr(cd_re_b)], name="s5_ds_re").reshape(shape3)
    ds_im = _bdmm([dyc], [tr(cd_imn_b)], name="s5_ds_im").reshape(shape3)
    kin_s, kin_u = s5s // nj, s5w // nj
    g_cd_re = _bdmm_tn_sized(s_re, dyc, nj, kin_s, kin_u, 0, "s5_dcd_re")
    g_cd_imn = _bdmm_tn_sized(s_im, dyc, nj, kin_s, kin_u, 0, "s5_dcd_im")
    g_re3, g_im3, g_a_re, g_a_im = _s5_scan_bwd(ds_re, ds_im, s_re3, s_im3, a_re, a_im, "s5_scan_bwd")
    g_re, g_im = g_re3.reshape(length, s5s), g_im3.reshape(length, s5s)
    du = _bdmm([g_re, g_im], [tr(bd_re_b), tr(bd_im_b)], addend=du_ab, out_dtype=BF16, name="s5_du")
    g_bd_re = _bdmm_tn_sized(ug, g_re, nj, kin_u, kin_s, 0, "s5_dbd_re")
    g_bd_im = _bdmm_tn_sized(ug, g_im, nj, kin_u, kin_s, 0, "s5_dbd_im")
    s5_g = s5_vjp((g_a_re, g_a_im, g_bd_re, g_bd_im, g_cd_re, g_cd_imn))
    for nm, g in zip(("lambda_re", "lambda_im", "log_step", "b_re", "b_im", "c_re", "c_im"), s5_g):
        grads["l0_s5_" + nm] = g
    grads["l0_s5_d"] = (g_dvec_a + g_dvec_b).reshape(sm["l0_s5_d"].shape)

    grads["l0_w_in"] = jnp.concatenate(
        [_matmul(h0, du, mode="tn", name="l0_in_dw_u"), _matmul(h0, dgate, mode="tn", name="l0_in_dw_g"),
         _matmul(h0, dz, mode="tn", name="l0_in_dw_z"), _matmul(h0, dxbc, mode="tn", name="l0_in_dw_xbc"),
         _unpad_heads(_matmul(h0, ddtraw, mode="tn", name="l0_in_dw_dt"), hpg)], axis=1)
    dh0 = _matmul(du, wb["w0_ug"], mode="nt", dims=(length, d, s5w), name="l0_in_dx_u")
    dh0 = _matmul(dgate, wb["w0_ug"], mode="nt", dims=(length, d, s5w), b_off=(0, s5w), addend=dh0,
                  name="l0_in_dx_g")
    dh0 = _matmul(dz, wb["w0_z"], mode="nt", addend=dh0, name="l0_in_dx_z")
    dh0 = _matmul(dxbc, wb["w0_xbc"], mode="nt", addend=dh0, name="l0_in_dx_xbc")
    dh0 = _matmul(ddtraw, wb["w0_dt"], mode="nt", addend=dh0, name="l0_in_dx_dt")
    dx, _, grads["l0_norm_w"] = _norm_bwd(x, dh0, dx1, nw0, "l0_norm_bwd")
    return loss_lanes, dx, grads


_ANY = pl.BlockSpec(memory_space=pl.ANY)


def _place():
    x, y, c = lax.axis_index("x"), lax.axis_index("y"), lax.axis_index("c")
    return x, y, c, [(1 - x, y), (x, 1 - y), (1 - x, 1 - y)]


def _remote(src, dst, send_sem, recv_sem, to):
    return pltpu.make_async_remote_copy(src_ref=src, dst_ref=dst, send_sem=send_sem, recv_sem=recv_sem,
                                        device_id=to, device_id_type=MESH)


def _comm_call(body, n_in, out_shape, n_sems, name):
    return pl.pallas_call(
        body, name=name, in_specs=[_ANY] * n_in, out_specs=[_ANY] * len(out_shape), out_shape=out_shape,
        scratch_shapes=[pltpu.SemaphoreType.DMA((k,)) for k in n_sems],
        compiler_params=pltpu.CompilerParams(has_side_effects=True),
    )


def _gather_rider(shards):
    n = len(shards)

    def sends(ins, outs, sems):
        send, recv = sems[:2]
        x, y, c, chips = _place()
        me = 2 * x + y
        return [_remote(ins[a].at[c], outs[a].at[me, c], send.at[3 * a + k], recv.at[3 * a + k], (px, py, c))
                for a in range(n) for k, (px, py) in enumerate(chips)]

    def start(ins, outs, sems):
        for cp in sends(ins, outs, sems):
            cp.start()

    def finish(ins, outs, sems):
        send, recv, fsend, frecv = sems
        x, y, c, chips = _place()
        passed = []
        for a in range(n):
            for k, (px, py) in enumerate(chips):
                got = outs[a].at[2 * px + py, c]
                _remote(got, got, send.at[3 * a + k], recv.at[3 * a + k], (px, py, c)).wait_recv()
                cp = _remote(got, got, fsend.at[3 * a + k], frecv.at[3 * a + k], (x, y, 1 - c))
                cp.start()
                passed.append(cp)
        for a in range(n):
            for k, (px, py) in enumerate(chips):
                got = outs[a].at[2 * px + py, 1 - c]
                _remote(got, got, fsend.at[3 * a + k], frecv.at[3 * a + k], (x, y, 1 - c)).wait_recv()
        for cp in sends(ins, outs, sems) + passed:
            cp.wait_send()

    out_shape = [jax.ShapeDtypeStruct((N_SHARD,) + s.shape, s.dtype) for s in shards]
    return _Rider(shards, out_shape, [3 * n] * 4, start, finish)


def _chip_rider(parts):
    n = len(parts)

    def copies(ins, outs, sems):
        send, recv = sems
        x, y, c, chips = _place()
        return [_remote(ins[a].at[2 * px + py], outs[a].at[k], send.at[3 * a + k], recv.at[3 * a + k], (px, py, c))
                for a in range(n) for k, (px, py) in enumerate(chips)]

    def start(ins, outs, sems):
        for cp in copies(ins, outs, sems):
            cp.start()

    def finish(ins, outs, sems):
        for cp in copies(ins, outs, sems):
            cp.wait()

    out_shape = [jax.ShapeDtypeStruct((3,) + p.shape[1:], p.dtype) for p in parts]
    return _Rider(parts, out_shape, [3 * n] * 2, start, finish)


def _run_rider(rider, name):
    n_in, n_out = len(rider.inputs), len(rider.out_shape)

    def body(*refs):
        ins, outs, sems = refs[:n_in], refs[n_in:n_in + n_out], refs[n_in + n_out:]
        rider.start(ins, outs, sems)
        rider.finish(ins, outs, sems)

    return _comm_call(body, n_in, rider.out_shape, rider.sems, name)(*rider.inputs)


def _sibling_halves(grads, name):
    n = len(grads)

    def body(*refs):
        ins, outs = refs[:n], refs[n:2 * n]
        send, recv = refs[2 * n:]
        x, y, c, _ = _place()
        copies = []
        for a in range(n):
            for j in range(N_SHARD):
                cp = _remote(ins[a].at[j, 1 - c], outs[a].at[j], send.at[N_SHARD * a + j],
                             recv.at[N_SHARD * a + j], (x, y, 1 - c))
                cp.start()
                copies.append(cp)
        for cp in copies:
            cp.wait()

    out_shape = [jax.ShapeDtypeStruct((N_SHARD,) + g.shape[2:], g.dtype) for g in grads]
    return _comm_call(body, n, out_shape, [N_SHARD * n, N_SHARD * n], name)(*grads)


def _join_halves(halves, name):
    n = len(halves)

    def body(*refs):
        outs = refs[n:2 * n]
        send, recv = refs[2 * n:]
        x, y, c, _ = _place()
        copies = [_remote(outs[a].at[c], outs[a].at[c], send.at[a], recv.at[a], (x, y, 1 - c)) for a in range(n)]
        for cp in copies:
            cp.start()
        for a in range(n):
            copies[a].wait_send()
            got = outs[a].at[1 - c]
            _remote(got, got, send.at[a], recv.at[a], (x, y, 1 - c)).wait_recv()

    return pl.pallas_call(
        body, name=name, in_specs=[_ANY] * n, out_specs=[_ANY] * n,
        out_shape=[jax.ShapeDtypeStruct(h.shape, h.dtype) for h in halves],
        input_output_aliases={a: a for a in range(n)},
        scratch_shapes=[pltpu.SemaphoreType.DMA((n,)), pltpu.SemaphoreType.DMA((n,))],
        compiler_params=pltpu.CompilerParams(has_side_effects=True),
    )(*halves)


def _gather_all(buf, name):
    def body(in_ref, out_ref, send, recv, lsem):
        x, y, c, _ = _place()
        me = 4 * x + 2 * y + c
        local = pltpu.make_async_copy(in_ref, out_ref.at[me], lsem.at[0])
        local.start()
        copies = []
        for k in range(1, N_DEV):
            fx, fy, fc = (k >> 2) & 1, (k >> 1) & 1, k & 1
            peer = (x + fx - 2 * x * fx, y + fy - 2 * y * fy, c + fc - 2 * c * fc)
            cp = _remote(in_ref, out_ref.at[me], send.at[k - 1], recv.at[k - 1], peer)
            cp.start()
            copies.append((cp, 4 * peer[0] + 2 * peer[1] + peer[2]))
        for k, (cp, slot) in enumerate(copies):
            cp.wait_send()
            got = out_ref.at[slot]
            _remote(got, got, send.at[k], recv.at[k], (x, y, c)).wait_recv()
        local.wait()

    out_shape = [jax.ShapeDtypeStruct((N_DEV,) + buf.shape, buf.dtype)]
    return _comm_call(body, 1, out_shape, [N_DEV - 1, N_DEV - 1, 1], name)(buf)[0]


def _sum_slots(buf, name):
    slots, rows, _ = buf.shape
    tr = _pick(rows, (512, 256, 128, 64, 32, 16, 8))

    def body(b_ref, o_ref):
        acc = b_ref[0]
        for s in range(1, slots):
            acc = acc + b_ref[s]
        o_ref[...] = acc

    return pl.pallas_call(
        body, name=name, grid=(rows // tr,),
        in_specs=[pl.BlockSpec((slots, tr, LANES), lambda i: (0, i, 0))],
        out_specs=pl.BlockSpec((tr, LANES), lambda i: (i, 0)),
        out_shape=jax.ShapeDtypeStruct((rows, LANES), F32),
        compiler_params=_params(("parallel",)),
    )(buf)


def _row_tile(cols, n_bufs):
    return max(8, min(512, (24 * 1024 * 1024 // (4 * cols * n_bufs)) // 8 * 8))


def _presum(grad, sib, name):
    ns, _, rh, cols = grad.shape
    tr = _pick(rh, [t for t in (512, 256, 128, 64, 32, 16) if t <= _row_tile(cols, 6)])

    def body(g_ref, r_ref, o_ref):
        o_ref[0] = (g_ref[0, 0] + r_ref[0]).astype(o_ref.dtype)

    return pl.pallas_call(
        body, name=name, grid=(ns, rh // tr),
        in_specs=[pl.BlockSpec((1, 1, tr, cols), lambda j, i: (j, lax.axis_index("c"), i, 0)),
                  pl.BlockSpec((1, tr, cols), lambda j, i: (j, i, 0))],
        out_specs=pl.BlockSpec((1, tr, cols), lambda j, i: (j, i, 0)),
        out_shape=jax.ShapeDtypeStruct((ns, rh, cols), BF16),
        compiler_params=_params(("parallel", "parallel")),
    )(grad, sib)


def _finish_half(grad, sib, others, name):
    _, _, rh, cols = grad.shape
    tr = _pick(rh, [t for t in (512, 256, 128, 64, 32, 16) if t <= _row_tile(cols, 10)])

    def body(g_ref, r_ref, q_ref, o_ref):
        acc = g_ref[0, 0] + r_ref[0]
        for k in range(3):
            acc = acc + q_ref[k].astype(F32)
        o_ref[0] = acc

    core = lambda: lax.axis_index("c")
    chip = lambda: 2 * lax.axis_index("x") + lax.axis_index("y")
    return pl.pallas_call(
        body, name=name, grid=(rh // tr,),
        in_specs=[pl.BlockSpec((1, 1, tr, cols), lambda i: (chip(), core(), i, 0)),
                  pl.BlockSpec((1, tr, cols), lambda i: (chip(), i, 0)),
                  pl.BlockSpec((3, tr, cols), lambda i: (0, i, 0))],
        out_specs=pl.BlockSpec((1, tr, cols), lambda i: (core(), i, 0)),
        out_shape=jax.ShapeDtypeStruct((2, rh, cols), F32),
        compiler_params=_params(("parallel",)),
    )(grad, sib, others)


def _cast_bf16(w, name):
    cols = w.shape[1]
    return _rowwise(lambda t: t, [w], [], [(cols, BF16)], tl=_row_tile(cols, 4), name=name)[0]


_WEIGHTS = ("l0_norm_w", "l0_w_in", "l0_s5_lambda_re", "l0_s5_lambda_im", "l0_s5_log_step", "l0_s5_b_re",
            "l0_s5_b_im", "l0_s5_c_re", "l0_s5_c_im", "l0_s5_d", "l0_s5_w_glu", "l0_s5_b_glu", "l0_ssd_conv_w",
            "l0_ssd_conv_b", "l0_ssd_dt_bias", "l0_ssd_a_log", "l0_ssd_d", "l0_ssd_norm_w", "l0_w_out",
            "l1_norm_w", "l1_w_in", "l1_fox_b_f", "l1_w_out", "final_norm_w")
_COL_SHARDED = ("l0_w_in", "l1_w_in")
_ROW_SHARDED = ("l0_s5_w_glu", "l0_w_out", "l1_w_out")
_BIG = ("l0_w_in", "l0_s5_w_glu", "l0_w_out", "l1_w_in", "l1_w_out")
_CONV = "l0_ssd_conv_w"
_SMALL = tuple(n for n in _WEIGHTS if n not in _BIG and n != _CONV)


def _pack(arrays):
    flat = jnp.concatenate([a.reshape(-1).astype(F32) for a in arrays])
    size = flat.shape[0]
    padded = -(-size // (512 * LANES)) * (512 * LANES)
    return jnp.pad(flat, (0, padded - size)).reshape(-1, LANES)


def _unpack(buf, like):
    flat = buf.reshape(-1)
    out, pos = [], 0
    for a in like:
        out.append(flat[pos:pos + a.size].reshape(a.shape))
        pos += a.size
    return out


def _step(p):
    x, tgt = p["x"][0], p["loss_target"][0]
    d = x.shape[1]
    chip = 2 * lax.axis_index("x") + lax.axis_index("y")

    halves = lambda w: w.reshape((2, w.shape[0] // 2) + w.shape[1:])
    shard = {n: halves(_cast_bf16(p[n], "cast_" + n)) for n in _BIG}
    shard[_CONV] = halves(p[_CONV])

    def whole(n, g):
        g = lax.dynamic_update_index_in_dim(g, shard[n][None], chip, 0)
        rows, cols = 2 * g.shape[2], g.shape[3]
        if n in _ROW_SHARDED:
            return g.reshape(N_SHARD * rows, cols)
        return g.reshape(N_SHARD, rows, cols).transpose(1, 0, 2).reshape(rows, N_SHARD * cols)

    mix = 2 * d
    s5w = mix // 4
    ssdw = mix - s5w
    n_ssd = ssdw // SSD_HEAD_DIM
    xbcw = ssdw + 2 * SSD_GROUPS * SSD_STATE
    fw = p["l1_w_out"].shape[0] * N_SHARD
    n_fox = fw // FOX_HEAD_DIM
    o1, o2, o3 = 2 * s5w, 2 * s5w + ssdw, 2 * s5w + ssdw + xbcw
    now = ("l0_w_in", _CONV)
    got = dict(zip(now, _run_rider(_gather_rider([shard[n] for n in now]), "gather_first")))
    w0 = whole("l0_w_in", got["l0_w_in"])
    wb = {"w0_ug": w0[:, :o1], "w0_z": w0[:, o1:o2], "w0_xbc": w0[:, o2:o3],
          "w0_dt": _pad_heads(w0[:, o3:], n_ssd // SSD_GROUPS)}
    sm = {n: p[n] for n in _SMALL}
    sm[_CONV] = whole(_CONV, got[_CONV])
    later = {"l0_in_xbc": ("l0_s5_w_glu", "l0_w_out"), "ssd_scan": ("l1_w_in", "l1_w_out")}
    early = ("l0_w_out", "l1_w_in", "l1_w_out")

    def by_shard(g, n):
        if n in _ROW_SHARDED:
            return g.reshape(N_SHARD, 2, g.shape[0] // (2 * N_SHARD), g.shape[1])
        rows, cols = g.shape[0], g.shape[1] // N_SHARD
        return g.reshape(2, rows // 2, N_SHARD, cols).transpose(2, 0, 1, 3)

    big, sib, others = {}, {}, {}

    def presummed(names, grads, tag):
        for n in names:
            big[n] = by_shard(grads[n], n)
        sib.update(zip(names, _sibling_halves([big[n] for n in names], "reduce_sibling_" + tag)))
        return [_presum(big[n], sib[n], "presum_" + n) for n in names]

    class Plan:
        def gather_rider(self, host):
            return _gather_rider([shard[n] for n in later[host]])

        def gathered(self, host, carried):
            w = {n: whole(n, g) for n, g in zip(later[host], carried)}
            if host == "l0_in_xbc":
                return {"w_glu": w["l0_s5_w_glu"], "w0_out": w["l0_w_out"]}
            w1 = w["l1_w_in"]
            return {"w1_main": w1[:, :4 * fw], "w1_f": jnp.pad(w1[:, 4 * fw:], ((0, 0), (0, LANES - n_fox))),
                    "w1_out": w["l1_w_out"]}

        def reduce_rider(self, grads):
            return _chip_rider(presummed(early, grads, "early"))

        def reduced(self, carried):
            others.update(zip(early, carried))

    loss_lanes, dx, grads = _local_step(x, tgt, wb, sm, Plan())

    small_like = [p[n] for n in _SMALL] + [sm[_CONV], jnp.zeros((1,), F32)]
    small_sum = _sum_slots(_gather_all(_pack([grads[n] for n in _SMALL] + [grads[_CONV], jnp.sum(loss_lanes)]),
                                       "gather_small"), "sum_small")
    *small_grads, conv_grad, loss = _unpack(small_sum, small_like)
    taps, ccols = p[_CONV].shape
    conv_grad = lax.dynamic_slice(conv_grad, (0, chip * ccols), (taps, ccols))
    final = dict(zip(_SMALL, small_grads))
    final[_CONV] = conv_grad

    late = tuple(n for n in _BIG if n not in early)
    others.update(zip(late, _run_rider(_chip_rider(presummed(late, grads, "late")), "reduce_chips_late")))
    done = [_finish_half(big[n], sib[n], others[n], "finish_" + n) for n in _BIG]
    for n, full in zip(_BIG, _join_halves(done, "join_halves")):
        final[n] = full.reshape(p[n].shape)

    delta, new_m, new_v = {}, {}, {}
    for n in _BIG:
        delta[n], new_m[n], new_v[n] = _adamw(p[n], final[n], p["m_" + n], p["v_" + n], "adamw_" + n)
    rest = _SMALL + (_CONV,)
    packed = [_pack([t[n] for n in rest]) for t in
              ({n: p[n] for n in rest}, final, {n: p["m_" + n] for n in rest}, {n: p["v_" + n] for n in rest})]
    for dst, buf in zip((delta, new_m, new_v), _adamw(*packed, "adamw_small")):
        dst.update(zip(rest, _unpack(buf, [p[n] for n in rest])))

    outs = [loss.reshape(()), dx[None]]
    for group in (final, delta, new_m, new_v):
        outs += [group[n].reshape(p[n].shape) for n in _WEIGHTS]
    return tuple(outs)


_INPUTS = ("x",) + _WEIGHTS + ("loss_target",) + tuple("m_" + n for n in _WEIGHTS) + tuple("v_" + n for n in _WEIGHTS)


def kernel(x, l0_norm_w, l0_w_in, l0_s5_lambda_re, l0_s5_lambda_im, l0_s5_log_step, l0_s5_b_re, l0_s5_b_im, l0_s5_c_re,
           l0_s5_c_im, l0_s5_d, l0_s5_w_glu, l0_s5_b_glu, l0_ssd_conv_w, l0_ssd_conv_b, l0_ssd_dt_bias,
           l0_ssd_a_log, l0_ssd_d, l0_ssd_norm_w, l0_w_out, l1_norm_w, l1_w_in, l1_fox_b_f, l1_w_out,
           final_norm_w, loss_target, m_l0_norm_w, m_l0_w_in, m_l0_s5_lambda_re, m_l0_s5_lambda_im,
           m_l0_s5_log_step, m_l0_s5_b_re, m_l0_s5_b_im, m_l0_s5_c_re, m_l0_s5_c_im, m_l0_s5_d,
           m_l0_s5_w_glu, m_l0_s5_b_glu, m_l0_ssd_conv_w, m_l0_ssd_conv_b, m_l0_ssd_dt_bias, m_l0_ssd_a_log,
           m_l0_ssd_d, m_l0_ssd_norm_w, m_l0_w_out, m_l1_norm_w, m_l1_w_in, m_l1_fox_b_f, m_l1_w_out,
           m_final_norm_w, v_l0_norm_w, v_l0_w_in, v_l0_s5_lambda_re, v_l0_s5_lambda_im, v_l0_s5_log_step,
           v_l0_s5_b_re, v_l0_s5_b_im, v_l0_s5_c_re, v_l0_s5_c_im, v_l0_s5_d, v_l0_s5_w_glu, v_l0_s5_b_glu,
           v_l0_ssd_conv_w, v_l0_ssd_conv_b, v_l0_ssd_dt_bias, v_l0_ssd_a_log, v_l0_ssd_d, v_l0_ssd_norm_w,
           v_l0_w_out, v_l1_norm_w, v_l1_w_in, v_l1_fox_b_f, v_l1_w_out, v_final_norm_w):
    values = (x, l0_norm_w, l0_w_in, l0_s5_lambda_re, l0_s5_lambda_im, l0_s5_log_step, l0_s5_b_re, l0_s5_b_im,
              l0_s5_c_re, l0_s5_c_im, l0_s5_d, l0_s5_w_glu, l0_s5_b_glu, l0_ssd_conv_w, l0_ssd_conv_b,
              l0_ssd_dt_bias, l0_ssd_a_log, l0_ssd_d, l0_ssd_norm_w, l0_w_out, l1_norm_w, l1_w_in,
              l1_fox_b_f, l1_w_out, final_norm_w, loss_target, m_l0_norm_w, m_l0_w_in,
              m_l0_s5_lambda_re, m_l0_s5_lambda_im, m_l0_s5_log_step, m_l0_s5_b_re, m_l0_s5_b_im,
              m_l0_s5_c_re, m_l0_s5_c_im, m_l0_s5_d, m_l0_s5_w_glu, m_l0_s5_b_glu, m_l0_ssd_conv_w,
              m_l0_ssd_conv_b, m_l0_ssd_dt_bias, m_l0_ssd_a_log, m_l0_ssd_d, m_l0_ssd_norm_w,
              m_l0_w_out, m_l1_norm_w, m_l1_w_in, m_l1_fox_b_f, m_l1_w_out, m_final_norm_w, v_l0_norm_w,
              v_l0_w_in, v_l0_s5_lambda_re, v_l0_s5_lambda_im, v_l0_s5_log_step, v_l0_s5_b_re,
              v_l0_s5_b_im, v_l0_s5_c_re, v_l0_s5_c_im, v_l0_s5_d, v_l0_s5_w_glu, v_l0_s5_b_glu,
              v_l0_ssd_conv_w, v_l0_ssd_conv_b, v_l0_ssd_dt_bias, v_l0_ssd_a_log, v_l0_ssd_d,
              v_l0_ssd_norm_w, v_l0_w_out, v_l1_norm_w, v_l1_w_in, v_l1_fox_b_f, v_l1_w_out,
              v_final_norm_w)
    return _step(dict(zip(_INPUTS, values)))
```

```python
import functools
import math

import jax
import jax.numpy as jnp
from jax import lax
from jax.experimental import pallas as pl
from jax.experimental.pallas import tpu as pltpu

F32 = jnp.float32
BF16 = jnp.bfloat16

S5_GROUP = 16
S5_STATE = 64
S5_EIG_CLIP = -1e-4
SSD_HEAD_DIM = 64
SSD_GROUPS = 8
SSD_STATE = 128
SSD_CONV = 4
SSD_CHUNK = 128
FOX_HEAD_DIM = 128
FOX_TILE = 1024
NORM_EPS = 1e-5
ADAM_LR = 0.001
ADAM_B1 = 0.9
ADAM_B2 = 0.999
ADAM_EPS = 1e-08
ADAM_WD = 0.01
ADAM_STEP = 10

N_SHARD = 4
N_DEV = 8
LANES = 128
VMEM_LIMIT = 56 * 1024 * 1024
MESH = pl.DeviceIdType.MESH


def _pick(dim, prefs, offs=()):
    for p in prefs:
        if dim % p == 0 and all(o % p == 0 for o in offs):
            return p
    return dim


def _params(sem=None, vmem=VMEM_LIMIT):
    return pltpu.CompilerParams(dimension_semantics=sem, vmem_limit_bytes=vmem)


class _Rider:
    def __init__(self, inputs, out_shape, sems, start, finish):
        self.inputs, self.out_shape, self.sems = list(inputs), list(out_shape), list(sems)
        self.start, self.finish = start, finish


def _hosted_call(body, *, name, grid, in_specs, out_specs, out_shape, scratch_shapes, sem, args, rider=None):
    single = not isinstance(out_shape, (list, tuple))
    out_specs = [out_specs] if single else list(out_specs)
    out_shape = [out_shape] if single else list(out_shape)
    if rider is None:
        res = pl.pallas_call(body, name=name, grid=grid, in_specs=in_specs, out_specs=out_specs,
                             out_shape=out_shape, scratch_shapes=scratch_shapes,
                             compiler_params=_params(sem))(*args)
        return (res[0] if single else res), []
    n_in, n_out, n_scr = len(in_specs), len(out_shape), len(scratch_shapes)
    n_rin, n_rout = len(rider.inputs), len(rider.out_shape)

    def carried(*refs):
        ins, refs = refs[:n_in], refs[n_in:]
        rin, refs = refs[:n_rin], refs[n_rin:]
        outs, refs = refs[:n_out], refs[n_out:]
        rout, refs = refs[:n_rout], refs[n_rout:]
        scr, rsem = refs[:n_scr], refs[n_scr:]
        ids = [pl.program_id(k) for k in range(len(grid))]
        first = functools.reduce(jnp.logical_and, [i == 0 for i in ids])
        last = functools.reduce(jnp.logical_and, [i == g - 1 for i, g in zip(ids, grid)])

        @pl.when(first)
        def _():
            rider.start(rin, rout, rsem)

        body(*ins, *outs, *scr)

        @pl.when(last)
        def _():
            rider.finish(rin, rout, rsem)

    res = pl.pallas_call(
        carried, name=name, grid=grid,
        in_specs=list(in_specs) + [_ANY] * n_rin, out_specs=out_specs + [_ANY] * n_rout,
        out_shape=out_shape + rider.out_shape,
        scratch_shapes=list(scratch_shapes) + [pltpu.SemaphoreType.DMA((k,)) for k in rider.sems],
        compiler_params=pltpu.CompilerParams(dimension_semantics=("arbitrary",) * len(grid),
                                             vmem_limit_bytes=VMEM_LIMIT, has_side_effects=True),
    )(*args, *rider.inputs)
    outs = res[:n_out]
    return (outs[0] if single else outs), list(res[n_out:])


def _matmul(a, b, *, mode="nn", dims=None, a_off=(0, 0), b_off=(0, 0), addend=None,
            out_dtype=F32, rider=None, rows_of=None, name):
    if dims is None:
        if mode == "nn":
            dims = (a.shape[0], b.shape[1], a.shape[1])
        elif mode == "nt":
            dims = (a.shape[0], b.shape[0], a.shape[1])
        else:
            dims = (a.shape[1], b.shape[1], a.shape[0])
    m, n, k = dims
    if mode == "nn":
        om, on, ok = (a_off[0],), (b_off[1],), (a_off[1], b_off[0])
    elif mode == "nt":
        om, on, ok = (a_off[0],), (b_off[0],), (a_off[1], b_off[1])
    else:
        om, on, ok = (a_off[1],), (b_off[1],), (a_off[0], b_off[0])
    tm = _pick(m, (1024, 512, 256, 128), om)
    tn = _pick(n, (1024, 768, 512, 384, 256, 128), on)
    tk = _pick(k, (1024, 512, 256, 128), ok)
    nk = k // tk
    if mode == "nn":
        a_blk, a_div = (tm, tk), (tm, tk)
        b_blk, b_div = (tk, tn), (tk, tn)
        a_map = lambda i, j, kk: (i + a_off[0] // tm, kk + a_off[1] // tk)
        b_map = lambda i, j, kk: (kk + b_off[0] // tk, j + b_off[1] // tn)
        dn = (((1,), (0,)), ((), ()))
    elif mode == "nt":
        a_blk, a_div = (tm, tk), (tm, tk)
        b_blk, b_div = (tn, tk), (tn, tk)
        a_map = lambda i, j, kk: (i + a_off[0] // tm, kk + a_off[1] // tk)
        b_map = lambda i, j, kk: (j + b_off[0] // tn, kk + b_off[1] // tk)
        dn = (((1,), (1,)), ((), ()))
    else:
        a_blk, a_div = (tk, tm), (tk, tm)
        b_blk, b_div = (tk, tn), (tk, tn)
        a_map = lambda i, j, kk: (kk + a_off[0] // tk, i + a_off[1] // tm)
        b_map = lambda i, j, kk: (kk + b_off[0] // tk, j + b_off[1] // tn)
        dn = (((0,), (0,)), ((), ()))
    assert a_off[0] % a_div[0] == 0 and a_off[1] % a_div[1] == 0, (name, a_off, a_div)
    assert b_off[0] % b_div[0] == 0 and b_off[1] % b_div[1] == 0, (name, b_off, b_div)
    has_add = addend is not None

    def body(*refs):
        if has_add:
            a_ref, b_ref, c_ref, o_ref, acc_ref = refs
        else:
            a_ref, b_ref, o_ref, acc_ref = refs
        kk = pl.program_id(2)

        @pl.when(kk == 0)
        def _():
            acc_ref[...] = jnp.zeros_like(acc_ref)

        acc_ref[...] += lax.dot_general(a_ref[...].astype(BF16), b_ref[...].astype(BF16), dn,
                                        preferred_element_type=F32)

        @pl.when(kk == nk - 1)
        def _():
            r = acc_ref[...]
            if has_add:
                r = r + c_ref[...].astype(F32)
            o_ref[...] = r.astype(o_ref.dtype)

    in_specs = [pl.BlockSpec(a_blk, a_map), pl.BlockSpec(b_blk, b_map)]
    args = [a, b]
    if has_add:
        in_specs.append(pl.BlockSpec((tm, tn), lambda i, j, kk: (i, j)))
        args.append(addend)
    if rows_of is not None:
        total, row_off, buf = rows_of
        assert row_off % tm == 0 and rider is None, (name, row_off, tm)
        if buf is not None:
            def body_into(*refs):
                body(*refs[:len(args)], *refs[len(args) + 1:])

            return pl.pallas_call(
                body_into, name=name, grid=(m // tm, n // tn, nk),
                in_specs=in_specs + [pl.BlockSpec(memory_space=pl.ANY)],
                out_specs=pl.BlockSpec((tm, tn), lambda i, j, kk: (i + row_off // tm, j)),
                out_shape=jax.ShapeDtypeStruct((total, n), out_dtype),
                input_output_aliases={len(args): 0},
                scratch_shapes=[pltpu.VMEM((tm, tn), F32)],
                compiler_params=_params(("parallel", "parallel", "arbitrary")),
            )(*args, buf)
        return pl.pallas_call(
            body, name=name, grid=(m // tm, n // tn, nk), in_specs=in_specs,
            out_specs=pl.BlockSpec((tm, tn), lambda i, j, kk: (i + row_off // tm, j)),
            out_shape=jax.ShapeDtypeStruct((total, n), out_dtype),
            scratch_shapes=[pltpu.VMEM((tm, tn), F32)],
            compiler_params=_params(("parallel", "parallel", "arbitrary")),
        )(*args)
    out, carried = _hosted_call(
        body, name=name, grid=(m // tm, n // tn, nk),
        in_specs=in_specs, out_specs=pl.BlockSpec((tm, tn), lambda i, j, kk: (i, j)),
        out_shape=jax.ShapeDtypeStruct((m, n), out_dtype),
        scratch_shapes=[pltpu.VMEM((tm, tn), F32)],
        sem=("parallel", "parallel", "arbitrary"), args=args, rider=rider)
    return out if rider is None else (out, carried)


def _rowwise(fn, rows, params, out_rows, out_accs=(), *, tl, ncol=1, name):
    rows = [r if isinstance(r, tuple) else (r, r.shape[1] // ncol, 0) for r in rows]
    n_rows, n_par, n_or, n_oa = len(rows), len(params), len(out_rows), len(out_accs)
    length = rows[0][0].shape[0]
    tl = _pick(length, [t for t in (1024, 512, 256, 128, 64, 32, 16, 8) if t <= tl])

    def body(*refs):
        row_refs = refs[:n_rows]
        par_refs = refs[n_rows:n_rows + n_par]
        or_refs = refs[n_rows + n_par:n_rows + n_par + n_or]
        oa_refs = refs[n_rows + n_par + n_or:]
        outs = fn(*[r[...] for r in row_refs], *[p[...] for p in par_refs])
        if not isinstance(outs, (tuple, list)):
            outs = (outs,)
        for r, v in zip(or_refs, outs[:n_or]):
            r[...] = v.astype(r.dtype)
        if n_oa:
            @pl.when(pl.program_id(1) == 0)
            def _():
                for r in oa_refs:
                    r[...] = jnp.zeros_like(r)

            for r, v in zip(oa_refs, outs[n_or:]):
                r[...] += v.astype(F32)

    in_specs = [pl.BlockSpec((tl, w), functools.partial(lambda j, i, b0: (i, b0 + j), b0=b0))
                for (_, w, b0) in rows]
    in_specs += [pl.BlockSpec((p.shape[0], p.shape[1] // ncol), lambda j, i: (0, j)) for p in params]
    out_specs = [pl.BlockSpec((tl, w), lambda j, i: (i, j)) for (w, _) in out_rows]
    out_specs += [pl.BlockSpec((1, w), lambda j, i: (0, j)) for w in out_accs]
    out_shape = [jax.ShapeDtypeStruct((length, ncol * w), dt) for (w, dt) in out_rows]
    out_shape += [jax.ShapeDtypeStruct((1, ncol * w), F32) for w in out_accs]
    res = pl.pallas_call(
        body, name=name, grid=(ncol, length // tl),
        in_specs=in_specs, out_specs=out_specs, out_shape=out_shape,
        compiler_params=_params(("parallel", "arbitrary" if n_oa else "parallel")),
    )(*[r[0] for r in rows], *params)
    return res


def _bdmm(xs, ws, *, addend=None, out_dtype=F32, name):
    nj, kin, kout = ws[0].shape
    xs = [x if isinstance(x, tuple) else (x, 0) for x in xs]
    length = xs[0][0].shape[0]
    tl = _pick(length, (512, 256, 128))
    n_x = len(xs)
    has_add = addend is not None

    def body(*refs):
        x_refs = refs[:n_x]
        w_refs = refs[n_x:2 * n_x]
        o_ref = refs[-1]
        acc = None
        for xr, wr in zip(x_refs, w_refs):
            t = jnp.dot(xr[...].astype(BF16), wr[0], preferred_element_type=F32)
            acc = t if acc is None else acc + t
        if has_add:
            acc = acc + refs[2 * n_x][...].astype(F32)
        o_ref[...] = acc.astype(o_ref.dtype)

    in_specs = [pl.BlockSpec((tl, kin), functools.partial(lambda i, j, b0: (i, b0 + j), b0=b0)) for (_, b0) in xs]
    in_specs += [pl.BlockSpec((1, kin, kout), lambda i, j: (j, 0, 0)) for _ in ws]
    args = [x[0] for x in xs] + list(ws)
    if has_add:
        in_specs.append(pl.BlockSpec((tl, kout), lambda i, j: (i, j)))
        args.append(addend)
    return pl.pallas_call(
        body, name=name, grid=(length // tl, nj),
        in_specs=in_specs, out_specs=pl.BlockSpec((tl, kout), lambda i, j: (i, j)),
        out_shape=jax.ShapeDtypeStruct((length, nj * kout), out_dtype),
        compiler_params=_params(("parallel", "parallel")),
    )(*args)


def _bdmm_tn_sized(x, g, nj, kin, kout, x_first, name):
    length = x.shape[0]
    tl = _pick(length, (512, 256, 128))
    nt = length // tl

    def body(x_ref, g_ref, o_ref):
        @pl.when(pl.program_id(1) == 0)
        def _():
            o_ref[...] = jnp.zeros_like(o_ref)

        o_ref[0] += lax.dot_general(x_ref[...].astype(BF16), g_ref[...].astype(BF16),
                                    (((0,), (0,)), ((), ())), preferred_element_type=F32)

    return pl.pallas_call(
        body, name=name, grid=(nj, nt),
        in_specs=[pl.BlockSpec((tl, kin), lambda j, t: (t, x_first + j)),
                  pl.BlockSpec((tl, kout), lambda j, t: (t, j))],
        out_specs=pl.BlockSpec((1, kin, kout), lambda j, t: (j, 0, 0)),
        out_shape=jax.ShapeDtypeStruct((nj, kin, kout), F32),
        compiler_params=_params(("parallel", "arbitrary")),
    )(x, g)


def _f_norm(x, w):
    return x * lax.rsqrt(jnp.mean(x * x, axis=-1, keepdims=True) + NORM_EPS) * w


def _gelu(y):
    return 0.5 * y * (1.0 + jnp.tanh(math.sqrt(2.0 / math.pi) * (y + 0.044715 * (y * y * y))))


def _sigmoid(x):
    return 1.0 / (1.0 + jnp.exp(-x))


def _silu(x):
    return x * _sigmoid(x)


def _softplus(x):
    return jnp.maximum(x, 0.0) + jnp.log(1.0 + jnp.exp(-jnp.abs(x)))


def _f_s5_gelu(yc, u, dvec):
    return _gelu(yc + dvec * u)


def _f_s5_out(yc, u, t, gate, dvec, bglu):
    gl = _gelu(yc + dvec * u)
    return gl * _sigmoid(t + bglu) * _silu(gate)


def _f_ssd_out(y, xs, z, dpar, nw):
    v = (y + dpar * xs) * _silu(z)
    return v * lax.rsqrt(jnp.mean(v * v, axis=-1, keepdims=True) + NORM_EPS) * nw


def _f_fox_out(att, gate):
    return att * _silu(gate)


def _norm_fwd(x, w, name):
    return _rowwise(lambda xt, wt: _f_norm(xt, wt), [x], [w], [(x.shape[1], BF16)], tl=256, name=name)[0]


def _norm_bwd(x, dh, dres, w, name):
    d = x.shape[1]

    def fn(xt, dht, drt, wt):
        _, vjp = jax.vjp(_f_norm, xt, wt)
        dx, dw = vjp(dht)
        dx = dx + drt
        return dx, dx, dw

    return _rowwise(fn, [x, dh, dres], [w], [(d, F32), (d, BF16)], [d], tl=128, name=name)


def _adamw_math(w, g, m, v):
    m = ADAM_B1 * m + (1.0 - ADAM_B1) * g
    v = ADAM_B2 * v + (1.0 - ADAM_B2) * jnp.square(g)
    m_hat = m / (1.0 - ADAM_B1 ** ADAM_STEP)
    v_hat = v / (1.0 - ADAM_B2 ** ADAM_STEP)
    delta = -ADAM_LR * (m_hat / (jnp.sqrt(v_hat) + ADAM_EPS) + ADAM_WD * w)
    return delta, m, v


def _adamw(w, g, m, v, name):
    return _elementwise(_adamw_math, [w, g, m, v], [F32] * 3, name)


def _s5_scan_fwd(bu_re, bu_im, a_re, a_im, name):
    length, rows, _ = bu_re.shape
    rb = _pick(rows, (32, 16, 8))
    tl = _pick(length, (64, 32, 16, 8))

    def body(bur_ref, bui_ref, ar_ref, ai_ref, sr_ref, si_ref, st_ref):
        @pl.when(pl.program_id(1) == 0)
        def _():
            st_ref[...] = jnp.zeros_like(st_ref)

        ar = ar_ref[...]
        ai = ai_ref[...]

        def step(l, carry):
            sr, si = carry
            nr = ar * sr - ai * si + bur_ref[l]
            ni = ar * si + ai * sr + bui_ref[l]
            sr_ref[l] = nr
            si_ref[l] = ni
            return nr, ni

        sr, si = lax.fori_loop(0, tl, step, (st_ref[0], st_ref[1]))
        st_ref[0] = sr
        st_ref[1] = si

    blk = pl.BlockSpec((tl, rb, LANES), lambda cb, t: (t, cb, 0))
    ablk = pl.BlockSpec((rb, LANES), lambda cb, t: (cb, 0))
    return pl.pallas_call(
        body, name=name, grid=(rows // rb, length // tl),
        in_specs=[blk, blk, ablk, ablk], out_specs=[blk, blk],
        out_shape=[jax.ShapeDtypeStruct(bu_re.shape, F32)] * 2,
        scratch_shapes=[pltpu.VMEM((2, rb, LANES), F32)],
        compiler_params=_params(("parallel", "arbitrary")),
    )(bu_re, bu_im, a_re, a_im)


def _s5_scan_bwd(ds_re, ds_im, s_re, s_im, a_re, a_im, name):
    length, rows, _ = ds_re.shape
    rb = _pick(rows, (32, 16, 8))
    tl = _pick(length, (64, 32, 16, 8))
    nt = length // tl

    def body(dsr_ref, dsi_ref, sr_ref, si_ref, pr_ref, pi_ref, ar_ref, ai_ref,
             gr_ref, gi_ref, dar_ref, dai_ref, st_ref):
        t = pl.program_id(1)

        @pl.when(t == 0)
        def _():
            st_ref[...] = jnp.zeros_like(st_ref)
            dar_ref[...] = jnp.zeros_like(dar_ref)
            dai_ref[...] = jnp.zeros_like(dai_ref)

        ar = ar_ref[...]
        ai = ai_ref[...]

        def adj(l, gr, gi):
            ngr = dsr_ref[l] + ar * gr + ai * gi
            ngi = dsi_ref[l] + ar * gi - ai * gr
            gr_ref[l] = ngr
            gi_ref[l] = ngi
            return ngr, ngi

        def step(idx, carry):
            gr, gi, dar, dai = carry
            l = tl - 1 - idx
            gr, gi = adj(l, gr, gi)
            pr = sr_ref[l - 1]
            pi = si_ref[l - 1]
            dar = dar + gr * pr + gi * pi
            dai = dai + gi * pr - gr * pi
            return gr, gi, dar, dai

        zero = jnp.zeros((rb, LANES), F32)
        gr, gi, dar, dai = lax.fori_loop(0, tl - 1, step, (st_ref[0], st_ref[1], zero, zero))
        gr, gi = adj(0, gr, gi)
        first = (t == nt - 1)
        pr = jnp.where(first, 0.0, pr_ref[0])
        pi = jnp.where(first, 0.0, pi_ref[0])
        dar = dar + gr * pr + gi * pi
        dai = dai + gi * pr - gr * pi
        st_ref[0] = gr
        st_ref[1] = gi
        dar_ref[...] += dar
        dai_ref[...] += dai

    blk = pl.BlockSpec((tl, rb, LANES), lambda cb, t: (nt - 1 - t, cb, 0))
    prev = pl.BlockSpec((1, rb, LANES), lambda cb, t: (jnp.maximum((nt - 1 - t) * tl - 1, 0), cb, 0))
    ablk = pl.BlockSpec((rb, LANES), lambda cb, t: (cb, 0))
    return pl.pallas_call(
        body, name=name, grid=(rows // rb, nt),
        in_specs=[blk, blk, blk, blk, prev, prev, ablk, ablk],
        out_specs=[blk, blk, ablk, ablk],
        out_shape=[jax.ShapeDtypeStruct(ds_re.shape, F32)] * 2 + [jax.ShapeDtypeStruct(a_re.shape, F32)] * 2,
        scratch_shapes=[pltpu.VMEM((2, rb, LANES), F32)],
        compiler_params=_params(("parallel", "arbitrary")),
    )(ds_re, ds_im, s_re, s_im, s_re, s_im, a_re, a_im)


def _s5_prepare(lam_re, lam_im, log_step, b_re, b_im, c_re, c_im):
    groups, state = lam_re.shape
    lr = jnp.minimum(lam_re, S5_EIG_CLIP)
    li = lam_im
    step = jnp.exp(log_step)[:, None]
    mag = jnp.exp(lr * step)
    ab_re = mag * jnp.cos(li * step)
    ab_im = mag * jnp.sin(li * step)
    denom = lr * lr + li * li
    nr = ab_re - 1.0
    ni = ab_im
    coef_re = (nr * lr + ni * li) / denom
    coef_im = (ni * lr - nr * li) / denom
    bb_re = coef_re[..., None] * b_re - coef_im[..., None] * b_im
    bb_im = coef_re[..., None] * b_im + coef_im[..., None] * b_re
    per = LANES // S5_GROUP
    nj = groups // per
    eye = jnp.eye(per, dtype=F32)

    def in_map(bb):
        return jnp.einsum('jgph,gk->jghkp', bb.reshape(nj, per, state, S5_GROUP), eye).reshape(
            nj, per * S5_GROUP, per * state)

    def out_map(cc):
        return jnp.einsum('jghp,gk->jgpkh', cc.reshape(nj, per, S5_GROUP, state), eye).reshape(
            nj, per * state, per * S5_GROUP)

    shape2 = (groups * state // LANES, LANES)
    return (ab_re.reshape(shape2), ab_im.reshape(shape2), in_map(bb_re), in_map(bb_im),
            out_map(c_re), -out_map(c_im))


def _shift_down(cur, prev8, j):
    rolled = pltpu.roll(cur, j, 0)
    pr = pltpu.roll(prev8, j, 0)
    row = lax.broadcasted_iota(jnp.int32, cur.shape, 0)
    return jnp.where(row < j, jnp.tile(pr, (cur.shape[0] // 8, 1)), rolled)


def _shift_up(cur, next8, j):
    tl = cur.shape[0]
    rolled = pltpu.roll(cur, tl - j, 0)
    nx = pltpu.roll(next8, 8 - j, 0)
    row = lax.broadcasted_iota(jnp.int32, cur.shape, 0)
    return jnp.where(row >= tl - j, jnp.tile(nx, (tl // 8, 1)), rolled)


def _conv_tiles(length, ch):
    return _pick(length, (256, 128, 64, 32, 16, 8)), _pick(ch, (1024, 512, 256, 128))


def _conv_fwd(xbc, w, b, name):
    length, ch = xbc.shape
    tl, tc = _conv_tiles(length, ch)

    def body(x_ref, p_ref, w_ref, b_ref, o_ref):
        cur = x_ref[...]
        prev8 = jnp.where(pl.program_id(1) == 0, 0.0, p_ref[...])
        pre = b_ref[...] + w_ref[SSD_CONV - 1:SSD_CONV, :] * cur
        for j in range(1, SSD_CONV):
            pre = pre + w_ref[SSD_CONV - 1 - j:SSD_CONV - j, :] * _shift_down(cur, prev8, j)
        o_ref[...] = _silu(pre)

    return pl.pallas_call(
        body, name=name, grid=(ch // tc, length // tl),
        in_specs=[pl.BlockSpec((tl, tc), lambda c, i: (i, c)),
                  pl.BlockSpec((8, tc), lambda c, i: (jnp.maximum(i * (tl // 8) - 1, 0), c)),
                  pl.BlockSpec((SSD_CONV, tc), lambda c, i: (0, c)),
                  pl.BlockSpec((1, tc), lambda c, i: (0, c))],
        out_specs=pl.BlockSpec((tl, tc), lambda c, i: (i, c)),
        out_shape=jax.ShapeDtypeStruct((length, ch), F32),
        compiler_params=_params(("parallel", "parallel")),
    )(xbc, xbc, w, b)


def _conv_bwd_pre(dxc, xbc, w, b, name):
    length, ch = xbc.shape
    tl, tc = _conv_tiles(length, ch)

    def body(d_ref, x_ref, p_ref, w_ref, b_ref, o_ref, dw_ref, db_ref):
        @pl.when(pl.program_id(1) == 0)
        def _():
            dw_ref[...] = jnp.zeros_like(dw_ref)
            db_ref[...] = jnp.zeros_like(db_ref)

        cur = x_ref[...]
        prev8 = jnp.where(pl.program_id(1) == 0, 0.0, p_ref[...])
        shifted = [cur] + [_shift_down(cur, prev8, j) for j in range(1, SSD_CONV)]
        pre = b_ref[...]
        for j in range(SSD_CONV):
            pre = pre + w_ref[SSD_CONV - 1 - j:SSD_CONV - j, :] * shifted[j]
        sg = _sigmoid(pre)
        dpre = d_ref[...] * (sg * (1.0 + pre * (1.0 - sg)))
        o_ref[...] = dpre
        db_ref[...] += jnp.sum(dpre, axis=0, keepdims=True)
        row = lax.broadcasted_iota(jnp.int32, (SSD_CONV, tc), 0)
        dw = jnp.zeros((SSD_CONV, tc), F32)
        for j in range(SSD_CONV):
            dw = dw + jnp.where(row == SSD_CONV - 1 - j, jnp.sum(dpre * shifted[j], axis=0, keepdims=True), 0.0)
        dw_ref[...] += dw

    return pl.pallas_call(
        body, name=name, grid=(ch // tc, length // tl),
        in_specs=[pl.BlockSpec((tl, tc), lambda c, i: (i, c)),
                  pl.BlockSpec((tl, tc), lambda c, i: (i, c)),
                  pl.BlockSpec((8, tc), lambda c, i: (jnp.maximum(i * (tl // 8) - 1, 0), c)),
                  pl.BlockSpec((SSD_CONV, tc), lambda c, i: (0, c)),
                  pl.BlockSpec((1, tc), lambda c, i: (0, c))],
        out_specs=[pl.BlockSpec((tl, tc), lambda c, i: (i, c)),
                   pl.BlockSpec((SSD_CONV, tc), lambda c, i: (0, c)),
                   pl.BlockSpec((1, tc), lambda c, i: (0, c))],
        out_shape=[jax.ShapeDtypeStruct((length, ch), F32), jax.ShapeDtypeStruct((SSD_CONV, ch), F32),
                   jax.ShapeDtypeStruct((1, ch), F32)],
        compiler_params=_params(("parallel", "arbitrary")),
    )(dxc, xbc, xbc, w, b)


def _conv_bwd_in(dpre, w, name):
    length, ch = dpre.shape
    tl, tc = _conv_tiles(length, ch)
    nt = length // tl

    def body(d_ref, n_ref, w_ref, o_ref):
        cur = d_ref[...]
        next8 = jnp.where(pl.program_id(1) == nt - 1, 0.0, n_ref[...])
        acc = w_ref[SSD_CONV - 1:SSD_CONV, :] * cur
        for j in range(1, SSD_CONV):
            acc = acc + w_ref[SSD_CONV - 1 - j:SSD_CONV - j, :] * _shift_up(cur, next8, j)
        o_ref[...] = acc.astype(o_ref.dtype)

    return pl.pallas_call(
        body, name=name, grid=(ch // tc, nt),
        in_specs=[pl.BlockSpec((tl, tc), lambda c, i: (i, c)),
                  pl.BlockSpec((8, tc), lambda c, i: (jnp.minimum((i + 1) * (tl // 8), length // 8 - 1), c)),
                  pl.BlockSpec((SSD_CONV, tc), lambda c, i: (0, c))],
        out_specs=pl.BlockSpec((tl, tc), lambda c, i: (i, c)),
        out_shape=jax.ShapeDtypeStruct((length, ch), BF16),
        compiler_params=_params(("parallel", "parallel")),
    )(dpre, dpre, w)


def _split(x, terms):
    parts = []
    for _ in range(terms):
        part = x.astype(BF16)
        parts.append(part)
        x = x - part.astype(F32)
    return parts


def _dot(a, b, dn=(((1,), (0,)), ((), ()))):
    return lax.dot_general(a, b, dn, preferred_element_type=F32)


_NN = (((1,), (0,)), ((), ()))
_NT = (((1,), (1,)), ((), ()))
_TN = (((0,), (0,)), ((), ()))


def _pdot(parts, sel, dn=_NN):
    return functools.reduce(lambda a, b: a + b, [_dot(part, sel, dn) for part in parts])


def _pdotr(sel, parts, dn=_NN):
    return functools.reduce(lambda a, b: a + b, [_dot(sel, part, dn) for part in parts])


def _dot3(x, sel, dn=_NN):
    return _pdot(_split(x, 3), sel, dn)


def _dot3r(sel, x, dn=_NN):
    return _pdotr(sel, _split(x, 3), dn)


def _dot2(x, sel, dn=_NN):
    return _pdot(_split(x, 2), sel, dn)


def _iota2(shape, axis):
    return lax.broadcasted_iota(jnp.int32, shape, axis)


def _ssd_masks():
    q = SSD_CHUNK
    r, c = _iota2((q, q), 0), _iota2((q, q), 1)
    tril = (c <= r)
    return r, c, tril


def _pair_sel(i):
    r, c, _ = _ssd_masks()
    return (r == 2 * i + c // SSD_HEAD_DIM).astype(BF16)


def _pair_sel_t(i):
    r, c, _ = _ssd_masks()
    return (c == 2 * i + r // SSD_HEAD_DIM).astype(F32)


def _ssd_common(dtraw, dtb, ap, b_t, c_t):
    r, c, tril = _ssd_masks()
    dt = _softplus(dtraw + dtb)
    la = dt * ap
    tril_b = tril.astype(BF16)
    cum = _dot3r(tril_b, la)
    rem = _dot3r((c > r).astype(BF16), la)
    total = _dot3(la, jnp.ones((SSD_CHUNK, SSD_CHUNK), BF16), _TN)
    scores = _dot(c_t, b_t, _NT)
    picks = {"dt": _split(dt, 2), "cum": _split(cum, 2), "rem": _split(rem, 2), "total": _split(total, 2)}
    return dt, la, cum, picks, scores, tril


def _head_decay(cum_parts, h, tril):
    r, c, _ = _ssd_masks()
    cq = _pdot(cum_parts, (r == h).astype(BF16))
    ck = _pdotr((c == h).astype(BF16), cum_parts, _NT)
    return jnp.exp(jnp.where(tril, cq - ck, -jnp.inf))


def _ssd_tiles(xc, n_heads):
    hpg = n_heads // SSD_GROUPS
    wg = hpg * SSD_HEAD_DIM
    xw = n_heads * SSD_HEAD_DIM
    return hpg, wg, xw // wg, xw // SSD_STATE


def _ssd_fwd(xc, dtraw, dtb, ap, n_heads, name, rider=None):
    length = xc.shape[0]
    q = SSD_CHUNK
    nc = length // q
    hpg, wg, _, b_blk0 = _ssd_tiles(xc, n_heads)
    c_blk0 = b_blk0 + SSD_GROUPS
    npair = hpg // 2

    def body(x_ref, b_ref, c_ref, dt_ref, dtb_ref, ap_ref, y_ref, st_ref, s_ref):
        @pl.when(pl.program_id(1) == 0)
        def _():
            s_ref[...] = jnp.zeros_like(s_ref)

        st_ref[0, 0] = s_ref[...]
        b_t = b_ref[...].astype(BF16)
        c_t = c_ref[...].astype(BF16)
        dt, la, cum, picks, scores, tril = _ssd_common(dt_ref[...], dtb_ref[...], ap_ref[...], b_t, c_t)
        lane = _iota2((q, q), 1)
        for i in range(npair):
            sel = _pair_sel(i)
            xp = x_ref[:, i * LANES:(i + 1) * LANES]
            xd = xp * _pdot(picks["dt"], sel)
            xd_b = xd.astype(BF16)
            s_prev = s_ref[i * LANES:(i + 1) * LANES, :]
            y = _dot(c_t, s_prev.astype(BF16), _NT) * jnp.exp(_pdot(picks["cum"], sel))
            for hh in range(2):
                own = (lane // SSD_HEAD_DIM) == hh
                wm = scores * _head_decay(picks["cum"], 2 * i + hh, tril)
                y = y + _dot(wm.astype(BF16), jnp.where(own, xd_b, 0))
            y_ref[:, i * LANES:(i + 1) * LANES] = y
            xw_b = (xd * jnp.exp(_pdot(picks["rem"], sel))).astype(BF16)
            grow = jnp.exp(_pdotr(sel, picks["total"], _TN))
            s_ref[i * LANES:(i + 1) * LANES, :] = grow * s_prev + _dot(xw_b, b_t, _TN)

    outs, carried = _hosted_call(
        body, name=name, grid=(SSD_GROUPS, nc),
        in_specs=[pl.BlockSpec((q, wg), lambda g, c: (c, g)),
                  pl.BlockSpec((q, SSD_STATE), lambda g, c: (c, b_blk0 + g)),
                  pl.BlockSpec((q, SSD_STATE), lambda g, c: (c, c_blk0 + g)),
                  pl.BlockSpec((q, LANES), lambda g, c: (c, g)),
                  pl.BlockSpec((1, LANES), lambda g, c: (0, g)),
                  pl.BlockSpec((1, LANES), lambda g, c: (0, g))],
        out_specs=[pl.BlockSpec((q, wg), lambda g, c: (c, g)),
                   pl.BlockSpec((1, 1, wg, SSD_STATE), lambda g, c: (c, g, 0, 0))],
        out_shape=[jax.ShapeDtypeStruct((length, SSD_GROUPS * wg), F32),
                   jax.ShapeDtypeStruct((nc, SSD_GROUPS, wg, SSD_STATE), F32)],
        scratch_shapes=[pltpu.VMEM((wg, SSD_STATE), F32)],
        sem=("parallel", "arbitrary"), args=(xc, xc, xc, dtraw, dtb, ap), rider=rider)
    return outs[0], outs[1], carried


def _ssd_bwd(dy, dxa, xc, dtraw, states, dtb, ap, n_heads, name, rider=None):
    length = xc.shape[0]
    q = SSD_CHUNK
    nc = length // q
    hpg, wg, _, b_blk0 = _ssd_tiles(xc, n_heads)
    c_blk0 = b_blk0 + SSD_GROUPS
    npair = hpg // 2

    def body(dy_ref, dxa_ref, x_ref, b_ref, c_ref, dt_ref, st_ref, dtb_ref, ap_ref,
             dx_ref, db_ref, dc_ref, ddt_ref, ddtb_ref, dap_ref, ds_ref):
        @pl.when(pl.program_id(1) == 0)
        def _():
            ds_ref[...] = jnp.zeros_like(ds_ref)
            ddtb_ref[...] = jnp.zeros_like(ddtb_ref)
            dap_ref[...] = jnp.zeros_like(dap_ref)

        b_f = b_ref[...]
        c_f = c_ref[...]
        b_t = b_f.astype(BF16)
        c_t = c_f.astype(BF16)
        dtraw_t = dt_ref[...]
        dt, la, cum, picks, scores, tril = _ssd_common(dtraw_t, dtb_ref[...], ap_ref[...], b_t, c_t)
        r, c, _ = _ssd_masks()
        lane = c
        ones_b = jnp.ones((q, q), BF16)
        dcum = jnp.zeros((q, q), F32)
        drem = jnp.zeros((q, q), F32)
        dtot = jnp.zeros((q, q), F32)
        ddt = jnp.zeros((q, q), F32)
        dscores = jnp.zeros((q, q), F32)
        db_acc = jnp.zeros((q, SSD_STATE), F32)
        dc_acc = jnp.zeros((q, SSD_STATE), F32)
        for i in range(npair):
            sel = _pair_sel(i)
            xp = x_ref[:, i * LANES:(i + 1) * LANES]
            dyp = dy_ref[:, i * LANES:(i + 1) * LANES]
            dyp_b = dyp.astype(BF16)
            dtp = _pdot(picks["dt"], sel)
            ecum = jnp.exp(_pdot(picks["cum"], sel))
            wrem = jnp.exp(_pdot(picks["rem"], sel))
            xd = xp * dtp
            xd_b = xd.astype(BF16)
            s_prev = s_prev_f = st_ref[0, 0, i * LANES:(i + 1) * LANES, :]
            ds1 = ds_ref[i * LANES:(i + 1) * LANES, :]
            ds1_b = ds1.astype(BF16)
            dxd = jnp.zeros((q, LANES), F32)
            for hh in range(2):
                h = 2 * i + hh
                own = (lane // SSD_HEAD_DIM) == hh
                decay = _head_decay(picks["cum"], h, tril)
                wm = scores * decay
                dwm = _dot(jnp.where(own, dyp_b, 0), xd_b, _NT)
                dxd = dxd + jnp.where(own, _dot(wm.astype(BF16), dyp_b, _TN), 0.0)
                dscores = dscores + dwm * decay
                e = (dwm * wm).astype(BF16)
                put = (c == h).astype(BF16)
                dcum = dcum + _dot(e, put) - _dot(e, put, _TN)
            t_mat = _dot(c_t, s_prev.astype(BF16), _NT)
            d_t = (dyp * ecum).astype(BF16)
            dc_acc = dc_acc + _dot(d_t, s_prev.astype(BF16))
            ds_prev = _dot(d_t, c_t, _TN)
            dcum = dcum + _dot2(dyp * t_mat * ecum, sel, _NT)
            grow = jnp.exp(_pdotr(sel, picks["total"], _TN))
            ds_prev = ds_prev + grow * ds1
            zs = jnp.sum(ds1 * s_prev_f * grow, axis=1, keepdims=True)
            dtot = dtot + _pdotr(ones_b, _split(zs * _pair_sel_t(i), 2))
            xw = xd * wrem
            dxw = _dot(b_t, ds1_b, _NT)
            db_acc = db_acc + _dot(xw.astype(BF16), ds1_b)
            dxd = dxd + dxw * wrem
            drem = drem + _dot2(dxw * xw, sel, _NT)
            dx_ref[:, i * LANES:(i + 1) * LANES] = dxd * dtp + dxa_ref[:, i * LANES:(i + 1) * LANES]
            ddt = ddt + _dot2(dxd * xp, sel, _NT)
            ds_ref[i * LANES:(i + 1) * LANES, :] = ds_prev
        ds_b = dscores.astype(BF16)
        dc_ref[...] = dc_acc + _dot(ds_b, b_t)
        db_ref[...] = db_acc + _dot(ds_b, c_t, _TN)
        dla = (_dot3r(tril.astype(BF16), dcum, _TN) + _dot3r((c > r).astype(BF16), drem, _TN) + dtot)
        ddt = ddt + dla * ap_ref[...]
        dap_ref[...] += jnp.sum(dla * dt, axis=0, keepdims=True)
        ddtraw = ddt * _sigmoid(dtraw_t + dtb_ref[...])
        ddt_ref[...] = ddtraw.astype(ddt_ref.dtype)
        ddtb_ref[...] += jnp.sum(ddtraw, axis=0, keepdims=True)

    rev = lambda g, c: (nc - 1 - c, g)
    outs, carried = _hosted_call(
        body, name=name, grid=(SSD_GROUPS, nc),
        in_specs=[pl.BlockSpec((q, wg), rev),
                  pl.BlockSpec((q, wg), rev),
                  pl.BlockSpec((q, wg), rev),
                  pl.BlockSpec((q, SSD_STATE), lambda g, c: (nc - 1 - c, b_blk0 + g)),
                  pl.BlockSpec((q, SSD_STATE), lambda g, c: (nc - 1 - c, c_blk0 + g)),
                  pl.BlockSpec((q, LANES), rev),
                  pl.BlockSpec((1, 1, wg, SSD_STATE), lambda g, c: (nc - 1 - c, g, 0, 0)),
                  pl.BlockSpec((1, LANES), lambda g, c: (0, g)),
                  pl.BlockSpec((1, LANES), lambda g, c: (0, g))],
        out_specs=[pl.BlockSpec((q, wg), rev),
                   pl.BlockSpec((q, SSD_STATE), rev),
                   pl.BlockSpec((q, SSD_STATE), rev),
                   pl.BlockSpec((q, LANES), rev),
                   pl.BlockSpec((1, LANES), lambda g, c: (0, g)),
                   pl.BlockSpec((1, LANES), lambda g, c: (0, g))],
        out_shape=[jax.ShapeDtypeStruct((length, SSD_GROUPS * wg), F32),
                   jax.ShapeDtypeStruct((length, SSD_GROUPS * SSD_STATE), F32),
                   jax.ShapeDtypeStruct((length, SSD_GROUPS * SSD_STATE), F32),
                   jax.ShapeDtypeStruct((length, SSD_GROUPS * LANES), BF16),
                   jax.ShapeDtypeStruct((1, SSD_GROUPS * LANES), F32),
                   jax.ShapeDtypeStruct((1, SSD_GROUPS * LANES), F32)],
        scratch_shapes=[pltpu.VMEM((wg, SSD_STATE), F32)],
        sem=("parallel", "arbitrary"), args=(dy, dxa, xc, xc, xc, dtraw, states, dtb, ap), rider=rider)
    return tuple(outs) + (carried,)


def _fox_cumsum(fraw, bf, name):
    length = fraw.shape[0]
    q = 128

    def body(f_ref, b_ref, o_ref, carry_ref):
        @pl.when(pl.program_id(0) == 0)
        def _():
            carry_ref[...] = jnp.zeros_like(carry_ref)

        lf = -_softplus(-(f_ref[...] + b_ref[...]))
        r, c = _iota2((q, q), 0), _iota2((q, q), 1)
        o_ref[...] = _dot3r((c <= r).astype(BF16), lf) + carry_ref[...]
        carry_ref[...] += jnp.sum(lf, axis=0, keepdims=True)

    return pl.pallas_call(
        body, name=name, grid=(length // q,),
        in_specs=[pl.BlockSpec((q, LANES), lambda i: (i, 0)), pl.BlockSpec((1, LANES), lambda i: (0, 0))],
        out_specs=pl.BlockSpec((q, LANES), lambda i: (i, 0)),
        out_shape=jax.ShapeDtypeStruct((length, LANES), F32),
        scratch_shapes=[pltpu.VMEM((1, LANES), F32)],
        compiler_params=_params(("arbitrary",)),
    )(fraw, bf)


def _fox_cumsum_bwd(dc, fraw, bf, name):
    length = fraw.shape[0]
    q = 128
    nt = length // q

    def body(d_ref, f_ref, b_ref, o_ref, db_ref, carry_ref):
        @pl.when(pl.program_id(0) == 0)
        def _():
            carry_ref[...] = jnp.zeros_like(carry_ref)
            db_ref[...] = jnp.zeros_like(db_ref)

        d = d_ref[...]
        r, c = _iota2((q, q), 0), _iota2((q, q), 1)
        dlf = _dot3r((c >= r).astype(BF16), d) + carry_ref[...]
        carry_ref[...] += jnp.sum(d, axis=0, keepdims=True)
        df = dlf * _sigmoid(-(f_ref[...] + b_ref[...]))
        o_ref[...] = df.astype(o_ref.dtype)
        db_ref[...] += jnp.sum(df, axis=0, keepdims=True)

    rev = lambda i: (nt - 1 - i, 0)
    return pl.pallas_call(
        body, name=name, grid=(nt,),
        in_specs=[pl.BlockSpec((q, LANES), rev), pl.BlockSpec((q, LANES), rev),
                  pl.BlockSpec((1, LANES), lambda i: (0, 0))],
        out_specs=[pl.BlockSpec((q, LANES), rev), pl.BlockSpec((1, LANES), lambda i: (0, 0))],
        out_shape=[jax.ShapeDtypeStruct((length, LANES), BF16), jax.ShapeDtypeStruct((1, LANES), F32)],
        scratch_shapes=[pltpu.VMEM((1, LANES), F32)],
        compiler_params=_params(("arbitrary",)),
    )(dc, fraw, bf)


def _fox_scores(q_ref, k_ref, cq_ref, ck_ref, diagonal):
    scale = 1.0 / math.sqrt(FOX_HEAD_DIM)
    s = _dot(q_ref[...].astype(BF16), k_ref[...].astype(BF16), _NT) * scale + (cq_ref[0] - ck_ref[0])
    if diagonal:
        s = jnp.where(_iota2(s.shape, 1) <= _iota2(s.shape, 0), s, -jnp.inf)
    return s


def _fox_tiles(i, j, step):
    @pl.when(j < i)
    def _():
        step(False)

    @pl.when(j == i)
    def _():
        step(True)


def _fox_fwd(qkvg, c_col, c_row, n_heads, name):
    length = qkvg.shape[0]
    hd = FOX_HEAD_DIM
    tq = _pick(length, (FOX_TILE, 512, 256, 128))
    nq = length // tq

    def body(q_ref, k_ref, v_ref, cq_ref, ck_ref, o_ref, lse_ref, m_ref, l_ref, acc_ref):
        i, j = pl.program_id(1), pl.program_id(2)

        @pl.when(j == 0)
        def _():
            m_ref[...] = jnp.full_like(m_ref, -jnp.inf)
            l_ref[...] = jnp.zeros_like(l_ref)
            acc_ref[...] = jnp.zeros_like(acc_ref)

        def step(diagonal):
            s = _fox_scores(q_ref, k_ref, cq_ref, ck_ref, diagonal)
            m_new = jnp.maximum(m_ref[...], jnp.max(s, axis=1, keepdims=True))
            alpha = jnp.exp(m_ref[...] - m_new)
            p = jnp.exp(s - m_new)
            l_ref[...] = alpha * l_ref[...] + jnp.sum(p, axis=1, keepdims=True)
            acc_ref[...] = alpha * acc_ref[...] + _dot(p.astype(BF16), v_ref[...].astype(BF16))
            m_ref[...] = m_new

        _fox_tiles(i, j, step)

        @pl.when(j == nq - 1)
        def _():
            o_ref[...] = acc_ref[...] / l_ref[...]
            lse_ref[0] = m_ref[...] + jnp.log(l_ref[...])

    kmap = lambda off: (lambda h, i, j: (jnp.minimum(j, i), off * n_heads + h))
    return pl.pallas_call(
        body, name=name, grid=(n_heads, nq, nq),
        in_specs=[pl.BlockSpec((tq, hd), lambda h, i, j: (i, h)),
                  pl.BlockSpec((tq, hd), kmap(1)),
                  pl.BlockSpec((tq, hd), kmap(2)),
                  pl.BlockSpec((1, tq, 1), lambda h, i, j: (h, i, 0)),
                  pl.BlockSpec((1, 1, tq), lambda h, i, j: (h, 0, jnp.minimum(j, i)))],
        out_specs=[pl.BlockSpec((tq, hd), lambda h, i, j: (i, h)),
                   pl.BlockSpec((1, tq, 1), lambda h, i, j: (h, i, 0))],
        out_shape=[jax.ShapeDtypeStruct((length, n_heads * hd), F32),
                   jax.ShapeDtypeStruct((n_heads, length, 1), F32)],
        scratch_shapes=[pltpu.VMEM((tq, 1), F32), pltpu.VMEM((tq, 1), F32), pltpu.VMEM((tq, hd), F32)],
        compiler_params=_params(("parallel", "parallel", "arbitrary")),
    )(qkvg, qkvg, qkvg, c_col, c_row)


def _fox_bwd_q(qkvg, datt, lse, c_col, c_row, n_heads, name):
    length = qkvg.shape[0]
    hd = FOX_HEAD_DIM
    tq = _pick(length, (FOX_TILE, 512, 256, 128))
    nq = length // tq
    scale = 1.0 / math.sqrt(hd)

    def body(q_ref, k_ref, v_ref, do_ref, lse_ref, cq_ref, ck_ref, dq_ref, dsum_ref, a1_ref, a2_ref, d_ref):
        i, j = pl.program_id(1), pl.program_id(2)

        @pl.when(j == 0)
        def _():
            a1_ref[...] = jnp.zeros_like(a1_ref)
            a2_ref[...] = jnp.zeros_like(a2_ref)
            d_ref[...] = jnp.zeros_like(d_ref)

        def step(diagonal):
            s = _fox_scores(q_ref, k_ref, cq_ref, ck_ref, diagonal)
            p = jnp.exp(s - lse_ref[0])
            pdp = p * _dot(do_ref[...].astype(BF16), v_ref[...].astype(BF16), _NT)
            d_ref[...] += jnp.sum(pdp, axis=1, keepdims=True)
            k_b = k_ref[...].astype(BF16)
            a1_ref[...] += _dot(pdp.astype(BF16), k_b)
            a2_ref[...] += _dot(p.astype(BF16), k_b)

        _fox_tiles(i, j, step)

        @pl.when(j == nq - 1)
        def _():
            dq_ref[...] = ((a1_ref[...] - d_ref[...] * a2_ref[...]) * scale).astype(dq_ref.dtype)
            dsum_ref[0] = d_ref[...]

    kmap = lambda off: (lambda h, i, j: (jnp.minimum(j, i), off * n_heads + h))
    qmap = lambda h, i, j: (i, h)
    col = pl.BlockSpec((1, tq, 1), lambda h, i, j: (h, i, 0))
    return pl.pallas_call(
        body, name=name, grid=(n_heads, nq, nq),
        in_specs=[pl.BlockSpec((tq, hd), qmap), pl.BlockSpec((tq, hd), kmap(1)), pl.BlockSpec((tq, hd), kmap(2)),
                  pl.BlockSpec((tq, hd), qmap), col, col,
                  pl.BlockSpec((1, 1, tq), lambda h, i, j: (h, 0, jnp.minimum(j, i)))],
        out_specs=[pl.BlockSpec((tq, hd), qmap), col],
        out_shape=[jax.ShapeDtypeStruct((length, n_heads * hd), BF16),
                   jax.ShapeDtypeStruct((n_heads, length, 1), F32)],
        scratch_shapes=[pltpu.VMEM((tq, hd), F32), pltpu.VMEM((tq, hd), F32), pltpu.VMEM((tq, 1), F32)],
        compiler_params=_params(("parallel", "parallel", "arbitrary")),
    )(qkvg, qkvg, qkvg, datt, lse, c_col, c_row)


def _fox_bwd_kv(qkvg, datt, lse, dsum, c_col, c_row, n_heads, name):
    length = qkvg.shape[0]
    hd = FOX_HEAD_DIM
    tq = _pick(length, (FOX_TILE, 512, 256, 128))
    nq = length // tq
    scale = 1.0 / math.sqrt(hd)

    def body(q_ref, k_ref, v_ref, do_ref, lse_ref, dsum_ref, cq_ref, ck_ref, dk_ref, dv_ref, dck_ref,
             dk_acc, dv_acc, dc_acc):
        j, i = pl.program_id(1), pl.program_id(2)

        @pl.when(i == 0)
        def _():
            dk_acc[...] = jnp.zeros_like(dk_acc)
            dv_acc[...] = jnp.zeros_like(dv_acc)
            dc_acc[...] = jnp.zeros_like(dc_acc)

        def step(diagonal):
            s = _fox_scores(q_ref, k_ref, cq_ref, ck_ref, diagonal)
            p = jnp.exp(s - lse_ref[0])
            do_b = do_ref[...].astype(BF16)
            dv_acc[...] += _dot(p.astype(BF16), do_b, _TN)
            dp = _dot(do_b, v_ref[...].astype(BF16), _NT)
            ds = p * (dp - dsum_ref[0])
            dk_acc[...] += _dot(ds.astype(BF16), q_ref[...].astype(BF16), _TN)
            dc_acc[...] -= jnp.sum(ds, axis=0, keepdims=True)

        _fox_tiles(i, j, step)

        @pl.when(i == nq - 1)
        def _():
            dk_ref[...] = (dk_acc[...] * scale).astype(dk_ref.dtype)
            dv_ref[...] = dv_acc[...].astype(dv_ref.dtype)
            dck_ref[0] = dc_acc[...]

    qmap = lambda h, j, i: (jnp.maximum(i, j), h)
    kmap = lambda off: (lambda h, j, i: (j, off * n_heads + h))
    col = pl.BlockSpec((1, tq, 1), lambda h, j, i: (h, jnp.maximum(i, j), 0))
    return pl.pallas_call(
        body, name=name, grid=(n_heads, nq, nq),
        in_specs=[pl.BlockSpec((tq, hd), qmap), pl.BlockSpec((tq, hd), kmap(1)), pl.BlockSpec((tq, hd), kmap(2)),
                  pl.BlockSpec((tq, hd), qmap), col, col, col,
                  pl.BlockSpec((1, 1, tq), lambda h, j, i: (h, 0, j))],
        out_specs=[pl.BlockSpec((tq, hd), lambda h, j, i: (j, h)), pl.BlockSpec((tq, hd), lambda h, j, i: (j, h)),
                   pl.BlockSpec((1, 1, tq), lambda h, j, i: (h, 0, j))],
        out_shape=[jax.ShapeDtypeStruct((length, n_heads * hd), BF16)] * 2
        + [jax.ShapeDtypeStruct((n_heads, 1, length), F32)],
        scratch_shapes=[pltpu.VMEM((tq, hd), F32), pltpu.VMEM((tq, hd), F32), pltpu.VMEM((1, tq), F32)],
        compiler_params=_params(("parallel", "parallel", "arbitrary")),
    )(qkvg, qkvg, qkvg, datt, lse, dsum, c_col, c_row)


def _row(v):
    return v.reshape(1, -1).astype(F32)


def _pad_heads(v, per_group):
    lead = v.shape[:-1]
    v = v.reshape(lead + (SSD_GROUPS, per_group))
    v = jnp.pad(v, [(0, 0)] * len(lead) + [(0, 0), (0, LANES - per_group)])
    return v.reshape(lead + (SSD_GROUPS * LANES,))


def _unpad_heads(v, per_group):
    lead = v.shape[:-1]
    return v.reshape(lead + (SSD_GROUPS, LANES))[..., :per_group].reshape(lead + (SSD_GROUPS * per_group,))


class _NoOverlap:
    def gather_rider(self, host):
        return None

    def gathered(self, host, carried):
        return {}

    def reduce_rider(self, grads):
        return None

    def reduced(self, carried):
        pass


def _pad_head_rows(w, per_group):
    w = w.reshape(SSD_GROUPS, per_group, w.shape[1])
    return jnp.pad(w, ((0, 0), (0, LANES - per_group), (0, 0))).reshape(SSD_GROUPS * LANES, w.shape[2])


def _unpad_head_rows(w, per_group):
    return w.reshape(SSD_GROUPS, LANES, w.shape[1])[:, :per_group].reshape(SSD_GROUPS * per_group, w.shape[1])


def _local_step(x, tgt, wb, sm, plan=None):
    plan = plan or _NoOverlap()
    wb = dict(wb)
    length, d = x.shape
    mix = 2 * d
    s5w = mix // 4
    ssdw = mix - s5w
    xbcw = ssdw + 2 * SSD_GROUPS * SSD_STATE
    n_ssd = ssdw // SSD_HEAD_DIM
    hpg = n_ssd // SSD_GROUPS
    fw = d
    o1, o2, o3 = 2 * s5w, 2 * s5w + ssdw, 2 * s5w + ssdw + xbcw
    w0t = wb["w0T"]
    w0_dt = _pad_head_rows(w0t[o3:], hpg)
    n_fox = fw // FOX_HEAD_DIM
    s5g = s5w // S5_GROUP
    s5s = s5g * S5_STATE
    grads = {}

    s5_in = (sm["l0_s5_lambda_re"], sm["l0_s5_lambda_im"], sm["l0_s5_log_step"], sm["l0_s5_b_re"],
             sm["l0_s5_b_im"], sm["l0_s5_c_re"], sm["l0_s5_c_im"])
    (a_re, a_im, bd_re, bd_im, cd_re, cd_imn), s5_vjp = jax.vjp(_s5_prepare, *s5_in)
    nj = bd_re.shape[0]
    bd_re_b, bd_im_b, cd_re_b, cd_imn_b = (t.astype(BF16) for t in (bd_re, bd_im, cd_re, cd_imn))
    tr = lambda t: jnp.swapaxes(t, 1, 2)
    dvec = _row(sm["l0_s5_d"])
    bglu = _row(sm["l0_s5_b_glu"])
    conv_w = sm["l0_ssd_conv_w"]
    conv_b = _row(sm["l0_ssd_conv_b"])

    def ssd_prepare(dt_bias, a_log, dd):
        return (_pad_heads(_row(dt_bias), hpg), _pad_heads(_row(-jnp.exp(a_log)), hpg),
                jnp.repeat(_row(dd), SSD_HEAD_DIM, axis=1))

    (dtb, ap, dpar), ssd_vjp = jax.vjp(ssd_prepare, sm["l0_ssd_dt_bias"], sm["l0_ssd_a_log"], sm["l0_ssd_d"])
    ssd_nw = _row(sm["l0_ssd_norm_w"])
    nw0, nw1, fnw = _row(sm["l0_norm_w"]), _row(sm["l1_norm_w"]), _row(sm["final_norm_w"])
    bf = jnp.pad(_row(sm["l1_fox_b_f"]), ((0, 0), (0, LANES - n_fox)))

    h0 = _norm_fwd(x, nw0, "l0_norm")
    ug = _matmul(h0, w0t, mode="nt", dims=(length, o1, d), name="l0_in_ug")
    z = _matmul(h0, w0t, mode="nt", dims=(length, ssdw, d), b_off=(o1, 0), name="l0_in_z")
    rider = plan.gather_rider("l0_in_xbc")
    xbc = _matmul(h0, w0t, mode="nt", dims=(length, xbcw, d), b_off=(o2, 0), name="l0_in_xbc", rider=rider)
    if rider is not None:
        xbc, carried = xbc
        wb.update(plan.gathered("l0_in_xbc", carried))
    dtraw = _matmul(h0, w0_dt, mode="nt", name="l0_in_dt")
    u_win, gate_win = (ug, s5w, 0), (ug, s5w, 1)

    shape3 = (length, s5s // LANES, LANES)
    bu_re = _bdmm([(ug, 0)], [bd_re_b], name="s5_bu_re").reshape(shape3)
    bu_im = _bdmm([(ug, 0)], [bd_im_b], name="s5_bu_im").reshape(shape3)
    s_re3, s_im3 = _s5_scan_fwd(bu_re, bu_im, a_re, a_im, "s5_scan")
    s_re, s_im = s_re3.reshape(length, s5s), s_im3.reshape(length, s5s)
    yc = _bdmm([s_re, s_im], [cd_re_b, cd_imn_b], name="s5_y")
    gl = _rowwise(_f_s5_gelu, [yc, u_win], [dvec], [(s5w, BF16)], tl=256, name="s5_gelu")[0]
    t_glu = _matmul(gl, wb["w_glu"], name="s5_glu")
    s5o = _rowwise(_f_s5_out, [yc, u_win, t_glu, gate_win], [dvec, bglu], [(s5w, BF16)], tl=256,
                   name="s5_out")[0]

    xc = _conv_fwd(xbc, conv_w, conv_b, "ssd_conv")
    y_ssd, states, carried = _ssd_fwd(xc, dtraw, dtb, ap, n_ssd, "ssd_scan", rider=plan.gather_rider("ssd_scan"))
    wb.update(plan.gathered("ssd_scan", carried))
    wg = ssdw // SSD_GROUPS
    ssdo = _rowwise(_f_ssd_out, [y_ssd, (xc, wg, 0), z], [dpar, ssd_nw], [(wg, BF16)], tl=256,
                    ncol=SSD_GROUPS, name="ssd_out")[0]
    x1 = _matmul(s5o, wb["w0_out"], dims=(length, d, s5w), addend=x, name="l0_out_s5")
    x1 = _matmul(ssdo, wb["w0_out"], dims=(length, d, ssdw), b_off=(s5w, 0), addend=x1, name="l0_out_ssd")

    h1 = _norm_fwd(x1, nw1, "l1_norm")
    w1t = wb["w1T"]
    w1_f = jnp.pad(w1t[4 * fw:], ((0, LANES - n_fox), (0, 0)))
    qkvg = _matmul(h1, w1t, mode="nt", dims=(length, 4 * fw, d), name="l1_in")
    fraw = _matmul(h1, w1_f, mode="nt", name="l1_in_f")
    cc = _fox_cumsum(fraw, bf, "fox_cumsum")
    c_t = cc[:, :n_fox].T
    c_col, c_row = c_t[:, :, None], c_t[:, None, :]
    att, lse = _fox_fwd(qkvg, c_col, c_row, n_fox, "fox_fwd")
    gate1_win = (qkvg, fw, 3)
    fox_o = _rowwise(_f_fox_out, [att, gate1_win], [], [(fw, BF16)], tl=256, name="fox_out")[0]
    x2 = _matmul(fox_o, wb["w1_out"], addend=x1, name="l1_out")

    def loss_fn(xt, tt, wt):
        def f(xx, ww):
            err = _f_norm(xx, ww) - tt
            return (0.5 / d) * err * err
        lanes, vjp = jax.vjp(f, xt, wt)
        dx, dw = vjp(jnp.ones_like(lanes))
        return dx, dx, jnp.sum(lanes, axis=0, keepdims=True), dw

    dx2, dx2b, loss_lanes, g_fnw = _rowwise(loss_fn, [x2, tgt], [fnw], [(d, F32), (d, BF16)], [d, d],
                                            tl=128, name="loss_head")
    grads["final_norm_w"] = g_fnw

    grads["l1_w_out"] = _matmul(fox_o, dx2b, mode="tn", name="l1_out_dw")
    do1 = _matmul(dx2b, wb["w1_out"], mode="nt", name="l1_out_dx")

    def fox_out_bwd(at, gt, dt_):
        _, vjp = jax.vjp(_f_fox_out, at, gt)
        return vjp(dt_)

    datt, dgate1 = _rowwise(fox_out_bwd, [att, gate1_win, do1], [], [(fw, F32), (fw, BF16)], tl=256,
                            name="fox_out_bwd")
    dq, dsum = _fox_bwd_q(qkvg, datt, lse, c_col, c_row, n_fox, "fox_bwd_q")
    dk, dv, dck = _fox_bwd_kv(qkvg, datt, lse, dsum, c_col, c_row, n_fox, "fox_bwd_kv")
    dcc = jnp.pad(dck[:, 0, :].T, ((0, 0), (0, LANES - n_fox)))
    dfraw, g_bf = _fox_cumsum_bwd(dcc, fraw, bf, "fox_cumsum_bwd")
    grads["l1_fox_b_f"] = g_bf[:, :n_fox]
    dsegs = [dq, dk, dv, dgate1]
    g1, n1 = None, w1t.shape[0]
    for i, s in enumerate(dsegs):
        g1 = _matmul(s, h1, mode="tn", rows_of=(n1, i * fw, g1), name=f"l1_in_dw{i}")
    g1_f = _matmul(dfraw, h1, mode="tn", name="l1_in_dwf")[:n_fox]
    grads["l1_w_inT"] = lax.dynamic_update_slice(g1, g1_f, (4 * fw, 0))
    dh1 = _matmul(dfraw, w1_f, mode="nn", name="l1_in_dxf")
    for i, s in enumerate(dsegs):
        dh1 = _matmul(s, w1t, mode="nn", dims=(length, d, fw), b_off=(i * fw, 0), addend=dh1,
                      name=f"l1_in_dx{i}")
    dx1, dx1b, grads["l1_norm_w"] = _norm_bwd(x1, dh1, dx2, nw1, "l1_norm_bwd")

    g_out = _matmul(s5o, dx1b, mode="tn", rows_of=(mix, 0, None), name="l0_out_dw_s5")
    grads["l0_w_out"] = _matmul(ssdo, dx1b, mode="tn", rows_of=(mix, s5w, g_out), name="l0_out_dw_ssd")
    ds5o = _matmul(dx1b, wb["w0_out"], mode="nt", dims=(length, s5w, d), name="l0_out_dx_s5")
    dssdo = _matmul(dx1b, wb["w0_out"], mode="nt", dims=(length, ssdw, d), b_off=(s5w, 0), name="l0_out_dx_ssd")

    def ssd_out_bwd(yt, xt, zt, dt_, dp, nw):
        _, vjp = jax.vjp(_f_ssd_out, yt, xt, zt, dp, nw)
        return vjp(dt_)

    dy_ssd, dxa, dz, g_dpar, g_ssd_nw = _rowwise(
        ssd_out_bwd, [y_ssd, (xc, wg, 0), z, dssdo], [dpar, ssd_nw],
        [(wg, F32), (wg, F32), (wg, BF16)], [wg, wg], tl=128, ncol=SSD_GROUPS, name="ssd_out_bwd")
    grads["l0_ssd_norm_w"] = g_ssd_nw
    dxs, db_ssd, dc_ssd, ddtraw, g_dtb, g_ap, carried = _ssd_bwd(
        dy_ssd, dxa, xc, dtraw, states, dtb, ap, n_ssd, "ssd_scan_bwd", rider=plan.reduce_rider(grads))
    plan.reduced(carried)
    g_dt_bias, g_a_log, g_ssd_d = ssd_vjp((g_dtb, g_ap, g_dpar))
    grads["l0_ssd_dt_bias"], grads["l0_ssd_a_log"], grads["l0_ssd_d"] = g_dt_bias, g_a_log, g_ssd_d
    dxc = jnp.concatenate([dxs, db_ssd, dc_ssd], axis=1)
    dpre, grads["l0_ssd_conv_w"], grads["l0_ssd_conv_b"] = _conv_bwd_pre(dxc, xbc, conv_w, conv_b, "ssd_conv_bwd_pre")
    dxbc = _conv_bwd_in(dpre, conv_w, "ssd_conv_bwd_in")

    def s5_out_bwd(yt, ut, tt, gt, dt_, dv_, bg):
        _, vjp = jax.vjp(_f_s5_out, yt, ut, tt, gt, dv_, bg)
        return vjp(dt_)

    dyc_a, du_a, dt_glu, dgate, g_dvec_a, g_bglu = _rowwise(
        s5_out_bwd, [yc, u_win, t_glu, gate_win, ds5o], [dvec, bglu],
        [(s5w, F32), (s5w, F32), (s5w, BF16), (s5w, BF16)], [s5w, s5w], tl=128, name="s5_out_bwd")
    grads["l0_s5_b_glu"] = g_bglu
    grads["l0_s5_w_glu"] = _matmul(gl, dt_glu, mode="tn", name="s5_glu_dw")
    dgl = _matmul(dt_glu, wb["w_glu"], mode="nt", name="s5_glu_dx")

    def s5_gelu_bwd(yt, ut, dg, dya, dua, dv_):
        _, vjp = jax.vjp(_f_s5_gelu, yt, ut, dv_)
        dy_, du_, ddv = vjp(dg)
        return dy_ + dya, du_ + dua, ddv

    dyc, du_ab, g_dvec_b = _rowwise(s5_gelu_bwd, [yc, u_win, dgl, dyc_a, du_a], [dvec],
                                    [(s5w, F32), (s5w, F32)], [s5w], tl=128, name="s5_gelu_bwd")
    ds_re = _bdmm([dyc], [tr(cd_re_b)], name="s5_ds_re").reshape(shape3)
    ds_im = _bdmm([dyc], [tr(cd_imn_b)], name="s5_ds_im").reshape(shape3)
    kin_s, kin_u = s5s // nj, s5w // nj
    g_cd_re = _bdmm_tn_sized(s_re, dyc, nj, kin_s, kin_u, 0, "s5_dcd_re")
    g_cd_imn = _bdmm_tn_sized(s_im, dyc, nj, kin_s, kin_u, 0, "s5_dcd_im")
    g_re3, g_im3, g_a_re, g_a_im = _s5_scan_bwd(ds_re, ds_im, s_re3, s_im3, a_re, a_im, "s5_scan_bwd")
    g_re, g_im = g_re3.reshape(length, s5s), g_im3.reshape(length, s5s)
    du = _bdmm([g_re, g_im], [tr(bd_re_b), tr(bd_im_b)], addend=du_ab, out_dtype=BF16, name="s5_du")
    g_bd_re = _bdmm_tn_sized(ug, g_re, nj, kin_u, kin_s, 0, "s5_dbd_re")
    g_bd_im = _bdmm_tn_sized(ug, g_im, nj, kin_u, kin_s, 0, "s5_dbd_im")
    s5_g = s5_vjp((g_a_re, g_a_im, g_bd_re, g_bd_im, g_cd_re, g_cd_imn))
    for nm, g in zip(("lambda_re", "lambda_im", "log_step", "b_re", "b_im", "c_re", "c_im"), s5_g):
        grads["l0_s5_" + nm] = g
    grads["l0_s5_d"] = (g_dvec_a + g_dvec_b).reshape(sm["l0_s5_d"].shape)

    g0, n0 = None, w0t.shape[0]
    for nm, s, off in (("u", du, 0), ("g", dgate, s5w), ("z", dz, o1), ("xbc", dxbc, o2)):
        g0 = _matmul(s, h0, mode="tn", rows_of=(n0, off, g0), name="l0_in_dw_" + nm)
    g0_dt = _unpad_head_rows(_matmul(ddtraw, h0, mode="tn", name="l0_in_dw_dt"), hpg)
    grads["l0_w_inT"] = lax.dynamic_update_slice(g0, g0_dt, (o3, 0))
    dh0 = _matmul(ddtraw, w0_dt, mode="nn", name="l0_in_dx_dt")
    for nm, s, off in (("u", du, 0), ("g", dgate, s5w), ("z", dz, o1), ("xbc", dxbc, o2)):
        dh0 = _matmul(s, w0t, mode="nn", dims=(length, d, s.shape[1]), b_off=(off, 0), addend=dh0,
                      name="l0_in_dx_" + nm)
    dx, _, grads["l0_norm_w"] = _norm_bwd(x, dh0, dx1, nw0, "l0_norm_bwd")
    return loss_lanes, dx, grads


_ANY = pl.BlockSpec(memory_space=pl.ANY)


def _place():
    x, y, c = lax.axis_index("x"), lax.axis_index("y"), lax.axis_index("c")
    return x, y, c, [(1 - x, y), (x, 1 - y), (1 - x, 1 - y)]


def _remote(src, dst, send_sem, recv_sem, to):
    return pltpu.make_async_remote_copy(src_ref=src, dst_ref=dst, send_sem=send_sem, recv_sem=recv_sem,
                                        device_id=to, device_id_type=MESH)


def _comm_call(body, n_in, out_shape, n_sems, name):
    return pl.pallas_call(
        body, name=name, in_specs=[_ANY] * n_in, out_specs=[_ANY] * len(out_shape), out_shape=out_shape,
        scratch_shapes=[pltpu.SemaphoreType.DMA((k,)) for k in n_sems],
        compiler_params=pltpu.CompilerParams(has_side_effects=True),
    )


def _half(ref_or_shape, c):
    ch = ref_or_shape.shape[-1] // 2
    return pl.ds(pl.multiple_of(c * ch, LANES), ch)


def _gather_rider(shards):
    n = len(shards)

    def sends(ins, outs, sems):
        send, recv = sems[:2]
        x, y, c, chips = _place()
        me = 2 * x + y
        return [_remote(ins[a].at[:, _half(ins[a], c)], outs[a].at[me, :, _half(ins[a], c)],
                        send.at[3 * a + k], recv.at[3 * a + k], (px, py, c))
                for a in range(n) for k, (px, py) in enumerate(chips)]

    def start(ins, outs, sems):
        for cp in sends(ins, outs, sems):
            cp.start()

    def finish(ins, outs, sems):
        send, recv, fsend, frecv = sems
        x, y, c, chips = _place()
        passed = []
        for a in range(n):
            for k, (px, py) in enumerate(chips):
                got = outs[a].at[2 * px + py, :, _half(ins[a], c)]
                _remote(got, got, send.at[3 * a + k], recv.at[3 * a + k], (px, py, c)).wait_recv()
                cp = _remote(got, got, fsend.at[3 * a + k], frecv.at[3 * a + k], (x, y, 1 - c))
                cp.start()
                passed.append(cp)
        for a in range(n):
            for k, (px, py) in enumerate(chips):
                got = outs[a].at[2 * px + py, :, _half(ins[a], 1 - c)]
                _remote(got, got, fsend.at[3 * a + k], frecv.at[3 * a + k], (x, y, 1 - c)).wait_recv()
        for cp in sends(ins, outs, sems) + passed:
            cp.wait_send()

    out_shape = [jax.ShapeDtypeStruct((N_SHARD,) + s.shape, s.dtype) for s in shards]
    return _Rider(shards, out_shape, [3 * n] * 4, start, finish)


def _chip_rider(parts):
    n = len(parts)

    def copies(ins, outs, sems):
        send, recv = sems
        x, y, c, chips = _place()
        return [_remote(ins[a].at[2 * px + py], outs[a].at[k], send.at[3 * a + k], recv.at[3 * a + k], (px, py, c))
                for a in range(n) for k, (px, py) in enumerate(chips)]

    def start(ins, outs, sems):
        for cp in copies(ins, outs, sems):
            cp.start()

    def finish(ins, outs, sems):
        for cp in copies(ins, outs, sems):
            cp.wait()

    out_shape = [jax.ShapeDtypeStruct((3,) + p.shape[1:], p.dtype) for p in parts]
    return _Rider(parts, out_shape, [3 * n] * 2, start, finish)


def _run_rider(rider, name):
    n_in, n_out = len(rider.inputs), len(rider.out_shape)

    def body(*refs):
        ins, outs, sems = refs[:n_in], refs[n_in:n_in + n_out], refs[n_in + n_out:]
        rider.start(ins, outs, sems)
        rider.finish(ins, outs, sems)

    return _comm_call(body, n_in, rider.out_shape, rider.sems, name)(*rider.inputs)


def _sibling_halves(grads, name):
    n = len(grads)

    def body(*refs):
        ins, outs = refs[:n], refs[n:2 * n]
        send, recv = refs[2 * n:]
        x, y, c, _ = _place()
        copies = [_remote(ins[a].at[:, :, _half(ins[a], 1 - c)], outs[a], send.at[a], recv.at[a], (x, y, 1 - c))
                  for a in range(n)]
        for cp in copies:
            cp.start()
        for cp in copies:
            cp.wait()

    out_shape = [jax.ShapeDtypeStruct(g.shape[:2] + (g.shape[2] // 2,), g.dtype) for g in grads]
    return _comm_call(body, n, out_shape, [n, n], name)(*grads)


def _join_halves(halves, name):
    n = len(halves)

    def body(*refs):
        outs = refs[n:2 * n]
        send, recv = refs[2 * n:]
        x, y, c, _ = _place()
        mine = [outs[a].at[:, _half(outs[a], c)] for a in range(n)]
        copies = [_remote(mine[a], mine[a], send.at[a], recv.at[a], (x, y, 1 - c)) for a in range(n)]
        for cp in copies:
            cp.start()
        for a in range(n):
            copies[a].wait_send()
            got = outs[a].at[:, _half(outs[a], 1 - c)]
            _remote(got, got, send.at[a], recv.at[a], (x, y, 1 - c)).wait_recv()

    return pl.pallas_call(
        body, name=name, in_specs=[_ANY] * n, out_specs=[_ANY] * n,
        out_shape=[jax.ShapeDtypeStruct(h.shape, h.dtype) for h in halves],
        input_output_aliases={a: a for a in range(n)},
        scratch_shapes=[pltpu.SemaphoreType.DMA((n,)), pltpu.SemaphoreType.DMA((n,))],
        compiler_params=pltpu.CompilerParams(has_side_effects=True),
    )(*halves)


def _gather_all(buf, name):
    def body(in_ref, out_ref, send, recv, lsem):
        x, y, c, _ = _place()
        me = 4 * x + 2 * y + c
        local = pltpu.make_async_copy(in_ref, out_ref.at[me], lsem.at[0])
        local.start()
        copies = []
        for k in range(1, N_DEV):
            fx, fy, fc = (k >> 2) & 1, (k >> 1) & 1, k & 1
            peer = (x + fx - 2 * x * fx, y + fy - 2 * y * fy, c + fc - 2 * c * fc)
            cp = _remote(in_ref, out_ref.at[me], send.at[k - 1], recv.at[k - 1], peer)
            cp.start()
            copies.append((cp, 4 * peer[0] + 2 * peer[1] + peer[2]))
        for k, (cp, slot) in enumerate(copies):
            cp.wait_send()
            got = out_ref.at[slot]
            _remote(got, got, send.at[k], recv.at[k], (x, y, c)).wait_recv()
        local.wait()

    out_shape = [jax.ShapeDtypeStruct((N_DEV,) + buf.shape, buf.dtype)]
    return _comm_call(body, 1, out_shape, [N_DEV - 1, N_DEV - 1, 1], name)(buf)[0]


def _sum_slots(buf, name):
    slots, rows, _ = buf.shape
    tr = _pick(rows, (512, 256, 128, 64, 32, 16, 8))

    def body(b_ref, o_ref):
        acc = b_ref[0]
        for s in range(1, slots):
            acc = acc + b_ref[s]
        o_ref[...] = acc

    return pl.pallas_call(
        body, name=name, grid=(rows // tr,),
        in_specs=[pl.BlockSpec((slots, tr, LANES), lambda i: (0, i, 0))],
        out_specs=pl.BlockSpec((tr, LANES), lambda i: (i, 0)),
        out_shape=jax.ShapeDtypeStruct((rows, LANES), F32),
        compiler_params=_params(("parallel",)),
    )(buf)


def _tile2(rows, cols, n_bufs):
    tr = max(t for t in range(8, min(rows, 2048) + 1, 8) if rows % t == 0) if rows % 8 == 0 else rows
    budget = 24 * 1024 * 1024 // (8 * n_bufs * tr)
    tc = max([t for t in range(LANES, cols + 1, LANES) if cols % t == 0 and t <= budget] or [LANES])
    return tr, tc


def _elementwise(fn, ins, out_dtypes, name):
    rows, cols = ins[0].shape
    tr, tc = _tile2(rows, cols, len(ins) + len(out_dtypes))
    n_in = len(ins)

    def body(*refs):
        outs = fn(*[r[...] for r in refs[:n_in]])
        for r, v in zip(refs[n_in:], outs if isinstance(outs, (tuple, list)) else (outs,)):
            r[...] = v.astype(r.dtype)

    blk = pl.BlockSpec((tr, tc), lambda i, j: (i, j))
    return pl.pallas_call(
        body, name=name, grid=(rows // tr, cols // tc), in_specs=[blk] * n_in, out_specs=[blk] * len(out_dtypes),
        out_shape=[jax.ShapeDtypeStruct((rows, cols), dt) for dt in out_dtypes],
        compiler_params=_params(("parallel", "parallel")),
    )(*ins)


def _presum(grad, sib, name):
    ns, rows, ch = sib.shape
    tr, tc = _tile2(rows, ch, 3)
    nct = ch // tc

    def body(g_ref, r_ref, o_ref):
        o_ref[...] = (g_ref[...] + r_ref[...]).astype(o_ref.dtype)

    blk = pl.BlockSpec((1, tr, tc), lambda j, i, k: (j, i, k))
    return pl.pallas_call(
        body, name=name, grid=(ns, rows // tr, nct),
        in_specs=[pl.BlockSpec((1, tr, tc), lambda j, i, k: (j, i, lax.axis_index("c") * nct + k)), blk],
        out_specs=blk, out_shape=jax.ShapeDtypeStruct((ns, rows, ch), BF16),
        compiler_params=_params(("parallel", "parallel", "parallel")),
    )(grad, sib)


def _finish_half(grad, sib, others, name):
    _, rows, ch = sib.shape
    tr, tc = _tile2(rows, ch, 6)
    nct = ch // tc

    def body(g_ref, r_ref, q_ref, o_ref):
        acc = g_ref[0] + r_ref[0]
        for k in range(3):
            acc = acc + q_ref[k].astype(F32)
        o_ref[...] = acc

    core = lambda: lax.axis_index("c")
    chip = lambda: 2 * lax.axis_index("x") + lax.axis_index("y")
    return pl.pallas_call(
        body, name=name, grid=(rows // tr, nct),
        in_specs=[pl.BlockSpec((1, tr, tc), lambda i, k: (chip(), i, core() * nct + k)),
                  pl.BlockSpec((1, tr, tc), lambda i, k: (chip(), i, k)),
                  pl.BlockSpec((3, tr, tc), lambda i, k: (0, i, k))],
        out_specs=pl.BlockSpec((tr, tc), lambda i, k: (i, core() * nct + k)),
        out_shape=jax.ShapeDtypeStruct((rows, 2 * ch), F32),
        compiler_params=_params(("parallel", "parallel")),
    )(grad, sib, others)


def _cast_bf16(w, name):
    return _elementwise(lambda t: t, [w], [BF16], name)[0]


_WEIGHTS = ("l0_norm_w", "l0_w_in", "l0_s5_lambda_re", "l0_s5_lambda_im", "l0_s5_log_step", "l0_s5_b_re",
            "l0_s5_b_im", "l0_s5_c_re", "l0_s5_c_im", "l0_s5_d", "l0_s5_w_glu", "l0_s5_b_glu", "l0_ssd_conv_w",
            "l0_ssd_conv_b", "l0_ssd_dt_bias", "l0_ssd_a_log", "l0_ssd_d", "l0_ssd_norm_w", "l0_w_out",
            "l1_norm_w", "l1_w_in", "l1_fox_b_f", "l1_w_out", "final_norm_w")
_COL_SHARDED = ("l0_w_in", "l1_w_in")
_ROW_SHARDED = ("l0_s5_w_glu", "l0_w_out", "l1_w_out")
_BIG = ("l0_w_in", "l0_s5_w_glu", "l0_w_out", "l1_w_in", "l1_w_out")
_CONV = "l0_ssd_conv_w"
_SMALL = tuple(n for n in _WEIGHTS if n not in _BIG and n != _CONV)


def _pack(arrays):
    flat = jnp.concatenate([a.reshape(-1).astype(F32) for a in arrays])
    size = flat.shape[0]
    padded = -(-size // (512 * LANES)) * (512 * LANES)
    return jnp.pad(flat, (0, padded - size)).reshape(-1, LANES)


def _unpack(buf, like):
    flat = buf.reshape(-1)
    out, pos = [], 0
    for a in like:
        out.append(flat[pos:pos + a.size].reshape(a.shape))
        pos += a.size
    return out


def _step(p):
    x, tgt = p["x"][0], p["loss_target"][0]
    d = x.shape[1]
    chip = 2 * lax.axis_index("x") + lax.axis_index("y")

    def rows_first(a, n):
        return a.T if n in _COL_SHARDED else a

    shard = {n: _cast_bf16(rows_first(p[n], n), "cast_" + n) for n in _BIG}
    shard[_CONV] = p[_CONV]

    def whole(n, g):
        g = lax.dynamic_update_index_in_dim(g, shard[n][None], chip, 0)
        return g.reshape(N_SHARD * g.shape[1], g.shape[2])

    now = ("l0_w_in", _CONV)
    got = dict(zip(now, _run_rider(_gather_rider([shard[n] for n in now]), "gather_first")))
    wb = {"w0T": whole("l0_w_in", got["l0_w_in"])}
    sm = {n: p[n] for n in _SMALL}
    taps, ccols = p[_CONV].shape
    conv_all = lax.dynamic_update_index_in_dim(got[_CONV], p[_CONV][None], chip, 0)
    sm[_CONV] = conv_all.transpose(1, 0, 2).reshape(taps, N_SHARD * ccols)
    later = {"l0_in_xbc": ("l0_s5_w_glu", "l0_w_out"), "ssd_scan": ("l1_w_in", "l1_w_out")}
    early = ("l0_w_out", "l1_w_in", "l1_w_out")
    grad_key = {n: n + "T" if n in _COL_SHARDED else n for n in _BIG}

    big, sib, others = {}, {}, {}

    def presummed(names, grads, tag):
        for n in names:
            g = grads[grad_key[n]]
            big[n] = g.reshape(N_SHARD, g.shape[0] // N_SHARD, g.shape[1])
        sib.update(zip(names, _sibling_halves([big[n] for n in names], "reduce_sibling_" + tag)))
        return [_presum(big[n], sib[n], "presum_" + n) for n in names]

    class Plan:
        def gather_rider(self, host):
            return _gather_rider([shard[n] for n in later[host]])

        def gathered(self, host, carried):
            w = {n: whole(n, g) for n, g in zip(later[host], carried)}
            if host == "l0_in_xbc":
                return {"w_glu": w["l0_s5_w_glu"], "w0_out": w["l0_w_out"]}
            return {"w1T": w["l1_w_in"], "w1_out": w["l1_w_out"]}

        def reduce_rider(self, grads):
            return _chip_rider(presummed(early, grads, "early"))

        def reduced(self, carried):
            others.update(zip(early, carried))

    loss_lanes, dx, grads = _local_step(x, tgt, wb, sm, Plan())

    small_like = [p[n] for n in _SMALL] + [sm[_CONV], jnp.zeros((1,), F32)]
    small_sum = _sum_slots(_gather_all(_pack([grads[n] for n in _SMALL] + [grads[_CONV], jnp.sum(loss_lanes)]),
                                       "gather_small"), "sum_small")
    *small_grads, conv_grad, loss = _unpack(small_sum, small_like)
    conv_grad = lax.dynamic_slice(conv_grad, (0, chip * ccols), (taps, ccols))
    final = dict(zip(_SMALL, small_grads))
    final[_CONV] = conv_grad

    late = tuple(n for n in _BIG if n not in early)
    others.update(zip(late, _run_rider(_chip_rider(presummed(late, grads, "late")), "reduce_chips_late")))
    done = [_finish_half(big[n], sib[n], others[n], "finish_" + n) for n in _BIG]

    delta, new_m, new_v = {}, {}, {}
    for n, full in zip(_BIG, _join_halves(done, "join_halves")):
        upd = _adamw(rows_first(p[n], n), full, rows_first(p["m_" + n], n), rows_first(p["v_" + n], n),
                     "adamw_" + n)
        final[n], delta[n], new_m[n], new_v[n] = (rows_first(t, n) for t in (full, *upd))
    rest = _SMALL + (_CONV,)
    packed = [_pack([t[n] for n in rest]) for t in
              ({n: p[n] for n in rest}, final, {n: p["m_" + n] for n in rest}, {n: p["v_" + n] for n in rest})]
    for dst, buf in zip((delta, new_m, new_v), _adamw(*packed, "adamw_small")):
        dst.update(zip(rest, _unpack(buf, [p[n] for n in rest])))

    outs = [loss.reshape(()), dx[None]]
    for group in (final, delta, new_m, new_v):
        outs += [group[n].reshape(p[n].shape) for n in _WEIGHTS]
    return tuple(outs)


_INPUTS = ("x",) + _WEIGHTS + ("loss_target",) + tuple("m_" + n for n in _WEIGHTS) + tuple("v_" + n for n in _WEIGHTS)


def kernel(x, l0_norm_w, l0_w_in, l0_s5_lambda_re, l0_s5_lambda_im, l0_s5_log_step, l0_s5_b_re, l0_s5_b_im, l0_s5_c_re,
           l0_s5_c_im, l0_s5_d, l0_s5_w_glu, l0_s5_b_glu, l0_ssd_conv_w, l0_ssd_conv_b, l0_ssd_dt_bias,
           l0_ssd_a_log, l0_ssd_d, l0_ssd_norm_w, l0_w_out, l1_norm_w, l1_w_in, l1_fox_b_f, l1_w_out,
           final_norm_w, loss_target, m_l0_norm_w, m_l0_w_in, m_l0_s5_lambda_re, m_l0_s5_lambda_im,
           m_l0_s5_log_step, m_l0_s5_b_re, m_l0_s5_b_im, m_l0_s5_c_re, m_l0_s5_c_im, m_l0_s5_d,
           m_l0_s5_w_glu, m_l0_s5_b_glu, m_l0_ssd_conv_w, m_l0_ssd_conv_b, m_l0_ssd_dt_bias, m_l0_ssd_a_log,
           m_l0_ssd_d, m_l0_ssd_norm_w, m_l0_w_out, m_l1_norm_w, m_l1_w_in, m_l1_fox_b_f, m_l1_w_out,
           m_final_norm_w, v_l0_norm_w, v_l0_w_in, v_l0_s5_lambda_re, v_l0_s5_lambda_im, v_l0_s5_log_step,
           v_l0_s5_b_re, v_l0_s5_b_im, v_l0_s5_c_re, v_l0_s5_c_im, v_l0_s5_d, v_l0_s5_w_glu, v_l0_s5_b_glu,
           v_l0_ssd_conv_w, v_l0_ssd_conv_b, v_l0_ssd_dt_bias, v_l0_ssd_a_log, v_l0_ssd_d, v_l0_ssd_norm_w,
           v_l0_w_out, v_l1_norm_w, v_l1_w_in, v_l1_fox_b_f, v_l1_w_out, v_final_norm_w):
    values = (x, l0_norm_w, l0_w_in, l0_s5_lambda_re, l0_s5_lambda_im, l0_s5_log_step, l0_s5_b_re, l0_s5_b_im,
              l0_s5_c_re, l0_s5_c_im, l0_s5_d, l0_s5_w_glu, l0_s5_b_glu, l0_ssd_conv_w, l0_ssd_conv_b,
              l0_ssd_dt_bias, l0_ssd_a_log, l0_ssd_d, l0_ssd_norm_w, l0_w_out, l1_norm_w, l1_w_in,
              l1_fox_b_f, l1_w_out, final_norm_w, loss_target, m_l0_norm_w, m_l0_w_in,
              m_l0_s5_lambda_re, m_l0_s5_lambda_im, m_l0_s5_log_step, m_l0_s5_b_re, m_l0_s5_b_im,
              m_l0_s5_c_re, m_l0_s5_c_im, m_l0_s5_d, m_l0_s5_w_glu, m_l0_s5_b_glu, m_l0_ssd_conv_w,
              m_l0_ssd_conv_b, m_l0_ssd_dt_bias, m_l0_ssd_a_log, m_l0_ssd_d, m_l0_ssd_norm_w,
              m_l0_w_out, m_l1_norm_w, m_l1_w_in, m_l1_fox_b_f, m_l1_w_out, m_final_norm_w, v_l0_norm_w,
              v_l0_w_in, v_l0_s5_lambda_re, v_l0_s5_lambda_im, v_l0_s5_log_step, v_l0_s5_b_re,
              v_l0_s5_b_im, v_l0_s5_c_re, v_l0_s5_c_im, v_l0_s5_d, v_l0_s5_w_glu, v_l0_s5_b_glu,
              v_l0_ssd_conv_w, v_l0_ssd_conv_b, v_l0_ssd_dt_bias, v_l0_ssd_a_log, v_l0_ssd_d,
              v_l0_ssd_norm_w, v_l0_w_out, v_l1_norm_w, v_l1_w_in, v_l1_fox_b_f, v_l1_w_out,
              v_final_norm_w)
    return _step(dict(zip(_INPUTS, values)))
```

```python
import functools
import math

import jax
import jax.numpy as jnp
from jax import lax
from jax.experimental import pallas as pl
from jax.experimental.pallas import tpu as pltpu

F32 = jnp.float32
BF16 = jnp.bfloat16

S5_GROUP = 16
S5_STATE = 64
S5_EIG_CLIP = -1e-4
SSD_HEAD_DIM = 64
SSD_GROUPS = 8
SSD_STATE = 128
SSD_CONV = 4
SSD_CHUNK = 128
FOX_HEAD_DIM = 128
FOX_TILE = 1024
NORM_EPS = 1e-5
ADAM_LR = 0.001
ADAM_B1 = 0.9
ADAM_B2 = 0.999
ADAM_EPS = 1e-08
ADAM_WD = 0.01
ADAM_STEP = 10

N_SHARD = 4
N_DEV = 8
LANES = 128
VMEM_LIMIT = 56 * 1024 * 1024
MESH = pl.DeviceIdType.MESH


def _pick(dim, prefs, offs=()):
    for p in prefs:
        if dim % p == 0 and all(o % p == 0 for o in offs):
            return p
    return dim


def _params(sem=None, vmem=VMEM_LIMIT):
    return pltpu.CompilerParams(dimension_semantics=sem, vmem_limit_bytes=vmem)


class _Rider:
    def __init__(self, inputs, out_shape, sems, start, finish):
        self.inputs, self.out_shape, self.sems = list(inputs), list(out_shape), list(sems)
        self.start, self.finish = start, finish


def _hosted_call(body, *, name, grid, in_specs, out_specs, out_shape, scratch_shapes, sem, args, rider=None):
    single = not isinstance(out_shape, (list, tuple))
    out_specs = [out_specs] if single else list(out_specs)
    out_shape = [out_shape] if single else list(out_shape)
    if rider is None:
        res = pl.pallas_call(body, name=name, grid=grid, in_specs=in_specs, out_specs=out_specs,
                             out_shape=out_shape, scratch_shapes=scratch_shapes,
                             compiler_params=_params(sem))(*args)
        return (res[0] if single else res), []
    n_in, n_out, n_scr = len(in_specs), len(out_shape), len(scratch_shapes)
    n_rin, n_rout = len(rider.inputs), len(rider.out_shape)

    def carried(*refs):
        ins, refs = refs[:n_in], refs[n_in:]
        rin, refs = refs[:n_rin], refs[n_rin:]
        outs, refs = refs[:n_out], refs[n_out:]
        rout, refs = refs[:n_rout], refs[n_rout:]
        scr, rsem = refs[:n_scr], refs[n_scr:]
        ids = [pl.program_id(k) for k in range(len(grid))]
        first = functools.reduce(jnp.logical_and, [i == 0 for i in ids])
        last = functools.reduce(jnp.logical_and, [i == g - 1 for i, g in zip(ids, grid)])

        @pl.when(first)
        def _():
            rider.start(rin, rout, rsem)

        body(*ins, *outs, *scr)

        @pl.when(last)
        def _():
            rider.finish(rin, rout, rsem)

    res = pl.pallas_call(
        carried, name=name, grid=grid,
        in_specs=list(in_specs) + [_ANY] * n_rin, out_specs=out_specs + [_ANY] * n_rout,
        out_shape=out_shape + rider.out_shape,
        scratch_shapes=list(scratch_shapes) + [pltpu.SemaphoreType.DMA((k,)) for k in rider.sems],
        compiler_params=pltpu.CompilerParams(dimension_semantics=("arbitrary",) * len(grid),
                                             vmem_limit_bytes=VMEM_LIMIT, has_side_effects=True),
    )(*args, *rider.inputs)
    outs = res[:n_out]
    return (outs[0] if single else outs), list(res[n_out:])


def _matmul(a, b, *, mode="nn", dims=None, a_off=(0, 0), b_off=(0, 0), addend=None,
            out_dtype=F32, rider=None, rows_of=None, name):
    if dims is None:
        if mode == "nn":
            dims = (a.shape[0], b.shape[1], a.shape[1])
        elif mode == "nt":
            dims = (a.shape[0], b.shape[0], a.shape[1])
        else:
            dims = (a.shape[1], b.shape[1], a.shape[0])
    m, n, k = dims
    if mode == "nn":
        om, on, ok = (a_off[0],), (b_off[1],), (a_off[1], b_off[0])
    elif mode == "nt":
        om, on, ok = (a_off[0],), (b_off[0],), (a_off[1], b_off[1])
    else:
        om, on, ok = (a_off[1],), (b_off[1],), (a_off[0], b_off[0])
    tm = _pick(m, (1024, 512, 256, 128), om)
    tn = _pick(n, (1024, 768, 512, 384, 256, 128), on)
    tk = _pick(k, (1024, 512, 256, 128), ok)
    nk = k // tk
    if mode == "nn":
        a_blk, a_div = (tm, tk), (tm, tk)
        b_blk, b_div = (tk, tn), (tk, tn)
        a_map = lambda i, j, kk: (i + a_off[0] // tm, kk + a_off[1] // tk)
        b_map = lambda i, j, kk: (kk + b_off[0] // tk, j + b_off[1] // tn)
        dn = (((1,), (0,)), ((), ()))
    elif mode == "nt":
        a_blk, a_div = (tm, tk), (tm, tk)
        b_blk, b_div = (tn, tk), (tn, tk)
        a_map = lambda i, j, kk: (i + a_off[0] // tm, kk + a_off[1] // tk)
        b_map = lambda i, j, kk: (j + b_off[0] // tn, kk + b_off[1] // tk)
        dn = (((1,), (1,)), ((), ()))
    else:
        a_blk, a_div = (tk, tm), (tk, tm)
        b_blk, b_div = (tk, tn), (tk, tn)
        a_map = lambda i, j, kk: (kk + a_off[0] // tk, i + a_off[1] // tm)
        b_map = lambda i, j, kk: (kk + b_off[0] // tk, j + b_off[1] // tn)
        dn = (((0,), (0,)), ((), ()))
    assert a_off[0] % a_div[0] == 0 and a_off[1] % a_div[1] == 0, (name, a_off, a_div)
    assert b_off[0] % b_div[0] == 0 and b_off[1] % b_div[1] == 0, (name, b_off, b_div)
    has_add = addend is not None

    def body(*refs):
        if has_add:
            a_ref, b_ref, c_ref, o_ref, acc_ref = refs
        else:
            a_ref, b_ref, o_ref, acc_ref = refs
        kk = pl.program_id(2)

        @pl.when(kk == 0)
        def _():
            acc_ref[...] = jnp.zeros_like(acc_ref)

        acc_ref[...] += lax.dot_general(a_ref[...].astype(BF16), b_ref[...].astype(BF16), dn,
                                        preferred_element_type=F32)

        @pl.when(kk == nk - 1)
        def _():
            r = acc_ref[...]
            if has_add:
                r = r + c_ref[...].astype(F32)
            o_ref[...] = r.astype(o_ref.dtype)

    in_specs = [pl.BlockSpec(a_blk, a_map), pl.BlockSpec(b_blk, b_map)]
    args = [a, b]
    if has_add:
        in_specs.append(pl.BlockSpec((tm, tn), lambda i, j, kk: (i, j)))
        args.append(addend)
    if rows_of is not None:
        total, row_off, buf = rows_of
        assert row_off % tm == 0 and rider is None, (name, row_off, tm)
        if buf is not None:
            def body_into(*refs):
                body(*refs[:len(args)], *refs[len(args) + 1:])

            return pl.pallas_call(
                body_into, name=name, grid=(m // tm, n // tn, nk),
                in_specs=in_specs + [pl.BlockSpec(memory_space=pl.ANY)],
                out_specs=pl.BlockSpec((tm, tn), lambda i, j, kk: (i + row_off // tm, j)),
                out_shape=jax.ShapeDtypeStruct((total, n), out_dtype),
                input_output_aliases={len(args): 0},
                scratch_shapes=[pltpu.VMEM((tm, tn), F32)],
                compiler_params=_params(("parallel", "parallel", "arbitrary")),
            )(*args, buf)
        return pl.pallas_call(
            body, name=name, grid=(m // tm, n // tn, nk), in_specs=in_specs,
            out_specs=pl.BlockSpec((tm, tn), lambda i, j, kk: (i + row_off // tm, j)),
            out_shape=jax.ShapeDtypeStruct((total, n), out_dtype),
            scratch_shapes=[pltpu.VMEM((tm, tn), F32)],
            compiler_params=_params(("parallel", "parallel", "arbitrary")),
        )(*args)
    out, carried = _hosted_call(
        body, name=name, grid=(m // tm, n // tn, nk),
        in_specs=in_specs, out_specs=pl.BlockSpec((tm, tn), lambda i, j, kk: (i, j)),
        out_shape=jax.ShapeDtypeStruct((m, n), out_dtype),
        scratch_shapes=[pltpu.VMEM((tm, tn), F32)],
        sem=("parallel", "parallel", "arbitrary"), args=args, rider=rider)
    return out if rider is None else (out, carried)


def _rowwise(fn, rows, params, out_rows, out_accs=(), *, tl, ncol=1, name):
    rows = [r if isinstance(r, tuple) else (r, r.shape[1] // ncol, 0) for r in rows]
    n_rows, n_par, n_or, n_oa = len(rows), len(params), len(out_rows), len(out_accs)
    length = rows[0][0].shape[0]
    tl = _pick(length, [t for t in (1024, 512, 256, 128, 64, 32, 16, 8) if t <= tl])

    def body(*refs):
        row_refs = refs[:n_rows]
        par_refs = refs[n_rows:n_rows + n_par]
        or_refs = refs[n_rows + n_par:n_rows + n_par + n_or]
        oa_refs = refs[n_rows + n_par + n_or:]
        outs = fn(*[r[...] for r in row_refs], *[p[...] for p in par_refs])
        if not isinstance(outs, (tuple, list)):
            outs = (outs,)
        for r, v in zip(or_refs, outs[:n_or]):
            r[...] = v.astype(r.dtype)
        if n_oa:
            @pl.when(pl.program_id(1) == 0)
            def _():
                for r in oa_refs:
                    r[...] = jnp.zeros_like(r)

            for r, v in zip(oa_refs, outs[n_or:]):
                r[...] += v.astype(F32)

    in_specs = [pl.BlockSpec((tl, w), functools.partial(lambda j, i, b0: (i, b0 + j), b0=b0))
                for (_, w, b0) in rows]
    in_specs += [pl.BlockSpec((p.shape[0], p.shape[1] // ncol), lambda j, i: (0, j)) for p in params]
    out_specs = [pl.BlockSpec((tl, w), lambda j, i: (i, j)) for (w, _) in out_rows]
    out_specs += [pl.BlockSpec((1, w), lambda j, i: (0, j)) for w in out_accs]
    out_shape = [jax.ShapeDtypeStruct((length, ncol * w), dt) for (w, dt) in out_rows]
    out_shape += [jax.ShapeDtypeStruct((1, ncol * w), F32) for w in out_accs]
    res = pl.pallas_call(
        body, name=name, grid=(ncol, length // tl),
        in_specs=in_specs, out_specs=out_specs, out_shape=out_shape,
        compiler_params=_params(("parallel", "arbitrary" if n_oa else "parallel")),
    )(*[r[0] for r in rows], *params)
    return res


def _bdmm(xs, ws, *, addend=None, out_dtype=F32, name):
    nj, kin, kout = ws[0].shape
    xs = [x if isinstance(x, tuple) else (x, 0) for x in xs]
    length = xs[0][0].shape[0]
    tl = _pick(length, (2048, 1024, 512, 256, 128))
    n_x = len(xs)
    has_add = addend is not None

    def body(*refs):
        x_refs = refs[:n_x]
        w_refs = refs[n_x:2 * n_x]
        o_ref = refs[-1]
        acc = None
        for xr, wr in zip(x_refs, w_refs):
            t = jnp.dot(xr[...].astype(BF16), wr[0], preferred_element_type=F32)
            acc = t if acc is None else acc + t
        if has_add:
            acc = acc + refs[2 * n_x][...].astype(F32)
        o_ref[...] = acc.astype(o_ref.dtype)

    in_specs = [pl.BlockSpec((tl, kin), functools.partial(lambda i, j, b0: (i, b0 + j), b0=b0)) for (_, b0) in xs]
    in_specs += [pl.BlockSpec((1, kin, kout), lambda i, j: (j, 0, 0)) for _ in ws]
    args = [x[0] for x in xs] + list(ws)
    if has_add:
        in_specs.append(pl.BlockSpec((tl, kout), lambda i, j: (i, j)))
        args.append(addend)
    return pl.pallas_call(
        body, name=name, grid=(length // tl, nj),
        in_specs=in_specs, out_specs=pl.BlockSpec((tl, kout), lambda i, j: (i, j)),
        out_shape=jax.ShapeDtypeStruct((length, nj * kout), out_dtype),
        compiler_params=_params(("parallel", "parallel")),
    )(*args)


def _bdmm_tn_sized(x, g, nj, kin, kout, x_first, name):
    length = x.shape[0]
    tl = _pick(length, (512, 256, 128))
    nt = length // tl

    def body(x_ref, g_ref, o_ref):
        @pl.when(pl.program_id(1) == 0)
        def _():
            o_ref[...] = jnp.zeros_like(o_ref)

        o_ref[0] += lax.dot_general(x_ref[...].astype(BF16), g_ref[...].astype(BF16),
                                    (((0,), (0,)), ((), ())), preferred_element_type=F32)

    return pl.pallas_call(
        body, name=name, grid=(nj, nt),
        in_specs=[pl.BlockSpec((tl, kin), lambda j, t: (t, x_first + j)),
                  pl.BlockSpec((tl, kout), lambda j, t: (t, j))],
        out_specs=pl.BlockSpec((1, kin, kout), lambda j, t: (j, 0, 0)),
        out_shape=jax.ShapeDtypeStruct((nj, kin, kout), F32),
        compiler_params=_params(("parallel", "arbitrary")),
    )(x, g)


def _f_norm(x, w):
    return x * lax.rsqrt(jnp.mean(x * x, axis=-1, keepdims=True) + NORM_EPS) * w


def _gelu(y):
    return 0.5 * y * (1.0 + jnp.tanh(math.sqrt(2.0 / math.pi) * (y + 0.044715 * (y * y * y))))


def _sigmoid(x):
    return 1.0 / (1.0 + jnp.exp(-x))


def _silu(x):
    return x * _sigmoid(x)


def _softplus(x):
    return jnp.maximum(x, 0.0) + jnp.log(1.0 + jnp.exp(-jnp.abs(x)))


def _f_s5_gelu(yc, u, dvec):
    return _gelu(yc + dvec * u)


def _f_s5_out(yc, u, t, gate, dvec, bglu):
    gl = _gelu(yc + dvec * u)
    return gl * _sigmoid(t + bglu) * _silu(gate)


def _f_ssd_out(y, xs, z, dpar, nw):
    v = (y + dpar * xs) * _silu(z)
    return v * lax.rsqrt(jnp.mean(v * v, axis=-1, keepdims=True) + NORM_EPS) * nw


def _f_fox_out(att, gate):
    return att * _silu(gate)


def _norm_fwd(x, w, name):
    return _rowwise(lambda xt, wt: _f_norm(xt, wt), [x], [w], [(x.shape[1], BF16)], tl=256, name=name)[0]


def _norm_bwd(x, dh, dres, w, name):
    d = x.shape[1]

    def fn(xt, dht, drt, wt):
        _, vjp = jax.vjp(_f_norm, xt, wt)
        dx, dw = vjp(dht)
        dx = dx + drt
        return dx, dx, dw

    return _rowwise(fn, [x, dh, dres], [w], [(d, F32), (d, BF16)], [d], tl=128, name=name)


def _adamw_math(w, g, m, v):
    m = ADAM_B1 * m + (1.0 - ADAM_B1) * g
    v = ADAM_B2 * v + (1.0 - ADAM_B2) * jnp.square(g)
    m_hat = m / (1.0 - ADAM_B1 ** ADAM_STEP)
    v_hat = v / (1.0 - ADAM_B2 ** ADAM_STEP)
    delta = -ADAM_LR * (m_hat / (jnp.sqrt(v_hat) + ADAM_EPS) + ADAM_WD * w)
    return delta, m, v


def _adamw(w, g, m, v, name):
    return _elementwise(_adamw_math, [w, g, m, v], [F32] * 3, name)


def _s5_scan_fwd(bu_re, bu_im, a_re, a_im, name):
    length, rows, _ = bu_re.shape
    rb = _pick(rows, (32, 16, 8))
    tl = _pick(length, (64, 32, 16, 8))

    def body(bur_ref, bui_ref, ar_ref, ai_ref, sr_ref, si_ref, st_ref):
        @pl.when(pl.program_id(1) == 0)
        def _():
            st_ref[...] = jnp.zeros_like(st_ref)

        ar = ar_ref[...]
        ai = ai_ref[...]

        def step(l, carry):
            sr, si = carry
            nr = ar * sr - ai * si + bur_ref[l]
            ni = ar * si + ai * sr + bui_ref[l]
            sr_ref[l] = nr
            si_ref[l] = ni
            return nr, ni

        sr, si = lax.fori_loop(0, tl, step, (st_ref[0], st_ref[1]))
        st_ref[0] = sr
        st_ref[1] = si

    blk = pl.BlockSpec((tl, rb, LANES), lambda cb, t: (t, cb, 0))
    ablk = pl.BlockSpec((rb, LANES), lambda cb, t: (cb, 0))
    return pl.pallas_call(
        body, name=name, grid=(rows // rb, length // tl),
        in_specs=[blk, blk, ablk, ablk], out_specs=[blk, blk],
        out_shape=[jax.ShapeDtypeStruct(bu_re.shape, F32)] * 2,
        scratch_shapes=[pltpu.VMEM((2, rb, LANES), F32)],
        compiler_params=_params(("parallel", "arbitrary")),
    )(bu_re, bu_im, a_re, a_im)


def _s5_scan_bwd(ds_re, ds_im, s_re, s_im, a_re, a_im, name):
    length, rows, _ = ds_re.shape
    rb = _pick(rows, (32, 16, 8))
    tl = _pick(length, (64, 32, 16, 8))
    nt = length // tl

    def body(dsr_ref, dsi_ref, sr_ref, si_ref, pr_ref, pi_ref, ar_ref, ai_ref,
             gr_ref, gi_ref, dar_ref, dai_ref, st_ref):
        t = pl.program_id(1)

        @pl.when(t == 0)
        def _():
            st_ref[...] = jnp.zeros_like(st_ref)
            dar_ref[...] = jnp.zeros_like(dar_ref)
            dai_ref[...] = jnp.zeros_like(dai_ref)

        ar = ar_ref[...]
        ai = ai_ref[...]

        def adj(l, gr, gi):
            ngr = dsr_ref[l] + ar * gr + ai * gi
            ngi = dsi_ref[l] + ar * gi - ai * gr
            gr_ref[l] = ngr
            gi_ref[l] = ngi
            return ngr, ngi

        def step(idx, carry):
            gr, gi, dar, dai = carry
            l = tl - 1 - idx
            gr, gi = adj(l, gr, gi)
            pr = sr_ref[l - 1]
            pi = si_ref[l - 1]
            dar = dar + gr * pr + gi * pi
            dai = dai + gi * pr - gr * pi
            return gr, gi, dar, dai

        zero = jnp.zeros((rb, LANES), F32)
        gr, gi, dar, dai = lax.fori_loop(0, tl - 1, step, (st_ref[0], st_ref[1], zero, zero))
        gr, gi = adj(0, gr, gi)
        first = (t == nt - 1)
        pr = jnp.where(first, 0.0, pr_ref[0])
        pi = jnp.where(first, 0.0, pi_ref[0])
        dar = dar + gr * pr + gi * pi
        dai = dai + gi * pr - gr * pi
        st_ref[0] = gr
        st_ref[1] = gi
        dar_ref[...] += dar
        dai_ref[...] += dai

    blk = pl.BlockSpec((tl, rb, LANES), lambda cb, t: (nt - 1 - t, cb, 0))
    prev = pl.BlockSpec((1, rb, LANES), lambda cb, t: (jnp.maximum((nt - 1 - t) * tl - 1, 0), cb, 0))
    ablk = pl.BlockSpec((rb, LANES), lambda cb, t: (cb, 0))
    return pl.pallas_call(
        body, name=name, grid=(rows // rb, nt),
        in_specs=[blk, blk, blk, blk, prev, prev, ablk, ablk],
        out_specs=[blk, blk, ablk, ablk],
        out_shape=[jax.ShapeDtypeStruct(ds_re.shape, F32)] * 2 + [jax.ShapeDtypeStruct(a_re.shape, F32)] * 2,
        scratch_shapes=[pltpu.VMEM((2, rb, LANES), F32)],
        compiler_params=_params(("parallel", "arbitrary")),
    )(ds_re, ds_im, s_re, s_im, s_re, s_im, a_re, a_im)


def _s5_prepare(lam_re, lam_im, log_step, b_re, b_im, c_re, c_im):
    groups, state = lam_re.shape
    lr = jnp.minimum(lam_re, S5_EIG_CLIP)
    li = lam_im
    step = jnp.exp(log_step)[:, None]
    mag = jnp.exp(lr * step)
    ab_re = mag * jnp.cos(li * step)
    ab_im = mag * jnp.sin(li * step)
    denom = lr * lr + li * li
    nr = ab_re - 1.0
    ni = ab_im
    coef_re = (nr * lr + ni * li) / denom
    coef_im = (ni * lr - nr * li) / denom
    bb_re = coef_re[..., None] * b_re - coef_im[..., None] * b_im
    bb_im = coef_re[..., None] * b_im + coef_im[..., None] * b_re
    per = LANES // S5_GROUP
    nj = groups // per
    eye = jnp.eye(per, dtype=F32)

    def in_map(bb):
        return jnp.einsum('jgph,gk->jghkp', bb.reshape(nj, per, state, S5_GROUP), eye).reshape(
            nj, per * S5_GROUP, per * state)

    def out_map(cc):
        return jnp.einsum('jghp,gk->jgpkh', cc.reshape(nj, per, S5_GROUP, state), eye).reshape(
            nj, per * state, per * S5_GROUP)

    shape2 = (groups * state // LANES, LANES)
    return (ab_re.reshape(shape2), ab_im.reshape(shape2), in_map(bb_re), in_map(bb_im),
            out_map(c_re), -out_map(c_im))


def _shift_down(cur, prev8, j):
    rolled = pltpu.roll(cur, j, 0)
    pr = pltpu.roll(prev8, j, 0)
    row = lax.broadcasted_iota(jnp.int32, cur.shape, 0)
    return jnp.where(row < j, jnp.tile(pr, (cur.shape[0] // 8, 1)), rolled)


def _shift_up(cur, next8, j):
    tl = cur.shape[0]
    rolled = pltpu.roll(cur, tl - j, 0)
    nx = pltpu.roll(next8, 8 - j, 0)
    row = lax.broadcasted_iota(jnp.int32, cur.shape, 0)
    return jnp.where(row >= tl - j, jnp.tile(nx, (tl // 8, 1)), rolled)


def _conv_tiles(length, ch):
    return _pick(length, (256, 128, 64, 32, 16, 8)), _pick(ch, (1024, 512, 256, 128))


def _conv_fwd(xbc, w, b, name):
    length, ch = xbc.shape
    tl, tc = _conv_tiles(length, ch)

    def body(x_ref, p_ref, w_ref, b_ref, o_ref):
        cur = x_ref[...]
        prev8 = jnp.where(pl.program_id(1) == 0, 0.0, p_ref[...])
        pre = b_ref[...] + w_ref[SSD_CONV - 1:SSD_CONV, :] * cur
        for j in range(1, SSD_CONV):
            pre = pre + w_ref[SSD_CONV - 1 - j:SSD_CONV - j, :] * _shift_down(cur, prev8, j)
        o_ref[...] = _silu(pre)

    return pl.pallas_call(
        body, name=name, grid=(ch // tc, length // tl),
        in_specs=[pl.BlockSpec((tl, tc), lambda c, i: (i, c)),
                  pl.BlockSpec((8, tc), lambda c, i: (jnp.maximum(i * (tl // 8) - 1, 0), c)),
                  pl.BlockSpec((SSD_CONV, tc), lambda c, i: (0, c)),
                  pl.BlockSpec((1, tc), lambda c, i: (0, c))],
        out_specs=pl.BlockSpec((tl, tc), lambda c, i: (i, c)),
        out_shape=jax.ShapeDtypeStruct((length, ch), F32),
        compiler_params=_params(("parallel", "parallel")),
    )(xbc, xbc, w, b)


def _conv_bwd_pre(dxc, xbc, w, b, name):
    length, ch = xbc.shape
    tl, tc = _conv_tiles(length, ch)

    def body(d_ref, x_ref, p_ref, w_ref, b_ref, o_ref, dw_ref, db_ref):
        @pl.when(pl.program_id(1) == 0)
        def _():
            dw_ref[...] = jnp.zeros_like(dw_ref)
            db_ref[...] = jnp.zeros_like(db_ref)

        cur = x_ref[...]
        prev8 = jnp.where(pl.program_id(1) == 0, 0.0, p_ref[...])
        shifted = [cur] + [_shift_down(cur, prev8, j) for j in range(1, SSD_CONV)]
        pre = b_ref[...]
        for j in range(SSD_CONV):
            pre = pre + w_ref[SSD_CONV - 1 - j:SSD_CONV - j, :] * shifted[j]
        sg = _sigmoid(pre)
        dpre = d_ref[...] * (sg * (1.0 + pre * (1.0 - sg)))
        o_ref[...] = dpre
        db_ref[...] += jnp.sum(dpre, axis=0, keepdims=True)
        row = lax.broadcasted_iota(jnp.int32, (SSD_CONV, tc), 0)
        dw = jnp.zeros((SSD_CONV, tc), F32)
        for j in range(SSD_CONV):
            dw = dw + jnp.where(row == SSD_CONV - 1 - j, jnp.sum(dpre * shifted[j], axis=0, keepdims=True), 0.0)
        dw_ref[...] += dw

    return pl.pallas_call(
        body, name=name, grid=(ch // tc, length // tl),
        in_specs=[pl.BlockSpec((tl, tc), lambda c, i: (i, c)),
                  pl.BlockSpec((tl, tc), lambda c, i: (i, c)),
                  pl.BlockSpec((8, tc), lambda c, i: (jnp.maximum(i * (tl // 8) - 1, 0), c)),
                  pl.BlockSpec((SSD_CONV, tc), lambda c, i: (0, c)),
                  pl.BlockSpec((1, tc), lambda c, i: (0, c))],
        out_specs=[pl.BlockSpec((tl, tc), lambda c, i: (i, c)),
                   pl.BlockSpec((SSD_CONV, tc), lambda c, i: (0, c)),
                   pl.BlockSpec((1, tc), lambda c, i: (0, c))],
        out_shape=[jax.ShapeDtypeStruct((length, ch), F32), jax.ShapeDtypeStruct((SSD_CONV, ch), F32),
                   jax.ShapeDtypeStruct((1, ch), F32)],
        compiler_params=_params(("parallel", "arbitrary")),
    )(dxc, xbc, xbc, w, b)


def _conv_bwd_in(dpre, w, name):
    length, ch = dpre.shape
    tl, tc = _conv_tiles(length, ch)
    nt = length // tl

    def body(d_ref, n_ref, w_ref, o_ref):
        cur = d_ref[...]
        next8 = jnp.where(pl.program_id(1) == nt - 1, 0.0, n_ref[...])
        acc = w_ref[SSD_CONV - 1:SSD_CONV, :] * cur
        for j in range(1, SSD_CONV):
            acc = acc + w_ref[SSD_CONV - 1 - j:SSD_CONV - j, :] * _shift_up(cur, next8, j)
        o_ref[...] = acc.astype(o_ref.dtype)

    return pl.pallas_call(
        body, name=name, grid=(ch // tc, nt),
        in_specs=[pl.BlockSpec((tl, tc), lambda c, i: (i, c)),
                  pl.BlockSpec((8, tc), lambda c, i: (jnp.minimum((i + 1) * (tl // 8), length // 8 - 1), c)),
                  pl.BlockSpec((SSD_CONV, tc), lambda c, i: (0, c))],
        out_specs=pl.BlockSpec((tl, tc), lambda c, i: (i, c)),
        out_shape=jax.ShapeDtypeStruct((length, ch), BF16),
        compiler_params=_params(("parallel", "parallel")),
    )(dpre, dpre, w)


def _split(x, terms):
    parts = []
    for _ in range(terms):
        part = x.astype(BF16)
        parts.append(part)
        x = x - part.astype(F32)
    return parts


def _dot(a, b, dn=(((1,), (0,)), ((), ()))):
    return lax.dot_general(a, b, dn, preferred_element_type=F32)


_NN = (((1,), (0,)), ((), ()))
_NT = (((1,), (1,)), ((), ()))
_TN = (((0,), (0,)), ((), ()))


def _pdot(parts, sel, dn=_NN):
    return functools.reduce(lambda a, b: a + b, [_dot(part, sel, dn) for part in parts])


def _pdotr(sel, parts, dn=_NN):
    return functools.reduce(lambda a, b: a + b, [_dot(sel, part, dn) for part in parts])


def _dot3(x, sel, dn=_NN):
    return _pdot(_split(x, 3), sel, dn)


def _dot3r(sel, x, dn=_NN):
    return _pdotr(sel, _split(x, 3), dn)


def _dot2(x, sel, dn=_NN):
    return _pdot(_split(x, 2), sel, dn)


def _iota2(shape, axis):
    return lax.broadcasted_iota(jnp.int32, shape, axis)


def _ssd_masks():
    q = SSD_CHUNK
    r, c = _iota2((q, q), 0), _iota2((q, q), 1)
    tril = (c <= r)
    return r, c, tril


def _head_of_lane(wg):
    return (_iota2((SSD_CHUNK, wg), 0) == _iota2((SSD_CHUNK, wg), 1) // SSD_HEAD_DIM).astype(BF16)


def _head_of_row(wg, dtype):
    return (_iota2((wg, SSD_CHUNK), 1) == _iota2((wg, SSD_CHUNK), 0) // SSD_HEAD_DIM).astype(dtype)


class _SsdChunk:
    def __init__(self, dtraw, dtb, ap, b_t, c_t, wg):
        q = SSD_CHUNK
        hpg = wg // SSD_HEAD_DIM
        r, c, self.tril = _ssd_masks()
        self.upper = (c > r)
        self.dt = _softplus(dtraw + dtb)
        self.la = self.dt * ap
        self.cum = _dot3r(self.tril.astype(BF16), self.la)
        rem = _dot3r(self.upper.astype(BF16), self.la)
        total = _dot3(self.la, jnp.ones((q, q), BF16), _TN)
        self.scores = _dot(c_t, b_t, _NT)
        cum2 = _split(self.cum, 2)
        stack = jnp.concatenate(_split(self.dt, 2) + cum2 + _split(rem, 2), axis=0)
        lanes = _dot(stack, _head_of_lane(wg))
        self.dt_l = lanes[0:q] + lanes[q:2 * q]
        self.cum_l = lanes[2 * q:3 * q] + lanes[3 * q:4 * q]
        self.rem_l = lanes[4 * q:5 * q] + lanes[5 * q:6 * q]
        self.grow = jnp.exp(_pdotr(_head_of_row(wg, BF16), _split(total, 2)))
        every_lane = (_iota2((q, hpg * q), 0) == _iota2((q, hpg * q), 1) // q).astype(BF16)
        cq = _dot(jnp.concatenate(cum2, axis=0), every_lane)
        self.cq = cq[0:q] + cq[q:2 * q]
        every_row = (_iota2((hpg * q, q), 1) == _iota2((hpg * q, q), 0) // q).astype(BF16)
        self.ck = _pdotr(every_row, cum2, _NT)

    def decay(self, h):
        q = SSD_CHUNK
        seg = self.cq[:, h * q:(h + 1) * q] - self.ck[h * q:(h + 1) * q, :]
        return jnp.exp(jnp.where(self.tril, seg, -jnp.inf))


def _by_head(x_b, lane):
    first = (lane // SSD_HEAD_DIM) == 0
    return jnp.concatenate([jnp.where(first, x_b, 0), jnp.where(first, 0, x_b)], axis=0)


def _ssd_tiles(xc, n_heads):
    hpg = n_heads // SSD_GROUPS
    wg = hpg * SSD_HEAD_DIM
    xw = n_heads * SSD_HEAD_DIM
    return hpg, wg, xw // wg, xw // SSD_STATE


def _ssd_fwd(xc, dtraw, dtb, ap, n_heads, name, rider=None):
    length = xc.shape[0]
    q = SSD_CHUNK
    nc = length // q
    hpg, wg, _, b_blk0 = _ssd_tiles(xc, n_heads)
    c_blk0 = b_blk0 + SSD_GROUPS
    npair = hpg // 2

    def body(x_ref, b_ref, c_ref, dt_ref, dtb_ref, ap_ref, y_ref, st_ref, s_ref):
        @pl.when(pl.program_id(1) == 0)
        def _():
            s_ref[...] = jnp.zeros_like(s_ref)

        st_ref[0, 0] = s_ref[...]
        b_t = b_ref[...].astype(BF16)
        c_t = c_ref[...].astype(BF16)
        ck = _SsdChunk(dt_ref[...], dtb_ref[...], ap_ref[...], b_t, c_t, wg)
        lane = _iota2((q, q), 1)
        xd = x_ref[...] * ck.dt_l
        xd_b = xd.astype(BF16)
        s_prev = s_ref[...]
        y_state = _dot(c_t, s_prev.astype(BF16), _NT) * jnp.exp(ck.cum_l)
        for i in range(npair):
            sl = slice(i * LANES, (i + 1) * LANES)
            wm = jnp.concatenate([(ck.scores * ck.decay(2 * i + hh)).astype(BF16) for hh in range(2)], axis=1)
            y_ref[:, sl] = y_state[:, sl] + _dot(wm, _by_head(xd_b[:, sl], lane))
        xw_b = (xd * jnp.exp(ck.rem_l)).astype(BF16)
        s_ref[...] = ck.grow * s_prev + _dot(xw_b, b_t, _TN)

    outs, carried = _hosted_call(
        body, name=name, grid=(SSD_GROUPS, nc),
        in_specs=[pl.BlockSpec((q, wg), lambda g, c: (c, g)),
                  pl.BlockSpec((q, SSD_STATE), lambda g, c: (c, b_blk0 + g)),
                  pl.BlockSpec((q, SSD_STATE), lambda g, c: (c, c_blk0 + g)),
                  pl.BlockSpec((q, LANES), lambda g, c: (c, g)),
                  pl.BlockSpec((1, LANES), lambda g, c: (0, g)),
                  pl.BlockSpec((1, LANES), lambda g, c: (0, g))],
        out_specs=[pl.BlockSpec((q, wg), lambda g, c: (c, g)),
                   pl.BlockSpec((1, 1, wg, SSD_STATE), lambda g, c: (c, g, 0, 0))],
        out_shape=[jax.ShapeDtypeStruct((length, SSD_GROUPS * wg), F32),
                   jax.ShapeDtypeStruct((nc, SSD_GROUPS, wg, SSD_STATE), F32)],
        scratch_shapes=[pltpu.VMEM((wg, SSD_STATE), F32)],
        sem=("parallel", "arbitrary"), args=(xc, xc, xc, dtraw, dtb, ap), rider=rider)
    return outs[0], outs[1], carried


def _ssd_bwd(dy, dxa, xc, dtraw, states, dtb, ap, n_heads, name, rider=None):
    length = xc.shape[0]
    q = SSD_CHUNK
    nc = length // q
    hpg, wg, _, b_blk0 = _ssd_tiles(xc, n_heads)
    c_blk0 = b_blk0 + SSD_GROUPS
    npair = hpg // 2

    def body(dy_ref, dxa_ref, x_ref, b_ref, c_ref, dt_ref, st_ref, dtb_ref, ap_ref,
             dx_ref, db_ref, dc_ref, ddt_ref, ddtb_ref, dap_ref, ds_ref, el_ref, er_ref):
        @pl.when(pl.program_id(1) == 0)
        def _():
            ds_ref[...] = jnp.zeros_like(ds_ref)
            ddtb_ref[...] = jnp.zeros_like(ddtb_ref)
            dap_ref[...] = jnp.zeros_like(dap_ref)

        b_t = b_ref[...].astype(BF16)
        c_t = c_ref[...].astype(BF16)
        dtraw_t = dt_ref[...]
        ck = _SsdChunk(dtraw_t, dtb_ref[...], ap_ref[...], b_t, c_t, wg)
        lane = _iota2((q, q), 1)
        to_head = _head_of_lane(wg)

        def per_head(v):
            return _pdot(_split(v, 2), to_head, _NT)

        x_all, dy_all = x_ref[...], dy_ref[...]
        dy_b = dy_all.astype(BF16)
        xd = x_all * ck.dt_l
        xd_b = xd.astype(BF16)
        s_prev = st_ref[0, 0]
        sp_b = s_prev.astype(BF16)
        ds1 = ds_ref[...]
        ds1_b = ds1.astype(BF16)
        ecum, wrem = jnp.exp(ck.cum_l), jnp.exp(ck.rem_l)
        dscores = jnp.zeros((q, q), F32)
        for i in range(npair):
            sl = slice(i * LANES, (i + 1) * LANES)
            dym = _by_head(dy_b[:, sl], lane)
            dwm2 = _dot(dym, xd_b[:, sl], _NT)
            wms = []
            for hh in range(2):
                h = 2 * i + hh
                decay = ck.decay(h)
                wm = ck.scores * decay
                dwm = dwm2[hh * q:(hh + 1) * q]
                dscores = dscores + dwm * decay
                e = (dwm * wm).astype(BF16)
                el_ref[:, h * q:(h + 1) * q] = e
                er_ref[h * q:(h + 1) * q, :] = e
                wms.append(wm.astype(BF16))
            dx_ref[:, sl] = _dot(jnp.concatenate(wms, axis=0), dym, _TN)
        put = (_iota2((hpg * q, q), 1) == _iota2((hpg * q, q), 0) // q).astype(BF16)
        dcum = _dot(el_ref[...], put) - _dot(er_ref[...], put, _TN)
        t_mat = _dot(c_t, sp_b, _NT)
        d_t = (dy_all * ecum).astype(BF16)
        dc_acc = _dot(d_t, sp_b)
        ds_prev = _dot(d_t, c_t, _TN)
        dcum = dcum + per_head(dy_all * t_mat * ecum)
        ds_prev = ds_prev + ck.grow * ds1
        zs = jnp.sum(ds1 * s_prev * ck.grow, axis=1, keepdims=True)
        dtot = _pdotr(jnp.ones((q, wg), BF16), _split(zs * _head_of_row(wg, F32), 2))
        xw = xd * wrem
        dxw = _dot(b_t, ds1_b, _NT)
        db_acc = _dot(xw.astype(BF16), ds1_b)
        dxd = dx_ref[...] + dxw * wrem
        drem = per_head(dxw * xw)
        dx_ref[...] = dxd * ck.dt_l + dxa_ref[...]
        ddt = per_head(dxd * x_all)
        ds_ref[...] = ds_prev
        ds_b = dscores.astype(BF16)
        dc_ref[...] = dc_acc + _dot(ds_b, b_t)
        db_ref[...] = db_acc + _dot(ds_b, c_t, _TN)
        dla = (_dot3r(ck.tril.astype(BF16), dcum, _TN) + _dot3r(ck.upper.astype(BF16), drem, _TN) + dtot)
        dt = ck.dt
        ddt = ddt + dla * ap_ref[...]
        dap_ref[...] += jnp.sum(dla * dt, axis=0, keepdims=True)
        ddtraw = ddt * _sigmoid(dtraw_t + dtb_ref[...])
        ddt_ref[...] = ddtraw.astype(ddt_ref.dtype)
        ddtb_ref[...] += jnp.sum(ddtraw, axis=0, keepdims=True)

    rev = lambda g, c: (nc - 1 - c, g)
    outs, carried = _hosted_call(
        body, name=name, grid=(SSD_GROUPS, nc),
        in_specs=[pl.BlockSpec((q, wg), rev),
                  pl.BlockSpec((q, wg), rev),
                  pl.BlockSpec((q, wg), rev),
                  pl.BlockSpec((q, SSD_STATE), lambda g, c: (nc - 1 - c, b_blk0 + g)),
                  pl.BlockSpec((q, SSD_STATE), lambda g, c: (nc - 1 - c, c_blk0 + g)),
                  pl.BlockSpec((q, LANES), rev),
                  pl.BlockSpec((1, 1, wg, SSD_STATE), lambda g, c: (nc - 1 - c, g, 0, 0)),
                  pl.BlockSpec((1, LANES), lambda g, c: (0, g)),
                  pl.BlockSpec((1, LANES), lambda g, c: (0, g))],
        out_specs=[pl.BlockSpec((q, wg), rev),
                   pl.BlockSpec((q, SSD_STATE), rev),
                   pl.BlockSpec((q, SSD_STATE), rev),
                   pl.BlockSpec((q, LANES), rev),
                   pl.BlockSpec((1, LANES), lambda g, c: (0, g)),
                   pl.BlockSpec((1, LANES), lambda g, c: (0, g))],
        out_shape=[jax.ShapeDtypeStruct((length, SSD_GROUPS * wg), F32),
                   jax.ShapeDtypeStruct((length, SSD_GROUPS * SSD_STATE), F32),
                   jax.ShapeDtypeStruct((length, SSD_GROUPS * SSD_STATE), F32),
                   jax.ShapeDtypeStruct((length, SSD_GROUPS * LANES), BF16),
                   jax.ShapeDtypeStruct((1, SSD_GROUPS * LANES), F32),
                   jax.ShapeDtypeStruct((1, SSD_GROUPS * LANES), F32)],
        scratch_shapes=[pltpu.VMEM((wg, SSD_STATE), F32), pltpu.VMEM((q, hpg * q), BF16),
                        pltpu.VMEM((hpg * q, q), BF16)],
        sem=("parallel", "arbitrary"), args=(dy, dxa, xc, xc, xc, dtraw, states, dtb, ap), rider=rider)
    return tuple(outs) + (carried,)


def _fox_cumsum(fraw, bf, name):
    length = fraw.shape[0]
    q = 128

    def body(f_ref, b_ref, o_ref, carry_ref):
        @pl.when(pl.program_id(0) == 0)
        def _():
            carry_ref[...] = jnp.zeros_like(carry_ref)

        lf = -_softplus(-(f_ref[...] + b_ref[...]))
        r, c = _iota2((q, q), 0), _iota2((q, q), 1)
        o_ref[...] = _dot3r((c <= r).astype(BF16), lf) + carry_ref[...]
        carry_ref[...] += jnp.sum(lf, axis=0, keepdims=True)

    return pl.pallas_call(
        body, name=name, grid=(length // q,),
        in_specs=[pl.BlockSpec((q, LANES), lambda i: (i, 0)), pl.BlockSpec((1, LANES), lambda i: (0, 0))],
        out_specs=pl.BlockSpec((q, LANES), lambda i: (i, 0)),
        out_shape=jax.ShapeDtypeStruct((length, LANES), F32),
        scratch_shapes=[pltpu.VMEM((1, LANES), F32)],
        compiler_params=_params(("arbitrary",)),
    )(fraw, bf)


def _fox_cumsum_bwd(dc, fraw, bf, name):
    length = fraw.shape[0]
    q = 128
    nt = length // q

    def body(d_ref, f_ref, b_ref, o_ref, db_ref, carry_ref):
        @pl.when(pl.program_id(0) == 0)
        def _():
            carry_ref[...] = jnp.zeros_like(carry_ref)
            db_ref[...] = jnp.zeros_like(db_ref)

        d = d_ref[...]
        r, c = _iota2((q, q), 0), _iota2((q, q), 1)
        dlf = _dot3r((c >= r).astype(BF16), d) + carry_ref[...]
        carry_ref[...] += jnp.sum(d, axis=0, keepdims=True)
        df = dlf * _sigmoid(-(f_ref[...] + b_ref[...]))
        o_ref[...] = df.astype(o_ref.dtype)
        db_ref[...] += jnp.sum(df, axis=0, keepdims=True)

    rev = lambda i: (nt - 1 - i, 0)
    return pl.pallas_call(
        body, name=name, grid=(nt,),
        in_specs=[pl.BlockSpec((q, LANES), rev), pl.BlockSpec((q, LANES), rev),
                  pl.BlockSpec((1, LANES), lambda i: (0, 0))],
        out_specs=[pl.BlockSpec((q, LANES), rev), pl.BlockSpec((1, LANES), lambda i: (0, 0))],
        out_shape=[jax.ShapeDtypeStruct((length, LANES), BF16), jax.ShapeDtypeStruct((1, LANES), F32)],
        scratch_shapes=[pltpu.VMEM((1, LANES), F32)],
        compiler_params=_params(("arbitrary",)),
    )(dc, fraw, bf)


def _fox_scores(q_ref, k_ref, cq_ref, ck_ref, diagonal):
    scale = 1.0 / math.sqrt(FOX_HEAD_DIM)
    s = _dot(q_ref[...].astype(BF16), k_ref[...].astype(BF16), _NT) * scale + (cq_ref[0] - ck_ref[0])
    if diagonal:
        s = jnp.where(_iota2(s.shape, 1) <= _iota2(s.shape, 0), s, -jnp.inf)
    return s


def _fox_tiles(i, j, step):
    @pl.when(j < i)
    def _():
        step(False)

    @pl.when(j == i)
    def _():
        step(True)


def _fox_fwd(qkvg, c_col, c_row, n_heads, name):
    length = qkvg.shape[0]
    hd = FOX_HEAD_DIM
    tq = _pick(length, (FOX_TILE, 512, 256, 128))
    nq = length // tq

    def body(q_ref, k_ref, v_ref, cq_ref, ck_ref, o_ref, lse_ref, m_ref, l_ref, acc_ref):
        i, j = pl.program_id(1), pl.program_id(2)

        @pl.when(j == 0)
        def _():
            m_ref[...] = jnp.full_like(m_ref, -jnp.inf)
            l_ref[...] = jnp.zeros_like(l_ref)
            acc_ref[...] = jnp.zeros_like(acc_ref)

        def step(diagonal):
            s = _fox_scores(q_ref, k_ref, cq_ref, ck_ref, diagonal)
            m_new = jnp.maximum(m_ref[...], jnp.max(s, axis=1, keepdims=True))
            alpha = jnp.exp(m_ref[...] - m_new)
            p = jnp.exp(s - m_new)
            l_ref[...] = alpha * l_ref[...] + jnp.sum(p, axis=1, keepdims=True)
            acc_ref[...] = alpha * acc_ref[...] + _dot(p.astype(BF16), v_ref[...].astype(BF16))
            m_ref[...] = m_new

        _fox_tiles(i, j, step)

        @pl.when(j == nq - 1)
        def _():
            o_ref[...] = acc_ref[...] / l_ref[...]
            lse_ref[0] = m_ref[...] + jnp.log(l_ref[...])

    kmap = lambda off: (lambda h, i, j: (jnp.minimum(j, i), off * n_heads + h))
    return pl.pallas_call(
        body, name=name, grid=(n_heads, nq, nq),
        in_specs=[pl.BlockSpec((tq, hd), lambda h, i, j: (i, h)),
                  pl.BlockSpec((tq, hd), kmap(1)),
                  pl.BlockSpec((tq, hd), kmap(2)),
                  pl.BlockSpec((1, tq, 1), lambda h, i, j: (h, i, 0)),
                  pl.BlockSpec((1, 1, tq), lambda h, i, j: (h, 0, jnp.minimum(j, i)))],
        out_specs=[pl.BlockSpec((tq, hd), lambda h, i, j: (i, h)),
                   pl.BlockSpec((1, tq, 1), lambda h, i, j: (h, i, 0))],
        out_shape=[jax.ShapeDtypeStruct((length, n_heads * hd), F32),
                   jax.ShapeDtypeStruct((n_heads, length, 1), F32)],
        scratch_shapes=[pltpu.VMEM((tq, 1), F32), pltpu.VMEM((tq, 1), F32), pltpu.VMEM((tq, hd), F32)],
        compiler_params=_params(("parallel", "parallel", "arbitrary")),
    )(qkvg, qkvg, qkvg, c_col, c_row)


def _fox_bwd_q(qkvg, datt, lse, c_col, c_row, n_heads, name):
    length = qkvg.shape[0]
    hd = FOX_HEAD_DIM
    tq = _pick(length, (FOX_TILE, 512, 256, 128))
    nq = length // tq
    scale = 1.0 / math.sqrt(hd)

    def body(q_ref, k_ref, v_ref, do_ref, lse_ref, cq_ref, ck_ref, dq_ref, dsum_ref, a1_ref, a2_ref, d_ref):
        i, j = pl.program_id(1), pl.program_id(2)

        @pl.when(j == 0)
        def _():
            a1_ref[...] = jnp.zeros_like(a1_ref)
            a2_ref[...] = jnp.zeros_like(a2_ref)
            d_ref[...] = jnp.zeros_like(d_ref)

        def step(diagonal):
            s = _fox_scores(q_ref, k_ref, cq_ref, ck_ref, diagonal)
            p = jnp.exp(s - lse_ref[0])
            pdp = p * _dot(do_ref[...].astype(BF16), v_ref[...].astype(BF16), _NT)
            d_ref[...] += jnp.sum(pdp, axis=1, keepdims=True)
            k_b = k_ref[...].astype(BF16)
            a1_ref[...] += _dot(pdp.astype(BF16), k_b)
            a2_ref[...] += _dot(p.astype(BF16), k_b)

        _fox_tiles(i, j, step)

        @pl.when(j == nq - 1)
        def _():
            dq_ref[...] = ((a1_ref[...] - d_ref[...] * a2_ref[...]) * scale).astype(dq_ref.dtype)
            dsum_ref[0] = d_ref[...]

    kmap = lambda off: (lambda h, i, j: (jnp.minimum(j, i), off * n_heads + h))
    qmap = lambda h, i, j: (i, h)
    col = pl.BlockSpec((1, tq, 1), lambda h, i, j: (h, i, 0))
    return pl.pallas_call(
        body, name=name, grid=(n_heads, nq, nq),
        in_specs=[pl.BlockSpec((tq, hd), qmap), pl.BlockSpec((tq, hd), kmap(1)), pl.BlockSpec((tq, hd), kmap(2)),
                  pl.BlockSpec((tq, hd), qmap), col, col,
                  pl.BlockSpec((1, 1, tq), lambda h, i, j: (h, 0, jnp.minimum(j, i)))],
        out_specs=[pl.BlockSpec((tq, hd), qmap), col],
        out_shape=[jax.ShapeDtypeStruct((length, n_heads * hd), BF16),
                   jax.ShapeDtypeStruct((n_heads, length, 1), F32)],
        scratch_shapes=[pltpu.VMEM((tq, hd), F32), pltpu.VMEM((tq, hd), F32), pltpu.VMEM((tq, 1), F32)],
        compiler_params=_params(("parallel", "parallel", "arbitrary")),
    )(qkvg, qkvg, qkvg, datt, lse, c_col, c_row)


def _fox_bwd_kv(qkvg, datt, lse, dsum, c_col, c_row, n_heads, name):
    length = qkvg.shape[0]
    hd = FOX_HEAD_DIM
    tq = _pick(length, (FOX_TILE, 512, 256, 128))
    nq = length // tq
    scale = 1.0 / math.sqrt(hd)

    def body(q_ref, k_ref, v_ref, do_ref, lse_ref, dsum_ref, cq_ref, ck_ref, dk_ref, dv_ref, dck_ref,
             dk_acc, dv_acc, dc_acc):
        j, i = pl.program_id(1), pl.program_id(2)

        @pl.when(i == 0)
        def _():
            dk_acc[...] = jnp.zeros_like(dk_acc)
            dv_acc[...] = jnp.zeros_like(dv_acc)
            dc_acc[...] = jnp.zeros_like(dc_acc)

        def step(diagonal):
            s = _fox_scores(q_ref, k_ref, cq_ref, ck_ref, diagonal)
            p = jnp.exp(s - lse_ref[0])
            do_b = do_ref[...].astype(BF16)
            dv_acc[...] += _dot(p.astype(BF16), do_b, _TN)
            dp = _dot(do_b, v_ref[...].astype(BF16), _NT)
            ds = p * (dp - dsum_ref[0])
            dk_acc[...] += _dot(ds.astype(BF16), q_ref[...].astype(BF16), _TN)
            dc_acc[...] -= jnp.sum(ds, axis=0, keepdims=True)

        _fox_tiles(i, j, step)

        @pl.when(i == nq - 1)
        def _():
            dk_ref[...] = (dk_acc[...] * scale).astype(dk_ref.dtype)
            dv_ref[...] = dv_acc[...].astype(dv_ref.dtype)
            dck_ref[0] = dc_acc[...]

    qmap = lambda h, j, i: (jnp.maximum(i, j), h)
    kmap = lambda off: (lambda h, j, i: (j, off * n_heads + h))
    col = pl.BlockSpec((1, tq, 1), lambda h, j, i: (h, jnp.maximum(i, j), 0))
    return pl.pallas_call(
        body, name=name, grid=(n_heads, nq, nq),
        in_specs=[pl.BlockSpec((tq, hd), qmap), pl.BlockSpec((tq, hd), kmap(1)), pl.BlockSpec((tq, hd), kmap(2)),
                  pl.BlockSpec((tq, hd), qmap), col, col, col,
                  pl.BlockSpec((1, 1, tq), lambda h, j, i: (h, 0, j))],
        out_specs=[pl.BlockSpec((tq, hd), lambda h, j, i: (j, h)), pl.BlockSpec((tq, hd), lambda h, j, i: (j, h)),
                   pl.BlockSpec((1, 1, tq), lambda h, j, i: (h, 0, j))],
        out_shape=[jax.ShapeDtypeStruct((length, n_heads * hd), BF16)] * 2
        + [jax.ShapeDtypeStruct((n_heads, 1, length), F32)],
        scratch_shapes=[pltpu.VMEM((tq, hd), F32), pltpu.VMEM((tq, hd), F32), pltpu.VMEM((1, tq), F32)],
        compiler_params=_params(("parallel", "parallel", "arbitrary")),
    )(qkvg, qkvg, qkvg, datt, lse, dsum, c_col, c_row)


def _row(v):
    return v.reshape(1, -1).astype(F32)


def _pad_heads(v, per_group):
    lead = v.shape[:-1]
    v = v.reshape(lead + (SSD_GROUPS, per_group))
    v = jnp.pad(v, [(0, 0)] * len(lead) + [(0, 0), (0, LANES - per_group)])
    return v.reshape(lead + (SSD_GROUPS * LANES,))


def _unpad_heads(v, per_group):
    lead = v.shape[:-1]
    return v.reshape(lead + (SSD_GROUPS, LANES))[..., :per_group].reshape(lead + (SSD_GROUPS * per_group,))


class _NoOverlap:
    def gather_rider(self, host):
        return None

    def gathered(self, host, carried):
        return {}

    def reduce_rider(self, grads):
        return None

    def reduced(self, carried):
        pass


def _pad_head_rows(w, per_group):
    w = w.reshape(SSD_GROUPS, per_group, w.shape[1])
    return jnp.pad(w, ((0, 0), (0, LANES - per_group), (0, 0))).reshape(SSD_GROUPS * LANES, w.shape[2])


def _unpad_head_rows(w, per_group):
    return w.reshape(SSD_GROUPS, LANES, w.shape[1])[:, :per_group].reshape(SSD_GROUPS * per_group, w.shape[1])


def _local_step(x, tgt, wb, sm, plan=None):
    plan = plan or _NoOverlap()
    wb = dict(wb)
    length, d = x.shape
    mix = 2 * d
    s5w = mix // 4
    ssdw = mix - s5w
    xbcw = ssdw + 2 * SSD_GROUPS * SSD_STATE
    n_ssd = ssdw // SSD_HEAD_DIM
    hpg = n_ssd // SSD_GROUPS
    fw = d
    o1, o2, o3 = 2 * s5w, 2 * s5w + ssdw, 2 * s5w + ssdw + xbcw
    w0t = wb["w0T"]
    w0_dt = _pad_head_rows(w0t[o3:], hpg)
    n_fox = fw // FOX_HEAD_DIM
    s5g = s5w // S5_GROUP
    s5s = s5g * S5_STATE
    grads = {}

    s5_in = (sm["l0_s5_lambda_re"], sm["l0_s5_lambda_im"], sm["l0_s5_log_step"], sm["l0_s5_b_re"],
             sm["l0_s5_b_im"], sm["l0_s5_c_re"], sm["l0_s5_c_im"])
    (a_re, a_im, bd_re, bd_im, cd_re, cd_imn), s5_vjp = jax.vjp(_s5_prepare, *s5_in)
    nj = bd_re.shape[0]
    bd_re_b, bd_im_b, cd_re_b, cd_imn_b = (t.astype(BF16) for t in (bd_re, bd_im, cd_re, cd_imn))
    tr = lambda t: jnp.swapaxes(t, 1, 2)
    dvec = _row(sm["l0_s5_d"])
    bglu = _row(sm["l0_s5_b_glu"])
    conv_w = sm["l0_ssd_conv_w"]
    conv_b = _row(sm["l0_ssd_conv_b"])

    def ssd_prepare(dt_bias, a_log, dd):
        return (_pad_heads(_row(dt_bias), hpg), _pad_heads(_row(-jnp.exp(a_log)), hpg),
                jnp.repeat(_row(dd), SSD_HEAD_DIM, axis=1))

    (dtb, ap, dpar), ssd_vjp = jax.vjp(ssd_prepare, sm["l0_ssd_dt_bias"], sm["l0_ssd_a_log"], sm["l0_ssd_d"])
    ssd_nw = _row(sm["l0_ssd_norm_w"])
    nw0, nw1, fnw = _row(sm["l0_norm_w"]), _row(sm["l1_norm_w"]), _row(sm["final_norm_w"])
    bf = jnp.pad(_row(sm["l1_fox_b_f"]), ((0, 0), (0, LANES - n_fox)))

    h0 = _norm_fwd(x, nw0, "l0_norm")
    ug = _matmul(h0, w0t, mode="nt", dims=(length, o1, d), name="l0_in_ug")
    z = _matmul(h0, w0t, mode="nt", dims=(length, ssdw, d), b_off=(o1, 0), name="l0_in_z")
    rider = plan.gather_rider("l0_in_xbc")
    xbc = _matmul(h0, w0t, mode="nt", dims=(length, xbcw, d), b_off=(o2, 0), name="l0_in_xbc", rider=rider)
    if rider is not None:
        xbc, carried = xbc
        wb.update(plan.gathered("l0_in_xbc", carried))
    dtraw = _matmul(h0, w0_dt, mode="nt", name="l0_in_dt")
    u_win, gate_win = (ug, s5w, 0), (ug, s5w, 1)

    shape3 = (length, s5s // LANES, LANES)
    bu_re = _bdmm([(ug, 0)], [bd_re_b], name="s5_bu_re").reshape(shape3)
    bu_im = _bdmm([(ug, 0)], [bd_im_b], name="s5_bu_im").reshape(shape3)
    s_re3, s_im3 = _s5_scan_fwd(bu_re, bu_im, a_re, a_im, "s5_scan")
    s_re, s_im = s_re3.reshape(length, s5s), s_im3.reshape(length, s5s)
    yc = _bdmm([s_re, s_im], [cd_re_b, cd_imn_b], name="s5_y")
    gl = _rowwise(_f_s5_gelu, [yc, u_win], [dvec], [(s5w, BF16)], tl=256, name="s5_gelu")[0]
    t_glu = _matmul(gl, wb["w_glu"], name="s5_glu")
    s5o = _rowwise(_f_s5_out, [yc, u_win, t_glu, gate_win], [dvec, bglu], [(s5w, BF16)], tl=256,
                   name="s5_out")[0]

    xc = _conv_fwd(xbc, conv_w, conv_b, "ssd_conv")
    y_ssd, states, carried = _ssd_fwd(xc, dtraw, dtb, ap, n_ssd, "ssd_scan", rider=plan.gather_rider("ssd_scan"))
    wb.update(plan.gathered("ssd_scan", carried))
    wg = ssdw // SSD_GROUPS
    ssdo = _rowwise(_f_ssd_out, [y_ssd, (xc, wg, 0), z], [dpar, ssd_nw], [(wg, BF16)], tl=256,
                    ncol=SSD_GROUPS, name="ssd_out")[0]
    x1 = _matmul(s5o, wb["w0_out"], dims=(length, d, s5w), addend=x, name="l0_out_s5")
    x1 = _matmul(ssdo, wb["w0_out"], dims=(length, d, ssdw), b_off=(s5w, 0), addend=x1, name="l0_out_ssd")

    h1 = _norm_fwd(x1, nw1, "l1_norm")
    w1t = wb["w1T"]
    w1_f = jnp.pad(w1t[4 * fw:], ((0, LANES - n_fox), (0, 0)))
    qkvg = _matmul(h1, w1t, mode="nt", dims=(length, 4 * fw, d), name="l1_in")
    fraw = _matmul(h1, w1_f, mode="nt", name="l1_in_f")
    cc = _fox_cumsum(fraw, bf, "fox_cumsum")
    c_t = cc[:, :n_fox].T
    c_col, c_row = c_t[:, :, None], c_t[:, None, :]
    att, lse = _fox_fwd(qkvg, c_col, c_row, n_fox, "fox_fwd")
    gate1_win = (qkvg, fw, 3)
    fox_o = _rowwise(_f_fox_out, [att, gate1_win], [], [(fw, BF16)], tl=256, name="fox_out")[0]
    x2 = _matmul(fox_o, wb["w1_out"], addend=x1, name="l1_out")

    def loss_fn(xt, tt, wt):
        def f(xx, ww):
            err = _f_norm(xx, ww) - tt
            return (0.5 / d) * err * err
        lanes, vjp = jax.vjp(f, xt, wt)
        dx, dw = vjp(jnp.ones_like(lanes))
        return dx, dx, jnp.sum(lanes, axis=0, keepdims=True), dw

    dx2, dx2b, loss_lanes, g_fnw = _rowwise(loss_fn, [x2, tgt], [fnw], [(d, F32), (d, BF16)], [d, d],
                                            tl=128, name="loss_head")
    grads["final_norm_w"] = g_fnw

    grads["l1_w_out"] = _matmul(fox_o, dx2b, mode="tn", name="l1_out_dw")
    do1 = _matmul(dx2b, wb["w1_out"], mode="nt", name="l1_out_dx")

    def fox_out_bwd(at, gt, dt_):
        _, vjp = jax.vjp(_f_fox_out, at, gt)
        return vjp(dt_)

    datt, dgate1 = _rowwise(fox_out_bwd, [att, gate1_win, do1], [], [(fw, F32), (fw, BF16)], tl=256,
                            name="fox_out_bwd")
    dq, dsum = _fox_bwd_q(qkvg, datt, lse, c_col, c_row, n_fox, "fox_bwd_q")
    dk, dv, dck = _fox_bwd_kv(qkvg, datt, lse, dsum, c_col, c_row, n_fox, "fox_bwd_kv")
    dcc = jnp.pad(dck[:, 0, :].T, ((0, 0), (0, LANES - n_fox)))
    dfraw, g_bf = _fox_cumsum_bwd(dcc, fraw, bf, "fox_cumsum_bwd")
    grads["l1_fox_b_f"] = g_bf[:, :n_fox]
    dsegs = [dq, dk, dv, dgate1]
    g1, n1 = None, w1t.shape[0]
    for i, s in enumerate(dsegs):
        g1 = _matmul(s, h1, mode="tn", rows_of=(n1, i * fw, g1), name=f"l1_in_dw{i}")
    g1_f = _matmul(dfraw, h1, mode="tn", name="l1_in_dwf")[:n_fox]
    grads["l1_w_inT"] = lax.dynamic_update_slice(g1, g1_f, (4 * fw, 0))
    dh1 = _matmul(dfraw, w1_f, mode="nn", name="l1_in_dxf")
    for i, s in enumerate(dsegs):
        dh1 = _matmul(s, w1t, mode="nn", dims=(length, d, fw), b_off=(i * fw, 0), addend=dh1,
                      name=f"l1_in_dx{i}")
    dx1, dx1b, grads["l1_norm_w"] = _norm_bwd(x1, dh1, dx2, nw1, "l1_norm_bwd")

    g_out = _matmul(s5o, dx1b, mode="tn", rows_of=(mix, 0, None), name="l0_out_dw_s5")
    grads["l0_w_out"] = _matmul(ssdo, dx1b, mode="tn", rows_of=(mix, s5w, g_out), name="l0_out_dw_ssd")
    ds5o = _matmul(dx1b, wb["w0_out"], mode="nt", dims=(length, s5w, d), name="l0_out_dx_s5")
    dssdo = _matmul(dx1b, wb["w0_out"], mode="nt", dims=(length, ssdw, d), b_off=(s5w, 0), name="l0_out_dx_ssd")

    def ssd_out_bwd(yt, xt, zt, dt_, dp, nw):
        _, vjp = jax.vjp(_f_ssd_out, yt, xt, zt, dp, nw)
        return vjp(dt_)

    dy_ssd, dxa, dz, g_dpar, g_ssd_nw = _rowwise(
        ssd_out_bwd, [y_ssd, (xc, wg, 0), z, dssdo], [dpar, ssd_nw],
        [(wg, F32), (wg, F32), (wg, BF16)], [wg, wg], tl=128, ncol=SSD_GROUPS, name="ssd_out_bwd")
    grads["l0_ssd_norm_w"] = g_ssd_nw
    dxs, db_ssd, dc_ssd, ddtraw, g_dtb, g_ap, carried = _ssd_bwd(
        dy_ssd, dxa, xc, dtraw, states, dtb, ap, n_ssd, "ssd_scan_bwd", rider=plan.reduce_rider(grads))
    plan.reduced(carried)
    g_dt_bias, g_a_log, g_ssd_d = ssd_vjp((g_dtb, g_ap, g_dpar))
    grads["l0_ssd_dt_bias"], grads["l0_ssd_a_log"], grads["l0_ssd_d"] = g_dt_bias, g_a_log, g_ssd_d
    dxc = jnp.concatenate([dxs, db_ssd, dc_ssd], axis=1)
    dpre, grads["l0_ssd_conv_w"], grads["l0_ssd_conv_b"] = _conv_bwd_pre(dxc, xbc, conv_w, conv_b, "ssd_conv_bwd_pre")
    dxbc = _conv_bwd_in(dpre, conv_w, "ssd_conv_bwd_in")

    def s5_out_bwd(yt, ut, tt, gt, dt_, dv_, bg):
        _, vjp = jax.vjp(_f_s5_out, yt, ut, tt, gt, dv_, bg)
        return vjp(dt_)

    dyc_a, du_a, dt_glu, dgate, g_dvec_a, g_bglu = _rowwise(
        s5_out_bwd, [yc, u_win, t_glu, gate_win, ds5o], [dvec, bglu],
        [(s5w, F32), (s5w, F32), (s5w, BF16), (s5w, BF16)], [s5w, s5w], tl=128, name="s5_out_bwd")
    grads["l0_s5_b_glu"] = g_bglu
    grads["l0_s5_w_glu"] = _matmul(gl, dt_glu, mode="tn", name="s5_glu_dw")
    dgl = _matmul(dt_glu, wb["w_glu"], mode="nt", name="s5_glu_dx")

    def s5_gelu_bwd(yt, ut, dg, dya, dua, dv_):
        _, vjp = jax.vjp(_f_s5_gelu, yt, ut, dv_)
        dy_, du_, ddv = vjp(dg)
        return dy_ + dya, du_ + dua, ddv

    dyc, du_ab, g_dvec_b = _rowwise(s5_gelu_bwd, [yc, u_win, dgl, dyc_a, du_a], [dvec],
                                    [(s5w, F32), (s5w, F32)], [s5w], tl=128, name="s5_gelu_bwd")
    ds_re = _bdmm([dyc], [tr(cd_re_b)], name="s5_ds_re").reshape(shape3)
    ds_im = _bdmm([dyc], [tr(cd_imn_b)], name="s5_ds_im").reshape(shape3)
    kin_s, kin_u = s5s // nj, s5w // nj
    g_cd_re = _bdmm_tn_sized(s_re, dyc, nj, kin_s, kin_u, 0, "s5_dcd_re")
    g_cd_imn = _bdmm_tn_sized(s_im, dyc, nj, kin_s, kin_u, 0, "s5_dcd_im")
    g_re3, g_im3, g_a_re, g_a_im = _s5_scan_bwd(ds_re, ds_im, s_re3, s_im3, a_re, a_im, "s5_scan_bwd")
    g_re, g_im = g_re3.reshape(length, s5s), g_im3.reshape(length, s5s)
    du = _bdmm([g_re, g_im], [tr(bd_re_b), tr(bd_im_b)], addend=du_ab, out_dtype=BF16, name="s5_du")
    g_bd_re = _bdmm_tn_sized(ug, g_re, nj, kin_u, kin_s, 0, "s5_dbd_re")
    g_bd_im = _bdmm_tn_sized(ug, g_im, nj, kin_u, kin_s, 0, "s5_dbd_im")
    s5_g = s5_vjp((g_a_re, g_a_im, g_bd_re, g_bd_im, g_cd_re, g_cd_imn))
    for nm, g in zip(("lambda_re", "lambda_im", "log_step", "b_re", "b_im", "c_re", "c_im"), s5_g):
        grads["l0_s5_" + nm] = g
    grads["l0_s5_d"] = (g_dvec_a + g_dvec_b).reshape(sm["l0_s5_d"].shape)

    g0, n0 = None, w0t.shape[0]
    for nm, s, off in (("u", du, 0), ("g", dgate, s5w), ("z", dz, o1), ("xbc", dxbc, o2)):
        g0 = _matmul(s, h0, mode="tn", rows_of=(n0, off, g0), name="l0_in_dw_" + nm)
    g0_dt = _unpad_head_rows(_matmul(ddtraw, h0, mode="tn", name="l0_in_dw_dt"), hpg)
    grads["l0_w_inT"] = lax.dynamic_update_slice(g0, g0_dt, (o3, 0))
    dh0 = _matmul(ddtraw, w0_dt, mode="nn", name="l0_in_dx_dt")
    for nm, s, off in (("u", du, 0), ("g", dgate, s5w), ("z", dz, o1), ("xbc", dxbc, o2)):
        dh0 = _matmul(s, w0t, mode="nn", dims=(length, d, s.shape[1]), b_off=(off, 0), addend=dh0,
                      name="l0_in_dx_" + nm)
    dx, _, grads["l0_norm_w"] = _norm_bwd(x, dh0, dx1, nw0, "l0_norm_bwd")
    return loss_lanes, dx, grads


_ANY = pl.BlockSpec(memory_space=pl.ANY)


def _place():
    x, y, c = lax.axis_index("x"), lax.axis_index("y"), lax.axis_index("c")
    return x, y, c, [(1 - x, y), (x, 1 - y), (1 - x, 1 - y)]


def _remote(src, dst, send_sem, recv_sem, to):
    return pltpu.make_async_remote_copy(src_ref=src, dst_ref=dst, send_sem=send_sem, recv_sem=recv_sem,
                                        device_id=to, device_id_type=MESH)


def _comm_call(body, n_in, out_shape, n_sems, name):
    return pl.pallas_call(
        body, name=name, in_specs=[_ANY] * n_in, out_specs=[_ANY] * len(out_shape), out_shape=out_shape,
        scratch_shapes=[pltpu.SemaphoreType.DMA((k,)) for k in n_sems],
        compiler_params=pltpu.CompilerParams(has_side_effects=True),
    )


def _half(ref_or_shape, c):
    ch = ref_or_shape.shape[-1] // 2
    return pl.ds(pl.multiple_of(c * ch, LANES), ch)


def _gather_rider(shards):
    n = len(shards)

    def sends(ins, outs, sems):
        send, recv = sems[:2]
        x, y, c, chips = _place()
        me = 2 * x + y
        return [_remote(ins[a].at[:, _half(ins[a], c)], outs[a].at[me, :, _half(ins[a], c)],
                        send.at[3 * a + k], recv.at[3 * a + k], (px, py, c))
                for a in range(n) for k, (px, py) in enumerate(chips)]

    def start(ins, outs, sems):
        for cp in sends(ins, outs, sems):
            cp.start()

    def finish(ins, outs, sems):
        send, recv, fsend, frecv = sems
        x, y, c, chips = _place()
        passed = []
        for a in range(n):
            for k, (px, py) in enumerate(chips):
                got = outs[a].at[2 * px + py, :, _half(ins[a], c)]
                _remote(got, got, send.at[3 * a + k], recv.at[3 * a + k], (px, py, c)).wait_recv()
                cp = _remote(got, got, fsend.at[3 * a + k], frecv.at[3 * a + k], (x, y, 1 - c))
                cp.start()
                passed.append(cp)
        for a in range(n):
            for k, (px, py) in enumerate(chips):
                got = outs[a].at[2 * px + py, :, _half(ins[a], 1 - c)]
                _remote(got, got, fsend.at[3 * a + k], frecv.at[3 * a + k], (x, y, 1 - c)).wait_recv()
        for cp in sends(ins, outs, sems) + passed:
            cp.wait_send()

    out_shape = [jax.ShapeDtypeStruct((N_SHARD,) + s.shape, s.dtype) for s in shards]
    return _Rider(shards, out_shape, [3 * n] * 4, start, finish)


def _chip_rider(parts):
    n = len(parts)

    def copies(ins, outs, sems):
        send, recv = sems
        x, y, c, chips = _place()
        return [_remote(ins[a].at[2 * px + py], outs[a].at[k], send.at[3 * a + k], recv.at[3 * a + k], (px, py, c))
                for a in range(n) for k, (px, py) in enumerate(chips)]

    def start(ins, outs, sems):
        for cp in copies(ins, outs, sems):
            cp.start()

    def finish(ins, outs, sems):
        for cp in copies(ins, outs, sems):
            cp.wait()

    out_shape = [jax.ShapeDtypeStruct((3,) + p.shape[1:], p.dtype) for p in parts]
    return _Rider(parts, out_shape, [3 * n] * 2, start, finish)


def _run_rider(rider, name):
    n_in, n_out = len(rider.inputs), len(rider.out_shape)

    def body(*refs):
        ins, outs, sems = refs[:n_in], refs[n_in:n_in + n_out], refs[n_in + n_out:]
        rider.start(ins, outs, sems)
        rider.finish(ins, outs, sems)

    return _comm_call(body, n_in, rider.out_shape, rider.sems, name)(*rider.inputs)


def _sibling_halves(grads, name):
    n = len(grads)

    def body(*refs):
        ins, outs = refs[:n], refs[n:2 * n]
        send, recv = refs[2 * n:]
        x, y, c, _ = _place()
        copies = [_remote(ins[a].at[:, :, _half(ins[a], 1 - c)], outs[a], send.at[a], recv.at[a], (x, y, 1 - c))
                  for a in range(n)]
        for cp in copies:
            cp.start()
        for cp in copies:
            cp.wait()

    out_shape = [jax.ShapeDtypeStruct(g.shape[:2] + (g.shape[2] // 2,), g.dtype) for g in grads]
    return _comm_call(body, n, out_shape, [n, n], name)(*grads)


def _join_halves(halves, name):
    n = len(halves)

    def body(*refs):
        outs = refs[n:2 * n]
        send, recv = refs[2 * n:]
        x, y, c, _ = _place()
        mine = [outs[a].at[:, _half(outs[a], c)] for a in range(n)]
        copies = [_remote(mine[a], mine[a], send.at[a], recv.at[a], (x, y, 1 - c)) for a in range(n)]
        for cp in copies:
            cp.start()
        for a in range(n):
            copies[a].wait_send()
            got = outs[a].at[:, _half(outs[a], 1 - c)]
            _remote(got, got, send.at[a], recv.at[a], (x, y, 1 - c)).wait_recv()

    return pl.pallas_call(
        body, name=name, in_specs=[_ANY] * n, out_specs=[_ANY] * n,
        out_shape=[jax.ShapeDtypeStruct(h.shape, h.dtype) for h in halves],
        input_output_aliases={a: a for a in range(n)},
        scratch_shapes=[pltpu.SemaphoreType.DMA((n,)), pltpu.SemaphoreType.DMA((n,))],
        compiler_params=pltpu.CompilerParams(has_side_effects=True),
    )(*halves)


def _gather_all(buf, name):
    def body(in_ref, out_ref, send, recv, lsem):
        x, y, c, _ = _place()
        me = 4 * x + 2 * y + c
        local = pltpu.make_async_copy(in_ref, out_ref.at[me], lsem.at[0])
        local.start()
        copies = []
        for k in range(1, N_DEV):
            fx, fy, fc = (k >> 2) & 1, (k >> 1) & 1, k & 1
            peer = (x + fx - 2 * x * fx, y + fy - 2 * y * fy, c + fc - 2 * c * fc)
            cp = _remote(in_ref, out_ref.at[me], send.at[k - 1], recv.at[k - 1], peer)
            cp.start()
            copies.append((cp, 4 * peer[0] + 2 * peer[1] + peer[2]))
        for k, (cp, slot) in enumerate(copies):
            cp.wait_send()
            got = out_ref.at[slot]
            _remote(got, got, send.at[k], recv.at[k], (x, y, c)).wait_recv()
        local.wait()

    out_shape = [jax.ShapeDtypeStruct((N_DEV,) + buf.shape, buf.dtype)]
    return _comm_call(body, 1, out_shape, [N_DEV - 1, N_DEV - 1, 1], name)(buf)[0]


def _sum_slots(buf, name):
    slots, rows, _ = buf.shape
    tr = _pick(rows, (512, 256, 128, 64, 32, 16, 8))

    def body(b_ref, o_ref):
        acc = b_ref[0]
        for s in range(1, slots):
            acc = acc + b_ref[s]
        o_ref[...] = acc

    return pl.pallas_call(
        body, name=name, grid=(rows // tr,),
        in_specs=[pl.BlockSpec((slots, tr, LANES), lambda i: (0, i, 0))],
        out_specs=pl.BlockSpec((tr, LANES), lambda i: (i, 0)),
        out_shape=jax.ShapeDtypeStruct((rows, LANES), F32),
        compiler_params=_params(("parallel",)),
    )(buf)


def _tile2(rows, cols, n_bufs):
    tr = max(t for t in range(8, min(rows, 2048) + 1, 8) if rows % t == 0) if rows % 8 == 0 else rows
    budget = 24 * 1024 * 1024 // (8 * n_bufs * tr)
    tc = max([t for t in range(LANES, cols + 1, LANES) if cols % t == 0 and t <= budget] or [LANES])
    return tr, tc


def _elementwise(fn, ins, out_dtypes, name):
    rows, cols = ins[0].shape
    tr, tc = _tile2(rows, cols, len(ins) + len(out_dtypes))
    n_in = len(ins)

    def body(*refs):
        outs = fn(*[r[...] for r in refs[:n_in]])
        for r, v in zip(refs[n_in:], outs if isinstance(outs, (tuple, list)) else (outs,)):
            r[...] = v.astype(r.dtype)

    blk = pl.BlockSpec((tr, tc), lambda i, j: (i, j))
    return pl.pallas_call(
        body, name=name, grid=(rows // tr, cols // tc), in_specs=[blk] * n_in, out_specs=[blk] * len(out_dtypes),
        out_shape=[jax.ShapeDtypeStruct((rows, cols), dt) for dt in out_dtypes],
        compiler_params=_params(("parallel", "parallel")),
    )(*ins)


def _presum(grad, sib, name):
    ns, rows, ch = sib.shape
    tr, tc = _tile2(rows, ch, 3)
    nct = ch // tc

    def body(g_ref, r_ref, o_ref):
        o_ref[...] = (g_ref[...] + r_ref[...]).astype(o_ref.dtype)

    blk = pl.BlockSpec((1, tr, tc), lambda j, i, k: (j, i, k))
    return pl.pallas_call(
        body, name=name, grid=(ns, rows // tr, nct),
        in_specs=[pl.BlockSpec((1, tr, tc), lambda j, i, k: (j, i, lax.axis_index("c") * nct + k)), blk],
        out_specs=blk, out_shape=jax.ShapeDtypeStruct((ns, rows, ch), BF16),
        compiler_params=_params(("parallel", "parallel", "parallel")),
    )(grad, sib)


def _finish_half(grad, sib, others, name):
    _, rows, ch = sib.shape
    tr, tc = _tile2(rows, ch, 6)
    nct = ch // tc

    def body(g_ref, r_ref, q_ref, o_ref):
        acc = g_ref[0] + r_ref[0]
        for k in range(3):
            acc = acc + q_ref[k].astype(F32)
        o_ref[...] = acc

    core = lambda: lax.axis_index("c")
    chip = lambda: 2 * lax.axis_index("x") + lax.axis_index("y")
    return pl.pallas_call(
        body, name=name, grid=(rows // tr, nct),
        in_specs=[pl.BlockSpec((1, tr, tc), lambda i, k: (chip(), i, core() * nct + k)),
                  pl.BlockSpec((1, tr, tc), lambda i, k: (chip(), i, k)),
                  pl.BlockSpec((3, tr, tc), lambda i, k: (0, i, k))],
        out_specs=pl.BlockSpec((tr, tc), lambda i, k: (i, core() * nct + k)),
        out_shape=jax.ShapeDtypeStruct((rows, 2 * ch), F32),
        compiler_params=_params(("parallel", "parallel")),
    )(grad, sib, others)


def _cast_bf16(w, name):
    return _elementwise(lambda t: t, [w], [BF16], name)[0]


_WEIGHTS = ("l0_norm_w", "l0_w_in", "l0_s5_lambda_re", "l0_s5_lambda_im", "l0_s5_log_step", "l0_s5_b_re",
            "l0_s5_b_im", "l0_s5_c_re", "l0_s5_c_im", "l0_s5_d", "l0_s5_w_glu", "l0_s5_b_glu", "l0_ssd_conv_w",
            "l0_ssd_conv_b", "l0_ssd_dt_bias", "l0_ssd_a_log", "l0_ssd_d", "l0_ssd_norm_w", "l0_w_out",
            "l1_norm_w", "l1_w_in", "l1_fox_b_f", "l1_w_out", "final_norm_w")
_COL_SHARDED = ("l0_w_in", "l1_w_in")
_ROW_SHARDED = ("l0_s5_w_glu", "l0_w_out", "l1_w_out")
_BIG = ("l0_w_in", "l0_s5_w_glu", "l0_w_out", "l1_w_in", "l1_w_out")
_CONV = "l0_ssd_conv_w"
_SMALL = tuple(n for n in _WEIGHTS if n not in _BIG and n != _CONV)


def _pack(arrays):
    flat = jnp.concatenate([a.reshape(-1).astype(F32) for a in arrays])
    size = flat.shape[0]
    padded = -(-size // (512 * LANES)) * (512 * LANES)
    return jnp.pad(flat, (0, padded - size)).reshape(-1, LANES)


def _unpack(buf, like):
    flat = buf.reshape(-1)
    out, pos = [], 0
    for a in like:
        out.append(flat[pos:pos + a.size].reshape(a.shape))
        pos += a.size
    return out


def _step(p):
    x, tgt = p["x"][0], p["loss_target"][0]
    d = x.shape[1]
    chip = 2 * lax.axis_index("x") + lax.axis_index("y")

    def rows_first(a, n):
        return a.T if n in _COL_SHARDED else a

    shard = {n: _cast_bf16(rows_first(p[n], n), "cast_" + n) for n in _BIG}
    shard[_CONV] = p[_CONV]

    def whole(n, g):
        g = lax.dynamic_update_index_in_dim(g, shard[n][None], chip, 0)
        return g.reshape(N_SHARD * g.shape[1], g.shape[2])

    now = ("l0_w_in", _CONV)
    got = dict(zip(now, _run_rider(_gather_rider([shard[n] for n in now]), "gather_first")))
    wb = {"w0T": whole("l0_w_in", got["l0_w_in"])}
    sm = {n: p[n] for n in _SMALL}
    taps, ccols = p[_CONV].shape
    conv_all = lax.dynamic_update_index_in_dim(got[_CONV], p[_CONV][None], chip, 0)
    sm[_CONV] = conv_all.transpose(1, 0, 2).reshape(taps, N_SHARD * ccols)
    later = {"l0_in_xbc": ("l0_s5_w_glu", "l0_w_out"), "ssd_scan": ("l1_w_in", "l1_w_out")}
    early = ("l0_w_out", "l1_w_in", "l1_w_out")
    grad_key = {n: n + "T" if n in _COL_SHARDED else n for n in _BIG}

    big, sib, others = {}, {}, {}

    def presummed(names, grads, tag):
        for n in names:
            g = grads[grad_key[n]]
            big[n] = g.reshape(N_SHARD, g.shape[0] // N_SHARD, g.shape[1])
        sib.update(zip(names, _sibling_halves([big[n] for n in names], "reduce_sibling_" + tag)))
        return [_presum(big[n], sib[n], "presum_" + n) for n in names]

    class Plan:
        def gather_rider(self, host):
            return _gather_rider([shard[n] for n in later[host]])

        def gathered(self, host, carried):
            w = {n: whole(n, g) for n, g in zip(later[host], carried)}
            if host == "l0_in_xbc":
                return {"w_glu": w["l0_s5_w_glu"], "w0_out": w["l0_w_out"]}
            return {"w1T": w["l1_w_in"], "w1_out": w["l1_w_out"]}

        def reduce_rider(self, grads):
            return _chip_rider(presummed(early, grads, "early"))

        def reduced(self, carried):
            others.update(zip(early, carried))

    loss_lanes, dx, grads = _local_step(x, tgt, wb, sm, Plan())

    small_like = [p[n] for n in _SMALL] + [sm[_CONV], jnp.zeros((1,), F32)]
    small_sum = _sum_slots(_gather_all(_pack([grads[n] for n in _SMALL] + [grads[_CONV], jnp.sum(loss_lanes)]),
                                       "gather_small"), "sum_small")
    *small_grads, conv_grad, loss = _unpack(small_sum, small_like)
    conv_grad = lax.dynamic_slice(conv_grad, (0, chip * ccols), (taps, ccols))
    final = dict(zip(_SMALL, small_grads))
    final[_CONV] = conv_grad

    late = tuple(n for n in _BIG if n not in early)
    others.update(zip(late, _run_rider(_chip_rider(presummed(late, grads, "late")), "reduce_chips_late")))
    done = [_finish_half(big[n], sib[n], others[n], "finish_" + n) for n in _BIG]

    delta, new_m, new_v = {}, {}, {}
    for n, full in zip(_BIG, _join_halves(done, "join_halves")):
        upd = _adamw(rows_first(p[n], n), full, rows_first(p["m_" + n], n), rows_first(p["v_" + n], n),
                     "adamw_" + n)
        final[n], delta[n], new_m[n], new_v[n] = (rows_first(t, n) for t in (full, *upd))
    rest = _SMALL + (_CONV,)
    packed = [_pack([t[n] for n in rest]) for t in
              ({n: p[n] for n in rest}, final, {n: p["m_" + n] for n in rest}, {n: p["v_" + n] for n in rest})]
    for dst, buf in zip((delta, new_m, new_v), _adamw(*packed, "adamw_small")):
        dst.update(zip(rest, _unpack(buf, [p[n] for n in rest])))

    outs = [loss.reshape(()), dx[None]]
    for group in (final, delta, new_m, new_v):
        outs += [group[n].reshape(p[n].shape) for n in _WEIGHTS]
    return tuple(outs)


_INPUTS = ("x",) + _WEIGHTS + ("loss_target",) + tuple("m_" + n for n in _WEIGHTS) + tuple("v_" + n for n in _WEIGHTS)


def kernel(x, l0_norm_w, l0_w_in, l0_s5_lambda_re, l0_s5_lambda_im, l0_s5_log_step, l0_s5_b_re, l0_s5_b_im, l0_s5_c_re,
           l0_s5_c_im, l0_s5_d, l0_s5_w_glu, l0_s5_b_glu, l0_ssd_conv_w, l0_ssd_conv_b, l0_ssd_dt_bias,
           l0_ssd_a_log, l0_ssd_d, l0_ssd_norm_w, l0_w_out, l1_norm_w, l1_w_in, l1_fox_b_f, l1_w_out,
           final_norm_w, loss_target, m_l0_norm_w, m_l0_w_in, m_l0_s5_lambda_re, m_l0_s5_lambda_im,
           m_l0_s5_log_step, m_l0_s5_b_re, m_l0_s5_b_im, m_l0_s5_c_re, m_l0_s5_c_im, m_l0_s5_d,
           m_l0_s5_w_glu, m_l0_s5_b_glu, m_l0_ssd_conv_w, m_l0_ssd_conv_b, m_l0_ssd_dt_bias, m_l0_ssd_a_log,
           m_l0_ssd_d, m_l0_ssd_norm_w, m_l0_w_out, m_l1_norm_w, m_l1_w_in, m_l1_fox_b_f, m_l1_w_out,
           m_final_norm_w, v_l0_norm_w, v_l0_w_in, v_l0_s5_lambda_re, v_l0_s5_lambda_im, v_l0_s5_log_step,
           v_l0_s5_b_re, v_l0_s5_b_im, v_l0_s5_c_re, v_l0_s5_c_im, v_l0_s5_d, v_l0_s5_w_glu, v_l0_s5_b_glu,
           v_l0_ssd_conv_w, v_l0_ssd_conv_b, v_l0_ssd_dt_bias, v_l0_ssd_a_log, v_l0_ssd_d, v_l0_ssd_norm_w,
           v_l0_w_out, v_l1_norm_w, v_l1_w_in, v_l1_fox_b_f, v_l1_w_out, v_final_norm_w):
    values = (x, l0_norm_w, l0_w_in, l0_s5_lambda_re, l0_s5_lambda_im, l0_s5_log_step, l0_s5_b_re, l0_s5_b_im,
              l0_s5_c_re, l0_s5_c_im, l0_s5_d, l0_s5_w_glu, l0_s5_b_glu, l0_ssd_conv_w, l0_ssd_conv_b,
              l0_ssd_dt_bias, l0_ssd_a_log, l0_ssd_d, l0_ssd_norm_w, l0_w_out, l1_norm_w, l1_w_in,
              l1_fox_b_f, l1_w_out, final_norm_w, loss_target, m_l0_norm_w, m_l0_w_in,
              m_l0_s5_lambda_re, m_l0_s5_lambda_im, m_l0_s5_log_step, m_l0_s5_b_re, m_l0_s5_b_im,
              m_l0_s5_c_re, m_l0_s5_c_im, m_l0_s5_d, m_l0_s5_w_glu, m_l0_s5_b_glu, m_l0_ssd_conv_w,
              m_l0_ssd_conv_b, m_l0_ssd_dt_bias, m_l0_ssd_a_log, m_l0_ssd_d, m_l0_ssd_norm_w,
              m_l0_w_out, m_l1_norm_w, m_l1_w_in, m_l1_fox_b_f, m_l1_w_out, m_final_norm_w, v_l0_norm_w,
              v_l0_w_in, v_l0_s5_lambda_re, v_l0_s5_lambda_im, v_l0_s5_log_step, v_l0_s5_b_re,
              v_l0_s5_b_im, v_l0_s5_c_re, v_l0_s5_c_im, v_l0_s5_d, v_l0_s5_w_glu, v_l0_s5_b_glu,
              v_l0_ssd_conv_w, v_l0_ssd_conv_b, v_l0_ssd_dt_bias, v_l0_ssd_a_log, v_l0_ssd_d,
              v_l0_ssd_norm_w, v_l0_w_out, v_l1_norm_w, v_l1_w_in, v_l1_fox_b_f, v_l1_w_out,
              v_final_norm_w)
    return _step(dict(zip(_INPUTS, values)))
```

```python
import functools
import math

import jax
import jax.numpy as jnp
from jax import lax
from jax.experimental import pallas as pl
from jax.experimental.pallas import tpu as pltpu

F32 = jnp.float32
BF16 = jnp.bfloat16

S5_GROUP = 16
S5_STATE = 64
S5_EIG_CLIP = -1e-4
SSD_HEAD_DIM = 64
SSD_GROUPS = 8
SSD_STATE = 128
SSD_CONV = 4
SSD_CHUNK = 128
FOX_HEAD_DIM = 128
FOX_TILE = 1024
NORM_EPS = 1e-5
ADAM_LR = 0.001
ADAM_B1 = 0.9
ADAM_B2 = 0.999
ADAM_EPS = 1e-08
ADAM_WD = 0.01
ADAM_STEP = 10

N_SHARD = 4
N_DEV = 8
LANES = 128
VMEM_LIMIT = 56 * 1024 * 1024
MESH = pl.DeviceIdType.MESH


def _pick(dim, prefs, offs=()):
    for p in prefs:
        if dim % p == 0 and all(o % p == 0 for o in offs):
            return p
    return dim


def _params(sem=None, vmem=VMEM_LIMIT):
    return pltpu.CompilerParams(dimension_semantics=sem, vmem_limit_bytes=vmem)


class _Rider:
    def __init__(self, inputs, out_shape, sems, start, finish):
        self.inputs, self.out_shape, self.sems = list(inputs), list(out_shape), list(sems)
        self.start, self.finish = start, finish


def _hosted_call(body, *, name, grid, in_specs, out_specs, out_shape, scratch_shapes, sem, args, rider=None):
    single = not isinstance(out_shape, (list, tuple))
    out_specs = [out_specs] if single else list(out_specs)
    out_shape = [out_shape] if single else list(out_shape)
    if rider is None:
        res = pl.pallas_call(body, name=name, grid=grid, in_specs=in_specs, out_specs=out_specs,
                             out_shape=out_shape, scratch_shapes=scratch_shapes,
                             compiler_params=_params(sem))(*args)
        return (res[0] if single else res), []
    n_in, n_out, n_scr = len(in_specs), len(out_shape), len(scratch_shapes)
    n_rin, n_rout = len(rider.inputs), len(rider.out_shape)

    def carried(*refs):
        ins, refs = refs[:n_in], refs[n_in:]
        rin, refs = refs[:n_rin], refs[n_rin:]
        outs, refs = refs[:n_out], refs[n_out:]
        rout, refs = refs[:n_rout], refs[n_rout:]
        scr, rsem = refs[:n_scr], refs[n_scr:]
        ids = [pl.program_id(k) for k in range(len(grid))]
        first = functools.reduce(jnp.logical_and, [i == 0 for i in ids])
        last = functools.reduce(jnp.logical_and, [i == g - 1 for i, g in zip(ids, grid)])

        @pl.when(first)
        def _():
            rider.start(rin, rout, rsem)

        body(*ins, *outs, *scr)

        @pl.when(last)
        def _():
            rider.finish(rin, rout, rsem)

    res = pl.pallas_call(
        carried, name=name, grid=grid,
        in_specs=list(in_specs) + [_ANY] * n_rin, out_specs=out_specs + [_ANY] * n_rout,
        out_shape=out_shape + rider.out_shape,
        scratch_shapes=list(scratch_shapes) + [pltpu.SemaphoreType.DMA((k,)) for k in rider.sems],
        compiler_params=pltpu.CompilerParams(dimension_semantics=("arbitrary",) * len(grid),
                                             vmem_limit_bytes=VMEM_LIMIT, has_side_effects=True),
    )(*args, *rider.inputs)
    outs = res[:n_out]
    return (outs[0] if single else outs), list(res[n_out:])


def _matmul(a, b, *, mode="nn", dims=None, a_off=(0, 0), b_off=(0, 0), addend=None,
            out_dtype=F32, rider=None, rows_of=None, name):
    if dims is None:
        if mode == "nn":
            dims = (a.shape[0], b.shape[1], a.shape[1])
        elif mode == "nt":
            dims = (a.shape[0], b.shape[0], a.shape[1])
        else:
            dims = (a.shape[1], b.shape[1], a.shape[0])
    m, n, k = dims
    if mode == "nn":
        om, on, ok = (a_off[0],), (b_off[1],), (a_off[1], b_off[0])
    elif mode == "nt":
        om, on, ok = (a_off[0],), (b_off[0],), (a_off[1], b_off[1])
    else:
        om, on, ok = (a_off[1],), (b_off[1],), (a_off[0], b_off[0])
    tm = _pick(m, (1024, 512, 256, 128), om)
    tn = _pick(n, (1024, 768, 512, 384, 256, 128), on)
    tk = _pick(k, (2048, 1024, 512, 256, 128), ok)
    nk = k // tk
    if mode == "nn":
        a_blk, a_div = (tm, tk), (tm, tk)
        b_blk, b_div = (tk, tn), (tk, tn)
        a_map = lambda i, j, kk: (i + a_off[0] // tm, kk + a_off[1] // tk)
        b_map = lambda i, j, kk: (kk + b_off[0] // tk, j + b_off[1] // tn)
        dn = (((1,), (0,)), ((), ()))
    elif mode == "nt":
        a_blk, a_div = (tm, tk), (tm, tk)
        b_blk, b_div = (tn, tk), (tn, tk)
        a_map = lambda i, j, kk: (i + a_off[0] // tm, kk + a_off[1] // tk)
        b_map = lambda i, j, kk: (j + b_off[0] // tn, kk + b_off[1] // tk)
        dn = (((1,), (1,)), ((), ()))
    else:
        a_blk, a_div = (tk, tm), (tk, tm)
        b_blk, b_div = (tk, tn), (tk, tn)
        a_map = lambda i, j, kk: (kk + a_off[0] // tk, i + a_off[1] // tm)
        b_map = lambda i, j, kk: (kk + b_off[0] // tk, j + b_off[1] // tn)
        dn = (((0,), (0,)), ((), ()))
    assert a_off[0] % a_div[0] == 0 and a_off[1] % a_div[1] == 0, (name, a_off, a_div)
    assert b_off[0] % b_div[0] == 0 and b_off[1] % b_div[1] == 0, (name, b_off, b_div)
    has_add = addend is not None

    def body(*refs):
        if has_add:
            a_ref, b_ref, c_ref, o_ref, acc_ref = refs
        else:
            a_ref, b_ref, o_ref, acc_ref = refs
        kk = pl.program_id(2)

        @pl.when(kk == 0)
        def _():
            acc_ref[...] = jnp.zeros_like(acc_ref)

        acc_ref[...] += lax.dot_general(a_ref[...].astype(BF16), b_ref[...].astype(BF16), dn,
                                        preferred_element_type=F32)

        @pl.when(kk == nk - 1)
        def _():
            r = acc_ref[...]
            if has_add:
                r = r + c_ref[...].astype(F32)
            o_ref[...] = r.astype(o_ref.dtype)

    in_specs = [pl.BlockSpec(a_blk, a_map), pl.BlockSpec(b_blk, b_map)]
    args = [a, b]
    if has_add:
        in_specs.append(pl.BlockSpec((tm, tn), lambda i, j, kk: (i, j)))
        args.append(addend)
    if rows_of is not None:
        total, row_off, buf = rows_of
        assert row_off % tm == 0 and rider is None, (name, row_off, tm)
        if buf is not None:
            def body_into(*refs):
                body(*refs[:len(args)], *refs[len(args) + 1:])

            return pl.pallas_call(
                body_into, name=name, grid=(m // tm, n // tn, nk),
                in_specs=in_specs + [pl.BlockSpec(memory_space=pl.ANY)],
                out_specs=pl.BlockSpec((tm, tn), lambda i, j, kk: (i + row_off // tm, j)),
                out_shape=jax.ShapeDtypeStruct((total, n), out_dtype),
                input_output_aliases={len(args): 0},
                scratch_shapes=[pltpu.VMEM((tm, tn), F32)],
                compiler_params=_params(("parallel", "parallel", "arbitrary")),
            )(*args, buf)
        return pl.pallas_call(
            body, name=name, grid=(m // tm, n // tn, nk), in_specs=in_specs,
            out_specs=pl.BlockSpec((tm, tn), lambda i, j, kk: (i + row_off // tm, j)),
            out_shape=jax.ShapeDtypeStruct((total, n), out_dtype),
            scratch_shapes=[pltpu.VMEM((tm, tn), F32)],
            compiler_params=_params(("parallel", "parallel", "arbitrary")),
        )(*args)
    out, carried = _hosted_call(
        body, name=name, grid=(m // tm, n // tn, nk),
        in_specs=in_specs, out_specs=pl.BlockSpec((tm, tn), lambda i, j, kk: (i, j)),
        out_shape=jax.ShapeDtypeStruct((m, n), out_dtype),
        scratch_shapes=[pltpu.VMEM((tm, tn), F32)],
        sem=("parallel", "parallel", "arbitrary"), args=args, rider=rider)
    return out if rider is None else (out, carried)


def _rowwise(fn, rows, params, out_rows, out_accs=(), *, tl, ncol=1, name):
    rows = [r if isinstance(r, tuple) else (r, r.shape[1] // ncol, 0) for r in rows]
    n_rows, n_par, n_or, n_oa = len(rows), len(params), len(out_rows), len(out_accs)
    length = rows[0][0].shape[0]
    tl = _pick(length, [t for t in (1024, 512, 256, 128, 64, 32, 16, 8) if t <= tl])

    def body(*refs):
        row_refs = refs[:n_rows]
        par_refs = refs[n_rows:n_rows + n_par]
        or_refs = refs[n_rows + n_par:n_rows + n_par + n_or]
        oa_refs = refs[n_rows + n_par + n_or:]
        outs = fn(*[r[...] for r in row_refs], *[p[...] for p in par_refs])
        if not isinstance(outs, (tuple, list)):
            outs = (outs,)
        for r, v in zip(or_refs, outs[:n_or]):
            r[...] = v.astype(r.dtype)
        if n_oa:
            @pl.when(pl.program_id(1) == 0)
            def _():
                for r in oa_refs:
                    r[...] = jnp.zeros_like(r)

            for r, v in zip(oa_refs, outs[n_or:]):
                r[...] += v.astype(F32)

    in_specs = [pl.BlockSpec((tl, w), functools.partial(lambda j, i, b0: (i, b0 + j), b0=b0))
                for (_, w, b0) in rows]
    in_specs += [pl.BlockSpec((p.shape[0], p.shape[1] // ncol), lambda j, i: (0, j)) for p in params]
    out_specs = [pl.BlockSpec((tl, w), lambda j, i: (i, j)) for (w, _) in out_rows]
    out_specs += [pl.BlockSpec((1, w), lambda j, i: (0, j)) for w in out_accs]
    out_shape = [jax.ShapeDtypeStruct((length, ncol * w), dt) for (w, dt) in out_rows]
    out_shape += [jax.ShapeDtypeStruct((1, ncol * w), F32) for w in out_accs]
    res = pl.pallas_call(
        body, name=name, grid=(ncol, length // tl),
        in_specs=in_specs, out_specs=out_specs, out_shape=out_shape,
        compiler_params=_params(("parallel", "arbitrary" if n_oa else "parallel")),
    )(*[r[0] for r in rows], *params)
    return res


def _bdmm(xs, ws, *, addend=None, out_dtype=F32, name):
    nj, kin, kout = ws[0].shape
    xs = [x if isinstance(x, tuple) else (x, 0) for x in xs]
    length = xs[0][0].shape[0]
    tl = _pick(length, (2048, 1024, 512, 256, 128))
    n_x = len(xs)
    has_add = addend is not None

    def body(*refs):
        x_refs = refs[:n_x]
        w_refs = refs[n_x:2 * n_x]
        o_ref = refs[-1]
        acc = None
        for xr, wr in zip(x_refs, w_refs):
            t = jnp.dot(xr[...].astype(BF16), wr[0], preferred_element_type=F32)
            acc = t if acc is None else acc + t
        if has_add:
            acc = acc + refs[2 * n_x][...].astype(F32)
        o_ref[...] = acc.astype(o_ref.dtype)

    in_specs = [pl.BlockSpec((tl, kin), functools.partial(lambda i, j, b0: (i, b0 + j), b0=b0)) for (_, b0) in xs]
    in_specs += [pl.BlockSpec((1, kin, kout), lambda i, j: (j, 0, 0)) for _ in ws]
    args = [x[0] for x in xs] + list(ws)
    if has_add:
        in_specs.append(pl.BlockSpec((tl, kout), lambda i, j: (i, j)))
        args.append(addend)
    return pl.pallas_call(
        body, name=name, grid=(length // tl, nj),
        in_specs=in_specs, out_specs=pl.BlockSpec((tl, kout), lambda i, j: (i, j)),
        out_shape=jax.ShapeDtypeStruct((length, nj * kout), out_dtype),
        compiler_params=_params(("parallel", "parallel")),
    )(*args)


def _bdmm_tn_sized(x, g, nj, kin, kout, x_first, name):
    length = x.shape[0]
    tl = _pick(length, (512, 256, 128))
    nt = length // tl

    def body(x_ref, g_ref, o_ref):
        @pl.when(pl.program_id(1) == 0)
        def _():
            o_ref[...] = jnp.zeros_like(o_ref)

        o_ref[0] += lax.dot_general(x_ref[...].astype(BF16), g_ref[...].astype(BF16),
                                    (((0,), (0,)), ((), ())), preferred_element_type=F32)

    return pl.pallas_call(
        body, name=name, grid=(nj, nt),
        in_specs=[pl.BlockSpec((tl, kin), lambda j, t: (t, x_first + j)),
                  pl.BlockSpec((tl, kout), lambda j, t: (t, j))],
        out_specs=pl.BlockSpec((1, kin, kout), lambda j, t: (j, 0, 0)),
        out_shape=jax.ShapeDtypeStruct((nj, kin, kout), F32),
        compiler_params=_params(("parallel", "arbitrary")),
    )(x, g)


def _f_norm(x, w):
    return x * lax.rsqrt(jnp.mean(x * x, axis=-1, keepdims=True) + NORM_EPS) * w


def _gelu(y):
    return 0.5 * y * (1.0 + jnp.tanh(math.sqrt(2.0 / math.pi) * (y + 0.044715 * (y * y * y))))


def _sigmoid(x):
    return 1.0 / (1.0 + jnp.exp(-x))


def _silu(x):
    return x * _sigmoid(x)


def _softplus(x):
    return jnp.maximum(x, 0.0) + jnp.log(1.0 + jnp.exp(-jnp.abs(x)))


def _f_s5_gelu(yc, u, dvec):
    return _gelu(yc + dvec * u)


def _f_s5_out(yc, u, t, gate, dvec, bglu):
    gl = _gelu(yc + dvec * u)
    return gl * _sigmoid(t + bglu) * _silu(gate)


def _f_ssd_out(y, xs, z, dpar, nw):
    v = (y + dpar * xs) * _silu(z)
    return v * lax.rsqrt(jnp.mean(v * v, axis=-1, keepdims=True) + NORM_EPS) * nw


def _f_fox_out(att, gate):
    return att * _silu(gate)


def _norm_fwd(x, w, name):
    return _rowwise(lambda xt, wt: _f_norm(xt, wt), [x], [w], [(x.shape[1], BF16)], tl=256, name=name)[0]


def _norm_bwd(x, dh, dres, w, name):
    d = x.shape[1]

    def fn(xt, dht, drt, wt):
        _, vjp = jax.vjp(_f_norm, xt, wt)
        dx, dw = vjp(dht)
        dx = dx + drt
        return dx, dx, dw

    return _rowwise(fn, [x, dh, dres], [w], [(d, F32), (d, BF16)], [d], tl=128, name=name)


def _adamw_math(w, g, m, v):
    m = ADAM_B1 * m + (1.0 - ADAM_B1) * g
    v = ADAM_B2 * v + (1.0 - ADAM_B2) * jnp.square(g)
    m_hat = m / (1.0 - ADAM_B1 ** ADAM_STEP)
    v_hat = v / (1.0 - ADAM_B2 ** ADAM_STEP)
    delta = -ADAM_LR * (m_hat / (jnp.sqrt(v_hat) + ADAM_EPS) + ADAM_WD * w)
    return delta, m, v


def _adamw(w, g, m, v, name):
    return _elementwise(_adamw_math, [w, g, m, v], [F32] * 3, name)


def _s5_scan_fwd(bu_re, bu_im, a_re, a_im, name):
    length, rows, _ = bu_re.shape
    rb = _pick(rows, (32, 16, 8))
    tl = _pick(length, (64, 32, 16, 8))

    def body(bur_ref, bui_ref, ar_ref, ai_ref, sr_ref, si_ref, st_ref):
        @pl.when(pl.program_id(1) == 0)
        def _():
            st_ref[...] = jnp.zeros_like(st_ref)

        ar = ar_ref[...]
        ai = ai_ref[...]

        def step(l, carry):
            sr, si = carry
            nr = ar * sr - ai * si + bur_ref[l]
            ni = ar * si + ai * sr + bui_ref[l]
            sr_ref[l] = nr
            si_ref[l] = ni
            return nr, ni

        sr, si = lax.fori_loop(0, tl, step, (st_ref[0], st_ref[1]))
        st_ref[0] = sr
        st_ref[1] = si

    blk = pl.BlockSpec((tl, rb, LANES), lambda cb, t: (t, cb, 0))
    ablk = pl.BlockSpec((rb, LANES), lambda cb, t: (cb, 0))
    return pl.pallas_call(
        body, name=name, grid=(rows // rb, length // tl),
        in_specs=[blk, blk, ablk, ablk], out_specs=[blk, blk],
        out_shape=[jax.ShapeDtypeStruct(bu_re.shape, F32)] * 2,
        scratch_shapes=[pltpu.VMEM((2, rb, LANES), F32)],
        compiler_params=_params(("parallel", "arbitrary")),
    )(bu_re, bu_im, a_re, a_im)


def _s5_scan_bwd(ds_re, ds_im, s_re, s_im, a_re, a_im, name):
    length, rows, _ = ds_re.shape
    rb = _pick(rows, (32, 16, 8))
    tl = _pick(length, (64, 32, 16, 8))
    nt = length // tl

    def body(dsr_ref, dsi_ref, sr_ref, si_ref, pr_ref, pi_ref, ar_ref, ai_ref,
             gr_ref, gi_ref, dar_ref, dai_ref, st_ref):
        t = pl.program_id(1)

        @pl.when(t == 0)
        def _():
            st_ref[...] = jnp.zeros_like(st_ref)
            dar_ref[...] = jnp.zeros_like(dar_ref)
            dai_ref[...] = jnp.zeros_like(dai_ref)

        ar = ar_ref[...]
        ai = ai_ref[...]

        def adj(l, gr, gi):
            ngr = dsr_ref[l] + ar * gr + ai * gi
            ngi = dsi_ref[l] + ar * gi - ai * gr
            gr_ref[l] = ngr
            gi_ref[l] = ngi
            return ngr, ngi

        def step(idx, carry):
            gr, gi, dar, dai = carry
            l = tl - 1 - idx
            gr, gi = adj(l, gr, gi)
            pr = sr_ref[l - 1]
            pi = si_ref[l - 1]
            dar = dar + gr * pr + gi * pi
            dai = dai + gi * pr - gr * pi
            return gr, gi, dar, dai

        zero = jnp.zeros((rb, LANES), F32)
        gr, gi, dar, dai = lax.fori_loop(0, tl - 1, step, (st_ref[0], st_ref[1], zero, zero))
        gr, gi = adj(0, gr, gi)
        first = (t == nt - 1)
        pr = jnp.where(first, 0.0, pr_ref[0])
        pi = jnp.where(first, 0.0, pi_ref[0])
        dar = dar + gr * pr + gi * pi
        dai = dai + gi * pr - gr * pi
        st_ref[0] = gr
        st_ref[1] = gi
        dar_ref[...] += dar
        dai_ref[...] += dai

    blk = pl.BlockSpec((tl, rb, LANES), lambda cb, t: (nt - 1 - t, cb, 0))
    prev = pl.BlockSpec((1, rb, LANES), lambda cb, t: (jnp.maximum((nt - 1 - t) * tl - 1, 0), cb, 0))
    ablk = pl.BlockSpec((rb, LANES), lambda cb, t: (cb, 0))
    return pl.pallas_call(
        body, name=name, grid=(rows // rb, nt),
        in_specs=[blk, blk, blk, blk, prev, prev, ablk, ablk],
        out_specs=[blk, blk, ablk, ablk],
        out_shape=[jax.ShapeDtypeStruct(ds_re.shape, F32)] * 2 + [jax.ShapeDtypeStruct(a_re.shape, F32)] * 2,
        scratch_shapes=[pltpu.VMEM((2, rb, LANES), F32)],
        compiler_params=_params(("parallel", "arbitrary")),
    )(ds_re, ds_im, s_re, s_im, s_re, s_im, a_re, a_im)


def _s5_prepare(lam_re, lam_im, log_step, b_re, b_im, c_re, c_im):
    groups, state = lam_re.shape
    lr = jnp.minimum(lam_re, S5_EIG_CLIP)
    li = lam_im
    step = jnp.exp(log_step)[:, None]
    mag = jnp.exp(lr * step)
    ab_re = mag * jnp.cos(li * step)
    ab_im = mag * jnp.sin(li * step)
    denom = lr * lr + li * li
    nr = ab_re - 1.0
    ni = ab_im
    coef_re = (nr * lr + ni * li) / denom
    coef_im = (ni * lr - nr * li) / denom
    bb_re = coef_re[..., None] * b_re - coef_im[..., None] * b_im
    bb_im = coef_re[..., None] * b_im + coef_im[..., None] * b_re
    per = LANES // S5_GROUP
    nj = groups // per
    eye = jnp.eye(per, dtype=F32)

    def in_map(bb):
        return jnp.einsum('jgph,gk->jghkp', bb.reshape(nj, per, state, S5_GROUP), eye).reshape(
            nj, per * S5_GROUP, per * state)

    def out_map(cc):
        return jnp.einsum('jghp,gk->jgpkh', cc.reshape(nj, per, S5_GROUP, state), eye).reshape(
            nj, per * state, per * S5_GROUP)

    shape2 = (groups * state // LANES, LANES)
    return (ab_re.reshape(shape2), ab_im.reshape(shape2), in_map(bb_re), in_map(bb_im),
            out_map(c_re), -out_map(c_im))


def _shift_down(cur, prev8, j):
    rolled = pltpu.roll(cur, j, 0)
    pr = pltpu.roll(prev8, j, 0)
    row = lax.broadcasted_iota(jnp.int32, cur.shape, 0)
    return jnp.where(row < j, jnp.tile(pr, (cur.shape[0] // 8, 1)), rolled)


def _shift_up(cur, next8, j):
    tl = cur.shape[0]
    rolled = pltpu.roll(cur, tl - j, 0)
    nx = pltpu.roll(next8, 8 - j, 0)
    row = lax.broadcasted_iota(jnp.int32, cur.shape, 0)
    return jnp.where(row >= tl - j, jnp.tile(nx, (tl // 8, 1)), rolled)


def _conv_tiles(length, ch):
    return _pick(length, (256, 128, 64, 32, 16, 8)), _pick(ch, (1024, 512, 256, 128))


def _conv_fwd(xbc, w, b, name):
    length, ch = xbc.shape
    tl, tc = _conv_tiles(length, ch)

    def body(x_ref, p_ref, w_ref, b_ref, o_ref):
        cur = x_ref[...]
        prev8 = jnp.where(pl.program_id(1) == 0, 0.0, p_ref[...])
        pre = b_ref[...] + w_ref[SSD_CONV - 1:SSD_CONV, :] * cur
        for j in range(1, SSD_CONV):
            pre = pre + w_ref[SSD_CONV - 1 - j:SSD_CONV - j, :] * _shift_down(cur, prev8, j)
        o_ref[...] = _silu(pre)

    return pl.pallas_call(
        body, name=name, grid=(ch // tc, length // tl),
        in_specs=[pl.BlockSpec((tl, tc), lambda c, i: (i, c)),
                  pl.BlockSpec((8, tc), lambda c, i: (jnp.maximum(i * (tl // 8) - 1, 0), c)),
                  pl.BlockSpec((SSD_CONV, tc), lambda c, i: (0, c)),
                  pl.BlockSpec((1, tc), lambda c, i: (0, c))],
        out_specs=pl.BlockSpec((tl, tc), lambda c, i: (i, c)),
        out_shape=jax.ShapeDtypeStruct((length, ch), F32),
        compiler_params=_params(("parallel", "parallel")),
    )(xbc, xbc, w, b)


def _conv_bwd_pre(dxc, xbc, w, b, name):
    length, ch = xbc.shape
    tl, tc = _conv_tiles(length, ch)

    def body(d_ref, x_ref, p_ref, w_ref, b_ref, o_ref, dw_ref, db_ref):
        @pl.when(pl.program_id(1) == 0)
        def _():
            dw_ref[...] = jnp.zeros_like(dw_ref)
            db_ref[...] = jnp.zeros_like(db_ref)

        cur = x_ref[...]
        prev8 = jnp.where(pl.program_id(1) == 0, 0.0, p_ref[...])
        shifted = [cur] + [_shift_down(cur, prev8, j) for j in range(1, SSD_CONV)]
        pre = b_ref[...]
        for j in range(SSD_CONV):
            pre = pre + w_ref[SSD_CONV - 1 - j:SSD_CONV - j, :] * shifted[j]
        sg = _sigmoid(pre)
        dpre = d_ref[...] * (sg * (1.0 + pre * (1.0 - sg)))
        o_ref[...] = dpre
        db_ref[...] += jnp.sum(dpre, axis=0, keepdims=True)
        row = lax.broadcasted_iota(jnp.int32, (SSD_CONV, tc), 0)
        dw = jnp.zeros((SSD_CONV, tc), F32)
        for j in range(SSD_CONV):
            dw = dw + jnp.where(row == SSD_CONV - 1 - j, jnp.sum(dpre * shifted[j], axis=0, keepdims=True), 0.0)
        dw_ref[...] += dw

    return pl.pallas_call(
        body, name=name, grid=(ch // tc, length // tl),
        in_specs=[pl.BlockSpec((tl, tc), lambda c, i: (i, c)),
                  pl.BlockSpec((tl, tc), lambda c, i: (i, c)),
                  pl.BlockSpec((8, tc), lambda c, i: (jnp.maximum(i * (tl // 8) - 1, 0), c)),
                  pl.BlockSpec((SSD_CONV, tc), lambda c, i: (0, c)),
                  pl.BlockSpec((1, tc), lambda c, i: (0, c))],
        out_specs=[pl.BlockSpec((tl, tc), lambda c, i: (i, c)),
                   pl.BlockSpec((SSD_CONV, tc), lambda c, i: (0, c)),
                   pl.BlockSpec((1, tc), lambda c, i: (0, c))],
        out_shape=[jax.ShapeDtypeStruct((length, ch), F32), jax.ShapeDtypeStruct((SSD_CONV, ch), F32),
                   jax.ShapeDtypeStruct((1, ch), F32)],
        compiler_params=_params(("parallel", "arbitrary")),
    )(dxc, xbc, xbc, w, b)


def _conv_bwd_in(dpre, w, name):
    length, ch = dpre.shape
    tl, tc = _conv_tiles(length, ch)
    nt = length // tl

    def body(d_ref, n_ref, w_ref, o_ref):
        cur = d_ref[...]
        next8 = jnp.where(pl.program_id(1) == nt - 1, 0.0, n_ref[...])
        acc = w_ref[SSD_CONV - 1:SSD_CONV, :] * cur
        for j in range(1, SSD_CONV):
            acc = acc + w_ref[SSD_CONV - 1 - j:SSD_CONV - j, :] * _shift_up(cur, next8, j)
        o_ref[...] = acc.astype(o_ref.dtype)

    return pl.pallas_call(
        body, name=name, grid=(ch // tc, nt),
        in_specs=[pl.BlockSpec((tl, tc), lambda c, i: (i, c)),
                  pl.BlockSpec((8, tc), lambda c, i: (jnp.minimum((i + 1) * (tl // 8), length // 8 - 1), c)),
                  pl.BlockSpec((SSD_CONV, tc), lambda c, i: (0, c))],
        out_specs=pl.BlockSpec((tl, tc), lambda c, i: (i, c)),
        out_shape=jax.ShapeDtypeStruct((length, ch), BF16),
        compiler_params=_params(("parallel", "parallel")),
    )(dpre, dpre, w)


def _split(x, terms):
    parts = []
    for _ in range(terms):
        part = x.astype(BF16)
        parts.append(part)
        x = x - part.astype(F32)
    return parts


def _dot(a, b, dn=(((1,), (0,)), ((), ()))):
    return lax.dot_general(a, b, dn, preferred_element_type=F32)


_NN = (((1,), (0,)), ((), ()))
_NT = (((1,), (1,)), ((), ()))
_TN = (((0,), (0,)), ((), ()))


def _pdot(parts, sel, dn=_NN):
    return functools.reduce(lambda a, b: a + b, [_dot(part, sel, dn) for part in parts])


def _pdotr(sel, parts, dn=_NN):
    return functools.reduce(lambda a, b: a + b, [_dot(sel, part, dn) for part in parts])


def _dot3(x, sel, dn=_NN):
    return _pdot(_split(x, 3), sel, dn)


def _dot3r(sel, x, dn=_NN):
    return _pdotr(sel, _split(x, 3), dn)


def _dot2(x, sel, dn=_NN):
    return _pdot(_split(x, 2), sel, dn)


def _iota2(shape, axis):
    return lax.broadcasted_iota(jnp.int32, shape, axis)


def _ssd_masks():
    q = SSD_CHUNK
    r, c = _iota2((q, q), 0), _iota2((q, q), 1)
    tril = (c <= r)
    return r, c, tril


def _head_of_lane(wg):
    return (_iota2((SSD_CHUNK, wg), 0) == _iota2((SSD_CHUNK, wg), 1) // SSD_HEAD_DIM).astype(BF16)


def _head_of_row(wg, dtype):
    return (_iota2((wg, SSD_CHUNK), 1) == _iota2((wg, SSD_CHUNK), 0) // SSD_HEAD_DIM).astype(dtype)


class _SsdChunk:
    def __init__(self, dtraw, dtb, ap, b_t, c_t, wg):
        q = SSD_CHUNK
        hpg = wg // SSD_HEAD_DIM
        r, c, self.tril = _ssd_masks()
        self.upper = (c > r)
        self.dt = _softplus(dtraw + dtb)
        self.la = self.dt * ap
        self.cum = _dot3r(self.tril.astype(BF16), self.la)
        rem = _dot3r(self.upper.astype(BF16), self.la)
        total = _dot3(self.la, jnp.ones((q, q), BF16), _TN)
        self.scores = _dot(c_t, b_t, _NT)
        cum2 = _split(self.cum, 2)
        stack = jnp.concatenate(_split(self.dt, 2) + cum2 + _split(rem, 2), axis=0)
        lanes = _dot(stack, _head_of_lane(wg))
        self.dt_l = lanes[0:q] + lanes[q:2 * q]
        self.cum_l = lanes[2 * q:3 * q] + lanes[3 * q:4 * q]
        self.rem_l = lanes[4 * q:5 * q] + lanes[5 * q:6 * q]
        self.grow = jnp.exp(_pdotr(_head_of_row(wg, BF16), _split(total, 2)))
        every_lane = (_iota2((q, hpg * q), 0) == _iota2((q, hpg * q), 1) // q).astype(BF16)
        cq = _dot(jnp.concatenate(cum2, axis=0), every_lane)
        self.cq = cq[0:q] + cq[q:2 * q]
        every_row = (_iota2((hpg * q, q), 1) == _iota2((hpg * q, q), 0) // q).astype(BF16)
        self.ck = _pdotr(every_row, cum2, _NT)

    def decay(self, h):
        q = SSD_CHUNK
        seg = self.cq[:, h * q:(h + 1) * q] - self.ck[h * q:(h + 1) * q, :]
        return jnp.exp(jnp.where(self.tril, seg, -jnp.inf))


def _by_head(x_b, lane):
    first = (lane // SSD_HEAD_DIM) == 0
    return jnp.concatenate([jnp.where(first, x_b, 0), jnp.where(first, 0, x_b)], axis=0)


def _ssd_tiles(xc, n_heads):
    hpg = n_heads // SSD_GROUPS
    wg = hpg * SSD_HEAD_DIM
    xw = n_heads * SSD_HEAD_DIM
    return hpg, wg, xw // wg, xw // SSD_STATE


def _ssd_fwd(xc, dtraw, dtb, ap, n_heads, name, rider=None):
    length = xc.shape[0]
    q = SSD_CHUNK
    nc = length // q
    hpg, wg, _, b_blk0 = _ssd_tiles(xc, n_heads)
    c_blk0 = b_blk0 + SSD_GROUPS
    npair = hpg // 2

    def body(x_ref, b_ref, c_ref, dt_ref, dtb_ref, ap_ref, y_ref, st_ref, s_ref):
        @pl.when(pl.program_id(1) == 0)
        def _():
            s_ref[...] = jnp.zeros_like(s_ref)

        st_ref[0, 0] = s_ref[...]
        b_t = b_ref[...].astype(BF16)
        c_t = c_ref[...].astype(BF16)
        ck = _SsdChunk(dt_ref[...], dtb_ref[...], ap_ref[...], b_t, c_t, wg)
        lane = _iota2((q, q), 1)
        xd = x_ref[...] * ck.dt_l
        xd_b = xd.astype(BF16)
        s_prev = s_ref[...]
        y_state = _dot(c_t, s_prev.astype(BF16), _NT) * jnp.exp(ck.cum_l)
        for i in range(npair):
            sl = slice(i * LANES, (i + 1) * LANES)
            wm = jnp.concatenate([(ck.scores * ck.decay(2 * i + hh)).astype(BF16) for hh in range(2)], axis=1)
            y_ref[:, sl] = y_state[:, sl] + _dot(wm, _by_head(xd_b[:, sl], lane))
        xw_b = (xd * jnp.exp(ck.rem_l)).astype(BF16)
        s_ref[...] = ck.grow * s_prev + _dot(xw_b, b_t, _TN)

    outs, carried = _hosted_call(
        body, name=name, grid=(SSD_GROUPS, nc),
        in_specs=[pl.BlockSpec((q, wg), lambda g, c: (c, g)),
                  pl.BlockSpec((q, SSD_STATE), lambda g, c: (c, b_blk0 + g)),
                  pl.BlockSpec((q, SSD_STATE), lambda g, c: (c, c_blk0 + g)),
                  pl.BlockSpec((q, LANES), lambda g, c: (c, g)),
                  pl.BlockSpec((1, LANES), lambda g, c: (0, g)),
                  pl.BlockSpec((1, LANES), lambda g, c: (0, g))],
        out_specs=[pl.BlockSpec((q, wg), lambda g, c: (c, g)),
                   pl.BlockSpec((1, 1, wg, SSD_STATE), lambda g, c: (c, g, 0, 0))],
        out_shape=[jax.ShapeDtypeStruct((length, SSD_GROUPS * wg), F32),
                   jax.ShapeDtypeStruct((nc, SSD_GROUPS, wg, SSD_STATE), F32)],
        scratch_shapes=[pltpu.VMEM((wg, SSD_STATE), F32)],
        sem=("parallel", "arbitrary"), args=(xc, xc, xc, dtraw, dtb, ap), rider=rider)
    return outs[0], outs[1], carried


def _ssd_bwd(dy, dxa, xc, dtraw, states, dtb, ap, n_heads, name, rider=None):
    length = xc.shape[0]
    q = SSD_CHUNK
    nc = length // q
    hpg, wg, _, b_blk0 = _ssd_tiles(xc, n_heads)
    c_blk0 = b_blk0 + SSD_GROUPS
    npair = hpg // 2

    def body(dy_ref, dxa_ref, x_ref, b_ref, c_ref, dt_ref, st_ref, dtb_ref, ap_ref,
             dx_ref, db_ref, dc_ref, ddt_ref, ddtb_ref, dap_ref, ds_ref, el_ref, er_ref):
        @pl.when(pl.program_id(1) == 0)
        def _():
            ds_ref[...] = jnp.zeros_like(ds_ref)
            ddtb_ref[...] = jnp.zeros_like(ddtb_ref)
            dap_ref[...] = jnp.zeros_like(dap_ref)

        b_t = b_ref[...].astype(BF16)
        c_t = c_ref[...].astype(BF16)
        dtraw_t = dt_ref[...]
        ck = _SsdChunk(dtraw_t, dtb_ref[...], ap_ref[...], b_t, c_t, wg)
        lane = _iota2((q, q), 1)
        to_head = _head_of_lane(wg)

        def per_head(v):
            return _pdot(_split(v, 2), to_head, _NT)

        x_all, dy_all = x_ref[...], dy_ref[...]
        dy_b = dy_all.astype(BF16)
        xd = x_all * ck.dt_l
        xd_b = xd.astype(BF16)
        s_prev = st_ref[0, 0]
        sp_b = s_prev.astype(BF16)
        ds1 = ds_ref[...]
        ds1_b = ds1.astype(BF16)
        ecum, wrem = jnp.exp(ck.cum_l), jnp.exp(ck.rem_l)
        dscores = jnp.zeros((q, q), F32)
        for i in range(npair):
            sl = slice(i * LANES, (i + 1) * LANES)
            dym = _by_head(dy_b[:, sl], lane)
            dwm2 = _dot(dym, xd_b[:, sl], _NT)
            wms = []
            for hh in range(2):
                h = 2 * i + hh
                decay = ck.decay(h)
                wm = ck.scores * decay
                dwm = dwm2[hh * q:(hh + 1) * q]
                dscores = dscores + dwm * decay
                e = (dwm * wm).astype(BF16)
                el_ref[:, h * q:(h + 1) * q] = e
                er_ref[h * q:(h + 1) * q, :] = e
                wms.append(wm.astype(BF16))
            dx_ref[:, sl] = _dot(jnp.concatenate(wms, axis=0), dym, _TN)
        put = (_iota2((hpg * q, q), 1) == _iota2((hpg * q, q), 0) // q).astype(BF16)
        dcum = _dot(el_ref[...], put) - _dot(er_ref[...], put, _TN)
        t_mat = _dot(c_t, sp_b, _NT)
        d_t = (dy_all * ecum).astype(BF16)
        dc_acc = _dot(d_t, sp_b)
        ds_prev = _dot(d_t, c_t, _TN)
        dcum = dcum + per_head(dy_all * t_mat * ecum)
        ds_prev = ds_prev + ck.grow * ds1
        zs = jnp.sum(ds1 * s_prev * ck.grow, axis=1, keepdims=True)
        dtot = _pdotr(jnp.ones((q, wg), BF16), _split(zs * _head_of_row(wg, F32), 2))
        xw = xd * wrem
        dxw = _dot(b_t, ds1_b, _NT)
        db_acc = _dot(xw.astype(BF16), ds1_b)
        dxd = dx_ref[...] + dxw * wrem
        drem = per_head(dxw * xw)
        dx_ref[...] = dxd * ck.dt_l + dxa_ref[...]
        ddt = per_head(dxd * x_all)
        ds_ref[...] = ds_prev
        ds_b = dscores.astype(BF16)
        dc_ref[...] = dc_acc + _dot(ds_b, b_t)
        db_ref[...] = db_acc + _dot(ds_b, c_t, _TN)
        dla = (_dot3r(ck.tril.astype(BF16), dcum, _TN) + _dot3r(ck.upper.astype(BF16), drem, _TN) + dtot)
        dt = ck.dt
        ddt = ddt + dla * ap_ref[...]
        dap_ref[...] += jnp.sum(dla * dt, axis=0, keepdims=True)
        ddtraw = ddt * _sigmoid(dtraw_t + dtb_ref[...])
        ddt_ref[...] = ddtraw.astype(ddt_ref.dtype)
        ddtb_ref[...] += jnp.sum(ddtraw, axis=0, keepdims=True)

    rev = lambda g, c: (nc - 1 - c, g)
    outs, carried = _hosted_call(
        body, name=name, grid=(SSD_GROUPS, nc),
        in_specs=[pl.BlockSpec((q, wg), rev),
                  pl.BlockSpec((q, wg), rev),
                  pl.BlockSpec((q, wg), rev),
                  pl.BlockSpec((q, SSD_STATE), lambda g, c: (nc - 1 - c, b_blk0 + g)),
                  pl.BlockSpec((q, SSD_STATE), lambda g, c: (nc - 1 - c, c_blk0 + g)),
                  pl.BlockSpec((q, LANES), rev),
                  pl.BlockSpec((1, 1, wg, SSD_STATE), lambda g, c: (nc - 1 - c, g, 0, 0)),
                  pl.BlockSpec((1, LANES), lambda g, c: (0, g)),
                  pl.BlockSpec((1, LANES), lambda g, c: (0, g))],
        out_specs=[pl.BlockSpec((q, wg), rev),
                   pl.BlockSpec((q, SSD_STATE), rev),
                   pl.BlockSpec((q, SSD_STATE), rev),
                   pl.BlockSpec((q, LANES), rev),
                   pl.BlockSpec((1, LANES), lambda g, c: (0, g)),
                   pl.BlockSpec((1, LANES), lambda g, c: (0, g))],
        out_shape=[jax.ShapeDtypeStruct((length, SSD_GROUPS * wg), F32),
                   jax.ShapeDtypeStruct((length, SSD_GROUPS * SSD_STATE), F32),
                   jax.ShapeDtypeStruct((length, SSD_GROUPS * SSD_STATE), F32),
                   jax.ShapeDtypeStruct((length, SSD_GROUPS * LANES), BF16),
                   jax.ShapeDtypeStruct((1, SSD_GROUPS * LANES), F32),
                   jax.ShapeDtypeStruct((1, SSD_GROUPS * LANES), F32)],
        scratch_shapes=[pltpu.VMEM((wg, SSD_STATE), F32), pltpu.VMEM((q, hpg * q), BF16),
                        pltpu.VMEM((hpg * q, q), BF16)],
        sem=("parallel", "arbitrary"), args=(dy, dxa, xc, xc, xc, dtraw, states, dtb, ap), rider=rider)
    return tuple(outs) + (carried,)


def _fox_cumsum(fraw, bf, name):
    length = fraw.shape[0]
    q = 128

    def body(f_ref, b_ref, o_ref, carry_ref):
        @pl.when(pl.program_id(0) == 0)
        def _():
            carry_ref[...] = jnp.zeros_like(carry_ref)

        lf = -_softplus(-(f_ref[...] + b_ref[...]))
        r, c = _iota2((q, q), 0), _iota2((q, q), 1)
        o_ref[...] = _dot3r((c <= r).astype(BF16), lf) + carry_ref[...]
        carry_ref[...] += jnp.sum(lf, axis=0, keepdims=True)

    return pl.pallas_call(
        body, name=name, grid=(length // q,),
        in_specs=[pl.BlockSpec((q, LANES), lambda i: (i, 0)), pl.BlockSpec((1, LANES), lambda i: (0, 0))],
        out_specs=pl.BlockSpec((q, LANES), lambda i: (i, 0)),
        out_shape=jax.ShapeDtypeStruct((length, LANES), F32),
        scratch_shapes=[pltpu.VMEM((1, LANES), F32)],
        compiler_params=_params(("arbitrary",)),
    )(fraw, bf)


def _fox_cumsum_bwd(dc, fraw, bf, name):
    length = fraw.shape[0]
    q = 128
    nt = length // q

    def body(d_ref, f_ref, b_ref, o_ref, db_ref, carry_ref):
        @pl.when(pl.program_id(0) == 0)
        def _():
            carry_ref[...] = jnp.zeros_like(carry_ref)
            db_ref[...] = jnp.zeros_like(db_ref)

        d = d_ref[...]
        r, c = _iota2((q, q), 0), _iota2((q, q), 1)
        dlf = _dot3r((c >= r).astype(BF16), d) + carry_ref[...]
        carry_ref[...] += jnp.sum(d, axis=0, keepdims=True)
        df = dlf * _sigmoid(-(f_ref[...] + b_ref[...]))
        o_ref[...] = df.astype(o_ref.dtype)
        db_ref[...] += jnp.sum(df, axis=0, keepdims=True)

    rev = lambda i: (nt - 1 - i, 0)
    return pl.pallas_call(
        body, name=name, grid=(nt,),
        in_specs=[pl.BlockSpec((q, LANES), rev), pl.BlockSpec((q, LANES), rev),
                  pl.BlockSpec((1, LANES), lambda i: (0, 0))],
        out_specs=[pl.BlockSpec((q, LANES), rev), pl.BlockSpec((1, LANES), lambda i: (0, 0))],
        out_shape=[jax.ShapeDtypeStruct((length, LANES), BF16), jax.ShapeDtypeStruct((1, LANES), F32)],
        scratch_shapes=[pltpu.VMEM((1, LANES), F32)],
        compiler_params=_params(("arbitrary",)),
    )(dc, fraw, bf)


def _fox_scores(q_ref, k_ref, cq_ref, ck_ref, diagonal):
    scale = 1.0 / math.sqrt(FOX_HEAD_DIM)
    s = _dot(q_ref[...].astype(BF16), k_ref[...].astype(BF16), _NT) * scale + (cq_ref[0] - ck_ref[0])
    if diagonal:
        s = jnp.where(_iota2(s.shape, 1) <= _iota2(s.shape, 0), s, -jnp.inf)
    return s


def _fox_tiles(i, j, step):
    @pl.when(j < i)
    def _():
        step(False)

    @pl.when(j == i)
    def _():
        step(True)


def _fox_fwd(qkvg, c_col, c_row, n_heads, name):
    length = qkvg.shape[0]
    hd = FOX_HEAD_DIM
    tq = _pick(length, (FOX_TILE, 512, 256, 128))
    nq = length // tq

    def body(q_ref, k_ref, v_ref, cq_ref, ck_ref, o_ref, lse_ref, m_ref, l_ref, acc_ref):
        i, j = pl.program_id(1), pl.program_id(2)

        @pl.when(j == 0)
        def _():
            m_ref[...] = jnp.full_like(m_ref, -jnp.inf)
            l_ref[...] = jnp.zeros_like(l_ref)
            acc_ref[...] = jnp.zeros_like(acc_ref)

        def step(diagonal):
            s = _fox_scores(q_ref, k_ref, cq_ref, ck_ref, diagonal)
            m_new = jnp.maximum(m_ref[...], jnp.max(s, axis=1, keepdims=True))
            alpha = jnp.exp(m_ref[...] - m_new)
            p = jnp.exp(s - m_new)
            l_ref[...] = alpha * l_ref[...] + jnp.sum(p, axis=1, keepdims=True)
            acc_ref[...] = alpha * acc_ref[...] + _dot(p.astype(BF16), v_ref[...].astype(BF16))
            m_ref[...] = m_new

        _fox_tiles(i, j, step)

        @pl.when(j == nq - 1)
        def _():
            o_ref[...] = acc_ref[...] / l_ref[...]
            lse_ref[0] = m_ref[...] + jnp.log(l_ref[...])

    kmap = lambda off: (lambda h, i, j: (jnp.minimum(j, i), off * n_heads + h))
    return pl.pallas_call(
        body, name=name, grid=(n_heads, nq, nq),
        in_specs=[pl.BlockSpec((tq, hd), lambda h, i, j: (i, h)),
                  pl.BlockSpec((tq, hd), kmap(1)),
                  pl.BlockSpec((tq, hd), kmap(2)),
                  pl.BlockSpec((1, tq, 1), lambda h, i, j: (h, i, 0)),
                  pl.BlockSpec((1, 1, tq), lambda h, i, j: (h, 0, jnp.minimum(j, i)))],
        out_specs=[pl.BlockSpec((tq, hd), lambda h, i, j: (i, h)),
                   pl.BlockSpec((1, tq, 1), lambda h, i, j: (h, i, 0))],
        out_shape=[jax.ShapeDtypeStruct((length, n_heads * hd), F32),
                   jax.ShapeDtypeStruct((n_heads, length, 1), F32)],
        scratch_shapes=[pltpu.VMEM((tq, 1), F32), pltpu.VMEM((tq, 1), F32), pltpu.VMEM((tq, hd), F32)],
        compiler_params=_params(("parallel", "parallel", "arbitrary")),
    )(qkvg, qkvg, qkvg, c_col, c_row)


def _fox_bwd_q(qkvg, datt, lse, c_col, c_row, n_heads, name, rider=None):
    length = qkvg.shape[0]
    hd = FOX_HEAD_DIM
    tq = _pick(length, (FOX_TILE, 512, 256, 128))
    nq = length // tq
    scale = 1.0 / math.sqrt(hd)

    def body(q_ref, k_ref, v_ref, do_ref, lse_ref, cq_ref, ck_ref, dq_ref, dsum_ref, a1_ref, a2_ref, d_ref):
        i, j = pl.program_id(1), pl.program_id(2)

        @pl.when(j == 0)
        def _():
            a1_ref[...] = jnp.zeros_like(a1_ref)
            a2_ref[...] = jnp.zeros_like(a2_ref)
            d_ref[...] = jnp.zeros_like(d_ref)

        def step(diagonal):
            s = _fox_scores(q_ref, k_ref, cq_ref, ck_ref, diagonal)
            p = jnp.exp(s - lse_ref[0])
            pdp = p * _dot(do_ref[...].astype(BF16), v_ref[...].astype(BF16), _NT)
            d_ref[...] += jnp.sum(pdp, axis=1, keepdims=True)
            k_b = k_ref[...].astype(BF16)
            a1_ref[...] += _dot(pdp.astype(BF16), k_b)
            a2_ref[...] += _dot(p.astype(BF16), k_b)

        _fox_tiles(i, j, step)

        @pl.when(j == nq - 1)
        def _():
            dq_ref[...] = ((a1_ref[...] - d_ref[...] * a2_ref[...]) * scale).astype(dq_ref.dtype)
            dsum_ref[0] = d_ref[...]

    kmap = lambda off: (lambda h, i, j: (jnp.minimum(j, i), off * n_heads + h))
    qmap = lambda h, i, j: (i, h)
    col = pl.BlockSpec((1, tq, 1), lambda h, i, j: (h, i, 0))
    outs, carried = _hosted_call(
        body, name=name, grid=(n_heads, nq, nq),
        in_specs=[pl.BlockSpec((tq, hd), qmap), pl.BlockSpec((tq, hd), kmap(1)), pl.BlockSpec((tq, hd), kmap(2)),
                  pl.BlockSpec((tq, hd), qmap), col, col,
                  pl.BlockSpec((1, 1, tq), lambda h, i, j: (h, 0, jnp.minimum(j, i)))],
        out_specs=[pl.BlockSpec((tq, hd), qmap), col],
        out_shape=[jax.ShapeDtypeStruct((length, n_heads * hd), BF16),
                   jax.ShapeDtypeStruct((n_heads, length, 1), F32)],
        scratch_shapes=[pltpu.VMEM((tq, hd), F32), pltpu.VMEM((tq, hd), F32), pltpu.VMEM((tq, 1), F32)],
        sem=("parallel", "parallel", "arbitrary"), args=(qkvg, qkvg, qkvg, datt, lse, c_col, c_row), rider=rider)
    return outs[0], outs[1], carried


def _fox_bwd_kv(qkvg, datt, lse, dsum, c_col, c_row, n_heads, name):
    length = qkvg.shape[0]
    hd = FOX_HEAD_DIM
    tq = _pick(length, (FOX_TILE, 512, 256, 128))
    nq = length // tq
    scale = 1.0 / math.sqrt(hd)

    def body(q_ref, k_ref, v_ref, do_ref, lse_ref, dsum_ref, cq_ref, ck_ref, dk_ref, dv_ref, dck_ref,
             dk_acc, dv_acc, dc_acc):
        j, i = pl.program_id(1), pl.program_id(2)

        @pl.when(i == 0)
        def _():
            dk_acc[...] = jnp.zeros_like(dk_acc)
            dv_acc[...] = jnp.zeros_like(dv_acc)
            dc_acc[...] = jnp.zeros_like(dc_acc)

        def step(diagonal):
            s = _fox_scores(q_ref, k_ref, cq_ref, ck_ref, diagonal)
            p = jnp.exp(s - lse_ref[0])
            do_b = do_ref[...].astype(BF16)
            dv_acc[...] += _dot(p.astype(BF16), do_b, _TN)
            dp = _dot(do_b, v_ref[...].astype(BF16), _NT)
            ds = p * (dp - dsum_ref[0])
            dk_acc[...] += _dot(ds.astype(BF16), q_ref[...].astype(BF16), _TN)
            dc_acc[...] -= jnp.sum(ds, axis=0, keepdims=True)

        _fox_tiles(i, j, step)

        @pl.when(i == nq - 1)
        def _():
            dk_ref[...] = (dk_acc[...] * scale).astype(dk_ref.dtype)
            dv_ref[...] = dv_acc[...].astype(dv_ref.dtype)
            dck_ref[0] = dc_acc[...]

    qmap = lambda h, j, i: (jnp.maximum(i, j), h)
    kmap = lambda off: (lambda h, j, i: (j, off * n_heads + h))
    col = pl.BlockSpec((1, tq, 1), lambda h, j, i: (h, jnp.maximum(i, j), 0))
    return pl.pallas_call(
        body, name=name, grid=(n_heads, nq, nq),
        in_specs=[pl.BlockSpec((tq, hd), qmap), pl.BlockSpec((tq, hd), kmap(1)), pl.BlockSpec((tq, hd), kmap(2)),
                  pl.BlockSpec((tq, hd), qmap), col, col, col,
                  pl.BlockSpec((1, 1, tq), lambda h, j, i: (h, 0, j))],
        out_specs=[pl.BlockSpec((tq, hd), lambda h, j, i: (j, h)), pl.BlockSpec((tq, hd), lambda h, j, i: (j, h)),
                   pl.BlockSpec((1, 1, tq), lambda h, j, i: (h, 0, j))],
        out_shape=[jax.ShapeDtypeStruct((length, n_heads * hd), BF16)] * 2
        + [jax.ShapeDtypeStruct((n_heads, 1, length), F32)],
        scratch_shapes=[pltpu.VMEM((tq, hd), F32), pltpu.VMEM((tq, hd), F32), pltpu.VMEM((1, tq), F32)],
        compiler_params=_params(("parallel", "parallel", "arbitrary")),
    )(qkvg, qkvg, qkvg, datt, lse, dsum, c_col, c_row)


def _row(v):
    return v.reshape(1, -1).astype(F32)


def _pad_heads(v, per_group):
    lead = v.shape[:-1]
    v = v.reshape(lead + (SSD_GROUPS, per_group))
    v = jnp.pad(v, [(0, 0)] * len(lead) + [(0, 0), (0, LANES - per_group)])
    return v.reshape(lead + (SSD_GROUPS * LANES,))


def _unpad_heads(v, per_group):
    lead = v.shape[:-1]
    return v.reshape(lead + (SSD_GROUPS, LANES))[..., :per_group].reshape(lead + (SSD_GROUPS * per_group,))


class _NoOverlap:
    def gather_rider(self, host):
        return None

    def gathered(self, host, carried):
        return {}

    def reduce_rider(self, host, grads):
        return None

    def reduced(self, host, carried):
        pass


def _pad_head_rows(w, per_group):
    w = w.reshape(SSD_GROUPS, per_group, w.shape[1])
    return jnp.pad(w, ((0, 0), (0, LANES - per_group), (0, 0))).reshape(SSD_GROUPS * LANES, w.shape[2])


def _unpad_head_rows(w, per_group):
    return w.reshape(SSD_GROUPS, LANES, w.shape[1])[:, :per_group].reshape(SSD_GROUPS * per_group, w.shape[1])


def _local_step(x, tgt, wb, sm, plan=None):
    plan = plan or _NoOverlap()
    wb = dict(wb)
    length, d = x.shape
    mix = 2 * d
    s5w = mix // 4
    ssdw = mix - s5w
    xbcw = ssdw + 2 * SSD_GROUPS * SSD_STATE
    n_ssd = ssdw // SSD_HEAD_DIM
    hpg = n_ssd // SSD_GROUPS
    fw = d
    o1, o2, o3 = 2 * s5w, 2 * s5w + ssdw, 2 * s5w + ssdw + xbcw
    w0t = wb["w0T"]
    w0_dt = _pad_head_rows(w0t[o3:], hpg)
    n_fox = fw // FOX_HEAD_DIM
    s5g = s5w // S5_GROUP
    s5s = s5g * S5_STATE
    grads = {}

    s5_in = (sm["l0_s5_lambda_re"], sm["l0_s5_lambda_im"], sm["l0_s5_log_step"], sm["l0_s5_b_re"],
             sm["l0_s5_b_im"], sm["l0_s5_c_re"], sm["l0_s5_c_im"])
    (a_re, a_im, bd_re, bd_im, cd_re, cd_imn), s5_vjp = jax.vjp(_s5_prepare, *s5_in)
    nj = bd_re.shape[0]
    bd_re_b, bd_im_b, cd_re_b, cd_imn_b = (t.astype(BF16) for t in (bd_re, bd_im, cd_re, cd_imn))
    tr = lambda t: jnp.swapaxes(t, 1, 2)
    dvec = _row(sm["l0_s5_d"])
    bglu = _row(sm["l0_s5_b_glu"])
    conv_w = sm["l0_ssd_conv_w"]
    conv_b = _row(sm["l0_ssd_conv_b"])

    def ssd_prepare(dt_bias, a_log, dd):
        return (_pad_heads(_row(dt_bias), hpg), _pad_heads(_row(-jnp.exp(a_log)), hpg),
                jnp.repeat(_row(dd), SSD_HEAD_DIM, axis=1))

    (dtb, ap, dpar), ssd_vjp = jax.vjp(ssd_prepare, sm["l0_ssd_dt_bias"], sm["l0_ssd_a_log"], sm["l0_ssd_d"])
    ssd_nw = _row(sm["l0_ssd_norm_w"])
    nw0, nw1, fnw = _row(sm["l0_norm_w"]), _row(sm["l1_norm_w"]), _row(sm["final_norm_w"])
    bf = jnp.pad(_row(sm["l1_fox_b_f"]), ((0, 0), (0, LANES - n_fox)))

    h0 = _norm_fwd(x, nw0, "l0_norm")
    def gathering(host, *args, **kw):
        rider = plan.gather_rider(host)
        out = _matmul(*args, name=host, rider=rider, **kw)
        if rider is None:
            return out
        wb.update(plan.gathered(host, out[1]))
        return out[0]

    def reducing(host, *args, **kw):
        rider = plan.reduce_rider(host, grads)
        out = _matmul(*args, name=host, rider=rider, **kw)
        if rider is None:
            return out
        plan.reduced(host, out[1])
        return out[0]

    ug = gathering("l0_in_ug", h0, w0t, mode="nt", dims=(length, o1, d))
    z = gathering("l0_in_z", h0, w0t, mode="nt", dims=(length, ssdw, d), b_off=(o1, 0))
    xbc = gathering("l0_in_xbc", h0, w0t, mode="nt", dims=(length, xbcw, d), b_off=(o2, 0))
    dtraw = _matmul(h0, w0_dt, mode="nt", name="l0_in_dt")
    u_win, gate_win = (ug, s5w, 0), (ug, s5w, 1)

    shape3 = (length, s5s // LANES, LANES)
    bu_re = _bdmm([(ug, 0)], [bd_re_b], name="s5_bu_re").reshape(shape3)
    bu_im = _bdmm([(ug, 0)], [bd_im_b], name="s5_bu_im").reshape(shape3)
    s_re3, s_im3 = _s5_scan_fwd(bu_re, bu_im, a_re, a_im, "s5_scan")
    s_re, s_im = s_re3.reshape(length, s5s), s_im3.reshape(length, s5s)
    yc = _bdmm([s_re, s_im], [cd_re_b, cd_imn_b], name="s5_y")
    gl = _rowwise(_f_s5_gelu, [yc, u_win], [dvec], [(s5w, BF16)], tl=256, name="s5_gelu")[0]
    t_glu = _matmul(gl, wb["w_glu"], name="s5_glu")
    s5o = _rowwise(_f_s5_out, [yc, u_win, t_glu, gate_win], [dvec, bglu], [(s5w, BF16)], tl=256,
                   name="s5_out")[0]

    xc = _conv_fwd(xbc, conv_w, conv_b, "ssd_conv")
    y_ssd, states, carried = _ssd_fwd(xc, dtraw, dtb, ap, n_ssd, "ssd_scan", rider=plan.gather_rider("ssd_scan"))
    wb.update(plan.gathered("ssd_scan", carried))
    wg = ssdw // SSD_GROUPS
    ssdo = _rowwise(_f_ssd_out, [y_ssd, (xc, wg, 0), z], [dpar, ssd_nw], [(wg, BF16)], tl=256,
                    ncol=SSD_GROUPS, name="ssd_out")[0]
    x1 = _matmul(s5o, wb["w0_out"], dims=(length, d, s5w), addend=x, name="l0_out_s5")
    x1 = _matmul(ssdo, wb["w0_out"], dims=(length, d, ssdw), b_off=(s5w, 0), addend=x1, name="l0_out_ssd")

    h1 = _norm_fwd(x1, nw1, "l1_norm")
    w1t = wb["w1T"]
    w1_f = jnp.pad(w1t[4 * fw:], ((0, LANES - n_fox), (0, 0)))
    qkvg = _matmul(h1, w1t, mode="nt", dims=(length, 4 * fw, d), name="l1_in")
    fraw = _matmul(h1, w1_f, mode="nt", name="l1_in_f")
    cc = _fox_cumsum(fraw, bf, "fox_cumsum")
    c_t = cc[:, :n_fox].T
    c_col, c_row = c_t[:, :, None], c_t[:, None, :]
    att, lse = _fox_fwd(qkvg, c_col, c_row, n_fox, "fox_fwd")
    gate1_win = (qkvg, fw, 3)
    fox_o = _rowwise(_f_fox_out, [att, gate1_win], [], [(fw, BF16)], tl=256, name="fox_out")[0]
    x2 = _matmul(fox_o, wb["w1_out"], addend=x1, name="l1_out")

    def loss_fn(xt, tt, wt):
        def f(xx, ww):
            err = _f_norm(xx, ww) - tt
            return (0.5 / d) * err * err
        lanes, vjp = jax.vjp(f, xt, wt)
        dx, dw = vjp(jnp.ones_like(lanes))
        return dx, dx, jnp.sum(lanes, axis=0, keepdims=True), dw

    dx2, dx2b, loss_lanes, g_fnw = _rowwise(loss_fn, [x2, tgt], [fnw], [(d, F32), (d, BF16)], [d, d],
                                            tl=128, name="loss_head")
    grads["final_norm_w"] = g_fnw

    grads["l1_w_out"] = _matmul(fox_o, dx2b, mode="tn", name="l1_out_dw")
    do1 = _matmul(dx2b, wb["w1_out"], mode="nt", name="l1_out_dx")

    def fox_out_bwd(at, gt, dt_):
        _, vjp = jax.vjp(_f_fox_out, at, gt)
        return vjp(dt_)

    datt, dgate1 = _rowwise(fox_out_bwd, [att, gate1_win, do1], [], [(fw, F32), (fw, BF16)], tl=256,
                            name="fox_out_bwd")
    dq, dsum, carried = _fox_bwd_q(qkvg, datt, lse, c_col, c_row, n_fox, "fox_bwd_q",
                                   rider=plan.reduce_rider("fox_bwd_q", grads))
    plan.reduced("fox_bwd_q", carried)
    dk, dv, dck = _fox_bwd_kv(qkvg, datt, lse, dsum, c_col, c_row, n_fox, "fox_bwd_kv")
    dcc = jnp.pad(dck[:, 0, :].T, ((0, 0), (0, LANES - n_fox)))
    dfraw, g_bf = _fox_cumsum_bwd(dcc, fraw, bf, "fox_cumsum_bwd")
    grads["l1_fox_b_f"] = g_bf[:, :n_fox]
    dsegs = [dq, dk, dv, dgate1]
    g1, n1 = None, w1t.shape[0]
    for i, s in enumerate(dsegs):
        g1 = _matmul(s, h1, mode="tn", rows_of=(n1, i * fw, g1), name=f"l1_in_dw{i}")
    g1_f = _matmul(dfraw, h1, mode="tn", name="l1_in_dwf")[:n_fox]
    grads["l1_w_inT"] = lax.dynamic_update_slice(g1, g1_f, (4 * fw, 0))
    dh1 = _matmul(dfraw, w1_f, mode="nn", name="l1_in_dxf")
    for i, s in enumerate(dsegs):
        dh1 = _matmul(s, w1t, mode="nn", dims=(length, d, fw), b_off=(i * fw, 0), addend=dh1,
                      name=f"l1_in_dx{i}")
    dx1, dx1b, grads["l1_norm_w"] = _norm_bwd(x1, dh1, dx2, nw1, "l1_norm_bwd")

    g_out = _matmul(s5o, dx1b, mode="tn", rows_of=(mix, 0, None), name="l0_out_dw_s5")
    grads["l0_w_out"] = _matmul(ssdo, dx1b, mode="tn", rows_of=(mix, s5w, g_out), name="l0_out_dw_ssd")
    ds5o = _matmul(dx1b, wb["w0_out"], mode="nt", dims=(length, s5w, d), name="l0_out_dx_s5")
    dssdo = reducing("l0_out_dx_ssd", dx1b, wb["w0_out"], mode="nt", dims=(length, ssdw, d), b_off=(s5w, 0))

    def ssd_out_bwd(yt, xt, zt, dt_, dp, nw):
        _, vjp = jax.vjp(_f_ssd_out, yt, xt, zt, dp, nw)
        return vjp(dt_)

    dy_ssd, dxa, dz, g_dpar, g_ssd_nw = _rowwise(
        ssd_out_bwd, [y_ssd, (xc, wg, 0), z, dssdo], [dpar, ssd_nw],
        [(wg, F32), (wg, F32), (wg, BF16)], [wg, wg], tl=128, ncol=SSD_GROUPS, name="ssd_out_bwd")
    grads["l0_ssd_norm_w"] = g_ssd_nw
    dxs, db_ssd, dc_ssd, ddtraw, g_dtb, g_ap, carried = _ssd_bwd(
        dy_ssd, dxa, xc, dtraw, states, dtb, ap, n_ssd, "ssd_scan_bwd",
        rider=plan.reduce_rider("ssd_scan_bwd", grads))
    plan.reduced("ssd_scan_bwd", carried)
    g_dt_bias, g_a_log, g_ssd_d = ssd_vjp((g_dtb, g_ap, g_dpar))
    grads["l0_ssd_dt_bias"], grads["l0_ssd_a_log"], grads["l0_ssd_d"] = g_dt_bias, g_a_log, g_ssd_d
    dxc = jnp.concatenate([dxs, db_ssd, dc_ssd], axis=1)
    dpre, grads["l0_ssd_conv_w"], grads["l0_ssd_conv_b"] = _conv_bwd_pre(dxc, xbc, conv_w, conv_b, "ssd_conv_bwd_pre")
    dxbc = _conv_bwd_in(dpre, conv_w, "ssd_conv_bwd_in")

    def s5_out_bwd(yt, ut, tt, gt, dt_, dv_, bg):
        _, vjp = jax.vjp(_f_s5_out, yt, ut, tt, gt, dv_, bg)
        return vjp(dt_)

    dyc_a, du_a, dt_glu, dgate, g_dvec_a, g_bglu = _rowwise(
        s5_out_bwd, [yc, u_win, t_glu, gate_win, ds5o], [dvec, bglu],
        [(s5w, F32), (s5w, F32), (s5w, BF16), (s5w, BF16)], [s5w, s5w], tl=128, name="s5_out_bwd")
    grads["l0_s5_b_glu"] = g_bglu
    grads["l0_s5_w_glu"] = _matmul(gl, dt_glu, mode="tn", name="s5_glu_dw")
    dgl = _matmul(dt_glu, wb["w_glu"], mode="nt", name="s5_glu_dx")

    def s5_gelu_bwd(yt, ut, dg, dya, dua, dv_):
        _, vjp = jax.vjp(_f_s5_gelu, yt, ut, dv_)
        dy_, du_, ddv = vjp(dg)
        return dy_ + dya, du_ + dua, ddv

    dyc, du_ab, g_dvec_b = _rowwise(s5_gelu_bwd, [yc, u_win, dgl, dyc_a, du_a], [dvec],
                                    [(s5w, F32), (s5w, F32)], [s5w], tl=128, name="s5_gelu_bwd")
    ds_re = _bdmm([dyc], [tr(cd_re_b)], name="s5_ds_re").reshape(shape3)
    ds_im = _bdmm([dyc], [tr(cd_imn_b)], name="s5_ds_im").reshape(shape3)
    kin_s, kin_u = s5s // nj, s5w // nj
    g_cd_re = _bdmm_tn_sized(s_re, dyc, nj, kin_s, kin_u, 0, "s5_dcd_re")
    g_cd_imn = _bdmm_tn_sized(s_im, dyc, nj, kin_s, kin_u, 0, "s5_dcd_im")
    g_re3, g_im3, g_a_re, g_a_im = _s5_scan_bwd(ds_re, ds_im, s_re3, s_im3, a_re, a_im, "s5_scan_bwd")
    g_re, g_im = g_re3.reshape(length, s5s), g_im3.reshape(length, s5s)
    du = _bdmm([g_re, g_im], [tr(bd_re_b), tr(bd_im_b)], addend=du_ab, out_dtype=BF16, name="s5_du")
    g_bd_re = _bdmm_tn_sized(ug, g_re, nj, kin_u, kin_s, 0, "s5_dbd_re")
    g_bd_im = _bdmm_tn_sized(ug, g_im, nj, kin_u, kin_s, 0, "s5_dbd_im")
    s5_g = s5_vjp((g_a_re, g_a_im, g_bd_re, g_bd_im, g_cd_re, g_cd_imn))
    for nm, g in zip(("lambda_re", "lambda_im", "log_step", "b_re", "b_im", "c_re", "c_im"), s5_g):
        grads["l0_s5_" + nm] = g
    grads["l0_s5_d"] = (g_dvec_a + g_dvec_b).reshape(sm["l0_s5_d"].shape)

    g0, n0 = None, w0t.shape[0]
    for nm, s, off in (("u", du, 0), ("g", dgate, s5w), ("z", dz, o1), ("xbc", dxbc, o2)):
        g0 = _matmul(s, h0, mode="tn", rows_of=(n0, off, g0), name="l0_in_dw_" + nm)
    g0_dt = _unpad_head_rows(_matmul(ddtraw, h0, mode="tn", name="l0_in_dw_dt"), hpg)
    grads["l0_w_inT"] = lax.dynamic_update_slice(g0, g0_dt, (o3, 0))
    dh0 = _matmul(ddtraw, w0_dt, mode="nn", name="l0_in_dx_dt")
    for nm, s, off in (("u", du, 0), ("g", dgate, s5w), ("z", dz, o1), ("xbc", dxbc, o2)):
        dh0 = _matmul(s, w0t, mode="nn", dims=(length, d, s.shape[1]), b_off=(off, 0), addend=dh0,
                      name="l0_in_dx_" + nm)
    dx, _, grads["l0_norm_w"] = _norm_bwd(x, dh0, dx1, nw0, "l0_norm_bwd")
    return loss_lanes, dx, grads


_ANY = pl.BlockSpec(memory_space=pl.ANY)


def _place():
    x, y, c = lax.axis_index("x"), lax.axis_index("y"), lax.axis_index("c")
    return x, y, c, [(1 - x, y), (x, 1 - y), (1 - x, 1 - y)]


def _remote(src, dst, send_sem, recv_sem, to):
    return pltpu.make_async_remote_copy(src_ref=src, dst_ref=dst, send_sem=send_sem, recv_sem=recv_sem,
                                        device_id=to, device_id_type=MESH)


def _comm_call(body, n_in, out_shape, n_sems, name):
    return pl.pallas_call(
        body, name=name, in_specs=[_ANY] * n_in, out_specs=[_ANY] * len(out_shape), out_shape=out_shape,
        scratch_shapes=[pltpu.SemaphoreType.DMA((k,)) for k in n_sems],
        compiler_params=pltpu.CompilerParams(has_side_effects=True),
    )


def _half(ref_or_shape, c):
    ch = ref_or_shape.shape[-1] // 2
    return pl.ds(pl.multiple_of(c * ch, LANES), ch)


def _gather_rider(shards):
    n = len(shards)

    def sends(ins, outs, sems):
        send, recv = sems[:2]
        x, y, c, chips = _place()
        me = 2 * x + y
        return [_remote(ins[a].at[:, _half(ins[a], c)], outs[a].at[me, :, _half(ins[a], c)],
                        send.at[3 * a + k], recv.at[3 * a + k], (px, py, c))
                for a in range(n) for k, (px, py) in enumerate(chips)]

    def start(ins, outs, sems):
        for cp in sends(ins, outs, sems):
            cp.start()

    def finish(ins, outs, sems):
        send, recv, fsend, frecv = sems
        x, y, c, chips = _place()
        passed = []
        for a in range(n):
            for k, (px, py) in enumerate(chips):
                got = outs[a].at[2 * px + py, :, _half(ins[a], c)]
                _remote(got, got, send.at[3 * a + k], recv.at[3 * a + k], (px, py, c)).wait_recv()
                cp = _remote(got, got, fsend.at[3 * a + k], frecv.at[3 * a + k], (x, y, 1 - c))
                cp.start()
                passed.append(cp)
        for a in range(n):
            for k, (px, py) in enumerate(chips):
                got = outs[a].at[2 * px + py, :, _half(ins[a], 1 - c)]
                _remote(got, got, fsend.at[3 * a + k], frecv.at[3 * a + k], (x, y, 1 - c)).wait_recv()
        for cp in sends(ins, outs, sems) + passed:
            cp.wait_send()

    out_shape = [jax.ShapeDtypeStruct((N_SHARD,) + s.shape, s.dtype) for s in shards]
    return _Rider(shards, out_shape, [3 * n] * 4, start, finish)


def _chip_rider(parts):
    n = len(parts)

    def copies(ins, outs, sems):
        send, recv = sems
        x, y, c, chips = _place()
        return [_remote(ins[a].at[2 * px + py], outs[a].at[k], send.at[3 * a + k], recv.at[3 * a + k], (px, py, c))
                for a in range(n) for k, (px, py) in enumerate(chips)]

    def start(ins, outs, sems):
        for cp in copies(ins, outs, sems):
            cp.start()

    def finish(ins, outs, sems):
        for cp in copies(ins, outs, sems):
            cp.wait()

    out_shape = [jax.ShapeDtypeStruct((3,) + p.shape[1:], p.dtype) for p in parts]
    return _Rider(parts, out_shape, [3 * n] * 2, start, finish)


def _run_rider(rider, name):
    n_in, n_out = len(rider.inputs), len(rider.out_shape)

    def body(*refs):
        ins, outs, sems = refs[:n_in], refs[n_in:n_in + n_out], refs[n_in + n_out:]
        rider.start(ins, outs, sems)
        rider.finish(ins, outs, sems)

    return _comm_call(body, n_in, rider.out_shape, rider.sems, name)(*rider.inputs)


def _sibling_halves(grads, name):
    n = len(grads)

    def body(*refs):
        ins, outs = refs[:n], refs[n:2 * n]
        send, recv = refs[2 * n:]
        x, y, c, _ = _place()
        copies = [_remote(ins[a].at[:, :, _half(ins[a], 1 - c)], outs[a], send.at[a], recv.at[a], (x, y, 1 - c))
                  for a in range(n)]
        for cp in copies:
            cp.start()
        for cp in copies:
            cp.wait()

    out_shape = [jax.ShapeDtypeStruct(g.shape[:2] + (g.shape[2] // 2,), g.dtype) for g in grads]
    return _comm_call(body, n, out_shape, [n, n], name)(*grads)


def _join_halves(halves, name):
    n = len(halves)

    def body(*refs):
        outs = refs[n:2 * n]
        send, recv = refs[2 * n:]
        x, y, c, _ = _place()
        mine = [outs[a].at[:, _half(outs[a], c)] for a in range(n)]
        copies = [_remote(mine[a], mine[a], send.at[a], recv.at[a], (x, y, 1 - c)) for a in range(n)]
        for cp in copies:
            cp.start()
        for a in range(n):
            copies[a].wait_send()
            got = outs[a].at[:, _half(outs[a], 1 - c)]
            _remote(got, got, send.at[a], recv.at[a], (x, y, 1 - c)).wait_recv()

    return pl.pallas_call(
        body, name=name, in_specs=[_ANY] * n, out_specs=[_ANY] * n,
        out_shape=[jax.ShapeDtypeStruct(h.shape, h.dtype) for h in halves],
        input_output_aliases={a: a for a in range(n)},
        scratch_shapes=[pltpu.SemaphoreType.DMA((n,)), pltpu.SemaphoreType.DMA((n,))],
        compiler_params=pltpu.CompilerParams(has_side_effects=True),
    )(*halves)


def _gather_all(buf, name):
    def body(in_ref, out_ref, send, recv, lsem):
        x, y, c, _ = _place()
        me = 4 * x + 2 * y + c
        local = pltpu.make_async_copy(in_ref, out_ref.at[me], lsem.at[0])
        local.start()
        copies = []
        for k in range(1, N_DEV):
            fx, fy, fc = (k >> 2) & 1, (k >> 1) & 1, k & 1
            peer = (x + fx - 2 * x * fx, y + fy - 2 * y * fy, c + fc - 2 * c * fc)
            cp = _remote(in_ref, out_ref.at[me], send.at[k - 1], recv.at[k - 1], peer)
            cp.start()
            copies.append((cp, 4 * peer[0] + 2 * peer[1] + peer[2]))
        for k, (cp, slot) in enumerate(copies):
            cp.wait_send()
            got = out_ref.at[slot]
            _remote(got, got, send.at[k], recv.at[k], (x, y, c)).wait_recv()
        local.wait()

    out_shape = [jax.ShapeDtypeStruct((N_DEV,) + buf.shape, buf.dtype)]
    return _comm_call(body, 1, out_shape, [N_DEV - 1, N_DEV - 1, 1], name)(buf)[0]


def _sum_slots(buf, name):
    slots, rows, _ = buf.shape
    tr = _pick(rows, (512, 256, 128, 64, 32, 16, 8))

    def body(b_ref, o_ref):
        acc = b_ref[0]
        for s in range(1, slots):
            acc = acc + b_ref[s]
        o_ref[...] = acc

    return pl.pallas_call(
        body, name=name, grid=(rows // tr,),
        in_specs=[pl.BlockSpec((slots, tr, LANES), lambda i: (0, i, 0))],
        out_specs=pl.BlockSpec((tr, LANES), lambda i: (i, 0)),
        out_shape=jax.ShapeDtypeStruct((rows, LANES), F32),
        compiler_params=_params(("parallel",)),
    )(buf)


def _tile2(rows, cols, n_bufs):
    tr = max(t for t in range(8, min(rows, 2048) + 1, 8) if rows % t == 0) if rows % 8 == 0 else rows
    budget = 24 * 1024 * 1024 // (8 * n_bufs * tr)
    tc = max([t for t in range(LANES, cols + 1, LANES) if cols % t == 0 and t <= budget] or [LANES])
    return tr, tc


def _elementwise(fn, ins, out_dtypes, name):
    rows, cols = ins[0].shape
    tr, tc = _tile2(rows, cols, len(ins) + len(out_dtypes))
    n_in = len(ins)

    def body(*refs):
        outs = fn(*[r[...] for r in refs[:n_in]])
        for r, v in zip(refs[n_in:], outs if isinstance(outs, (tuple, list)) else (outs,)):
            r[...] = v.astype(r.dtype)

    blk = pl.BlockSpec((tr, tc), lambda i, j: (i, j))
    return pl.pallas_call(
        body, name=name, grid=(rows // tr, cols // tc), in_specs=[blk] * n_in, out_specs=[blk] * len(out_dtypes),
        out_shape=[jax.ShapeDtypeStruct((rows, cols), dt) for dt in out_dtypes],
        compiler_params=_params(("parallel", "parallel")),
    )(*ins)


def _presum(grad, sib, name):
    ns, rows, ch = sib.shape
    tr, tc = _tile2(rows, ch, 3)
    nct = ch // tc

    def body(g_ref, r_ref, o_ref):
        o_ref[...] = (g_ref[...] + r_ref[...]).astype(o_ref.dtype)

    blk = pl.BlockSpec((1, tr, tc), lambda j, i, k: (j, i, k))
    return pl.pallas_call(
        body, name=name, grid=(ns, rows // tr, nct),
        in_specs=[pl.BlockSpec((1, tr, tc), lambda j, i, k: (j, i, lax.axis_index("c") * nct + k)), blk],
        out_specs=blk, out_shape=jax.ShapeDtypeStruct((ns, rows, ch), BF16),
        compiler_params=_params(("parallel", "parallel", "parallel")),
    )(grad, sib)


def _finish_half(grad, sib, others, name):
    _, rows, ch = sib.shape
    tr, tc = _tile2(rows, ch, 6)
    nct = ch // tc

    def body(g_ref, r_ref, q_ref, o_ref):
        acc = g_ref[0] + r_ref[0]
        for k in range(3):
            acc = acc + q_ref[k].astype(F32)
        o_ref[...] = acc

    core = lambda: lax.axis_index("c")
    chip = lambda: 2 * lax.axis_index("x") + lax.axis_index("y")
    return pl.pallas_call(
        body, name=name, grid=(rows // tr, nct),
        in_specs=[pl.BlockSpec((1, tr, tc), lambda i, k: (chip(), i, core() * nct + k)),
                  pl.BlockSpec((1, tr, tc), lambda i, k: (chip(), i, k)),
                  pl.BlockSpec((3, tr, tc), lambda i, k: (0, i, k))],
        out_specs=pl.BlockSpec((tr, tc), lambda i, k: (i, core() * nct + k)),
        out_shape=jax.ShapeDtypeStruct((rows, 2 * ch), F32),
        compiler_params=_params(("parallel", "parallel")),
    )(grad, sib, others)


def _cast_bf16(w, name):
    return _elementwise(lambda t: t, [w], [BF16], name)[0]


_WEIGHTS = ("l0_norm_w", "l0_w_in", "l0_s5_lambda_re", "l0_s5_lambda_im", "l0_s5_log_step", "l0_s5_b_re",
            "l0_s5_b_im", "l0_s5_c_re", "l0_s5_c_im", "l0_s5_d", "l0_s5_w_glu", "l0_s5_b_glu", "l0_ssd_conv_w",
            "l0_ssd_conv_b", "l0_ssd_dt_bias", "l0_ssd_a_log", "l0_ssd_d", "l0_ssd_norm_w", "l0_w_out",
            "l1_norm_w", "l1_w_in", "l1_fox_b_f", "l1_w_out", "final_norm_w")
_COL_SHARDED = ("l0_w_in", "l1_w_in")
_ROW_SHARDED = ("l0_s5_w_glu", "l0_w_out", "l1_w_out")
_BIG = ("l0_w_in", "l0_s5_w_glu", "l0_w_out", "l1_w_in", "l1_w_out")
_CONV = "l0_ssd_conv_w"
_SMALL = tuple(n for n in _WEIGHTS if n not in _BIG and n != _CONV)


def _pack(arrays):
    flat = jnp.concatenate([a.reshape(-1).astype(F32) for a in arrays])
    size = flat.shape[0]
    padded = -(-size // (512 * LANES)) * (512 * LANES)
    return jnp.pad(flat, (0, padded - size)).reshape(-1, LANES)


def _unpack(buf, like):
    flat = buf.reshape(-1)
    out, pos = [], 0
    for a in like:
        out.append(flat[pos:pos + a.size].reshape(a.shape))
        pos += a.size
    return out


def _step(p):
    x, tgt = p["x"][0], p["loss_target"][0]
    d = x.shape[1]
    chip = 2 * lax.axis_index("x") + lax.axis_index("y")

    def rows_first(a, n):
        return a.T if n in _COL_SHARDED else a

    shard = {n: _cast_bf16(rows_first(p[n], n), "cast_" + n) for n in _BIG}
    shard[_CONV] = p[_CONV]

    def with_own(n, g):
        return lax.dynamic_update_index_in_dim(g, shard[n][None], chip, 0)

    def whole(n, g):
        g = with_own(n, g)
        return g.reshape(N_SHARD * g.shape[1], g.shape[2])

    now = ("l0_w_in", _CONV)
    got = dict(zip(now, _run_rider(_gather_rider([shard[n] for n in now]), "gather_first")))
    wb = {"w0T": whole("l0_w_in", got["l0_w_in"])}
    sm = {n: p[n] for n in _SMALL}
    taps, ccols = p[_CONV].shape
    conv_all = lax.dynamic_update_index_in_dim(got[_CONV], p[_CONV][None], chip, 0)
    sm[_CONV] = conv_all.transpose(1, 0, 2).reshape(taps, N_SHARD * ccols)
    cut = shard["l1_w_in"].shape[0] // 3 // 8 * 8
    shard["l1_w_in#0"], shard["l1_w_in#1"] = shard["l1_w_in"][:cut], shard["l1_w_in"][cut:]
    later = {"l0_in_ug": ("l1_w_out",), "l0_in_z": ("l0_s5_w_glu", "l1_w_in#0"), "l0_in_xbc": ("l0_w_out",),
             "ssd_scan": ("l1_w_in#1",)}
    early = {"fox_bwd_q": ("l1_w_out",), "l0_out_dx_ssd": ("l0_w_out",), "ssd_scan_bwd": ("l1_w_in",)}
    grad_key = {n: n + "T" if n in _COL_SHARDED else n for n in _BIG}

    big, sib, others, pieces = {}, {}, {}, {}

    def presummed(names, grads, tag):
        for n in names:
            g = grads[grad_key[n]]
            big[n] = g.reshape(N_SHARD, g.shape[0] // N_SHARD, g.shape[1])
        sib.update(zip(names, _sibling_halves([big[n] for n in names], "reduce_sibling_" + tag)))
        return [_presum(big[n], sib[n], "presum_" + n) for n in names]

    class Plan:
        def gather_rider(self, host):
            return _gather_rider([shard[n] for n in later[host]])

        def gathered(self, host, carried):
            got = dict(zip(later[host], carried))
            pieces.update({n: with_own(n, g) for n, g in got.items() if "#" in n})
            w = {n: whole(n, g) for n, g in got.items() if "#" not in n}
            if host == "ssd_scan":
                both = jnp.concatenate([pieces["l1_w_in#0"], pieces["l1_w_in#1"]], axis=1)
                w["l1_w_in"] = both.reshape(N_SHARD * both.shape[1], both.shape[2])
            names = {"l0_s5_w_glu": "w_glu", "l0_w_out": "w0_out", "l1_w_in": "w1T", "l1_w_out": "w1_out"}
            return {names[n]: v for n, v in w.items()}

        def reduce_rider(self, host, grads):
            return _chip_rider(presummed(early[host], grads, host))

        def reduced(self, host, carried):
            others.update(zip(early[host], carried))

    loss_lanes, dx, grads = _local_step(x, tgt, wb, sm, Plan())

    small_like = [p[n] for n in _SMALL] + [sm[_CONV], jnp.zeros((1,), F32)]
    small_sum = _sum_slots(_gather_all(_pack([grads[n] for n in _SMALL] + [grads[_CONV], jnp.sum(loss_lanes)]),
                                       "gather_small"), "sum_small")
    *small_grads, conv_grad, loss = _unpack(small_sum, small_like)
    conv_grad = lax.dynamic_slice(conv_grad, (0, chip * ccols), (taps, ccols))
    final = dict(zip(_SMALL, small_grads))
    final[_CONV] = conv_grad

    late = tuple(n for n in _BIG if n not in others)
    others.update(zip(late, _run_rider(_chip_rider(presummed(late, grads, "late")), "reduce_chips_late")))
    done = [_finish_half(big[n], sib[n], others[n], "finish_" + n) for n in _BIG]

    delta, new_m, new_v = {}, {}, {}
    for n, full in zip(_BIG, _join_halves(done, "join_halves")):
        upd = _adamw(rows_first(p[n], n), full, rows_first(p["m_" + n], n), rows_first(p["v_" + n], n),
                     "adamw_" + n)
        final[n], delta[n], new_m[n], new_v[n] = (rows_first(t, n) for t in (full, *upd))
    rest = _SMALL + (_CONV,)
    packed = [_pack([t[n] for n in rest]) for t in
              ({n: p[n] for n in rest}, final, {n: p["m_" + n] for n in rest}, {n: p["v_" + n] for n in rest})]
    for dst, buf in zip((delta, new_m, new_v), _adamw(*packed, "adamw_small")):
        dst.update(zip(rest, _unpack(buf, [p[n] for n in rest])))

    outs = [loss.reshape(()), dx[None]]
    for group in (final, delta, new_m, new_v):
        outs += [group[n].reshape(p[n].shape) for n in _WEIGHTS]
    return tuple(outs)


_INPUTS = ("x",) + _WEIGHTS + ("loss_target",) + tuple("m_" + n for n in _WEIGHTS) + tuple("v_" + n for n in _WEIGHTS)


def kernel(x, l0_norm_w, l0_w_in, l0_s5_lambda_re, l0_s5_lambda_im, l0_s5_log_step, l0_s5_b_re, l0_s5_b_im, l0_s5_c_re,
           l0_s5_c_im, l0_s5_d, l0_s5_w_glu, l0_s5_b_glu, l0_ssd_conv_w, l0_ssd_conv_b, l0_ssd_dt_bias,
           l0_ssd_a_log, l0_ssd_d, l0_ssd_norm_w, l0_w_out, l1_norm_w, l1_w_in, l1_fox_b_f, l1_w_out,
           final_norm_w, loss_target, m_l0_norm_w, m_l0_w_in, m_l0_s5_lambda_re, m_l0_s5_lambda_im,
           m_l0_s5_log_step, m_l0_s5_b_re, m_l0_s5_b_im, m_l0_s5_c_re, m_l0_s5_c_im, m_l0_s5_d,
           m_l0_s5_w_glu, m_l0_s5_b_glu, m_l0_ssd_conv_w, m_l0_ssd_conv_b, m_l0_ssd_dt_bias, m_l0_ssd_a_log,
           m_l0_ssd_d, m_l0_ssd_norm_w, m_l0_w_out, m_l1_norm_w, m_l1_w_in, m_l1_fox_b_f, m_l1_w_out,
           m_final_norm_w, v_l0_norm_w, v_l0_w_in, v_l0_s5_lambda_re, v_l0_s5_lambda_im, v_l0_s5_log_step,
           v_l0_s5_b_re, v_l0_s5_b_im, v_l0_s5_c_re, v_l0_s5_c_im, v_l0_s5_d, v_l0_s5_w_glu, v_l0_s5_b_glu,
           v_l0_ssd_conv_w, v_l0_ssd_conv_b, v_l0_ssd_dt_bias, v_l0_ssd_a_log, v_l0_ssd_d, v_l0_ssd_norm_w,
           v_l0_w_out, v_l1_norm_w, v_l1_w_in, v_l1_fox_b_f, v_l1_w_out, v_final_norm_w):
    values = (x, l0_norm_w, l0_w_in, l0_s5_lambda_re, l0_s5_lambda_im, l0_s5_log_step, l0_s5_b_re, l0_s5_b_im,
              l0_s5_c_re, l0_s5_c_im, l0_s5_d, l0_s5_w_glu, l0_s5_b_glu, l0_ssd_conv_w, l0_ssd_conv_b,
              l0_ssd_dt_bias, l0_ssd_a_log, l0_ssd_d, l0_ssd_norm_w, l0_w_out, l1_norm_w, l1_w_in,
              l1_fox_b_f, l1_w_out, final_norm_w, loss_target, m_l0_norm_w, m_l0_w_in,
              m_l0_s5_lambda_re, m_l0_s5_lambda_im, m_l0_s5_log_step, m_l0_s5_b_re, m_l0_s5_b_im,
              m_l0_s5_c_re, m_l0_s5_c_im, m_l0_s5_d, m_l0_s5_w_glu, m_l0_s5_b_glu, m_l0_ssd_conv_w,
              m_l0_ssd_conv_b, m_l0_ssd_dt_bias, m_l0_ssd_a_log, m_l0_ssd_d, m_l0_ssd_norm_w,
              m_l0_w_out, m_l1_norm_w, m_l1_w_in, m_l1_fox_b_f, m_l1_w_out, m_final_norm_w, v_l0_norm_w,
              v_l0_w_in, v_l0_s5_lambda_re, v_l0_s5_lambda_im, v_l0_s5_log_step, v_l0_s5_b_re,
              v_l0_s5_b_im, v_l0_s5_c_re, v_l0_s5_c_im, v_l0_s5_d, v_l0_s5_w_glu, v_l0_s5_b_glu,
              v_l0_ssd_conv_w, v_l0_ssd_conv_b, v_l0_ssd_dt_bias, v_l0_ssd_a_log, v_l0_ssd_d,
              v_l0_ssd_norm_w, v_l0_w_out, v_l1_norm_w, v_l1_w_in, v_l1_fox_b_f, v_l1_w_out,
              v_final_norm_w)
    return _step(dict(zip(_INPUTS, values)))
```

```python
import functools
import math

import jax
import jax.numpy as jnp
from jax import lax
from jax.experimental import pallas as pl
from jax.experimental.pallas import tpu as pltpu

F32 = jnp.float32
BF16 = jnp.bfloat16

S5_GROUP = 16
S5_STATE = 64
S5_EIG_CLIP = -1e-4
SSD_HEAD_DIM = 64
SSD_GROUPS = 8
SSD_STATE = 128
SSD_CONV = 4
SSD_CHUNK = 128
FOX_HEAD_DIM = 128
FOX_TILE = 1024
NORM_EPS = 1e-5
ADAM_LR = 0.001
ADAM_B1 = 0.9
ADAM_B2 = 0.999
ADAM_EPS = 1e-08
ADAM_WD = 0.01
ADAM_STEP = 10

N_SHARD = 4
N_DEV = 8
LANES = 128
VMEM_LIMIT = 56 * 1024 * 1024
MESH = pl.DeviceIdType.MESH


def _pick(dim, prefs, offs=()):
    for p in prefs:
        if dim % p == 0 and all(o % p == 0 for o in offs):
            return p
    return dim


def _params(sem=None, vmem=VMEM_LIMIT):
    return pltpu.CompilerParams(dimension_semantics=sem, vmem_limit_bytes=vmem)


class _Rider:
    def __init__(self, inputs, out_shape, sems, start, finish):
        self.inputs, self.out_shape, self.sems = list(inputs), list(out_shape), list(sems)
        self.start, self.finish = start, finish


def _hosted_call(body, *, name, grid, in_specs, out_specs, out_shape, scratch_shapes, sem, args, rider=None):
    single = not isinstance(out_shape, (list, tuple))
    out_specs = [out_specs] if single else list(out_specs)
    out_shape = [out_shape] if single else list(out_shape)
    if rider is None:
        res = pl.pallas_call(body, name=name, grid=grid, in_specs=in_specs, out_specs=out_specs,
                             out_shape=out_shape, scratch_shapes=scratch_shapes,
                             compiler_params=_params(sem))(*args)
        return (res[0] if single else res), []
    n_in, n_out, n_scr = len(in_specs), len(out_shape), len(scratch_shapes)
    n_rin, n_rout = len(rider.inputs), len(rider.out_shape)

    def carried(*refs):
        ins, refs = refs[:n_in], refs[n_in:]
        rin, refs = refs[:n_rin], refs[n_rin:]
        outs, refs = refs[:n_out], refs[n_out:]
        rout, refs = refs[:n_rout], refs[n_rout:]
        scr, rsem = refs[:n_scr], refs[n_scr:]
        ids = [pl.program_id(k) for k in range(len(grid))]
        first = functools.reduce(jnp.logical_and, [i == 0 for i in ids])
        last = functools.reduce(jnp.logical_and, [i == g - 1 for i, g in zip(ids, grid)])

        @pl.when(first)
        def _():
            rider.start(rin, rout, rsem)

        body(*ins, *outs, *scr)

        @pl.when(last)
        def _():
            rider.finish(rin, rout, rsem)

    res = pl.pallas_call(
        carried, name=name, grid=grid,
        in_specs=list(in_specs) + [_ANY] * n_rin, out_specs=out_specs + [_ANY] * n_rout,
        out_shape=out_shape + rider.out_shape,
        scratch_shapes=list(scratch_shapes) + [pltpu.SemaphoreType.DMA((k,)) for k in rider.sems],
        compiler_params=pltpu.CompilerParams(dimension_semantics=("arbitrary",) * len(grid),
                                             vmem_limit_bytes=VMEM_LIMIT, has_side_effects=True),
    )(*args, *rider.inputs)
    outs = res[:n_out]
    return (outs[0] if single else outs), list(res[n_out:])


def _matmul(a, b, *, mode="nn", dims=None, a_off=(0, 0), b_off=(0, 0), addend=None,
            out_dtype=F32, rider=None, rows_of=None, name):
    if dims is None:
        if mode == "nn":
            dims = (a.shape[0], b.shape[1], a.shape[1])
        elif mode == "nt":
            dims = (a.shape[0], b.shape[0], a.shape[1])
        else:
            dims = (a.shape[1], b.shape[1], a.shape[0])
    m, n, k = dims
    if mode == "nn":
        om, on, ok = (a_off[0],), (b_off[1],), (a_off[1], b_off[0])
    elif mode == "nt":
        om, on, ok = (a_off[0],), (b_off[0],), (a_off[1], b_off[1])
    else:
        om, on, ok = (a_off[1],), (b_off[1],), (a_off[0], b_off[0])
    tm = _pick(m, (1024, 512, 256, 128), om)
    tn = _pick(n, (1024, 768, 512, 384, 256, 128), on)
    tk = _pick(k, (2048, 1024, 512, 256, 128), ok)
    nk = k // tk
    if mode == "nn":
        a_blk, a_div = (tm, tk), (tm, tk)
        b_blk, b_div = (tk, tn), (tk, tn)
        a_map = lambda i, j, kk: (i + a_off[0] // tm, kk + a_off[1] // tk)
        b_map = lambda i, j, kk: (kk + b_off[0] // tk, j + b_off[1] // tn)
        dn = (((1,), (0,)), ((), ()))
    elif mode == "nt":
        a_blk, a_div = (tm, tk), (tm, tk)
        b_blk, b_div = (tn, tk), (tn, tk)
        a_map = lambda i, j, kk: (i + a_off[0] // tm, kk + a_off[1] // tk)
        b_map = lambda i, j, kk: (j + b_off[0] // tn, kk + b_off[1] // tk)
        dn = (((1,), (1,)), ((), ()))
    else:
        a_blk, a_div = (tk, tm), (tk, tm)
        b_blk, b_div = (tk, tn), (tk, tn)
        a_map = lambda i, j, kk: (kk + a_off[0] // tk, i + a_off[1] // tm)
        b_map = lambda i, j, kk: (kk + b_off[0] // tk, j + b_off[1] // tn)
        dn = (((0,), (0,)), ((), ()))
    assert a_off[0] % a_div[0] == 0 and a_off[1] % a_div[1] == 0, (name, a_off, a_div)
    assert b_off[0] % b_div[0] == 0 and b_off[1] % b_div[1] == 0, (name, b_off, b_div)
    has_add = addend is not None

    def body(*refs):
        if has_add:
            a_ref, b_ref, c_ref, o_ref, acc_ref = refs
        else:
            a_ref, b_ref, o_ref, acc_ref = refs
        kk = pl.program_id(2)

        @pl.when(kk == 0)
        def _():
            acc_ref[...] = jnp.zeros_like(acc_ref)

        acc_ref[...] += lax.dot_general(a_ref[...].astype(BF16), b_ref[...].astype(BF16), dn,
                                        preferred_element_type=F32)

        @pl.when(kk == nk - 1)
        def _():
            r = acc_ref[...]
            if has_add:
                r = r + c_ref[...].astype(F32)
            o_ref[...] = r.astype(o_ref.dtype)

    in_specs = [pl.BlockSpec(a_blk, a_map), pl.BlockSpec(b_blk, b_map)]
    args = [a, b]
    if has_add:
        in_specs.append(pl.BlockSpec((tm, tn), lambda i, j, kk: (i, j)))
        args.append(addend)
    if rows_of is not None:
        total, row_off, buf = rows_of
        assert row_off % tm == 0 and rider is None, (name, row_off, tm)
        if buf is not None:
            def body_into(*refs):
                body(*refs[:len(args)], *refs[len(args) + 1:])

            return pl.pallas_call(
                body_into, name=name, grid=(m // tm, n // tn, nk),
                in_specs=in_specs + [pl.BlockSpec(memory_space=pl.ANY)],
                out_specs=pl.BlockSpec((tm, tn), lambda i, j, kk: (i + row_off // tm, j)),
                out_shape=jax.ShapeDtypeStruct((total, n), out_dtype),
                input_output_aliases={len(args): 0},
                scratch_shapes=[pltpu.VMEM((tm, tn), F32)],
                compiler_params=_params(("parallel", "parallel", "arbitrary")),
            )(*args, buf)
        return pl.pallas_call(
            body, name=name, grid=(m // tm, n // tn, nk), in_specs=in_specs,
            out_specs=pl.BlockSpec((tm, tn), lambda i, j, kk: (i + row_off // tm, j)),
            out_shape=jax.ShapeDtypeStruct((total, n), out_dtype),
            scratch_shapes=[pltpu.VMEM((tm, tn), F32)],
            compiler_params=_params(("parallel", "parallel", "arbitrary")),
        )(*args)
    out, carried = _hosted_call(
        body, name=name, grid=(m // tm, n // tn, nk),
        in_specs=in_specs, out_specs=pl.BlockSpec((tm, tn), lambda i, j, kk: (i, j)),
        out_shape=jax.ShapeDtypeStruct((m, n), out_dtype),
        scratch_shapes=[pltpu.VMEM((tm, tn), F32)],
        sem=("parallel", "parallel", "arbitrary"), args=args, rider=rider)
    return out if rider is None else (out, carried)


def _rowwise(fn, rows, params, out_rows, out_accs=(), *, tl, ncol=1, name):
    rows = [r if isinstance(r, tuple) else (r, r.shape[1] // ncol, 0) for r in rows]
    n_rows, n_par, n_or, n_oa = len(rows), len(params), len(out_rows), len(out_accs)
    length = rows[0][0].shape[0]
    tl = _pick(length, [t for t in (1024, 512, 256, 128, 64, 32, 16, 8) if t <= tl])

    def body(*refs):
        row_refs = refs[:n_rows]
        par_refs = refs[n_rows:n_rows + n_par]
        or_refs = refs[n_rows + n_par:n_rows + n_par + n_or]
        oa_refs = refs[n_rows + n_par + n_or:]
        outs = fn(*[r[...] for r in row_refs], *[p[...] for p in par_refs])
        if not isinstance(outs, (tuple, list)):
            outs = (outs,)
        for r, v in zip(or_refs, outs[:n_or]):
            r[...] = v.astype(r.dtype)
        if n_oa:
            @pl.when(pl.program_id(1) == 0)
            def _():
                for r in oa_refs:
                    r[...] = jnp.zeros_like(r)

            for r, v in zip(oa_refs, outs[n_or:]):
                r[...] += v.astype(F32)

    in_specs = [pl.BlockSpec((tl, w), functools.partial(lambda j, i, b0: (i, b0 + j), b0=b0))
                for (_, w, b0) in rows]
    in_specs += [pl.BlockSpec((p.shape[0], p.shape[1] // ncol), lambda j, i: (0, j)) for p in params]
    out_specs = [pl.BlockSpec((tl, w), lambda j, i: (i, j)) for (w, _) in out_rows]
    out_specs += [pl.BlockSpec((1, w), lambda j, i: (0, j)) for w in out_accs]
    out_shape = [jax.ShapeDtypeStruct((length, ncol * w), dt) for (w, dt) in out_rows]
    out_shape += [jax.ShapeDtypeStruct((1, ncol * w), F32) for w in out_accs]
    res = pl.pallas_call(
        body, name=name, grid=(ncol, length // tl),
        in_specs=in_specs, out_specs=out_specs, out_shape=out_shape,
        compiler_params=_params(("parallel", "arbitrary" if n_oa else "parallel")),
    )(*[r[0] for r in rows], *params)
    return res


def _bdmm(xs, ws, *, addend=None, out_dtype=F32, name):
    nj, kin, kout = ws[0].shape
    xs = [x if isinstance(x, tuple) else (x, 0) for x in xs]
    length = xs[0][0].shape[0]
    tl = _pick(length, (2048, 1024, 512, 256, 128))
    n_x = len(xs)
    has_add = addend is not None

    def body(*refs):
        x_refs = refs[:n_x]
        w_refs = refs[n_x:2 * n_x]
        o_ref = refs[-1]
        acc = None
        for xr, wr in zip(x_refs, w_refs):
            t = jnp.dot(xr[...].astype(BF16), wr[0], preferred_element_type=F32)
            acc = t if acc is None else acc + t
        if has_add:
            acc = acc + refs[2 * n_x][...].astype(F32)
        o_ref[...] = acc.astype(o_ref.dtype)

    in_specs = [pl.BlockSpec((tl, kin), functools.partial(lambda i, j, b0: (i, b0 + j), b0=b0)) for (_, b0) in xs]
    in_specs += [pl.BlockSpec((1, kin, kout), lambda i, j: (j, 0, 0)) for _ in ws]
    args = [x[0] for x in xs] + list(ws)
    if has_add:
        in_specs.append(pl.BlockSpec((tl, kout), lambda i, j: (i, j)))
        args.append(addend)
    return pl.pallas_call(
        body, name=name, grid=(length // tl, nj),
        in_specs=in_specs, out_specs=pl.BlockSpec((tl, kout), lambda i, j: (i, j)),
        out_shape=jax.ShapeDtypeStruct((length, nj * kout), out_dtype),
        compiler_params=_params(("parallel", "parallel")),
    )(*args)


def _bdmm_tn_sized(x, g, nj, kin, kout, x_first, name):
    length = x.shape[0]
    tl = _pick(length, (512, 256, 128))
    nt = length // tl

    def body(x_ref, g_ref, o_ref):
        @pl.when(pl.program_id(1) == 0)
        def _():
            o_ref[...] = jnp.zeros_like(o_ref)

        o_ref[0] += lax.dot_general(x_ref[...].astype(BF16), g_ref[...].astype(BF16),
                                    (((0,), (0,)), ((), ())), preferred_element_type=F32)

    return pl.pallas_call(
        body, name=name, grid=(nj, nt),
        in_specs=[pl.BlockSpec((tl, kin), lambda j, t: (t, x_first + j)),
                  pl.BlockSpec((tl, kout), lambda j, t: (t, j))],
        out_specs=pl.BlockSpec((1, kin, kout), lambda j, t: (j, 0, 0)),
        out_shape=jax.ShapeDtypeStruct((nj, kin, kout), F32),
        compiler_params=_params(("parallel", "arbitrary")),
    )(x, g)


def _f_norm(x, w):
    return x * lax.rsqrt(jnp.mean(x * x, axis=-1, keepdims=True) + NORM_EPS) * w


def _gelu(y):
    return 0.5 * y * (1.0 + jnp.tanh(math.sqrt(2.0 / math.pi) * (y + 0.044715 * (y * y * y))))


def _sigmoid(x):
    return 1.0 / (1.0 + jnp.exp(-x))


def _silu(x):
    return x * _sigmoid(x)


def _softplus(x):
    return jnp.maximum(x, 0.0) + jnp.log(1.0 + jnp.exp(-jnp.abs(x)))


def _f_s5_gelu(yc, u, dvec):
    return _gelu(yc + dvec * u)


def _f_s5_out(yc, u, t, gate, dvec, bglu):
    gl = _gelu(yc + dvec * u)
    return gl * _sigmoid(t + bglu) * _silu(gate)


def _f_ssd_out(y, xs, z, dpar, nw):
    v = (y + dpar * xs) * _silu(z)
    return v * lax.rsqrt(jnp.mean(v * v, axis=-1, keepdims=True) + NORM_EPS) * nw


def _f_fox_out(att, gate):
    return att * _silu(gate)


def _norm_fwd(x, w, name):
    return _rowwise(lambda xt, wt: _f_norm(xt, wt), [x], [w], [(x.shape[1], BF16)], tl=256, name=name)[0]


def _norm_bwd(x, dh, dres, w, name):
    d = x.shape[1]

    def fn(xt, dht, drt, wt):
        _, vjp = jax.vjp(_f_norm, xt, wt)
        dx, dw = vjp(dht)
        dx = dx + drt
        return dx, dx, dw

    return _rowwise(fn, [x, dh, dres], [w], [(d, F32), (d, BF16)], [d], tl=128, name=name)


def _adamw_math(w, g, m, v):
    m = ADAM_B1 * m + (1.0 - ADAM_B1) * g
    v = ADAM_B2 * v + (1.0 - ADAM_B2) * jnp.square(g)
    m_hat = m / (1.0 - ADAM_B1 ** ADAM_STEP)
    v_hat = v / (1.0 - ADAM_B2 ** ADAM_STEP)
    delta = -ADAM_LR * (m_hat / (jnp.sqrt(v_hat) + ADAM_EPS) + ADAM_WD * w)
    return delta, m, v


def _adamw(w, g, m, v, name):
    return _elementwise(_adamw_math, [w, g, m, v], [F32] * 3, name)


def _s5_scan_fwd(bu_re, bu_im, a_re, a_im, name):
    length, rows, _ = bu_re.shape
    rb = _pick(rows, (32, 16, 8))
    tl = _pick(length, (64, 32, 16, 8))

    def body(bur_ref, bui_ref, ar_ref, ai_ref, sr_ref, si_ref, st_ref):
        @pl.when(pl.program_id(1) == 0)
        def _():
            st_ref[...] = jnp.zeros_like(st_ref)

        ar = ar_ref[...]
        ai = ai_ref[...]

        def step(l, carry):
            sr, si = carry
            nr = ar * sr - ai * si + bur_ref[l]
            ni = ar * si + ai * sr + bui_ref[l]
            sr_ref[l] = nr
            si_ref[l] = ni
            return nr, ni

        sr, si = lax.fori_loop(0, tl, step, (st_ref[0], st_ref[1]))
        st_ref[0] = sr
        st_ref[1] = si

    blk = pl.BlockSpec((tl, rb, LANES), lambda cb, t: (t, cb, 0))
    ablk = pl.BlockSpec((rb, LANES), lambda cb, t: (cb, 0))
    return pl.pallas_call(
        body, name=name, grid=(rows // rb, length // tl),
        in_specs=[blk, blk, ablk, ablk], out_specs=[blk, blk],
        out_shape=[jax.ShapeDtypeStruct(bu_re.shape, F32)] * 2,
        scratch_shapes=[pltpu.VMEM((2, rb, LANES), F32)],
        compiler_params=_params(("parallel", "arbitrary")),
    )(bu_re, bu_im, a_re, a_im)


def _s5_scan_bwd(ds_re, ds_im, s_re, s_im, a_re, a_im, name):
    length, rows, _ = ds_re.shape
    rb = _pick(rows, (32, 16, 8))
    tl = _pick(length, (64, 32, 16, 8))
    nt = length // tl

    def body(dsr_ref, dsi_ref, sr_ref, si_ref, pr_ref, pi_ref, ar_ref, ai_ref,
             gr_ref, gi_ref, dar_ref, dai_ref, st_ref):
        t = pl.program_id(1)

        @pl.when(t == 0)
        def _():
            st_ref[...] = jnp.zeros_like(st_ref)
            dar_ref[...] = jnp.zeros_like(dar_ref)
            dai_ref[...] = jnp.zeros_like(dai_ref)

        ar = ar_ref[...]
        ai = ai_ref[...]

        def adj(l, gr, gi):
            ngr = dsr_ref[l] + ar * gr + ai * gi
            ngi = dsi_ref[l] + ar * gi - ai * gr
            gr_ref[l] = ngr
            gi_ref[l] = ngi
            return ngr, ngi

        def step(idx, carry):
            gr, gi, dar, dai = carry
            l = tl - 1 - idx
            gr, gi = adj(l, gr, gi)
            pr = sr_ref[l - 1]
            pi = si_ref[l - 1]
            dar = dar + gr * pr + gi * pi
            dai = dai + gi * pr - gr * pi
            return gr, gi, dar, dai

        zero = jnp.zeros((rb, LANES), F32)
        gr, gi, dar, dai = lax.fori_loop(0, tl - 1, step, (st_ref[0], st_ref[1], zero, zero))
        gr, gi = adj(0, gr, gi)
        first = (t == nt - 1)
        pr = jnp.where(first, 0.0, pr_ref[0])
        pi = jnp.where(first, 0.0, pi_ref[0])
        dar = dar + gr * pr + gi * pi
        dai = dai + gi * pr - gr * pi
        st_ref[0] = gr
        st_ref[1] = gi
        dar_ref[...] += dar
        dai_ref[...] += dai

    blk = pl.BlockSpec((tl, rb, LANES), lambda cb, t: (nt - 1 - t, cb, 0))
    prev = pl.BlockSpec((1, rb, LANES), lambda cb, t: (jnp.maximum((nt - 1 - t) * tl - 1, 0), cb, 0))
    ablk = pl.BlockSpec((rb, LANES), lambda cb, t: (cb, 0))
    return pl.pallas_call(
        body, name=name, grid=(rows // rb, nt),
        in_specs=[blk, blk, blk, blk, prev, prev, ablk, ablk],
        out_specs=[blk, blk, ablk, ablk],
        out_shape=[jax.ShapeDtypeStruct(ds_re.shape, F32)] * 2 + [jax.ShapeDtypeStruct(a_re.shape, F32)] * 2,
        scratch_shapes=[pltpu.VMEM((2, rb, LANES), F32)],
        compiler_params=_params(("parallel", "arbitrary")),
    )(ds_re, ds_im, s_re, s_im, s_re, s_im, a_re, a_im)


def _s5_prepare(lam_re, lam_im, log_step, b_re, b_im, c_re, c_im):
    groups, state = lam_re.shape
    lr = jnp.minimum(lam_re, S5_EIG_CLIP)
    li = lam_im
    step = jnp.exp(log_step)[:, None]
    mag = jnp.exp(lr * step)
    ab_re = mag * jnp.cos(li * step)
    ab_im = mag * jnp.sin(li * step)
    denom = lr * lr + li * li
    nr = ab_re - 1.0
    ni = ab_im
    coef_re = (nr * lr + ni * li) / denom
    coef_im = (ni * lr - nr * li) / denom
    bb_re = coef_re[..., None] * b_re - coef_im[..., None] * b_im
    bb_im = coef_re[..., None] * b_im + coef_im[..., None] * b_re
    per = LANES // S5_GROUP
    nj = groups // per
    eye = jnp.eye(per, dtype=F32)

    def in_map(bb):
        return jnp.einsum('jgph,gk->jghkp', bb.reshape(nj, per, state, S5_GROUP), eye).reshape(
            nj, per * S5_GROUP, per * state)

    def out_map(cc):
        return jnp.einsum('jghp,gk->jgpkh', cc.reshape(nj, per, S5_GROUP, state), eye).reshape(
            nj, per * state, per * S5_GROUP)

    shape2 = (groups * state // LANES, LANES)
    return (ab_re.reshape(shape2), ab_im.reshape(shape2), in_map(bb_re), in_map(bb_im),
            out_map(c_re), -out_map(c_im))


def _shift_down(cur, prev8, j):
    rolled = pltpu.roll(cur, j, 0)
    pr = pltpu.roll(prev8, j, 0)
    row = lax.broadcasted_iota(jnp.int32, cur.shape, 0)
    return jnp.where(row < j, jnp.tile(pr, (cur.shape[0] // 8, 1)), rolled)


def _shift_up(cur, next8, j):
    tl = cur.shape[0]
    rolled = pltpu.roll(cur, tl - j, 0)
    nx = pltpu.roll(next8, 8 - j, 0)
    row = lax.broadcasted_iota(jnp.int32, cur.shape, 0)
    return jnp.where(row >= tl - j, jnp.tile(nx, (tl // 8, 1)), rolled)


def _conv_tiles(length, ch):
    return _pick(length, (256, 128, 64, 32, 16, 8)), _pick(ch, (1024, 512, 256, 128))


def _conv_fwd(xbc, w, b, name):
    length, ch = xbc.shape
    tl, tc = _conv_tiles(length, ch)

    def body(x_ref, p_ref, w_ref, b_ref, o_ref):
        cur = x_ref[...]
        prev8 = jnp.where(pl.program_id(1) == 0, 0.0, p_ref[...])
        pre = b_ref[...] + w_ref[SSD_CONV - 1:SSD_CONV, :] * cur
        for j in range(1, SSD_CONV):
            pre = pre + w_ref[SSD_CONV - 1 - j:SSD_CONV - j, :] * _shift_down(cur, prev8, j)
        o_ref[...] = _silu(pre)

    return pl.pallas_call(
        body, name=name, grid=(ch // tc, length // tl),
        in_specs=[pl.BlockSpec((tl, tc), lambda c, i: (i, c)),
                  pl.BlockSpec((8, tc), lambda c, i: (jnp.maximum(i * (tl // 8) - 1, 0), c)),
                  pl.BlockSpec((SSD_CONV, tc), lambda c, i: (0, c)),
                  pl.BlockSpec((1, tc), lambda c, i: (0, c))],
        out_specs=pl.BlockSpec((tl, tc), lambda c, i: (i, c)),
        out_shape=jax.ShapeDtypeStruct((length, ch), F32),
        compiler_params=_params(("parallel", "parallel")),
    )(xbc, xbc, w, b)


def _conv_bwd_pre(dxc, xbc, w, b, name):
    length, ch = xbc.shape
    tl, tc = _conv_tiles(length, ch)

    def body(d_ref, x_ref, p_ref, w_ref, b_ref, o_ref, dw_ref, db_ref):
        @pl.when(pl.program_id(1) == 0)
        def _():
            dw_ref[...] = jnp.zeros_like(dw_ref)
            db_ref[...] = jnp.zeros_like(db_ref)

        cur = x_ref[...]
        prev8 = jnp.where(pl.program_id(1) == 0, 0.0, p_ref[...])
        shifted = [cur] + [_shift_down(cur, prev8, j) for j in range(1, SSD_CONV)]
        pre = b_ref[...]
        for j in range(SSD_CONV):
            pre = pre + w_ref[SSD_CONV - 1 - j:SSD_CONV - j, :] * shifted[j]
        sg = _sigmoid(pre)
        dpre = d_ref[...] * (sg * (1.0 + pre * (1.0 - sg)))
        o_ref[...] = dpre
        db_ref[...] += jnp.sum(dpre, axis=0, keepdims=True)
        row = lax.broadcasted_iota(jnp.int32, (SSD_CONV, tc), 0)
        dw = jnp.zeros((SSD_CONV, tc), F32)
        for j in range(SSD_CONV):
            dw = dw + jnp.where(row == SSD_CONV - 1 - j, jnp.sum(dpre * shifted[j], axis=0, keepdims=True), 0.0)
        dw_ref[...] += dw

    return pl.pallas_call(
        body, name=name, grid=(ch // tc, length // tl),
        in_specs=[pl.BlockSpec((tl, tc), lambda c, i: (i, c)),
                  pl.BlockSpec((tl, tc), lambda c, i: (i, c)),
                  pl.BlockSpec((8, tc), lambda c, i: (jnp.maximum(i * (tl // 8) - 1, 0), c)),
                  pl.BlockSpec((SSD_CONV, tc), lambda c, i: (0, c)),
                  pl.BlockSpec((1, tc), lambda c, i: (0, c))],
        out_specs=[pl.BlockSpec((tl, tc), lambda c, i: (i, c)),
                   pl.BlockSpec((SSD_CONV, tc), lambda c, i: (0, c)),
                   pl.BlockSpec((1, tc), lambda c, i: (0, c))],
        out_shape=[jax.ShapeDtypeStruct((length, ch), F32), jax.ShapeDtypeStruct((SSD_CONV, ch), F32),
                   jax.ShapeDtypeStruct((1, ch), F32)],
        compiler_params=_params(("parallel", "arbitrary")),
    )(dxc, xbc, xbc, w, b)


def _conv_bwd_in(dpre, w, name):
    length, ch = dpre.shape
    tl, tc = _conv_tiles(length, ch)
    nt = length // tl

    def body(d_ref, n_ref, w_ref, o_ref):
        cur = d_ref[...]
        next8 = jnp.where(pl.program_id(1) == nt - 1, 0.0, n_ref[...])
        acc = w_ref[SSD_CONV - 1:SSD_CONV, :] * cur
        for j in range(1, SSD_CONV):
            acc = acc + w_ref[SSD_CONV - 1 - j:SSD_CONV - j, :] * _shift_up(cur, next8, j)
        o_ref[...] = acc.astype(o_ref.dtype)

    return pl.pallas_call(
        body, name=name, grid=(ch // tc, nt),
        in_specs=[pl.BlockSpec((tl, tc), lambda c, i: (i, c)),
                  pl.BlockSpec((8, tc), lambda c, i: (jnp.minimum((i + 1) * (tl // 8), length // 8 - 1), c)),
                  pl.BlockSpec((SSD_CONV, tc), lambda c, i: (0, c))],
        out_specs=pl.BlockSpec((tl, tc), lambda c, i: (i, c)),
        out_shape=jax.ShapeDtypeStruct((length, ch), BF16),
        compiler_params=_params(("parallel", "parallel")),
    )(dpre, dpre, w)


def _split(x, terms):
    parts = []
    for _ in range(terms):
        part = x.astype(BF16)
        parts.append(part)
        x = x - part.astype(F32)
    return parts


def _dot(a, b, dn=(((1,), (0,)), ((), ()))):
    return lax.dot_general(a, b, dn, preferred_element_type=F32)


_NN = (((1,), (0,)), ((), ()))
_NT = (((1,), (1,)), ((), ()))
_TN = (((0,), (0,)), ((), ()))


def _pdot(parts, sel, dn=_NN):
    return functools.reduce(lambda a, b: a + b, [_dot(part, sel, dn) for part in parts])


def _pdotr(sel, parts, dn=_NN):
    return functools.reduce(lambda a, b: a + b, [_dot(sel, part, dn) for part in parts])


def _dot3(x, sel, dn=_NN):
    return _pdot(_split(x, 3), sel, dn)


def _dot3r(sel, x, dn=_NN):
    return _pdotr(sel, _split(x, 3), dn)


def _dot2(x, sel, dn=_NN):
    return _pdot(_split(x, 2), sel, dn)


def _iota2(shape, axis):
    return lax.broadcasted_iota(jnp.int32, shape, axis)


def _ssd_masks():
    q = SSD_CHUNK
    r, c = _iota2((q, q), 0), _iota2((q, q), 1)
    tril = (c <= r)
    return r, c, tril


def _head_of_lane(wg):
    return (_iota2((SSD_CHUNK, wg), 0) == _iota2((SSD_CHUNK, wg), 1) // SSD_HEAD_DIM).astype(BF16)


def _head_of_row(wg, dtype):
    return (_iota2((wg, SSD_CHUNK), 1) == _iota2((wg, SSD_CHUNK), 0) // SSD_HEAD_DIM).astype(dtype)


class _SsdChunk:
    def __init__(self, dtraw, dtb, ap, b_t, c_t, wg):
        q = SSD_CHUNK
        hpg = wg // SSD_HEAD_DIM
        r, c, self.tril = _ssd_masks()
        self.upper = (c > r)
        self.dt = _softplus(dtraw + dtb)
        self.la = self.dt * ap
        self.cum = _dot3r(self.tril.astype(BF16), self.la)
        rem = _dot3r(self.upper.astype(BF16), self.la)
        total = _dot3(self.la, jnp.ones((q, q), BF16), _TN)
        self.scores = _dot(c_t, b_t, _NT)
        cum2 = _split(self.cum, 2)
        stack = jnp.concatenate(_split(self.dt, 2) + cum2 + _split(rem, 2), axis=0)
        lanes = _dot(stack, _head_of_lane(wg))
        self.dt_l = lanes[0:q] + lanes[q:2 * q]
        self.cum_l = lanes[2 * q:3 * q] + lanes[3 * q:4 * q]
        self.rem_l = lanes[4 * q:5 * q] + lanes[5 * q:6 * q]
        self.grow = jnp.exp(_pdotr(_head_of_row(wg, BF16), _split(total, 2)))
        every_lane = (_iota2((q, hpg * q), 0) == _iota2((q, hpg * q), 1) // q).astype(BF16)
        cq = _dot(jnp.concatenate(cum2, axis=0), every_lane)
        self.cq = cq[0:q] + cq[q:2 * q]
        every_row = (_iota2((hpg * q, q), 1) == _iota2((hpg * q, q), 0) // q).astype(BF16)
        self.ck = _pdotr(every_row, cum2, _NT)

    def decay(self, h):
        q = SSD_CHUNK
        seg = self.cq[:, h * q:(h + 1) * q] - self.ck[h * q:(h + 1) * q, :]
        return jnp.exp(jnp.where(self.tril, seg, -jnp.inf))


def _by_head(x_b, lane):
    first = (lane // SSD_HEAD_DIM) == 0
    return jnp.concatenate([jnp.where(first, x_b, 0), jnp.where(first, 0, x_b)], axis=0)


def _ssd_tiles(xc, n_heads):
    hpg = n_heads // SSD_GROUPS
    wg = hpg * SSD_HEAD_DIM
    xw = n_heads * SSD_HEAD_DIM
    return hpg, wg, xw // wg, xw // SSD_STATE


def _ssd_fwd(xc, dtraw, dtb, ap, n_heads, name, rider=None):
    length = xc.shape[0]
    q = SSD_CHUNK
    nc = length // q
    hpg, wg, _, b_blk0 = _ssd_tiles(xc, n_heads)
    c_blk0 = b_blk0 + SSD_GROUPS
    npair = hpg // 2

    def body(x_ref, b_ref, c_ref, dt_ref, dtb_ref, ap_ref, y_ref, st_ref, s_ref):
        @pl.when(pl.program_id(1) == 0)
        def _():
            s_ref[...] = jnp.zeros_like(s_ref)

        st_ref[0, 0] = s_ref[...]
        b_t = b_ref[...].astype(BF16)
        c_t = c_ref[...].astype(BF16)
        ck = _SsdChunk(dt_ref[...], dtb_ref[...], ap_ref[...], b_t, c_t, wg)
        lane = _iota2((q, q), 1)
        xd = x_ref[...] * ck.dt_l
        xd_b = xd.astype(BF16)
        s_prev = s_ref[...]
        y_state = _dot(c_t, s_prev.astype(BF16), _NT) * jnp.exp(ck.cum_l)
        for i in range(npair):
            sl = slice(i * LANES, (i + 1) * LANES)
            wm = jnp.concatenate([(ck.scores * ck.decay(2 * i + hh)).astype(BF16) for hh in range(2)], axis=1)
            y_ref[:, sl] = y_state[:, sl] + _dot(wm, _by_head(xd_b[:, sl], lane))
        xw_b = (xd * jnp.exp(ck.rem_l)).astype(BF16)
        s_ref[...] = ck.grow * s_prev + _dot(xw_b, b_t, _TN)

    outs, carried = _hosted_call(
        body, name=name, grid=(SSD_GROUPS, nc),
        in_specs=[pl.BlockSpec((q, wg), lambda g, c: (c, g)),
                  pl.BlockSpec((q, SSD_STATE), lambda g, c: (c, b_blk0 + g)),
                  pl.BlockSpec((q, SSD_STATE), lambda g, c: (c, c_blk0 + g)),
                  pl.BlockSpec((q, LANES), lambda g, c: (c, g)),
                  pl.BlockSpec((1, LANES), lambda g, c: (0, g)),
                  pl.BlockSpec((1, LANES), lambda g, c: (0, g))],
        out_specs=[pl.BlockSpec((q, wg), lambda g, c: (c, g)),
                   pl.BlockSpec((1, 1, wg, SSD_STATE), lambda g, c: (c, g, 0, 0))],
        out_shape=[jax.ShapeDtypeStruct((length, SSD_GROUPS * wg), F32),
                   jax.ShapeDtypeStruct((nc, SSD_GROUPS, wg, SSD_STATE), F32)],
        scratch_shapes=[pltpu.VMEM((wg, SSD_STATE), F32)],
        sem=("parallel", "arbitrary"), args=(xc, xc, xc, dtraw, dtb, ap), rider=rider)
    return outs[0], outs[1], carried


def _ssd_bwd(dy, dxa, xc, dtraw, states, dtb, ap, n_heads, name, rider=None):
    length = xc.shape[0]
    q = SSD_CHUNK
    nc = length // q
    hpg, wg, _, b_blk0 = _ssd_tiles(xc, n_heads)
    c_blk0 = b_blk0 + SSD_GROUPS
    npair = hpg // 2

    def body(dy_ref, dxa_ref, x_ref, b_ref, c_ref, dt_ref, st_ref, dtb_ref, ap_ref,
             dx_ref, db_ref, dc_ref, ddt_ref, ddtb_ref, dap_ref, ds_ref, el_ref, er_ref):
        @pl.when(pl.program_id(1) == 0)
        def _():
            ds_ref[...] = jnp.zeros_like(ds_ref)
            ddtb_ref[...] = jnp.zeros_like(ddtb_ref)
            dap_ref[...] = jnp.zeros_like(dap_ref)

        b_t = b_ref[...].astype(BF16)
        c_t = c_ref[...].astype(BF16)
        dtraw_t = dt_ref[...]
        ck = _SsdChunk(dtraw_t, dtb_ref[...], ap_ref[...], b_t, c_t, wg)
        lane = _iota2((q, q), 1)
        to_head = _head_of_lane(wg)

        def per_head(v):
            return _pdot(_split(v, 2), to_head, _NT)

        x_all, dy_all = x_ref[...], dy_ref[...]
        dy_b = dy_all.astype(BF16)
        xd = x_all * ck.dt_l
        xd_b = xd.astype(BF16)
        s_prev = st_ref[0, 0]
        sp_b = s_prev.astype(BF16)
        ds1 = ds_ref[...]
        ds1_b = ds1.astype(BF16)
        ecum, wrem = jnp.exp(ck.cum_l), jnp.exp(ck.rem_l)
        dscores = jnp.zeros((q, q), F32)
        for i in range(npair):
            sl = slice(i * LANES, (i + 1) * LANES)
            dym = _by_head(dy_b[:, sl], lane)
            dwm2 = _dot(dym, xd_b[:, sl], _NT)
            wms = []
            for hh in range(2):
                h = 2 * i + hh
                decay = ck.decay(h)
                wm = ck.scores * decay
                dwm = dwm2[hh * q:(hh + 1) * q]
                dscores = dscores + dwm * decay
                e = (dwm * wm).astype(BF16)
                el_ref[:, h * q:(h + 1) * q] = e
                er_ref[h * q:(h + 1) * q, :] = e
                wms.append(wm.astype(BF16))
            dx_ref[:, sl] = _dot(jnp.concatenate(wms, axis=0), dym, _TN)
        put = (_iota2((hpg * q, q), 1) == _iota2((hpg * q, q), 0) // q).astype(BF16)
        dcum = _dot(el_ref[...], put) - _dot(er_ref[...], put, _TN)
        t_mat = _dot(c_t, sp_b, _NT)
        d_t = (dy_all * ecum).astype(BF16)
        dc_acc = _dot(d_t, sp_b)
        ds_prev = _dot(d_t, c_t, _TN)
        dcum = dcum + per_head(dy_all * t_mat * ecum)
        ds_prev = ds_prev + ck.grow * ds1
        zs = jnp.sum(ds1 * s_prev * ck.grow, axis=1, keepdims=True)
        dtot = _pdotr(jnp.ones((q, wg), BF16), _split(zs * _head_of_row(wg, F32), 2))
        xw = xd * wrem
        dxw = _dot(b_t, ds1_b, _NT)
        db_acc = _dot(xw.astype(BF16), ds1_b)
        dxd = dx_ref[...] + dxw * wrem
        drem = per_head(dxw * xw)
        dx_ref[...] = dxd * ck.dt_l + dxa_ref[...]
        ddt = per_head(dxd * x_all)
        ds_ref[...] = ds_prev
        ds_b = dscores.astype(BF16)
        dc_ref[...] = dc_acc + _dot(ds_b, b_t)
        db_ref[...] = db_acc + _dot(ds_b, c_t, _TN)
        dla = (_dot3r(ck.tril.astype(BF16), dcum, _TN) + _dot3r(ck.upper.astype(BF16), drem, _TN) + dtot)
        dt = ck.dt
        ddt = ddt + dla * ap_ref[...]
        dap_ref[...] += jnp.sum(dla * dt, axis=0, keepdims=True)
        ddtraw = ddt * _sigmoid(dtraw_t + dtb_ref[...])
        ddt_ref[...] = ddtraw.astype(ddt_ref.dtype)
        ddtb_ref[...] += jnp.sum(ddtraw, axis=0, keepdims=True)

    rev = lambda g, c: (nc - 1 - c, g)
    outs, carried = _hosted_call(
        body, name=name, grid=(SSD_GROUPS, nc),
        in_specs=[pl.BlockSpec((q, wg), rev),
                  pl.BlockSpec((q, wg), rev),
                  pl.BlockSpec((q, wg), rev),
                  pl.BlockSpec((q, SSD_STATE), lambda g, c: (nc - 1 - c, b_blk0 + g)),
                  pl.BlockSpec((q, SSD_STATE), lambda g, c: (nc - 1 - c, c_blk0 + g)),
                  pl.BlockSpec((q, LANES), rev),
                  pl.BlockSpec((1, 1, wg, SSD_STATE), lambda g, c: (nc - 1 - c, g, 0, 0)),
                  pl.BlockSpec((1, LANES), lambda g, c: (0, g)),
                  pl.BlockSpec((1, LANES), lambda g, c: (0, g))],
        out_specs=[pl.BlockSpec((q, wg), rev),
                   pl.BlockSpec((q, SSD_STATE), rev),
                   pl.BlockSpec((q, SSD_STATE), rev),
                   pl.BlockSpec((q, LANES), rev),
                   pl.BlockSpec((1, LANES), lambda g, c: (0, g)),
                   pl.BlockSpec((1, LANES), lambda g, c: (0, g))],
        out_shape=[jax.ShapeDtypeStruct((length, SSD_GROUPS * wg), F32),
                   jax.ShapeDtypeStruct((length, SSD_GROUPS * SSD_STATE), F32),
                   jax.ShapeDtypeStruct((length, SSD_GROUPS * SSD_STATE), F32),
                   jax.ShapeDtypeStruct((length, SSD_GROUPS * LANES), BF16),
                   jax.ShapeDtypeStruct((1, SSD_GROUPS * LANES), F32),
                   jax.ShapeDtypeStruct((1, SSD_GROUPS * LANES), F32)],
        scratch_shapes=[pltpu.VMEM((wg, SSD_STATE), F32), pltpu.VMEM((q, hpg * q), BF16),
                        pltpu.VMEM((hpg * q, q), BF16)],
        sem=("parallel", "arbitrary"), args=(dy, dxa, xc, xc, xc, dtraw, states, dtb, ap), rider=rider)
    return tuple(outs) + (carried,)


def _fox_cumsum(fraw, bf, name):
    length = fraw.shape[0]
    q = 128

    def body(f_ref, b_ref, o_ref, carry_ref):
        @pl.when(pl.program_id(0) == 0)
        def _():
            carry_ref[...] = jnp.zeros_like(carry_ref)

        lf = -_softplus(-(f_ref[...] + b_ref[...]))
        r, c = _iota2((q, q), 0), _iota2((q, q), 1)
        o_ref[...] = _dot3r((c <= r).astype(BF16), lf) + carry_ref[...]
        carry_ref[...] += jnp.sum(lf, axis=0, keepdims=True)

    return pl.pallas_call(
        body, name=name, grid=(length // q,),
        in_specs=[pl.BlockSpec((q, LANES), lambda i: (i, 0)), pl.BlockSpec((1, LANES), lambda i: (0, 0))],
        out_specs=pl.BlockSpec((q, LANES), lambda i: (i, 0)),
        out_shape=jax.ShapeDtypeStruct((length, LANES), F32),
        scratch_shapes=[pltpu.VMEM((1, LANES), F32)],
        compiler_params=_params(("arbitrary",)),
    )(fraw, bf)


def _fox_cumsum_bwd(dc, fraw, bf, name):
    length = fraw.shape[0]
    heads = dc.shape[0]
    q = 128
    nt = length // q

    def body(d_ref, f_ref, b_ref, o_ref, db_ref, carry_ref):
        @pl.when(pl.program_id(0) == 0)
        def _():
            carry_ref[...] = jnp.zeros_like(carry_ref)
            db_ref[...] = jnp.zeros_like(db_ref)

        d = jnp.concatenate([d_ref[...], jnp.zeros((LANES - heads, q), F32)], axis=0).T
        r, c = _iota2((q, q), 0), _iota2((q, q), 1)
        dlf = _dot3r((c >= r).astype(BF16), d) + carry_ref[...]
        carry_ref[...] += jnp.sum(d, axis=0, keepdims=True)
        df = dlf * _sigmoid(-(f_ref[...] + b_ref[...]))
        o_ref[...] = df.astype(o_ref.dtype)
        db_ref[...] += jnp.sum(df, axis=0, keepdims=True)

    rev = lambda i: (nt - 1 - i, 0)
    return pl.pallas_call(
        body, name=name, grid=(nt,),
        in_specs=[pl.BlockSpec((heads, q), lambda i: (0, nt - 1 - i)), pl.BlockSpec((q, LANES), rev),
                  pl.BlockSpec((1, LANES), lambda i: (0, 0))],
        out_specs=[pl.BlockSpec((q, LANES), rev), pl.BlockSpec((1, LANES), lambda i: (0, 0))],
        out_shape=[jax.ShapeDtypeStruct((length, LANES), BF16), jax.ShapeDtypeStruct((1, LANES), F32)],
        scratch_shapes=[pltpu.VMEM((1, LANES), F32)],
        compiler_params=_params(("arbitrary",)),
    )(dc, fraw, bf)


def _fox_scores(q_ref, k_ref, cq_ref, ck_ref, diagonal):
    scale = 1.0 / math.sqrt(FOX_HEAD_DIM)
    s = _dot(q_ref[...].astype(BF16), k_ref[...].astype(BF16), _NT) * scale + (cq_ref[0] - ck_ref[0])
    if diagonal:
        s = jnp.where(_iota2(s.shape, 1) <= _iota2(s.shape, 0), s, -jnp.inf)
    return s


def _fox_tiles(i, j, step):
    @pl.when(j < i)
    def _():
        step(False)

    @pl.when(j == i)
    def _():
        step(True)


def _fox_fwd(qkvg, c_col, c_row, n_heads, name):
    length = qkvg.shape[0]
    hd = FOX_HEAD_DIM
    tq = _pick(length, (FOX_TILE, 512, 256, 128))
    nq = length // tq

    def body(q_ref, k_ref, v_ref, cq_ref, ck_ref, o_ref, lse_ref, m_ref, l_ref, acc_ref):
        i, j = pl.program_id(1), pl.program_id(2)

        @pl.when(j == 0)
        def _():
            m_ref[...] = jnp.full_like(m_ref, -jnp.inf)
            l_ref[...] = jnp.zeros_like(l_ref)
            acc_ref[...] = jnp.zeros_like(acc_ref)

        def step(diagonal):
            s = _fox_scores(q_ref, k_ref, cq_ref, ck_ref, diagonal)
            m_new = jnp.maximum(m_ref[...], jnp.max(s, axis=1, keepdims=True))
            alpha = jnp.exp(m_ref[...] - m_new)
            p = jnp.exp(s - m_new)
            l_ref[...] = alpha * l_ref[...] + jnp.sum(p, axis=1, keepdims=True)
            acc_ref[...] = alpha * acc_ref[...] + _dot(p.astype(BF16), v_ref[...].astype(BF16))
            m_ref[...] = m_new

        _fox_tiles(i, j, step)

        @pl.when(j == nq - 1)
        def _():
            o_ref[...] = acc_ref[...] / l_ref[...]
            lse_ref[0] = m_ref[...] + jnp.log(l_ref[...])

    kmap = lambda off: (lambda h, i, j: (jnp.minimum(j, i), off * n_heads + h))
    return pl.pallas_call(
        body, name=name, grid=(n_heads, nq, nq),
        in_specs=[pl.BlockSpec((tq, hd), lambda h, i, j: (i, h)),
                  pl.BlockSpec((tq, hd), kmap(1)),
                  pl.BlockSpec((tq, hd), kmap(2)),
                  pl.BlockSpec((1, tq, 1), lambda h, i, j: (h, i, 0)),
                  pl.BlockSpec((1, 1, tq), lambda h, i, j: (h, 0, jnp.minimum(j, i)))],
        out_specs=[pl.BlockSpec((tq, hd), lambda h, i, j: (i, h)),
                   pl.BlockSpec((1, tq, 1), lambda h, i, j: (h, i, 0))],
        out_shape=[jax.ShapeDtypeStruct((length, n_heads * hd), F32),
                   jax.ShapeDtypeStruct((n_heads, length, 1), F32)],
        scratch_shapes=[pltpu.VMEM((tq, 1), F32), pltpu.VMEM((tq, 1), F32), pltpu.VMEM((tq, hd), F32)],
        compiler_params=_params(("parallel", "parallel", "arbitrary")),
    )(qkvg, qkvg, qkvg, c_col, c_row)


def _fox_bwd_q(qkvg, datt, lse, c_col, c_row, n_heads, name, rider=None):
    length = qkvg.shape[0]
    hd = FOX_HEAD_DIM
    tq = _pick(length, (FOX_TILE, 512, 256, 128))
    nq = length // tq
    scale = 1.0 / math.sqrt(hd)

    def body(q_ref, k_ref, v_ref, do_ref, lse_ref, cq_ref, ck_ref, dq_ref, dsum_ref, a1_ref, a2_ref, d_ref):
        i, j = pl.program_id(1), pl.program_id(2)

        @pl.when(j == 0)
        def _():
            a1_ref[...] = jnp.zeros_like(a1_ref)
            a2_ref[...] = jnp.zeros_like(a2_ref)
            d_ref[...] = jnp.zeros_like(d_ref)

        def step(diagonal):
            s = _fox_scores(q_ref, k_ref, cq_ref, ck_ref, diagonal)
            p = jnp.exp(s - lse_ref[0])
            pdp = p * _dot(do_ref[...].astype(BF16), v_ref[...].astype(BF16), _NT)
            d_ref[...] += jnp.sum(pdp, axis=1, keepdims=True)
            k_b = k_ref[...].astype(BF16)
            a1_ref[...] += _dot(pdp.astype(BF16), k_b)
            a2_ref[...] += _dot(p.astype(BF16), k_b)

        _fox_tiles(i, j, step)

        @pl.when(j == nq - 1)
        def _():
            dq_ref[...] = ((a1_ref[...] - d_ref[...] * a2_ref[...]) * scale).astype(dq_ref.dtype)
            dsum_ref[0] = d_ref[...]

    kmap = lambda off: (lambda h, i, j: (jnp.minimum(j, i), off * n_heads + h))
    qmap = lambda h, i, j: (i, h)
    col = pl.BlockSpec((1, tq, 1), lambda h, i, j: (h, i, 0))
    outs, carried = _hosted_call(
        body, name=name, grid=(n_heads, nq, nq),
        in_specs=[pl.BlockSpec((tq, hd), qmap), pl.BlockSpec((tq, hd), kmap(1)), pl.BlockSpec((tq, hd), kmap(2)),
                  pl.BlockSpec((tq, hd), qmap), col, col,
                  pl.BlockSpec((1, 1, tq), lambda h, i, j: (h, 0, jnp.minimum(j, i)))],
        out_specs=[pl.BlockSpec((tq, hd), qmap), col],
        out_shape=[jax.ShapeDtypeStruct((length, n_heads * hd), BF16),
                   jax.ShapeDtypeStruct((n_heads, length, 1), F32)],
        scratch_shapes=[pltpu.VMEM((tq, hd), F32), pltpu.VMEM((tq, hd), F32), pltpu.VMEM((tq, 1), F32)],
        sem=("parallel", "parallel", "arbitrary"), args=(qkvg, qkvg, qkvg, datt, lse, c_col, c_row), rider=rider)
    return outs[0], outs[1], carried


def _fox_bwd_kv(qkvg, datt, lse, dsum, c_col, c_row, n_heads, name):
    length = qkvg.shape[0]
    hd = FOX_HEAD_DIM
    tq = _pick(length, (FOX_TILE, 512, 256, 128))
    nq = length // tq
    scale = 1.0 / math.sqrt(hd)

    def body(q_ref, k_ref, v_ref, do_ref, lse_ref, dsum_ref, cq_ref, ck_ref, dk_ref, dv_ref, dck_ref,
             dk_acc, dv_acc, dc_acc):
        j, i = pl.program_id(1), pl.program_id(2)

        @pl.when(i == 0)
        def _():
            dk_acc[...] = jnp.zeros_like(dk_acc)
            dv_acc[...] = jnp.zeros_like(dv_acc)
            dc_acc[...] = jnp.zeros_like(dc_acc)

        def step(diagonal):
            s = _fox_scores(q_ref, k_ref, cq_ref, ck_ref, diagonal)
            p = jnp.exp(s - lse_ref[0])
            do_b = do_ref[...].astype(BF16)
            dv_acc[...] += _dot(p.astype(BF16), do_b, _TN)
            dp = _dot(do_b, v_ref[...].astype(BF16), _NT)
            ds = p * (dp - dsum_ref[0])
            dk_acc[...] += _dot(ds.astype(BF16), q_ref[...].astype(BF16), _TN)
            dc_acc[...] -= jnp.sum(ds, axis=0, keepdims=True)

        _fox_tiles(i, j, step)

        @pl.when(i == nq - 1)
        def _():
            dk_ref[...] = (dk_acc[...] * scale).astype(dk_ref.dtype)
            dv_ref[...] = dv_acc[...].astype(dv_ref.dtype)
            dck_ref[0] = dc_acc[...]

    qmap = lambda h, j, i: (jnp.maximum(i, j), h)
    kmap = lambda off: (lambda h, j, i: (j, off * n_heads + h))
    col = pl.BlockSpec((1, tq, 1), lambda h, j, i: (h, jnp.maximum(i, j), 0))
    return pl.pallas_call(
        body, name=name, grid=(n_heads, nq, nq),
        in_specs=[pl.BlockSpec((tq, hd), qmap), pl.BlockSpec((tq, hd), kmap(1)), pl.BlockSpec((tq, hd), kmap(2)),
                  pl.BlockSpec((tq, hd), qmap), col, col, col,
                  pl.BlockSpec((1, 1, tq), lambda h, j, i: (h, 0, j))],
        out_specs=[pl.BlockSpec((tq, hd), lambda h, j, i: (j, h)), pl.BlockSpec((tq, hd), lambda h, j, i: (j, h)),
                   pl.BlockSpec((1, 1, tq), lambda h, j, i: (h, 0, j))],
        out_shape=[jax.ShapeDtypeStruct((length, n_heads * hd), BF16)] * 2
        + [jax.ShapeDtypeStruct((n_heads, 1, length), F32)],
        scratch_shapes=[pltpu.VMEM((tq, hd), F32), pltpu.VMEM((tq, hd), F32), pltpu.VMEM((1, tq), F32)],
        compiler_params=_params(("parallel", "parallel", "arbitrary")),
    )(qkvg, qkvg, qkvg, datt, lse, dsum, c_col, c_row)


def _row(v):
    return v.reshape(1, -1).astype(F32)


def _pad_heads(v, per_group):
    lead = v.shape[:-1]
    v = v.reshape(lead + (SSD_GROUPS, per_group))
    v = jnp.pad(v, [(0, 0)] * len(lead) + [(0, 0), (0, LANES - per_group)])
    return v.reshape(lead + (SSD_GROUPS * LANES,))


def _unpad_heads(v, per_group):
    lead = v.shape[:-1]
    return v.reshape(lead + (SSD_GROUPS, LANES))[..., :per_group].reshape(lead + (SSD_GROUPS * per_group,))


class _NoOverlap:
    def gather_rider(self, host):
        return None

    def gathered(self, host, carried):
        return {}

    def reduce_rider(self, host, grads):
        return None

    def reduced(self, host, carried):
        pass


def _pad_head_rows(w, per_group):
    w = w.reshape(SSD_GROUPS, per_group, w.shape[1])
    return jnp.pad(w, ((0, 0), (0, LANES - per_group), (0, 0))).reshape(SSD_GROUPS * LANES, w.shape[2])


def _unpad_head_rows(w, per_group):
    return w.reshape(SSD_GROUPS, LANES, w.shape[1])[:, :per_group].reshape(SSD_GROUPS * per_group, w.shape[1])


def _local_step(x, tgt, wb, sm, plan=None):
    plan = plan or _NoOverlap()
    wb = dict(wb)
    length, d = x.shape
    mix = 2 * d
    s5w = mix // 4
    ssdw = mix - s5w
    xbcw = ssdw + 2 * SSD_GROUPS * SSD_STATE
    n_ssd = ssdw // SSD_HEAD_DIM
    hpg = n_ssd // SSD_GROUPS
    fw = d
    o1, o2, o3 = 2 * s5w, 2 * s5w + ssdw, 2 * s5w + ssdw + xbcw
    w0t = wb["w0T"]
    w0_dt = _pad_head_rows(w0t[o3:], hpg)
    n_fox = fw // FOX_HEAD_DIM
    s5g = s5w // S5_GROUP
    s5s = s5g * S5_STATE
    grads = {}

    s5_in = (sm["l0_s5_lambda_re"], sm["l0_s5_lambda_im"], sm["l0_s5_log_step"], sm["l0_s5_b_re"],
             sm["l0_s5_b_im"], sm["l0_s5_c_re"], sm["l0_s5_c_im"])
    (a_re, a_im, bd_re, bd_im, cd_re, cd_imn), s5_vjp = jax.vjp(_s5_prepare, *s5_in)
    nj = bd_re.shape[0]
    bd_re_b, bd_im_b, cd_re_b, cd_imn_b = (t.astype(BF16) for t in (bd_re, bd_im, cd_re, cd_imn))
    tr = lambda t: jnp.swapaxes(t, 1, 2)
    dvec = _row(sm["l0_s5_d"])
    bglu = _row(sm["l0_s5_b_glu"])
    conv_w = sm["l0_ssd_conv_w"]
    conv_b = _row(sm["l0_ssd_conv_b"])

    def ssd_prepare(dt_bias, a_log, dd):
        return (_pad_heads(_row(dt_bias), hpg), _pad_heads(_row(-jnp.exp(a_log)), hpg),
                jnp.repeat(_row(dd), SSD_HEAD_DIM, axis=1))

    (dtb, ap, dpar), ssd_vjp = jax.vjp(ssd_prepare, sm["l0_ssd_dt_bias"], sm["l0_ssd_a_log"], sm["l0_ssd_d"])
    ssd_nw = _row(sm["l0_ssd_norm_w"])
    nw0, nw1, fnw = _row(sm["l0_norm_w"]), _row(sm["l1_norm_w"]), _row(sm["final_norm_w"])
    bf = jnp.pad(_row(sm["l1_fox_b_f"]), ((0, 0), (0, LANES - n_fox)))

    h0 = _norm_fwd(x, nw0, "l0_norm")
    def gathering(host, *args, **kw):
        rider = plan.gather_rider(host)
        out = _matmul(*args, name=host, rider=rider, **kw)
        if rider is None:
            return out
        wb.update(plan.gathered(host, out[1]))
        return out[0]

    def reducing(host, *args, **kw):
        rider = plan.reduce_rider(host, grads)
        out = _matmul(*args, name=host, rider=rider, **kw)
        if rider is None:
            return out
        plan.reduced(host, out[1])
        return out[0]

    ug = gathering("l0_in_ug", h0, w0t, mode="nt", dims=(length, o1, d))
    z = gathering("l0_in_z", h0, w0t, mode="nt", dims=(length, ssdw, d), b_off=(o1, 0))
    xbc = gathering("l0_in_xbc", h0, w0t, mode="nt", dims=(length, xbcw, d), b_off=(o2, 0))
    dtraw = _matmul(h0, w0_dt, mode="nt", name="l0_in_dt")
    u_win, gate_win = (ug, s5w, 0), (ug, s5w, 1)

    shape3 = (length, s5s // LANES, LANES)
    bu_re = _bdmm([(ug, 0)], [bd_re_b], name="s5_bu_re").reshape(shape3)
    bu_im = _bdmm([(ug, 0)], [bd_im_b], name="s5_bu_im").reshape(shape3)
    s_re3, s_im3 = _s5_scan_fwd(bu_re, bu_im, a_re, a_im, "s5_scan")
    s_re, s_im = s_re3.reshape(length, s5s), s_im3.reshape(length, s5s)
    yc = _bdmm([s_re, s_im], [cd_re_b, cd_imn_b], name="s5_y")
    gl = _rowwise(_f_s5_gelu, [yc, u_win], [dvec], [(s5w, BF16)], tl=256, name="s5_gelu")[0]
    t_glu = _matmul(gl, wb["w_glu"], name="s5_glu")
    s5o = _rowwise(_f_s5_out, [yc, u_win, t_glu, gate_win], [dvec, bglu], [(s5w, BF16)], tl=256,
                   name="s5_out")[0]

    xc = _conv_fwd(xbc, conv_w, conv_b, "ssd_conv")
    y_ssd, states, carried = _ssd_fwd(xc, dtraw, dtb, ap, n_ssd, "ssd_scan", rider=plan.gather_rider("ssd_scan"))
    wb.update(plan.gathered("ssd_scan", carried))
    wg = ssdw // SSD_GROUPS
    ssdo = _rowwise(_f_ssd_out, [y_ssd, (xc, wg, 0), z], [dpar, ssd_nw], [(wg, BF16)], tl=256,
                    ncol=SSD_GROUPS, name="ssd_out")[0]
    x1 = _matmul(s5o, wb["w0_out"], dims=(length, d, s5w), addend=x, name="l0_out_s5")
    x1 = _matmul(ssdo, wb["w0_out"], dims=(length, d, ssdw), b_off=(s5w, 0), addend=x1, name="l0_out_ssd")

    h1 = _norm_fwd(x1, nw1, "l1_norm")
    w1t = wb["w1T"]
    w1_f = jnp.pad(w1t[4 * fw:], ((0, LANES - n_fox), (0, 0)))
    qkvg = _matmul(h1, w1t, mode="nt", dims=(length, 4 * fw, d), name="l1_in")
    fraw = _matmul(h1, w1_f, mode="nt", name="l1_in_f")
    cc = _fox_cumsum(fraw, bf, "fox_cumsum")
    c_t = cc[:, :n_fox].T
    c_col, c_row = c_t[:, :, None], c_t[:, None, :]
    att, lse = _fox_fwd(qkvg, c_col, c_row, n_fox, "fox_fwd")
    gate1_win = (qkvg, fw, 3)
    fox_o = _rowwise(_f_fox_out, [att, gate1_win], [], [(fw, BF16)], tl=256, name="fox_out")[0]
    x2 = _matmul(fox_o, wb["w1_out"], addend=x1, name="l1_out")

    def loss_fn(xt, tt, wt):
        def f(xx, ww):
            err = _f_norm(xx, ww) - tt
            return (0.5 / d) * err * err
        lanes, vjp = jax.vjp(f, xt, wt)
        dx, dw = vjp(jnp.ones_like(lanes))
        return dx, dx, jnp.sum(lanes, axis=0, keepdims=True), dw

    dx2, dx2b, loss_lanes, g_fnw = _rowwise(loss_fn, [x2, tgt], [fnw], [(d, F32), (d, BF16)], [d, d],
                                            tl=128, name="loss_head")
    grads["final_norm_w"] = g_fnw

    grads["l1_w_out"] = _matmul(fox_o, dx2b, mode="tn", name="l1_out_dw")
    do1 = _matmul(dx2b, wb["w1_out"], mode="nt", name="l1_out_dx")

    def fox_out_bwd(at, gt, dt_):
        _, vjp = jax.vjp(_f_fox_out, at, gt)
        return vjp(dt_)

    datt, dgate1 = _rowwise(fox_out_bwd, [att, gate1_win, do1], [], [(fw, F32), (fw, BF16)], tl=256,
                            name="fox_out_bwd")
    dq, dsum, carried = _fox_bwd_q(qkvg, datt, lse, c_col, c_row, n_fox, "fox_bwd_q",
                                   rider=plan.reduce_rider("fox_bwd_q", grads))
    plan.reduced("fox_bwd_q", carried)
    dk, dv, dck = _fox_bwd_kv(qkvg, datt, lse, dsum, c_col, c_row, n_fox, "fox_bwd_kv")
    dfraw, g_bf = _fox_cumsum_bwd(dck.reshape(n_fox, length), fraw, bf, "fox_cumsum_bwd")
    grads["l1_fox_b_f"] = g_bf[:, :n_fox]
    dsegs = [dq, dk, dv, dgate1]
    g1, n1 = None, w1t.shape[0]
    for i, s in enumerate(dsegs):
        g1 = _matmul(s, h1, mode="tn", rows_of=(n1, i * fw, g1), name=f"l1_in_dw{i}")
    g1_f = _matmul(dfraw, h1, mode="tn", name="l1_in_dwf")[:n_fox]
    grads["l1_w_inT"] = lax.dynamic_update_slice(g1, g1_f, (4 * fw, 0))
    dh1 = _matmul(dfraw, w1_f, mode="nn", name="l1_in_dxf")
    for i, s in enumerate(dsegs):
        dh1 = _matmul(s, w1t, mode="nn", dims=(length, d, fw), b_off=(i * fw, 0), addend=dh1,
                      name=f"l1_in_dx{i}")
    dx1, dx1b, grads["l1_norm_w"] = _norm_bwd(x1, dh1, dx2, nw1, "l1_norm_bwd")

    g_out = _matmul(s5o, dx1b, mode="tn", rows_of=(mix, 0, None), name="l0_out_dw_s5")
    grads["l0_w_out"] = _matmul(ssdo, dx1b, mode="tn", rows_of=(mix, s5w, g_out), name="l0_out_dw_ssd")
    ds5o = _matmul(dx1b, wb["w0_out"], mode="nt", dims=(length, s5w, d), name="l0_out_dx_s5")
    dssdo = reducing("l0_out_dx_ssd", dx1b, wb["w0_out"], mode="nt", dims=(length, ssdw, d), b_off=(s5w, 0))

    def ssd_out_bwd(yt, xt, zt, dt_, dp, nw):
        _, vjp = jax.vjp(_f_ssd_out, yt, xt, zt, dp, nw)
        return vjp(dt_)

    dy_ssd, dxa, dz, g_dpar, g_ssd_nw = _rowwise(
        ssd_out_bwd, [y_ssd, (xc, wg, 0), z, dssdo], [dpar, ssd_nw],
        [(wg, F32), (wg, F32), (wg, BF16)], [wg, wg], tl=128, ncol=SSD_GROUPS, name="ssd_out_bwd")
    grads["l0_ssd_norm_w"] = g_ssd_nw
    dxs, db_ssd, dc_ssd, ddtraw, g_dtb, g_ap, carried = _ssd_bwd(
        dy_ssd, dxa, xc, dtraw, states, dtb, ap, n_ssd, "ssd_scan_bwd",
        rider=plan.reduce_rider("ssd_scan_bwd", grads))
    plan.reduced("ssd_scan_bwd", carried)
    g_dt_bias, g_a_log, g_ssd_d = ssd_vjp((g_dtb, g_ap, g_dpar))
    grads["l0_ssd_dt_bias"], grads["l0_ssd_a_log"], grads["l0_ssd_d"] = g_dt_bias, g_a_log, g_ssd_d
    dxc = jnp.concatenate([dxs, db_ssd, dc_ssd], axis=1)
    dpre, grads["l0_ssd_conv_w"], grads["l0_ssd_conv_b"] = _conv_bwd_pre(dxc, xbc, conv_w, conv_b, "ssd_conv_bwd_pre")
    dxbc = _conv_bwd_in(dpre, conv_w, "ssd_conv_bwd_in")

    def s5_out_bwd(yt, ut, tt, gt, dt_, dv_, bg):
        _, vjp = jax.vjp(_f_s5_out, yt, ut, tt, gt, dv_, bg)
        return vjp(dt_)

    dyc_a, du_a, dt_glu, dgate, g_dvec_a, g_bglu = _rowwise(
        s5_out_bwd, [yc, u_win, t_glu, gate_win, ds5o], [dvec, bglu],
        [(s5w, F32), (s5w, F32), (s5w, BF16), (s5w, BF16)], [s5w, s5w], tl=128, name="s5_out_bwd")
    grads["l0_s5_b_glu"] = g_bglu
    grads["l0_s5_w_glu"] = _matmul(gl, dt_glu, mode="tn", name="s5_glu_dw")
    dgl = _matmul(dt_glu, wb["w_glu"], mode="nt", name="s5_glu_dx")

    def s5_gelu_bwd(yt, ut, dg, dya, dua, dv_):
        _, vjp = jax.vjp(_f_s5_gelu, yt, ut, dv_)
        dy_, du_, ddv = vjp(dg)
        return dy_ + dya, du_ + dua, ddv

    dyc, du_ab, g_dvec_b = _rowwise(s5_gelu_bwd, [yc, u_win, dgl, dyc_a, du_a], [dvec],
                                    [(s5w, F32), (s5w, F32)], [s5w], tl=128, name="s5_gelu_bwd")
    ds_re = _bdmm([dyc], [tr(cd_re_b)], name="s5_ds_re").reshape(shape3)
    ds_im = _bdmm([dyc], [tr(cd_imn_b)], name="s5_ds_im").reshape(shape3)
    kin_s, kin_u = s5s // nj, s5w // nj
    g_cd_re = _bdmm_tn_sized(s_re, dyc, nj, kin_s, kin_u, 0, "s5_dcd_re")
    g_cd_imn = _bdmm_tn_sized(s_im, dyc, nj, kin_s, kin_u, 0, "s5_dcd_im")
    g_re3, g_im3, g_a_re, g_a_im = _s5_scan_bwd(ds_re, ds_im, s_re3, s_im3, a_re, a_im, "s5_scan_bwd")
    g_re, g_im = g_re3.reshape(length, s5s), g_im3.reshape(length, s5s)
    du = _bdmm([g_re, g_im], [tr(bd_re_b), tr(bd_im_b)], addend=du_ab, out_dtype=BF16, name="s5_du")
    g_bd_re = _bdmm_tn_sized(ug, g_re, nj, kin_u, kin_s, 0, "s5_dbd_re")
    g_bd_im = _bdmm_tn_sized(ug, g_im, nj, kin_u, kin_s, 0, "s5_dbd_im")
    s5_g = s5_vjp((g_a_re, g_a_im, g_bd_re, g_bd_im, g_cd_re, g_cd_imn))
    for nm, g in zip(("lambda_re", "lambda_im", "log_step", "b_re", "b_im", "c_re", "c_im"), s5_g):
        grads["l0_s5_" + nm] = g
    grads["l0_s5_d"] = (g_dvec_a + g_dvec_b).reshape(sm["l0_s5_d"].shape)

    g0, n0 = None, w0t.shape[0]
    for nm, s, off in (("u", du, 0), ("g", dgate, s5w), ("z", dz, o1), ("xbc", dxbc, o2)):
        g0 = _matmul(s, h0, mode="tn", rows_of=(n0, off, g0), name="l0_in_dw_" + nm)
    g0_dt = _unpad_head_rows(_matmul(ddtraw, h0, mode="tn", name="l0_in_dw_dt"), hpg)
    grads["l0_w_inT"] = lax.dynamic_update_slice(g0, g0_dt, (o3, 0))
    dh0 = _matmul(ddtraw, w0_dt, mode="nn", name="l0_in_dx_dt")
    for nm, s, off in (("u", du, 0), ("g", dgate, s5w), ("z", dz, o1), ("xbc", dxbc, o2)):
        dh0 = reducing("l0_in_dx_" + nm, s, w0t, mode="nn", dims=(length, d, s.shape[1]), b_off=(off, 0),
                       addend=dh0)
    dx, _, grads["l0_norm_w"] = _norm_bwd(x, dh0, dx1, nw0, "l0_norm_bwd")
    return loss_lanes, dx, grads


_ANY = pl.BlockSpec(memory_space=pl.ANY)


def _place():
    x, y, c = lax.axis_index("x"), lax.axis_index("y"), lax.axis_index("c")
    return x, y, c, [(1 - x, y), (x, 1 - y), (1 - x, 1 - y)]


def _remote(src, dst, send_sem, recv_sem, to):
    return pltpu.make_async_remote_copy(src_ref=src, dst_ref=dst, send_sem=send_sem, recv_sem=recv_sem,
                                        device_id=to, device_id_type=MESH)


def _comm_call(body, n_in, out_shape, n_sems, name):
    return pl.pallas_call(
        body, name=name, in_specs=[_ANY] * n_in, out_specs=[_ANY] * len(out_shape), out_shape=out_shape,
        scratch_shapes=[pltpu.SemaphoreType.DMA((k,)) for k in n_sems],
        compiler_params=pltpu.CompilerParams(has_side_effects=True),
    )


def _half(ref_or_shape, c):
    ch = ref_or_shape.shape[-1] // 2
    return pl.ds(pl.multiple_of(c * ch, LANES), ch)


def _gather_rider(shards):
    n = len(shards)

    def sends(ins, outs, sems):
        send, recv = sems[:2]
        x, y, c, chips = _place()
        me = 2 * x + y
        return [_remote(ins[a].at[:, _half(ins[a], c)], outs[a].at[me, :, _half(ins[a], c)],
                        send.at[3 * a + k], recv.at[3 * a + k], (px, py, c))
                for a in range(n) for k, (px, py) in enumerate(chips)]

    def start(ins, outs, sems):
        for cp in sends(ins, outs, sems):
            cp.start()

    def finish(ins, outs, sems):
        send, recv, fsend, frecv = sems
        x, y, c, chips = _place()
        passed = []
        for a in range(n):
            for k, (px, py) in enumerate(chips):
                got = outs[a].at[2 * px + py, :, _half(ins[a], c)]
                _remote(got, got, send.at[3 * a + k], recv.at[3 * a + k], (px, py, c)).wait_recv()
                cp = _remote(got, got, fsend.at[3 * a + k], frecv.at[3 * a + k], (x, y, 1 - c))
                cp.start()
                passed.append(cp)
        for a in range(n):
            for k, (px, py) in enumerate(chips):
                got = outs[a].at[2 * px + py, :, _half(ins[a], 1 - c)]
                _remote(got, got, fsend.at[3 * a + k], frecv.at[3 * a + k], (x, y, 1 - c)).wait_recv()
        for cp in sends(ins, outs, sems) + passed:
            cp.wait_send()

    out_shape = [jax.ShapeDtypeStruct((N_SHARD,) + s.shape, s.dtype) for s in shards]
    return _Rider(shards, out_shape, [3 * n] * 4, start, finish)


def _chip_rider(parts):
    n = len(parts)

    def copies(ins, outs, sems):
        send, recv = sems
        x, y, c, chips = _place()
        return [_remote(ins[a].at[2 * px + py], outs[a].at[k], send.at[3 * a + k], recv.at[3 * a + k], (px, py, c))
                for a in range(n) for k, (px, py) in enumerate(chips)]

    def start(ins, outs, sems):
        for cp in copies(ins, outs, sems):
            cp.start()

    def finish(ins, outs, sems):
        for cp in copies(ins, outs, sems):
            cp.wait()

    out_shape = [jax.ShapeDtypeStruct((3,) + p.shape[1:], p.dtype) for p in parts]
    return _Rider(parts, out_shape, [3 * n] * 2, start, finish)


def _run_rider(rider, name):
    n_in, n_out = len(rider.inputs), len(rider.out_shape)

    def body(*refs):
        ins, outs, sems = refs[:n_in], refs[n_in:n_in + n_out], refs[n_in + n_out:]
        rider.start(ins, outs, sems)
        rider.finish(ins, outs, sems)

    return _comm_call(body, n_in, rider.out_shape, rider.sems, name)(*rider.inputs)


def _sibling_halves(grads, name):
    n = len(grads)

    def body(*refs):
        ins, outs = refs[:n], refs[n:2 * n]
        send, recv = refs[2 * n:]
        x, y, c, _ = _place()
        copies = [_remote(ins[a].at[:, :, _half(ins[a], 1 - c)], outs[a], send.at[a], recv.at[a], (x, y, 1 - c))
                  for a in range(n)]
        for cp in copies:
            cp.start()
        for cp in copies:
            cp.wait()

    out_shape = [jax.ShapeDtypeStruct(g.shape[:2] + (g.shape[2] // 2,), g.dtype) for g in grads]
    return _comm_call(body, n, out_shape, [n, n], name)(*grads)


def _join_halves(halves, name):
    n = len(halves)

    def body(*refs):
        outs = refs[n:2 * n]
        send, recv = refs[2 * n:]
        x, y, c, _ = _place()
        mine = [outs[a].at[:, _half(outs[a], c)] for a in range(n)]
        copies = [_remote(mine[a], mine[a], send.at[a], recv.at[a], (x, y, 1 - c)) for a in range(n)]
        for cp in copies:
            cp.start()
        for a in range(n):
            copies[a].wait_send()
            got = outs[a].at[:, _half(outs[a], 1 - c)]
            _remote(got, got, send.at[a], recv.at[a], (x, y, 1 - c)).wait_recv()

    return pl.pallas_call(
        body, name=name, in_specs=[_ANY] * n, out_specs=[_ANY] * n,
        out_shape=[jax.ShapeDtypeStruct(h.shape, h.dtype) for h in halves],
        input_output_aliases={a: a for a in range(n)},
        scratch_shapes=[pltpu.SemaphoreType.DMA((n,)), pltpu.SemaphoreType.DMA((n,))],
        compiler_params=pltpu.CompilerParams(has_side_effects=True),
    )(*halves)


def _gather_all(buf, name):
    def body(in_ref, out_ref, send, recv, lsem):
        x, y, c, _ = _place()
        me = 4 * x + 2 * y + c
        local = pltpu.make_async_copy(in_ref, out_ref.at[me], lsem.at[0])
        local.start()
        copies = []
        for k in range(1, N_DEV):
            fx, fy, fc = (k >> 2) & 1, (k >> 1) & 1, k & 1
            peer = (x + fx - 2 * x * fx, y + fy - 2 * y * fy, c + fc - 2 * c * fc)
            cp = _remote(in_ref, out_ref.at[me], send.at[k - 1], recv.at[k - 1], peer)
            cp.start()
            copies.append((cp, 4 * peer[0] + 2 * peer[1] + peer[2]))
        for k, (cp, slot) in enumerate(copies):
            cp.wait_send()
            got = out_ref.at[slot]
            _remote(got, got, send.at[k], recv.at[k], (x, y, c)).wait_recv()
        local.wait()

    out_shape = [jax.ShapeDtypeStruct((N_DEV,) + buf.shape, buf.dtype)]
    return _comm_call(body, 1, out_shape, [N_DEV - 1, N_DEV - 1, 1], name)(buf)[0]


def _sum_slots(buf, name):
    slots, rows, _ = buf.shape
    tr = _pick(rows, (512, 256, 128, 64, 32, 16, 8))

    def body(b_ref, o_ref):
        acc = b_ref[0]
        for s in range(1, slots):
            acc = acc + b_ref[s]
        o_ref[...] = acc

    return pl.pallas_call(
        body, name=name, grid=(rows // tr,),
        in_specs=[pl.BlockSpec((slots, tr, LANES), lambda i: (0, i, 0))],
        out_specs=pl.BlockSpec((tr, LANES), lambda i: (i, 0)),
        out_shape=jax.ShapeDtypeStruct((rows, LANES), F32),
        compiler_params=_params(("parallel",)),
    )(buf)


def _tile2(rows, cols, n_bufs):
    tr = max(t for t in range(8, min(rows, 2048) + 1, 8) if rows % t == 0) if rows % 8 == 0 else rows
    budget = 24 * 1024 * 1024 // (8 * n_bufs * tr)
    tc = max([t for t in range(LANES, cols + 1, LANES) if cols % t == 0 and t <= budget] or [LANES])
    return tr, tc


def _elementwise(fn, ins, out_dtypes, name):
    rows, cols = ins[0].shape
    tr, tc = _tile2(rows, cols, len(ins) + len(out_dtypes))
    n_in = len(ins)

    def body(*refs):
        outs = fn(*[r[...] for r in refs[:n_in]])
        for r, v in zip(refs[n_in:], outs if isinstance(outs, (tuple, list)) else (outs,)):
            r[...] = v.astype(r.dtype)

    blk = pl.BlockSpec((tr, tc), lambda i, j: (i, j))
    return pl.pallas_call(
        body, name=name, grid=(rows // tr, cols // tc), in_specs=[blk] * n_in, out_specs=[blk] * len(out_dtypes),
        out_shape=[jax.ShapeDtypeStruct((rows, cols), dt) for dt in out_dtypes],
        compiler_params=_params(("parallel", "parallel")),
    )(*ins)


def _presum(grad, sib, name):
    ns, rows, ch = sib.shape
    tr, tc = _tile2(rows, ch, 3)
    nct = ch // tc

    def body(g_ref, r_ref, o_ref):
        o_ref[...] = (g_ref[...] + r_ref[...]).astype(o_ref.dtype)

    blk = pl.BlockSpec((1, tr, tc), lambda j, i, k: (j, i, k))
    return pl.pallas_call(
        body, name=name, grid=(ns, rows // tr, nct),
        in_specs=[pl.BlockSpec((1, tr, tc), lambda j, i, k: (j, i, lax.axis_index("c") * nct + k)), blk],
        out_specs=blk, out_shape=jax.ShapeDtypeStruct((ns, rows, ch), BF16),
        compiler_params=_params(("parallel", "parallel", "parallel")),
    )(grad, sib)


def _finish_half(grad, sib, others, name):
    _, rows, ch = sib.shape
    tr, tc = _tile2(rows, ch, 6)
    nct = ch // tc

    def body(g_ref, r_ref, q_ref, o_ref):
        acc = g_ref[0] + r_ref[0]
        for k in range(3):
            acc = acc + q_ref[k].astype(F32)
        o_ref[...] = acc

    core = lambda: lax.axis_index("c")
    chip = lambda: 2 * lax.axis_index("x") + lax.axis_index("y")
    return pl.pallas_call(
        body, name=name, grid=(rows // tr, nct),
        in_specs=[pl.BlockSpec((1, tr, tc), lambda i, k: (chip(), i, core() * nct + k)),
                  pl.BlockSpec((1, tr, tc), lambda i, k: (chip(), i, k)),
                  pl.BlockSpec((3, tr, tc), lambda i, k: (0, i, k))],
        out_specs=pl.BlockSpec((tr, tc), lambda i, k: (i, core() * nct + k)),
        out_shape=jax.ShapeDtypeStruct((rows, 2 * ch), F32),
        compiler_params=_params(("parallel", "parallel")),
    )(grad, sib, others)


def _cast_bf16(w, name):
    return _elementwise(lambda t: t, [w], [BF16], name)[0]


_WEIGHTS = ("l0_norm_w", "l0_w_in", "l0_s5_lambda_re", "l0_s5_lambda_im", "l0_s5_log_step", "l0_s5_b_re",
            "l0_s5_b_im", "l0_s5_c_re", "l0_s5_c_im", "l0_s5_d", "l0_s5_w_glu", "l0_s5_b_glu", "l0_ssd_conv_w",
            "l0_ssd_conv_b", "l0_ssd_dt_bias", "l0_ssd_a_log", "l0_ssd_d", "l0_ssd_norm_w", "l0_w_out",
            "l1_norm_w", "l1_w_in", "l1_fox_b_f", "l1_w_out", "final_norm_w")
_COL_SHARDED = ("l0_w_in", "l1_w_in")
_ROW_SHARDED = ("l0_s5_w_glu", "l0_w_out", "l1_w_out")
_BIG = ("l0_w_in", "l0_s5_w_glu", "l0_w_out", "l1_w_in", "l1_w_out")
_CONV = "l0_ssd_conv_w"
_SMALL = tuple(n for n in _WEIGHTS if n not in _BIG and n != _CONV)


def _pack(arrays):
    flat = jnp.concatenate([a.reshape(-1).astype(F32) for a in arrays])
    size = flat.shape[0]
    padded = -(-size // (512 * LANES)) * (512 * LANES)
    return jnp.pad(flat, (0, padded - size)).reshape(-1, LANES)


def _unpack(buf, like):
    flat = buf.reshape(-1)
    out, pos = [], 0
    for a in like:
        out.append(flat[pos:pos + a.size].reshape(a.shape))
        pos += a.size
    return out


def _step(p):
    x, tgt = p["x"][0], p["loss_target"][0]
    d = x.shape[1]
    chip = 2 * lax.axis_index("x") + lax.axis_index("y")

    def rows_first(a, n):
        return a.T if n in _COL_SHARDED else a

    shard = {n: _cast_bf16(rows_first(p[n], n), "cast_" + n) for n in _BIG}
    shard[_CONV] = p[_CONV]

    def with_own(n, g):
        return lax.dynamic_update_index_in_dim(g, shard[n][None], chip, 0)

    def whole(n, g):
        g = with_own(n, g)
        return g.reshape(N_SHARD * g.shape[1], g.shape[2])

    now = ("l0_w_in", _CONV)
    got = dict(zip(now, _run_rider(_gather_rider([shard[n] for n in now]), "gather_first")))
    wb = {"w0T": whole("l0_w_in", got["l0_w_in"])}
    sm = {n: p[n] for n in _SMALL}
    taps, ccols = p[_CONV].shape
    conv_all = lax.dynamic_update_index_in_dim(got[_CONV], p[_CONV][None], chip, 0)
    sm[_CONV] = conv_all.transpose(1, 0, 2).reshape(taps, N_SHARD * ccols)
    cut = shard["l1_w_in"].shape[0] // 3 // 8 * 8
    shard["l1_w_in#0"], shard["l1_w_in#1"] = shard["l1_w_in"][:cut], shard["l1_w_in"][cut:]
    later = {"l0_in_ug": ("l1_w_out",), "l0_in_z": ("l0_s5_w_glu", "l1_w_in#0"), "l0_in_xbc": ("l0_w_out",),
             "ssd_scan": ("l1_w_in#1",)}
    early = {"fox_bwd_q": ("l1_w_out",), "l0_out_dx_ssd": ("l0_w_out",), "ssd_scan_bwd": ("l1_w_in",),
             "l0_in_dx_xbc": ("l0_w_in", "l0_s5_w_glu")}
    grad_key = {n: n + "T" if n in _COL_SHARDED else n for n in _BIG}

    big, sib, others, pieces = {}, {}, {}, {}

    def presummed(names, grads, tag):
        for n in names:
            g = grads[grad_key[n]]
            big[n] = g.reshape(N_SHARD, g.shape[0] // N_SHARD, g.shape[1])
        sib.update(zip(names, _sibling_halves([big[n] for n in names], "reduce_sibling_" + tag)))
        return [_presum(big[n], sib[n], "presum_" + n) for n in names]

    class Plan:
        def gather_rider(self, host):
            return _gather_rider([shard[n] for n in later[host]])

        def gathered(self, host, carried):
            got = dict(zip(later[host], carried))
            pieces.update({n: with_own(n, g) for n, g in got.items() if "#" in n})
            w = {n: whole(n, g) for n, g in got.items() if "#" not in n}
            if host == "ssd_scan":
                both = jnp.concatenate([pieces["l1_w_in#0"], pieces["l1_w_in#1"]], axis=1)
                w["l1_w_in"] = both.reshape(N_SHARD * both.shape[1], both.shape[2])
            names = {"l0_s5_w_glu": "w_glu", "l0_w_out": "w0_out", "l1_w_in": "w1T", "l1_w_out": "w1_out"}
            return {names[n]: v for n, v in w.items()}

        def reduce_rider(self, host, grads):
            return _chip_rider(presummed(early[host], grads, host)) if host in early else None

        def reduced(self, host, carried):
            others.update(zip(early[host], carried))

    loss_lanes, dx, grads = _local_step(x, tgt, wb, sm, Plan())

    small_like = [p[n] for n in _SMALL] + [sm[_CONV], jnp.zeros((1,), F32)]
    small_sum = _sum_slots(_gather_all(_pack([grads[n] for n in _SMALL] + [grads[_CONV], jnp.sum(loss_lanes)]),
                                       "gather_small"), "sum_small")
    *small_grads, conv_grad, loss = _unpack(small_sum, small_like)
    conv_grad = lax.dynamic_slice(conv_grad, (0, chip * ccols), (taps, ccols))
    final = dict(zip(_SMALL, small_grads))
    final[_CONV] = conv_grad

    late = tuple(n for n in _BIG if n not in others)
    if late:
        others.update(zip(late, _run_rider(_chip_rider(presummed(late, grads, "late")), "reduce_chips_late")))
    done = [_finish_half(big[n], sib[n], others[n], "finish_" + n) for n in _BIG]

    delta, new_m, new_v = {}, {}, {}
    for n, full in zip(_BIG, _join_halves(done, "join_halves")):
        upd = _adamw(rows_first(p[n], n), full, rows_first(p["m_" + n], n), rows_first(p["v_" + n], n),
                     "adamw_" + n)
        final[n], delta[n], new_m[n], new_v[n] = (rows_first(t, n) for t in (full, *upd))
    rest = _SMALL + (_CONV,)
    packed = [_pack([t[n] for n in rest]) for t in
              ({n: p[n] for n in rest}, final, {n: p["m_" + n] for n in rest}, {n: p["v_" + n] for n in rest})]
    for dst, buf in zip((delta, new_m, new_v), _adamw(*packed, "adamw_small")):
        dst.update(zip(rest, _unpack(buf, [p[n] for n in rest])))

    outs = [loss.reshape(()), dx[None]]
    for group in (final, delta, new_m, new_v):
        outs += [group[n].reshape(p[n].shape) for n in _WEIGHTS]
    return tuple(outs)


_INPUTS = ("x",) + _WEIGHTS + ("loss_target",) + tuple("m_" + n for n in _WEIGHTS) + tuple("v_" + n for n in _WEIGHTS)


def kernel(x, l0_norm_w, l0_w_in, l0_s5_lambda_re, l0_s5_lambda_im, l0_s5_log_step, l0_s5_b_re, l0_s5_b_im, l0_s5_c_re,
           l0_s5_c_im, l0_s5_d, l0_s5_w_glu, l0_s5_b_glu, l0_ssd_conv_w, l0_ssd_conv_b, l0_ssd_dt_bias,
           l0_ssd_a_log, l0_ssd_d, l0_ssd_norm_w, l0_w_out, l1_norm_w, l1_w_in, l1_fox_b_f, l1_w_out,
           final_norm_w, loss_target, m_l0_norm_w, m_l0_w_in, m_l0_s5_lambda_re, m_l0_s5_lambda_im,
           m_l0_s5_log_step, m_l0_s5_b_re, m_l0_s5_b_im, m_l0_s5_c_re, m_l0_s5_c_im, m_l0_s5_d,
           m_l0_s5_w_glu, m_l0_s5_b_glu, m_l0_ssd_conv_w, m_l0_ssd_conv_b, m_l0_ssd_dt_bias, m_l0_ssd_a_log,
           m_l0_ssd_d, m_l0_ssd_norm_w, m_l0_w_out, m_l1_norm_w, m_l1_w_in, m_l1_fox_b_f, m_l1_w_out,
           m_final_norm_w, v_l0_norm_w, v_l0_w_in, v_l0_s5_lambda_re, v_l0_s5_lambda_im, v_l0_s5_log_step,
           v_l0_s5_b_re, v_l0_s5_b_im, v_l0_s5_c_re, v_l0_s5_c_im, v_l0_s5_d, v_l0_s5_w_glu, v_l0_s5_b_glu,
           v_l0_ssd_conv_w, v_l0_ssd_conv_b, v_l0_ssd_dt_bias, v_l0_ssd_a_log, v_l0_ssd_d, v_l0_ssd_norm_w,
           v_l0_w_out, v_l1_norm_w, v_l1_w_in, v_l1_fox_b_f, v_l1_w_out, v_final_norm_w):
    values = (x, l0_norm_w, l0_w_in, l0_s5_lambda_re, l0_s5_lambda_im, l0_s5_log_step, l0_s5_b_re, l0_s5_b_im,
              l0_s5_c_re, l0_s5_c_im, l0_s5_d, l0_s5_w_glu, l0_s5_b_glu, l0_ssd_conv_w, l0_ssd_conv_b,
              l0_ssd_dt_bias, l0_ssd_a_log, l0_ssd_d, l0_ssd_norm_w, l0_w_out, l1_norm_w, l1_w_in,
              l1_fox_b_f, l1_w_out, final_norm_w, loss_target, m_l0_norm_w, m_l0_w_in,
              m_l0_s5_lambda_re, m_l0_s5_lambda_im, m_l0_s5_log_step, m_l0_s5_b_re, m_l0_s5_b_im,
              m_l0_s5_c_re, m_l0_s5_c_im, m_l0_s5_d, m_l0_s5_w_glu, m_l0_s5_b_glu, m_l0_ssd_conv_w,
              m_l0_ssd_conv_b, m_l0_ssd_dt_bias, m_l0_ssd_a_log, m_l0_ssd_d, m_l0_ssd_norm_w,
              m_l0_w_out, m_l1_norm_w, m_l1_w_in, m_l1_fox_b_f, m_l1_w_out, m_final_norm_w, v_l0_norm_w,
              v_l0_w_in, v_l0_s5_lambda_re, v_l0_s5_lambda_im, v_l0_s5_log_step, v_l0_s5_b_re,
              v_l0_s5_b_im, v_l0_s5_c_re, v_l0_s5_c_im, v_l0_s5_d, v_l0_s5_w_glu, v_l0_s5_b_glu,
              v_l0_ssd_conv_w, v_l0_ssd_conv_b, v_l0_ssd_dt_bias, v_l0_ssd_a_log, v_l0_ssd_d,
              v_l0_ssd_norm_w, v_l0_w_out, v_l1_norm_w, v_l1_w_in, v_l1_fox_b_f, v_l1_w_out,
              v_final_norm_w)
    return _step(dict(zip(_INPUTS, values)))
```

```python
import functools
import math

import jax
import jax.numpy as jnp
from jax import lax
from jax.experimental import pallas as pl
from jax.experimental.pallas import tpu as pltpu

F32 = jnp.float32
BF16 = jnp.bfloat16

S5_GROUP = 16
S5_STATE = 64
S5_EIG_CLIP = -1e-4
SSD_HEAD_DIM = 64
SSD_GROUPS = 8
SSD_STATE = 128
SSD_CONV = 4
SSD_CHUNK = 128
FOX_HEAD_DIM = 128
FOX_TILE = 1024
NORM_EPS = 1e-5
ADAM_LR = 0.001
ADAM_B1 = 0.9
ADAM_B2 = 0.999
ADAM_EPS = 1e-08
ADAM_WD = 0.01
ADAM_STEP = 10

N_SHARD = 4
N_DEV = 8
LANES = 128
VMEM_LIMIT = 56 * 1024 * 1024
MESH = pl.DeviceIdType.MESH


def _pick(dim, prefs, offs=()):
    for p in prefs:
        if dim % p == 0 and all(o % p == 0 for o in offs):
            return p
    return dim


def _params(sem=None, vmem=VMEM_LIMIT):
    return pltpu.CompilerParams(dimension_semantics=sem, vmem_limit_bytes=vmem)


class _Rider:
    def __init__(self, inputs, out_shape, sems, start, finish):
        self.inputs, self.out_shape, self.sems = list(inputs), list(out_shape), list(sems)
        self.start, self.finish = start, finish


def _hosted_call(body, *, name, grid, in_specs, out_specs, out_shape, scratch_shapes, sem, args, rider=None):
    single = not isinstance(out_shape, (list, tuple))
    out_specs = [out_specs] if single else list(out_specs)
    out_shape = [out_shape] if single else list(out_shape)
    if rider is None:
        res = pl.pallas_call(body, name=name, grid=grid, in_specs=in_specs, out_specs=out_specs,
                             out_shape=out_shape, scratch_shapes=scratch_shapes,
                             compiler_params=_params(sem))(*args)
        return (res[0] if single else res), []
    n_in, n_out, n_scr = len(in_specs), len(out_shape), len(scratch_shapes)
    n_rin, n_rout = len(rider.inputs), len(rider.out_shape)

    def carried(*refs):
        ins, refs = refs[:n_in], refs[n_in:]
        rin, refs = refs[:n_rin], refs[n_rin:]
        outs, refs = refs[:n_out], refs[n_out:]
        rout, refs = refs[:n_rout], refs[n_rout:]
        scr, rsem = refs[:n_scr], refs[n_scr:]
        ids = [pl.program_id(k) for k in range(len(grid))]
        first = functools.reduce(jnp.logical_and, [i == 0 for i in ids])
        last = functools.reduce(jnp.logical_and, [i == g - 1 for i, g in zip(ids, grid)])

        @pl.when(first)
        def _():
            rider.start(rin, rout, rsem)

        body(*ins, *outs, *scr)

        @pl.when(last)
        def _():
            rider.finish(rin, rout, rsem)

    res = pl.pallas_call(
        carried, name=name, grid=grid,
        in_specs=list(in_specs) + [_ANY] * n_rin, out_specs=out_specs + [_ANY] * n_rout,
        out_shape=out_shape + rider.out_shape,
        scratch_shapes=list(scratch_shapes) + [pltpu.SemaphoreType.DMA((k,)) for k in rider.sems],
        compiler_params=pltpu.CompilerParams(dimension_semantics=("arbitrary",) * len(grid),
                                             vmem_limit_bytes=VMEM_LIMIT, has_side_effects=True),
    )(*args, *rider.inputs)
    outs = res[:n_out]
    return (outs[0] if single else outs), list(res[n_out:])


def _matmul(a, b, *, mode="nn", dims=None, a_off=(0, 0), b_off=(0, 0), addend=None,
            out_dtype=F32, rider=None, rows_of=None, name):
    if dims is None:
        if mode == "nn":
            dims = (a.shape[0], b.shape[1], a.shape[1])
        elif mode == "nt":
            dims = (a.shape[0], b.shape[0], a.shape[1])
        else:
            dims = (a.shape[1], b.shape[1], a.shape[0])
    m, n, k = dims
    if mode == "nn":
        om, on, ok = (a_off[0],), (b_off[1],), (a_off[1], b_off[0])
    elif mode == "nt":
        om, on, ok = (a_off[0],), (b_off[0],), (a_off[1], b_off[1])
    else:
        om, on, ok = (a_off[1],), (b_off[1],), (a_off[0], b_off[0])
    tm = _pick(m, (1024, 512, 256, 128), om)
    tn = _pick(n, (1024, 768, 512, 384, 256, 128), on)
    tk = _pick(k, (2048, 1024, 512, 256, 128), ok)
    nk = k // tk
    if mode == "nn":
        a_blk, a_div = (tm, tk), (tm, tk)
        b_blk, b_div = (tk, tn), (tk, tn)
        a_map = lambda i, j, kk: (i + a_off[0] // tm, kk + a_off[1] // tk)
        b_map = lambda i, j, kk: (kk + b_off[0] // tk, j + b_off[1] // tn)
        dn = (((1,), (0,)), ((), ()))
    elif mode == "nt":
        a_blk, a_div = (tm, tk), (tm, tk)
        b_blk, b_div = (tn, tk), (tn, tk)
        a_map = lambda i, j, kk: (i + a_off[0] // tm, kk + a_off[1] // tk)
        b_map = lambda i, j, kk: (j + b_off[0] // tn, kk + b_off[1] // tk)
        dn = (((1,), (1,)), ((), ()))
    else:
        a_blk, a_div = (tk, tm), (tk, tm)
        b_blk, b_div = (tk, tn), (tk, tn)
        a_map = lambda i, j, kk: (kk + a_off[0] // tk, i + a_off[1] // tm)
        b_map = lambda i, j, kk: (kk + b_off[0] // tk, j + b_off[1] // tn)
        dn = (((0,), (0,)), ((), ()))
    assert a_off[0] % a_div[0] == 0 and a_off[1] % a_div[1] == 0, (name, a_off, a_div)
    assert b_off[0] % b_div[0] == 0 and b_off[1] % b_div[1] == 0, (name, b_off, b_div)
    has_add = addend is not None

    def body(*refs):
        if has_add:
            a_ref, b_ref, c_ref, o_ref, acc_ref = refs
        else:
            a_ref, b_ref, o_ref, acc_ref = refs
        kk = pl.program_id(2)

        @pl.when(kk == 0)
        def _():
            acc_ref[...] = jnp.zeros_like(acc_ref)

        acc_ref[...] += lax.dot_general(a_ref[...].astype(BF16), b_ref[...].astype(BF16), dn,
                                        preferred_element_type=F32)

        @pl.when(kk == nk - 1)
        def _():
            r = acc_ref[...]
            if has_add:
                r = r + c_ref[...].astype(F32)
            o_ref[...] = r.astype(o_ref.dtype)

    in_specs = [pl.BlockSpec(a_blk, a_map), pl.BlockSpec(b_blk, b_map)]
    args = [a, b]
    if has_add:
        in_specs.append(pl.BlockSpec((tm, tn), lambda i, j, kk: (i, j)))
        args.append(addend)
    if rows_of is not None:
        total, row_off, buf = rows_of
        assert row_off % tm == 0 and rider is None, (name, row_off, tm)
        if buf is not None:
            def body_into(*refs):
                body(*refs[:len(args)], *refs[len(args) + 1:])

            return pl.pallas_call(
                body_into, name=name, grid=(m // tm, n // tn, nk),
                in_specs=in_specs + [pl.BlockSpec(memory_space=pl.ANY)],
                out_specs=pl.BlockSpec((tm, tn), lambda i, j, kk: (i + row_off // tm, j)),
                out_shape=jax.ShapeDtypeStruct((total, n), out_dtype),
                input_output_aliases={len(args): 0},
                scratch_shapes=[pltpu.VMEM((tm, tn), F32)],
                compiler_params=_params(("parallel", "parallel", "arbitrary")),
            )(*args, buf)
        return pl.pallas_call(
            body, name=name, grid=(m // tm, n // tn, nk), in_specs=in_specs,
            out_specs=pl.BlockSpec((tm, tn), lambda i, j, kk: (i + row_off // tm, j)),
            out_shape=jax.ShapeDtypeStruct((total, n), out_dtype),
            scratch_shapes=[pltpu.VMEM((tm, tn), F32)],
            compiler_params=_params(("parallel", "parallel", "arbitrary")),
        )(*args)
    out, carried = _hosted_call(
        body, name=name, grid=(m // tm, n // tn, nk),
        in_specs=in_specs, out_specs=pl.BlockSpec((tm, tn), lambda i, j, kk: (i, j)),
        out_shape=jax.ShapeDtypeStruct((m, n), out_dtype),
        scratch_shapes=[pltpu.VMEM((tm, tn), F32)],
        sem=("parallel", "parallel", "arbitrary"), args=args, rider=rider)
    return out if rider is None else (out, carried)


def _rowwise(fn, rows, params, out_rows, out_accs=(), *, tl, ncol=1, name):
    rows = [r if isinstance(r, tuple) else (r, r.shape[1] // ncol, 0) for r in rows]
    n_rows, n_par, n_or, n_oa = len(rows), len(params), len(out_rows), len(out_accs)
    length = rows[0][0].shape[0]
    tl = _pick(length, [t for t in (1024, 512, 256, 128, 64, 32, 16, 8) if t <= tl])

    def body(*refs):
        row_refs = refs[:n_rows]
        par_refs = refs[n_rows:n_rows + n_par]
        or_refs = refs[n_rows + n_par:n_rows + n_par + n_or]
        oa_refs = refs[n_rows + n_par + n_or:]
        outs = fn(*[r[...] for r in row_refs], *[p[...] for p in par_refs])
        if not isinstance(outs, (tuple, list)):
            outs = (outs,)
        for r, v in zip(or_refs, outs[:n_or]):
            r[...] = v.astype(r.dtype)
        if n_oa:
            @pl.when(pl.program_id(1) == 0)
            def _():
                for r in oa_refs:
                    r[...] = jnp.zeros_like(r)

            for r, v in zip(oa_refs, outs[n_or:]):
                r[...] += v.astype(F32)

    in_specs = [pl.BlockSpec((tl, w), functools.partial(lambda j, i, b0: (i, b0 + j), b0=b0))
                for (_, w, b0) in rows]
    in_specs += [pl.BlockSpec((p.shape[0], p.shape[1] // ncol), lambda j, i: (0, j)) for p in params]
    out_specs = [pl.BlockSpec((tl, w), lambda j, i: (i, j)) for (w, _) in out_rows]
    out_specs += [pl.BlockSpec((1, w), lambda j, i: (0, j)) for w in out_accs]
    out_shape = [jax.ShapeDtypeStruct((length, ncol * w), dt) for (w, dt) in out_rows]
    out_shape += [jax.ShapeDtypeStruct((1, ncol * w), F32) for w in out_accs]
    res = pl.pallas_call(
        body, name=name, grid=(ncol, length // tl),
        in_specs=in_specs, out_specs=out_specs, out_shape=out_shape,
        compiler_params=_params(("parallel", "arbitrary" if n_oa else "parallel")),
    )(*[r[0] for r in rows], *params)
    return res


def _bdmm(xs, ws, *, addend=None, out_dtype=F32, name):
    nj, kin, kout = ws[0].shape
    xs = [x if isinstance(x, tuple) else (x, 0) for x in xs]
    length = xs[0][0].shape[0]
    tl = _pick(length, (2048, 1024, 512, 256, 128))
    n_x = len(xs)
    has_add = addend is not None

    def body(*refs):
        x_refs = refs[:n_x]
        w_refs = refs[n_x:2 * n_x]
        o_ref = refs[-1]
        acc = None
        for xr, wr in zip(x_refs, w_refs):
            t = jnp.dot(xr[...].astype(BF16), wr[0], preferred_element_type=F32)
            acc = t if acc is None else acc + t
        if has_add:
            acc = acc + refs[2 * n_x][...].astype(F32)
        o_ref[...] = acc.astype(o_ref.dtype)

    in_specs = [pl.BlockSpec((tl, kin), functools.partial(lambda i, j, b0: (i, b0 + j), b0=b0)) for (_, b0) in xs]
    in_specs += [pl.BlockSpec((1, kin, kout), lambda i, j: (j, 0, 0)) for _ in ws]
    args = [x[0] for x in xs] + list(ws)
    if has_add:
        in_specs.append(pl.BlockSpec((tl, kout), lambda i, j: (i, j)))
        args.append(addend)
    return pl.pallas_call(
        body, name=name, grid=(length // tl, nj),
        in_specs=in_specs, out_specs=pl.BlockSpec((tl, kout), lambda i, j: (i, j)),
        out_shape=jax.ShapeDtypeStruct((length, nj * kout), out_dtype),
        compiler_params=_params(("parallel", "parallel")),
    )(*args)


def _bdmm_tn_sized(x, g, nj, kin, kout, x_first, name):
    length = x.shape[0]
    tl = _pick(length, (512, 256, 128))
    nt = length // tl

    def body(x_ref, g_ref, o_ref):
        @pl.when(pl.program_id(1) == 0)
        def _():
            o_ref[...] = jnp.zeros_like(o_ref)

        o_ref[0] += lax.dot_general(x_ref[...].astype(BF16), g_ref[...].astype(BF16),
                                    (((0,), (0,)), ((), ())), preferred_element_type=F32)

    return pl.pallas_call(
        body, name=name, grid=(nj, nt),
        in_specs=[pl.BlockSpec((tl, kin), lambda j, t: (t, x_first + j)),
                  pl.BlockSpec((tl, kout), lambda j, t: (t, j))],
        out_specs=pl.BlockSpec((1, kin, kout), lambda j, t: (j, 0, 0)),
        out_shape=jax.ShapeDtypeStruct((nj, kin, kout), F32),
        compiler_params=_params(("parallel", "arbitrary")),
    )(x, g)


def _f_norm(x, w):
    return x * lax.rsqrt(jnp.mean(x * x, axis=-1, keepdims=True) + NORM_EPS) * w


def _gelu(y):
    return 0.5 * y * (1.0 + jnp.tanh(math.sqrt(2.0 / math.pi) * (y + 0.044715 * (y * y * y))))


def _sigmoid(x):
    return 1.0 / (1.0 + jnp.exp(-x))


def _silu(x):
    return x * _sigmoid(x)


def _softplus(x):
    return jnp.maximum(x, 0.0) + jnp.log(1.0 + jnp.exp(-jnp.abs(x)))


def _f_s5_gelu(yc, u, dvec):
    return _gelu(yc + dvec * u)


def _f_s5_out(yc, u, t, gate, dvec, bglu):
    gl = _gelu(yc + dvec * u)
    return gl * _sigmoid(t + bglu) * _silu(gate)


def _f_ssd_out(y, xs, z, dpar, nw):
    v = (y + dpar * xs) * _silu(z)
    return v * lax.rsqrt(jnp.mean(v * v, axis=-1, keepdims=True) + NORM_EPS) * nw


def _f_fox_out(att, gate):
    return att * _silu(gate)


def _norm_fwd(x, w, name):
    return _rowwise(lambda xt, wt: _f_norm(xt, wt), [x], [w], [(x.shape[1], BF16)], tl=256, name=name)[0]


def _norm_bwd(x, dh, dres, w, name):
    d = x.shape[1]

    def fn(xt, dht, drt, wt):
        _, vjp = jax.vjp(_f_norm, xt, wt)
        dx, dw = vjp(dht)
        dx = dx + drt
        return dx, dx, dw

    return _rowwise(fn, [x, dh, dres], [w], [(d, F32), (d, BF16)], [d], tl=128, name=name)


def _adamw_math(w, g, m, v):
    m = ADAM_B1 * m + (1.0 - ADAM_B1) * g
    v = ADAM_B2 * v + (1.0 - ADAM_B2) * jnp.square(g)
    m_hat = m / (1.0 - ADAM_B1 ** ADAM_STEP)
    v_hat = v / (1.0 - ADAM_B2 ** ADAM_STEP)
    delta = -ADAM_LR * (m_hat / (jnp.sqrt(v_hat) + ADAM_EPS) + ADAM_WD * w)
    return delta, m, v


def _adamw(w, g, m, v, name):
    return _elementwise(_adamw_math, [w, g, m, v], [F32] * 3, name)


def _s5_scan_fwd(bu_re, bu_im, a_re, a_im, name):
    length, rows, _ = bu_re.shape
    rb = _pick(rows, (32, 16, 8))
    tl = _pick(length, (64, 32, 16, 8))

    def body(bur_ref, bui_ref, ar_ref, ai_ref, sr_ref, si_ref, st_ref):
        @pl.when(pl.program_id(1) == 0)
        def _():
            st_ref[...] = jnp.zeros_like(st_ref)

        ar = ar_ref[...]
        ai = ai_ref[...]

        def step(l, carry):
            sr, si = carry
            nr = ar * sr - ai * si + bur_ref[l]
            ni = ar * si + ai * sr + bui_ref[l]
            sr_ref[l] = nr
            si_ref[l] = ni
            return nr, ni

        sr, si = lax.fori_loop(0, tl, step, (st_ref[0], st_ref[1]))
        st_ref[0] = sr
        st_ref[1] = si

    blk = pl.BlockSpec((tl, rb, LANES), lambda cb, t: (t, cb, 0))
    ablk = pl.BlockSpec((rb, LANES), lambda cb, t: (cb, 0))
    return pl.pallas_call(
        body, name=name, grid=(rows // rb, length // tl),
        in_specs=[blk, blk, ablk, ablk], out_specs=[blk, blk],
        out_shape=[jax.ShapeDtypeStruct(bu_re.shape, F32)] * 2,
        scratch_shapes=[pltpu.VMEM((2, rb, LANES), F32)],
        compiler_params=_params(("parallel", "arbitrary")),
    )(bu_re, bu_im, a_re, a_im)


def _s5_scan_bwd(ds_re, ds_im, s_re, s_im, a_re, a_im, name):
    length, rows, _ = ds_re.shape
    rb = _pick(rows, (32, 16, 8))
    tl = _pick(length, (64, 32, 16, 8))
    nt = length // tl

    def body(dsr_ref, dsi_ref, sr_ref, si_ref, pr_ref, pi_ref, ar_ref, ai_ref,
             gr_ref, gi_ref, dar_ref, dai_ref, st_ref):
        t = pl.program_id(1)

        @pl.when(t == 0)
        def _():
            st_ref[...] = jnp.zeros_like(st_ref)
            dar_ref[...] = jnp.zeros_like(dar_ref)
            dai_ref[...] = jnp.zeros_like(dai_ref)

        ar = ar_ref[...]
        ai = ai_ref[...]

        def adj(l, gr, gi):
            ngr = dsr_ref[l] + ar * gr + ai * gi
            ngi = dsi_ref[l] + ar * gi - ai * gr
            gr_ref[l] = ngr
            gi_ref[l] = ngi
            return ngr, ngi

        def step(idx, carry):
            gr, gi, dar, dai = carry
            l = tl - 1 - idx
            gr, gi = adj(l, gr, gi)
            pr = sr_ref[l - 1]
            pi = si_ref[l - 1]
            dar = dar + gr * pr + gi * pi
            dai = dai + gi * pr - gr * pi
            return gr, gi, dar, dai

        zero = jnp.zeros((rb, LANES), F32)
        gr, gi, dar, dai = lax.fori_loop(0, tl - 1, step, (st_ref[0], st_ref[1], zero, zero))
        gr, gi = adj(0, gr, gi)
        first = (t == nt - 1)
        pr = jnp.where(first, 0.0, pr_ref[0])
        pi = jnp.where(first, 0.0, pi_ref[0])
        dar = dar + gr * pr + gi * pi
        dai = dai + gi * pr - gr * pi
        st_ref[0] = gr
        st_ref[1] = gi
        dar_ref[...] += dar
        dai_ref[...] += dai

    blk = pl.BlockSpec((tl, rb, LANES), lambda cb, t: (nt - 1 - t, cb, 0))
    prev = pl.BlockSpec((1, rb, LANES), lambda cb, t: (jnp.maximum((nt - 1 - t) * tl - 1, 0), cb, 0))
    ablk = pl.BlockSpec((rb, LANES), lambda cb, t: (cb, 0))
    return pl.pallas_call(
        body, name=name, grid=(rows // rb, nt),
        in_specs=[blk, blk, blk, blk, prev, prev, ablk, ablk],
        out_specs=[blk, blk, ablk, ablk],
        out_shape=[jax.ShapeDtypeStruct(ds_re.shape, F32)] * 2 + [jax.ShapeDtypeStruct(a_re.shape, F32)] * 2,
        scratch_shapes=[pltpu.VMEM((2, rb, LANES), F32)],
        compiler_params=_params(("parallel", "arbitrary")),
    )(ds_re, ds_im, s_re, s_im, s_re, s_im, a_re, a_im)


def _s5_prepare(lam_re, lam_im, log_step, b_re, b_im, c_re, c_im):
    groups, state = lam_re.shape
    lr = jnp.minimum(lam_re, S5_EIG_CLIP)
    li = lam_im
    step = jnp.exp(log_step)[:, None]
    mag = jnp.exp(lr * step)
    ab_re = mag * jnp.cos(li * step)
    ab_im = mag * jnp.sin(li * step)
    denom = lr * lr + li * li
    nr = ab_re - 1.0
    ni = ab_im
    coef_re = (nr * lr + ni * li) / denom
    coef_im = (ni * lr - nr * li) / denom
    bb_re = coef_re[..., None] * b_re - coef_im[..., None] * b_im
    bb_im = coef_re[..., None] * b_im + coef_im[..., None] * b_re
    per = LANES // S5_GROUP
    nj = groups // per
    eye = jnp.eye(per, dtype=F32)

    def in_map(bb):
        return jnp.einsum('jgph,gk->jghkp', bb.reshape(nj, per, state, S5_GROUP), eye).reshape(
            nj, per * S5_GROUP, per * state)

    def out_map(cc):
        return jnp.einsum('jghp,gk->jgpkh', cc.reshape(nj, per, S5_GROUP, state), eye).reshape(
            nj, per * state, per * S5_GROUP)

    shape2 = (groups * state // LANES, LANES)
    return (ab_re.reshape(shape2), ab_im.reshape(shape2), in_map(bb_re), in_map(bb_im),
            out_map(c_re), -out_map(c_im))


def _shift_down(cur, prev8, j):
    rolled = pltpu.roll(cur, j, 0)
    pr = pltpu.roll(prev8, j, 0)
    row = lax.broadcasted_iota(jnp.int32, cur.shape, 0)
    return jnp.where(row < j, jnp.tile(pr, (cur.shape[0] // 8, 1)), rolled)


def _shift_up(cur, next8, j):
    tl = cur.shape[0]
    rolled = pltpu.roll(cur, tl - j, 0)
    nx = pltpu.roll(next8, 8 - j, 0)
    row = lax.broadcasted_iota(jnp.int32, cur.shape, 0)
    return jnp.where(row >= tl - j, jnp.tile(nx, (tl // 8, 1)), rolled)


def _conv_tiles(length, ch):
    return _pick(length, (256, 128, 64, 32, 16, 8)), _pick(ch, (1024, 512, 256, 128))


def _conv_fwd(xbc, w, b, name):
    length, ch = xbc.shape
    tl, tc = _conv_tiles(length, ch)

    def body(x_ref, p_ref, w_ref, b_ref, o_ref):
        cur = x_ref[...]
        prev8 = jnp.where(pl.program_id(1) == 0, 0.0, p_ref[...])
        pre = b_ref[...] + w_ref[SSD_CONV - 1:SSD_CONV, :] * cur
        for j in range(1, SSD_CONV):
            pre = pre + w_ref[SSD_CONV - 1 - j:SSD_CONV - j, :] * _shift_down(cur, prev8, j)
        o_ref[...] = _silu(pre)

    return pl.pallas_call(
        body, name=name, grid=(ch // tc, length // tl),
        in_specs=[pl.BlockSpec((tl, tc), lambda c, i: (i, c)),
                  pl.BlockSpec((8, tc), lambda c, i: (jnp.maximum(i * (tl // 8) - 1, 0), c)),
                  pl.BlockSpec((SSD_CONV, tc), lambda c, i: (0, c)),
                  pl.BlockSpec((1, tc), lambda c, i: (0, c))],
        out_specs=pl.BlockSpec((tl, tc), lambda c, i: (i, c)),
        out_shape=jax.ShapeDtypeStruct((length, ch), F32),
        compiler_params=_params(("parallel", "parallel")),
    )(xbc, xbc, w, b)


def _conv_bwd_pre(dxc, xbc, w, b, name):
    length, ch = xbc.shape
    tl, tc = _conv_tiles(length, ch)

    def body(d_ref, x_ref, p_ref, w_ref, b_ref, o_ref, dw_ref, db_ref):
        @pl.when(pl.program_id(1) == 0)
        def _():
            dw_ref[...] = jnp.zeros_like(dw_ref)
            db_ref[...] = jnp.zeros_like(db_ref)

        cur = x_ref[...]
        prev8 = jnp.where(pl.program_id(1) == 0, 0.0, p_ref[...])
        shifted = [cur] + [_shift_down(cur, prev8, j) for j in range(1, SSD_CONV)]
        pre = b_ref[...]
        for j in range(SSD_CONV):
            pre = pre + w_ref[SSD_CONV - 1 - j:SSD_CONV - j, :] * shifted[j]
        sg = _sigmoid(pre)
        dpre = d_ref[...] * (sg * (1.0 + pre * (1.0 - sg)))
        o_ref[...] = dpre
        db_ref[...] += jnp.sum(dpre, axis=0, keepdims=True)
        row = lax.broadcasted_iota(jnp.int32, (SSD_CONV, tc), 0)
        dw = jnp.zeros((SSD_CONV, tc), F32)
        for j in range(SSD_CONV):
            dw = dw + jnp.where(row == SSD_CONV - 1 - j, jnp.sum(dpre * shifted[j], axis=0, keepdims=True), 0.0)
        dw_ref[...] += dw

    return pl.pallas_call(
        body, name=name, grid=(ch // tc, length // tl),
        in_specs=[pl.BlockSpec((tl, tc), lambda c, i: (i, c)),
                  pl.BlockSpec((tl, tc), lambda c, i: (i, c)),
                  pl.BlockSpec((8, tc), lambda c, i: (jnp.maximum(i * (tl // 8) - 1, 0), c)),
                  pl.BlockSpec((SSD_CONV, tc), lambda c, i: (0, c)),
                  pl.BlockSpec((1, tc), lambda c, i: (0, c))],
        out_specs=[pl.BlockSpec((tl, tc), lambda c, i: (i, c)),
                   pl.BlockSpec((SSD_CONV, tc), lambda c, i: (0, c)),
                   pl.BlockSpec((1, tc), lambda c, i: (0, c))],
        out_shape=[jax.ShapeDtypeStruct((length, ch), F32), jax.ShapeDtypeStruct((SSD_CONV, ch), F32),
                   jax.ShapeDtypeStruct((1, ch), F32)],
        compiler_params=_params(("parallel", "arbitrary")),
    )(dxc, xbc, xbc, w, b)


def _conv_bwd_in(dpre, w, name):
    length, ch = dpre.shape
    tl, tc = _conv_tiles(length, ch)
    nt = length // tl

    def body(d_ref, n_ref, w_ref, o_ref):
        cur = d_ref[...]
        next8 = jnp.where(pl.program_id(1) == nt - 1, 0.0, n_ref[...])
        acc = w_ref[SSD_CONV - 1:SSD_CONV, :] * cur
        for j in range(1, SSD_CONV):
            acc = acc + w_ref[SSD_CONV - 1 - j:SSD_CONV - j, :] * _shift_up(cur, next8, j)
        o_ref[...] = acc.astype(o_ref.dtype)

    return pl.pallas_call(
        body, name=name, grid=(ch // tc, nt),
        in_specs=[pl.BlockSpec((tl, tc), lambda c, i: (i, c)),
                  pl.BlockSpec((8, tc), lambda c, i: (jnp.minimum((i + 1) * (tl // 8), length // 8 - 1), c)),
                  pl.BlockSpec((SSD_CONV, tc), lambda c, i: (0, c))],
        out_specs=pl.BlockSpec((tl, tc), lambda c, i: (i, c)),
        out_shape=jax.ShapeDtypeStruct((length, ch), BF16),
        compiler_params=_params(("parallel", "parallel")),
    )(dpre, dpre, w)


def _split(x, terms):
    parts = []
    for _ in range(terms):
        part = x.astype(BF16)
        parts.append(part)
        x = x - part.astype(F32)
    return parts


def _dot(a, b, dn=(((1,), (0,)), ((), ()))):
    return lax.dot_general(a, b, dn, preferred_element_type=F32)


_NN = (((1,), (0,)), ((), ()))
_NT = (((1,), (1,)), ((), ()))
_TN = (((0,), (0,)), ((), ()))


def _pdot(parts, sel, dn=_NN):
    return functools.reduce(lambda a, b: a + b, [_dot(part, sel, dn) for part in parts])


def _pdotr(sel, parts, dn=_NN):
    return functools.reduce(lambda a, b: a + b, [_dot(sel, part, dn) for part in parts])


def _dot3(x, sel, dn=_NN):
    return _pdot(_split(x, 3), sel, dn)


def _dot3r(sel, x, dn=_NN):
    return _pdotr(sel, _split(x, 3), dn)


def _dot2(x, sel, dn=_NN):
    return _pdot(_split(x, 2), sel, dn)


def _iota2(shape, axis):
    return lax.broadcasted_iota(jnp.int32, shape, axis)


def _ssd_masks():
    q = SSD_CHUNK
    r, c = _iota2((q, q), 0), _iota2((q, q), 1)
    tril = (c <= r)
    return r, c, tril


def _head_of_lane(wg):
    return (_iota2((SSD_CHUNK, wg), 0) == _iota2((SSD_CHUNK, wg), 1) // SSD_HEAD_DIM).astype(BF16)


def _head_of_row(wg, dtype):
    return (_iota2((wg, SSD_CHUNK), 1) == _iota2((wg, SSD_CHUNK), 0) // SSD_HEAD_DIM).astype(dtype)


class _SsdChunk:
    def __init__(self, dtraw, dtb, ap, b_t, c_t, wg):
        q = SSD_CHUNK
        hpg = wg // SSD_HEAD_DIM
        r, c, self.tril = _ssd_masks()
        self.upper = (c > r)
        self.dt = _softplus(dtraw + dtb)
        self.la = self.dt * ap
        self.cum = _dot3r(self.tril.astype(BF16), self.la)
        rem = _dot3r(self.upper.astype(BF16), self.la)
        total = _dot3(self.la, jnp.ones((q, q), BF16), _TN)
        self.scores = _dot(c_t, b_t, _NT)
        cum2 = _split(self.cum, 2)
        stack = jnp.concatenate(_split(self.dt, 2) + cum2 + _split(rem, 2), axis=0)
        lanes = _dot(stack, _head_of_lane(wg))
        self.dt_l = lanes[0:q] + lanes[q:2 * q]
        self.cum_l = lanes[2 * q:3 * q] + lanes[3 * q:4 * q]
        self.rem_l = lanes[4 * q:5 * q] + lanes[5 * q:6 * q]
        self.grow = jnp.exp(_pdotr(_head_of_row(wg, BF16), _split(total, 2)))
        every_lane = (_iota2((q, hpg * q), 0) == _iota2((q, hpg * q), 1) // q).astype(BF16)
        cq = _dot(jnp.concatenate(cum2, axis=0), every_lane)
        self.cq = cq[0:q] + cq[q:2 * q]
        every_row = (_iota2((hpg * q, q), 1) == _iota2((hpg * q, q), 0) // q).astype(BF16)
        self.ck = _pdotr(every_row, cum2, _NT)

    def decay(self, h):
        q = SSD_CHUNK
        seg = self.cq[:, h * q:(h + 1) * q] - self.ck[h * q:(h + 1) * q, :]
        return jnp.exp(jnp.where(self.tril, seg, -jnp.inf))


def _by_head(x_b, lane):
    first = (lane // SSD_HEAD_DIM) == 0
    return jnp.concatenate([jnp.where(first, x_b, 0), jnp.where(first, 0, x_b)], axis=0)


def _ssd_tiles(xc, n_heads):
    hpg = n_heads // SSD_GROUPS
    wg = hpg * SSD_HEAD_DIM
    xw = n_heads * SSD_HEAD_DIM
    return hpg, wg, xw // wg, xw // SSD_STATE


def _ssd_fwd(xc, dtraw, dtb, ap, n_heads, name, rider=None):
    length = xc.shape[0]
    q = SSD_CHUNK
    nc = length // q
    hpg, wg, _, b_blk0 = _ssd_tiles(xc, n_heads)
    c_blk0 = b_blk0 + SSD_GROUPS
    npair = hpg // 2

    def body(x_ref, b_ref, c_ref, dt_ref, dtb_ref, ap_ref, y_ref, st_ref, s_ref):
        @pl.when(pl.program_id(1) == 0)
        def _():
            s_ref[...] = jnp.zeros_like(s_ref)

        st_ref[0, 0] = s_ref[...]
        b_t = b_ref[...].astype(BF16)
        c_t = c_ref[...].astype(BF16)
        ck = _SsdChunk(dt_ref[...], dtb_ref[...], ap_ref[...], b_t, c_t, wg)
        lane = _iota2((q, q), 1)
        xd = x_ref[...] * ck.dt_l
        xd_b = xd.astype(BF16)
        s_prev = s_ref[...]
        y_state = _dot(c_t, s_prev.astype(BF16), _NT) * jnp.exp(ck.cum_l)
        for i in range(npair):
            sl = slice(i * LANES, (i + 1) * LANES)
            wm = jnp.concatenate([(ck.scores * ck.decay(2 * i + hh)).astype(BF16) for hh in range(2)], axis=1)
            y_ref[:, sl] = y_state[:, sl] + _dot(wm, _by_head(xd_b[:, sl], lane))
        xw_b = (xd * jnp.exp(ck.rem_l)).astype(BF16)
        s_ref[...] = ck.grow * s_prev + _dot(xw_b, b_t, _TN)

    outs, carried = _hosted_call(
        body, name=name, grid=(SSD_GROUPS, nc),
        in_specs=[pl.BlockSpec((q, wg), lambda g, c: (c, g)),
                  pl.BlockSpec((q, SSD_STATE), lambda g, c: (c, b_blk0 + g)),
                  pl.BlockSpec((q, SSD_STATE), lambda g, c: (c, c_blk0 + g)),
                  pl.BlockSpec((q, LANES), lambda g, c: (c, g)),
                  pl.BlockSpec((1, LANES), lambda g, c: (0, g)),
                  pl.BlockSpec((1, LANES), lambda g, c: (0, g))],
        out_specs=[pl.BlockSpec((q, wg), lambda g, c: (c, g)),
                   pl.BlockSpec((1, 1, wg, SSD_STATE), lambda g, c: (c, g, 0, 0))],
        out_shape=[jax.ShapeDtypeStruct((length, SSD_GROUPS * wg), F32),
                   jax.ShapeDtypeStruct((nc, SSD_GROUPS, wg, SSD_STATE), F32)],
        scratch_shapes=[pltpu.VMEM((wg, SSD_STATE), F32)],
        sem=("parallel", "arbitrary"), args=(xc, xc, xc, dtraw, dtb, ap), rider=rider)
    return outs[0], outs[1], carried


def _ssd_bwd(dy, dxa, xc, dtraw, states, dtb, ap, n_heads, name, rider=None):
    length = xc.shape[0]
    q = SSD_CHUNK
    nc = length // q
    hpg, wg, _, b_blk0 = _ssd_tiles(xc, n_heads)
    c_blk0 = b_blk0 + SSD_GROUPS
    npair = hpg // 2

    def body(dy_ref, dxa_ref, x_ref, b_ref, c_ref, dt_ref, st_ref, dtb_ref, ap_ref,
             dx_ref, db_ref, dc_ref, ddt_ref, ddtb_ref, dap_ref, ds_ref, el_ref, er_ref):
        @pl.when(pl.program_id(1) == 0)
        def _():
            ds_ref[...] = jnp.zeros_like(ds_ref)
            ddtb_ref[...] = jnp.zeros_like(ddtb_ref)
            dap_ref[...] = jnp.zeros_like(dap_ref)

        b_t = b_ref[...].astype(BF16)
        c_t = c_ref[...].astype(BF16)
        dtraw_t = dt_ref[...]
        ck = _SsdChunk(dtraw_t, dtb_ref[...], ap_ref[...], b_t, c_t, wg)
        lane = _iota2((q, q), 1)
        to_head = _head_of_lane(wg)

        def per_head(v):
            return _pdot(_split(v, 2), to_head, _NT)

        x_all, dy_all = x_ref[...], dy_ref[...]
        dy_b = dy_all.astype(BF16)
        xd = x_all * ck.dt_l
        xd_b = xd.astype(BF16)
        s_prev = st_ref[0, 0]
        sp_b = s_prev.astype(BF16)
        ds1 = ds_ref[...]
        ds1_b = ds1.astype(BF16)
        ecum, wrem = jnp.exp(ck.cum_l), jnp.exp(ck.rem_l)
        dscores = jnp.zeros((q, q), F32)
        for i in range(npair):
            sl = slice(i * LANES, (i + 1) * LANES)
            dym = _by_head(dy_b[:, sl], lane)
            dwm2 = _dot(dym, xd_b[:, sl], _NT)
            wms = []
            for hh in range(2):
                h = 2 * i + hh
                decay = ck.decay(h)
                wm = ck.scores * decay
                dwm = dwm2[hh * q:(hh + 1) * q]
                dscores = dscores + dwm * decay
                e = (dwm * wm).astype(BF16)
                el_ref[:, h * q:(h + 1) * q] = e
                er_ref[h * q:(h + 1) * q, :] = e
                wms.append(wm.astype(BF16))
            dx_ref[:, sl] = _dot(jnp.concatenate(wms, axis=0), dym, _TN)
        put = (_iota2((hpg * q, q), 1) == _iota2((hpg * q, q), 0) // q).astype(BF16)
        dcum = _dot(el_ref[...], put) - _dot(er_ref[...], put, _TN)
        t_mat = _dot(c_t, sp_b, _NT)
        d_t = (dy_all * ecum).astype(BF16)
        dc_acc = _dot(d_t, sp_b)
        ds_prev = _dot(d_t, c_t, _TN)
        dcum = dcum + per_head(dy_all * t_mat * ecum)
        ds_prev = ds_prev + ck.grow * ds1
        zs = jnp.sum(ds1 * s_prev * ck.grow, axis=1, keepdims=True)
        dtot = _pdotr(jnp.ones((q, wg), BF16), _split(zs * _head_of_row(wg, F32), 2))
        xw = xd * wrem
        dxw = _dot(b_t, ds1_b, _NT)
        db_acc = _dot(xw.astype(BF16), ds1_b)
        dxd = dx_ref[...] + dxw * wrem
        drem = per_head(dxw * xw)
        dx_ref[...] = dxd * ck.dt_l + dxa_ref[...]
        ddt = per_head(dxd * x_all)
        ds_ref[...] = ds_prev
        ds_b = dscores.astype(BF16)
        dc_ref[...] = dc_acc + _dot(ds_b, b_t)
        db_ref[...] = db_acc + _dot(ds_b, c_t, _TN)
        dla = (_dot3r(ck.tril.astype(BF16), dcum, _TN) + _dot3r(ck.upper.astype(BF16), drem, _TN) + dtot)
        dt = ck.dt
        ddt = ddt + dla * ap_ref[...]
        dap_ref[...] += jnp.sum(dla * dt, axis=0, keepdims=True)
        ddtraw = ddt * _sigmoid(dtraw_t + dtb_ref[...])
        ddt_ref[...] = ddtraw.astype(ddt_ref.dtype)
        ddtb_ref[...] += jnp.sum(ddtraw, axis=0, keepdims=True)

    rev = lambda g, c: (nc - 1 - c, g)
    outs, carried = _hosted_call(
        body, name=name, grid=(SSD_GROUPS, nc),
        in_specs=[pl.BlockSpec((q, wg), rev),
                  pl.BlockSpec((q, wg), rev),
                  pl.BlockSpec((q, wg), rev),
                  pl.BlockSpec((q, SSD_STATE), lambda g, c: (nc - 1 - c, b_blk0 + g)),
                  pl.BlockSpec((q, SSD_STATE), lambda g, c: (nc - 1 - c, c_blk0 + g)),
                  pl.BlockSpec((q, LANES), rev),
                  pl.BlockSpec((1, 1, wg, SSD_STATE), lambda g, c: (nc - 1 - c, g, 0, 0)),
                  pl.BlockSpec((1, LANES), lambda g, c: (0, g)),
                  pl.BlockSpec((1, LANES), lambda g, c: (0, g))],
        out_specs=[pl.BlockSpec((q, wg), rev),
                   pl.BlockSpec((q, SSD_STATE), rev),
                   pl.BlockSpec((q, SSD_STATE), rev),
                   pl.BlockSpec((q, LANES), rev),
                   pl.BlockSpec((1, LANES), lambda g, c: (0, g)),
                   pl.BlockSpec((1, LANES), lambda g, c: (0, g))],
        out_shape=[jax.ShapeDtypeStruct((length, SSD_GROUPS * wg), F32),
                   jax.ShapeDtypeStruct((length, SSD_GROUPS * SSD_STATE), F32),
                   jax.ShapeDtypeStruct((length, SSD_GROUPS * SSD_STATE), F32),
                   jax.ShapeDtypeStruct((length, SSD_GROUPS * LANES), BF16),
                   jax.ShapeDtypeStruct((1, SSD_GROUPS * LANES), F32),
                   jax.ShapeDtypeStruct((1, SSD_GROUPS * LANES), F32)],
        scratch_shapes=[pltpu.VMEM((wg, SSD_STATE), F32), pltpu.VMEM((q, hpg * q), BF16),
                        pltpu.VMEM((hpg * q, q), BF16)],
        sem=("parallel", "arbitrary"), args=(dy, dxa, xc, xc, xc, dtraw, states, dtb, ap), rider=rider)
    return tuple(outs) + (carried,)


def _fox_cumsum(fraw, bf, name):
    length = fraw.shape[0]
    q = 128

    def body(f_ref, b_ref, o_ref, carry_ref):
        @pl.when(pl.program_id(0) == 0)
        def _():
            carry_ref[...] = jnp.zeros_like(carry_ref)

        lf = -_softplus(-(f_ref[...] + b_ref[...]))
        r, c = _iota2((q, q), 0), _iota2((q, q), 1)
        o_ref[...] = _dot3r((c <= r).astype(BF16), lf) + carry_ref[...]
        carry_ref[...] += jnp.sum(lf, axis=0, keepdims=True)

    return pl.pallas_call(
        body, name=name, grid=(length // q,),
        in_specs=[pl.BlockSpec((q, LANES), lambda i: (i, 0)), pl.BlockSpec((1, LANES), lambda i: (0, 0))],
        out_specs=pl.BlockSpec((q, LANES), lambda i: (i, 0)),
        out_shape=jax.ShapeDtypeStruct((length, LANES), F32),
        scratch_shapes=[pltpu.VMEM((1, LANES), F32)],
        compiler_params=_params(("arbitrary",)),
    )(fraw, bf)


def _fox_cumsum_bwd(dc, fraw, bf, name):
    length = fraw.shape[0]
    heads = dc.shape[0]
    q = 128
    nt = length // q

    def body(d_ref, f_ref, b_ref, o_ref, db_ref, carry_ref):
        @pl.when(pl.program_id(0) == 0)
        def _():
            carry_ref[...] = jnp.zeros_like(carry_ref)
            db_ref[...] = jnp.zeros_like(db_ref)

        d = jnp.concatenate([d_ref[...], jnp.zeros((LANES - heads, q), F32)], axis=0).T
        r, c = _iota2((q, q), 0), _iota2((q, q), 1)
        dlf = _dot3r((c >= r).astype(BF16), d) + carry_ref[...]
        carry_ref[...] += jnp.sum(d, axis=0, keepdims=True)
        df = dlf * _sigmoid(-(f_ref[...] + b_ref[...]))
        o_ref[...] = df.astype(o_ref.dtype)
        db_ref[...] += jnp.sum(df, axis=0, keepdims=True)

    rev = lambda i: (nt - 1 - i, 0)
    return pl.pallas_call(
        body, name=name, grid=(nt,),
        in_specs=[pl.BlockSpec((heads, q), lambda i: (0, nt - 1 - i)), pl.BlockSpec((q, LANES), rev),
                  pl.BlockSpec((1, LANES), lambda i: (0, 0))],
        out_specs=[pl.BlockSpec((q, LANES), rev), pl.BlockSpec((1, LANES), lambda i: (0, 0))],
        out_shape=[jax.ShapeDtypeStruct((length, LANES), BF16), jax.ShapeDtypeStruct((1, LANES), F32)],
        scratch_shapes=[pltpu.VMEM((1, LANES), F32)],
        compiler_params=_params(("arbitrary",)),
    )(dc, fraw, bf)


def _fox_scores(q_ref, k_ref, cq_ref, ck_ref, diagonal):
    scale = 1.0 / math.sqrt(FOX_HEAD_DIM)
    s = _dot(q_ref[...].astype(BF16), k_ref[...].astype(BF16), _NT) * scale + (cq_ref[0] - ck_ref[0])
    if diagonal:
        s = jnp.where(_iota2(s.shape, 1) <= _iota2(s.shape, 0), s, -jnp.inf)
    return s


def _fox_tiles(i, j, step):
    @pl.when(j < i)
    def _():
        step(False)

    @pl.when(j == i)
    def _():
        step(True)


def _fox_fwd(qkvg, c_col, c_row, n_heads, name):
    length = qkvg.shape[0]
    hd = FOX_HEAD_DIM
    tq = _pick(length, (FOX_TILE, 512, 256, 128))
    nq = length // tq

    def body(q_ref, k_ref, v_ref, cq_ref, ck_ref, o_ref, lse_ref, m_ref, l_ref, acc_ref):
        i, j = pl.program_id(1), pl.program_id(2)

        @pl.when(j == 0)
        def _():
            m_ref[...] = jnp.full_like(m_ref, -jnp.inf)
            l_ref[...] = jnp.zeros_like(l_ref)
            acc_ref[...] = jnp.zeros_like(acc_ref)

        def step(diagonal):
            s = _fox_scores(q_ref, k_ref, cq_ref, ck_ref, diagonal)
            m_new = jnp.maximum(m_ref[...], jnp.max(s, axis=1, keepdims=True))
            alpha = jnp.exp(m_ref[...] - m_new)
            p = jnp.exp(s - m_new)
            l_ref[...] = alpha * l_ref[...] + jnp.sum(p, axis=1, keepdims=True)
            acc_ref[...] = alpha * acc_ref[...] + _dot(p.astype(BF16), v_ref[...].astype(BF16))
            m_ref[...] = m_new

        _fox_tiles(i, j, step)

        @pl.when(j == nq - 1)
        def _():
            o_ref[...] = acc_ref[...] / l_ref[...]
            lse_ref[0] = m_ref[...] + jnp.log(l_ref[...])

    kmap = lambda off: (lambda h, i, j: (jnp.minimum(j, i), off * n_heads + h))
    return pl.pallas_call(
        body, name=name, grid=(n_heads, nq, nq),
        in_specs=[pl.BlockSpec((tq, hd), lambda h, i, j: (i, h)),
                  pl.BlockSpec((tq, hd), kmap(1)),
                  pl.BlockSpec((tq, hd), kmap(2)),
                  pl.BlockSpec((1, tq, 1), lambda h, i, j: (h, i, 0)),
                  pl.BlockSpec((1, 1, tq), lambda h, i, j: (h, 0, jnp.minimum(j, i)))],
        out_specs=[pl.BlockSpec((tq, hd), lambda h, i, j: (i, h)),
                   pl.BlockSpec((1, tq, 1), lambda h, i, j: (h, i, 0))],
        out_shape=[jax.ShapeDtypeStruct((length, n_heads * hd), F32),
                   jax.ShapeDtypeStruct((n_heads, length, 1), F32)],
        scratch_shapes=[pltpu.VMEM((tq, 1), F32), pltpu.VMEM((tq, 1), F32), pltpu.VMEM((tq, hd), F32)],
        compiler_params=_params(("parallel", "parallel", "arbitrary")),
    )(qkvg, qkvg, qkvg, c_col, c_row)


def _fox_bwd_q(qkvg, datt, lse, c_col, c_row, n_heads, name, rider=None):
    length = qkvg.shape[0]
    hd = FOX_HEAD_DIM
    tq = _pick(length, (FOX_TILE, 512, 256, 128))
    nq = length // tq
    scale = 1.0 / math.sqrt(hd)

    def body(q_ref, k_ref, v_ref, do_ref, lse_ref, cq_ref, ck_ref, dq_ref, dsum_ref, a1_ref, a2_ref, d_ref):
        i, j = pl.program_id(1), pl.program_id(2)

        @pl.when(j == 0)
        def _():
            a1_ref[...] = jnp.zeros_like(a1_ref)
            a2_ref[...] = jnp.zeros_like(a2_ref)
            d_ref[...] = jnp.zeros_like(d_ref)

        def step(diagonal):
            s = _fox_scores(q_ref, k_ref, cq_ref, ck_ref, diagonal)
            p = jnp.exp(s - lse_ref[0])
            pdp = p * _dot(do_ref[...].astype(BF16), v_ref[...].astype(BF16), _NT)
            d_ref[...] += jnp.sum(pdp, axis=1, keepdims=True)
            k_b = k_ref[...].astype(BF16)
            a1_ref[...] += _dot(pdp.astype(BF16), k_b)
            a2_ref[...] += _dot(p.astype(BF16), k_b)

        _fox_tiles(i, j, step)

        @pl.when(j == nq - 1)
        def _():
            dq_ref[...] = ((a1_ref[...] - d_ref[...] * a2_ref[...]) * scale).astype(dq_ref.dtype)
            dsum_ref[0] = d_ref[...]

    kmap = lambda off: (lambda h, i, j: (jnp.minimum(j, i), off * n_heads + h))
    qmap = lambda h, i, j: (i, h)
    col = pl.BlockSpec((1, tq, 1), lambda h, i, j: (h, i, 0))
    outs, carried = _hosted_call(
        body, name=name, grid=(n_heads, nq, nq),
        in_specs=[pl.BlockSpec((tq, hd), qmap), pl.BlockSpec((tq, hd), kmap(1)), pl.BlockSpec((tq, hd), kmap(2)),
                  pl.BlockSpec((tq, hd), qmap), col, col,
                  pl.BlockSpec((1, 1, tq), lambda h, i, j: (h, 0, jnp.minimum(j, i)))],
        out_specs=[pl.BlockSpec((tq, hd), qmap), col],
        out_shape=[jax.ShapeDtypeStruct((length, n_heads * hd), BF16),
                   jax.ShapeDtypeStruct((n_heads, length, 1), F32)],
        scratch_shapes=[pltpu.VMEM((tq, hd), F32), pltpu.VMEM((tq, hd), F32), pltpu.VMEM((tq, 1), F32)],
        sem=("parallel", "parallel", "arbitrary"), args=(qkvg, qkvg, qkvg, datt, lse, c_col, c_row), rider=rider)
    return outs[0], outs[1], carried


def _fox_bwd_kv(qkvg, datt, lse, dsum, c_col, c_row, n_heads, name):
    length = qkvg.shape[0]
    hd = FOX_HEAD_DIM
    tq = _pick(length, (FOX_TILE, 512, 256, 128))
    nq = length // tq
    scale = 1.0 / math.sqrt(hd)

    def body(q_ref, k_ref, v_ref, do_ref, lse_ref, dsum_ref, cq_ref, ck_ref, dk_ref, dv_ref, dck_ref,
             dk_acc, dv_acc, dc_acc):
        j, i = pl.program_id(1), pl.program_id(2)

        @pl.when(i == 0)
        def _():
            dk_acc[...] = jnp.zeros_like(dk_acc)
            dv_acc[...] = jnp.zeros_like(dv_acc)
            dc_acc[...] = jnp.zeros_like(dc_acc)

        def step(diagonal):
            s = _fox_scores(q_ref, k_ref, cq_ref, ck_ref, diagonal)
            p = jnp.exp(s - lse_ref[0])
            do_b = do_ref[...].astype(BF16)
            dv_acc[...] += _dot(p.astype(BF16), do_b, _TN)
            dp = _dot(do_b, v_ref[...].astype(BF16), _NT)
            ds = p * (dp - dsum_ref[0])
            dk_acc[...] += _dot(ds.astype(BF16), q_ref[...].astype(BF16), _TN)
            dc_acc[...] -= jnp.sum(ds, axis=0, keepdims=True)

        _fox_tiles(i, j, step)

        @pl.when(i == nq - 1)
        def _():
            dk_ref[...] = (dk_acc[...] * scale).astype(dk_ref.dtype)
            dv_ref[...] = dv_acc[...].astype(dv_ref.dtype)
            dck_ref[0] = dc_acc[...]

    qmap = lambda h, j, i: (jnp.maximum(i, j), h)
    kmap = lambda off: (lambda h, j, i: (j, off * n_heads + h))
    col = pl.BlockSpec((1, tq, 1), lambda h, j, i: (h, jnp.maximum(i, j), 0))
    return pl.pallas_call(
        body, name=name, grid=(n_heads, nq, nq),
        in_specs=[pl.BlockSpec((tq, hd), qmap), pl.BlockSpec((tq, hd), kmap(1)), pl.BlockSpec((tq, hd), kmap(2)),
                  pl.BlockSpec((tq, hd), qmap), col, col, col,
                  pl.BlockSpec((1, 1, tq), lambda h, j, i: (h, 0, j))],
        out_specs=[pl.BlockSpec((tq, hd), lambda h, j, i: (j, h)), pl.BlockSpec((tq, hd), lambda h, j, i: (j, h)),
                   pl.BlockSpec((1, 1, tq), lambda h, j, i: (h, 0, j))],
        out_shape=[jax.ShapeDtypeStruct((length, n_heads * hd), BF16)] * 2
        + [jax.ShapeDtypeStruct((n_heads, 1, length), F32)],
        scratch_shapes=[pltpu.VMEM((tq, hd), F32), pltpu.VMEM((tq, hd), F32), pltpu.VMEM((1, tq), F32)],
        compiler_params=_params(("parallel", "parallel", "arbitrary")),
    )(qkvg, qkvg, qkvg, datt, lse, dsum, c_col, c_row)


def _row(v):
    return v.reshape(1, -1).astype(F32)


def _pad_heads(v, per_group):
    lead = v.shape[:-1]
    v = v.reshape(lead + (SSD_GROUPS, per_group))
    v = jnp.pad(v, [(0, 0)] * len(lead) + [(0, 0), (0, LANES - per_group)])
    return v.reshape(lead + (SSD_GROUPS * LANES,))


def _unpad_heads(v, per_group):
    lead = v.shape[:-1]
    return v.reshape(lead + (SSD_GROUPS, LANES))[..., :per_group].reshape(lead + (SSD_GROUPS * per_group,))


class _NoOverlap:
    def gather_rider(self, host):
        return None

    def gathered(self, host, carried):
        return {}

    def reduce_rider(self, host, grads):
        return None

    def reduced(self, host, carried):
        pass


def _pad_head_rows(w, per_group):
    w = w.reshape(SSD_GROUPS, per_group, w.shape[1])
    return jnp.pad(w, ((0, 0), (0, LANES - per_group), (0, 0))).reshape(SSD_GROUPS * LANES, w.shape[2])


def _unpad_head_rows(w, per_group):
    return w.reshape(SSD_GROUPS, LANES, w.shape[1])[:, :per_group].reshape(SSD_GROUPS * per_group, w.shape[1])


def _local_step(x, tgt, wb, sm, plan=None):
    plan = plan or _NoOverlap()
    wb = dict(wb)
    length, d = x.shape
    mix = 2 * d
    s5w = mix // 4
    ssdw = mix - s5w
    xbcw = ssdw + 2 * SSD_GROUPS * SSD_STATE
    n_ssd = ssdw // SSD_HEAD_DIM
    hpg = n_ssd // SSD_GROUPS
    fw = d
    o1, o2, o3 = 2 * s5w, 2 * s5w + ssdw, 2 * s5w + ssdw + xbcw
    w0t = wb["w0T"]
    w0_dt = _pad_head_rows(w0t[o3:], hpg)
    n_fox = fw // FOX_HEAD_DIM
    s5g = s5w // S5_GROUP
    s5s = s5g * S5_STATE
    grads = {}

    s5_in = (sm["l0_s5_lambda_re"], sm["l0_s5_lambda_im"], sm["l0_s5_log_step"], sm["l0_s5_b_re"],
             sm["l0_s5_b_im"], sm["l0_s5_c_re"], sm["l0_s5_c_im"])
    (a_re, a_im, bd_re, bd_im, cd_re, cd_imn), s5_vjp = jax.vjp(_s5_prepare, *s5_in)
    nj = bd_re.shape[0]
    bd_re_b, bd_im_b, cd_re_b, cd_imn_b = (t.astype(BF16) for t in (bd_re, bd_im, cd_re, cd_imn))
    tr = lambda t: jnp.swapaxes(t, 1, 2)
    dvec = _row(sm["l0_s5_d"])
    bglu = _row(sm["l0_s5_b_glu"])
    conv_w = sm["l0_ssd_conv_w"]
    conv_b = _row(sm["l0_ssd_conv_b"])

    def ssd_prepare(dt_bias, a_log, dd):
        return (_pad_heads(_row(dt_bias), hpg), _pad_heads(_row(-jnp.exp(a_log)), hpg),
                jnp.repeat(_row(dd), SSD_HEAD_DIM, axis=1))

    (dtb, ap, dpar), ssd_vjp = jax.vjp(ssd_prepare, sm["l0_ssd_dt_bias"], sm["l0_ssd_a_log"], sm["l0_ssd_d"])
    ssd_nw = _row(sm["l0_ssd_norm_w"])
    nw0, nw1, fnw = _row(sm["l0_norm_w"]), _row(sm["l1_norm_w"]), _row(sm["final_norm_w"])
    bf = jnp.pad(_row(sm["l1_fox_b_f"]), ((0, 0), (0, LANES - n_fox)))

    h0 = _norm_fwd(x, nw0, "l0_norm")
    def gathering(host, *args, **kw):
        rider = plan.gather_rider(host)
        out = _matmul(*args, name=host, rider=rider, **kw)
        if rider is None:
            return out
        wb.update(plan.gathered(host, out[1]))
        return out[0]

    def reducing(host, *args, **kw):
        rider = plan.reduce_rider(host, grads)
        out = _matmul(*args, name=host, rider=rider, **kw)
        if rider is None:
            return out
        plan.reduced(host, out[1])
        return out[0]

    ug = gathering("l0_in_ug", h0, w0t, mode="nt", dims=(length, o1, d))
    z = gathering("l0_in_z", h0, w0t, mode="nt", dims=(length, ssdw, d), b_off=(o1, 0))
    xbc = gathering("l0_in_xbc", h0, w0t, mode="nt", dims=(length, xbcw, d), b_off=(o2, 0))
    dtraw = _matmul(h0, w0_dt, mode="nt", name="l0_in_dt")
    u_win, gate_win = (ug, s5w, 0), (ug, s5w, 1)

    shape3 = (length, s5s // LANES, LANES)
    bu_re = _bdmm([(ug, 0)], [bd_re_b], name="s5_bu_re").reshape(shape3)
    bu_im = _bdmm([(ug, 0)], [bd_im_b], name="s5_bu_im").reshape(shape3)
    s_re3, s_im3 = _s5_scan_fwd(bu_re, bu_im, a_re, a_im, "s5_scan")
    s_re, s_im = s_re3.reshape(length, s5s), s_im3.reshape(length, s5s)
    yc = _bdmm([s_re, s_im], [cd_re_b, cd_imn_b], name="s5_y")
    gl = _rowwise(_f_s5_gelu, [yc, u_win], [dvec], [(s5w, BF16)], tl=256, name="s5_gelu")[0]
    t_glu = _matmul(gl, wb["w_glu"], name="s5_glu")
    s5o = _rowwise(_f_s5_out, [yc, u_win, t_glu, gate_win], [dvec, bglu], [(s5w, BF16)], tl=256,
                   name="s5_out")[0]

    xc = _conv_fwd(xbc, conv_w, conv_b, "ssd_conv")
    y_ssd, states, carried = _ssd_fwd(xc, dtraw, dtb, ap, n_ssd, "ssd_scan", rider=plan.gather_rider("ssd_scan"))
    wb.update(plan.gathered("ssd_scan", carried))
    wg = ssdw // SSD_GROUPS
    ssdo = _rowwise(_f_ssd_out, [y_ssd, (xc, wg, 0), z], [dpar, ssd_nw], [(wg, BF16)], tl=256,
                    ncol=SSD_GROUPS, name="ssd_out")[0]
    x1 = _matmul(s5o, wb["w0_out"], dims=(length, d, s5w), addend=x, name="l0_out_s5")
    x1 = _matmul(ssdo, wb["w0_out"], dims=(length, d, ssdw), b_off=(s5w, 0), addend=x1, name="l0_out_ssd")

    h1 = _norm_fwd(x1, nw1, "l1_norm")
    w1t = wb["w1T"]
    w1_f = jnp.pad(w1t[4 * fw:], ((0, LANES - n_fox), (0, 0)))
    qkvg = _matmul(h1, w1t, mode="nt", dims=(length, 4 * fw, d), name="l1_in")
    fraw = _matmul(h1, w1_f, mode="nt", name="l1_in_f")
    cc = _fox_cumsum(fraw, bf, "fox_cumsum")
    c_t = cc[:, :n_fox].T
    c_col, c_row = c_t[:, :, None], c_t[:, None, :]
    att, lse = _fox_fwd(qkvg, c_col, c_row, n_fox, "fox_fwd")
    gate1_win = (qkvg, fw, 3)
    fox_o = _rowwise(_f_fox_out, [att, gate1_win], [], [(fw, BF16)], tl=256, name="fox_out")[0]
    x2 = _matmul(fox_o, wb["w1_out"], addend=x1, name="l1_out")

    def loss_fn(xt, tt, wt):
        def f(xx, ww):
            err = _f_norm(xx, ww) - tt
            return (0.5 / d) * err * err
        lanes, vjp = jax.vjp(f, xt, wt)
        dx, dw = vjp(jnp.ones_like(lanes))
        return dx, dx, jnp.sum(lanes, axis=0, keepdims=True), dw

    dx2, dx2b, loss_lanes, g_fnw = _rowwise(loss_fn, [x2, tgt], [fnw], [(d, F32), (d, BF16)], [d, d],
                                            tl=128, name="loss_head")
    grads["final_norm_w"] = g_fnw

    grads["l1_w_out"] = _matmul(fox_o, dx2b, mode="tn", name="l1_out_dw")
    do1 = _matmul(dx2b, wb["w1_out"], mode="nt", name="l1_out_dx")

    def fox_out_bwd(at, gt, dt_):
        _, vjp = jax.vjp(_f_fox_out, at, gt)
        return vjp(dt_)

    datt, dgate1 = _rowwise(fox_out_bwd, [att, gate1_win, do1], [], [(fw, F32), (fw, BF16)], tl=256,
                            name="fox_out_bwd")
    dq, dsum, carried = _fox_bwd_q(qkvg, datt, lse, c_col, c_row, n_fox, "fox_bwd_q",
                                   rider=plan.reduce_rider("fox_bwd_q", grads))
    plan.reduced("fox_bwd_q", carried)
    dk, dv, dck = _fox_bwd_kv(qkvg, datt, lse, dsum, c_col, c_row, n_fox, "fox_bwd_kv")
    dfraw, g_bf = _fox_cumsum_bwd(dck.reshape(n_fox, length), fraw, bf, "fox_cumsum_bwd")
    grads["l1_fox_b_f"] = g_bf[:, :n_fox]
    dsegs = [dq, dk, dv, dgate1]
    g1, n1 = None, w1t.shape[0]
    for i, s in enumerate(dsegs):
        g1 = _matmul(s, h1, mode="tn", rows_of=(n1, i * fw, g1), name=f"l1_in_dw{i}")
    g1_f = _matmul(dfraw, h1, mode="tn", name="l1_in_dwf")[:n_fox]
    grads["l1_w_inT"] = lax.dynamic_update_slice(g1, g1_f, (4 * fw, 0))
    dh1 = _matmul(dfraw, w1_f, mode="nn", name="l1_in_dxf")
    for i, s in enumerate(dsegs):
        dh1 = _matmul(s, w1t, mode="nn", dims=(length, d, fw), b_off=(i * fw, 0), addend=dh1,
                      name=f"l1_in_dx{i}")
    dx1, dx1b, grads["l1_norm_w"] = _norm_bwd(x1, dh1, dx2, nw1, "l1_norm_bwd")

    g_out = _matmul(s5o, dx1b, mode="tn", rows_of=(mix, 0, None), name="l0_out_dw_s5")
    grads["l0_w_out"] = _matmul(ssdo, dx1b, mode="tn", rows_of=(mix, s5w, g_out), name="l0_out_dw_ssd")
    ds5o = _matmul(dx1b, wb["w0_out"], mode="nt", dims=(length, s5w, d), name="l0_out_dx_s5")
    dssdo = reducing("l0_out_dx_ssd", dx1b, wb["w0_out"], mode="nt", dims=(length, ssdw, d), b_off=(s5w, 0))

    def ssd_out_bwd(yt, xt, zt, dt_, dp, nw):
        _, vjp = jax.vjp(_f_ssd_out, yt, xt, zt, dp, nw)
        return vjp(dt_)

    dy_ssd, dxa, dz, g_dpar, g_ssd_nw = _rowwise(
        ssd_out_bwd, [y_ssd, (xc, wg, 0), z, dssdo], [dpar, ssd_nw],
        [(wg, F32), (wg, F32), (wg, BF16)], [wg, wg], tl=128, ncol=SSD_GROUPS, name="ssd_out_bwd")
    grads["l0_ssd_norm_w"] = g_ssd_nw
    dxs, db_ssd, dc_ssd, ddtraw, g_dtb, g_ap, carried = _ssd_bwd(
        dy_ssd, dxa, xc, dtraw, states, dtb, ap, n_ssd, "ssd_scan_bwd",
        rider=plan.reduce_rider("ssd_scan_bwd", grads))
    plan.reduced("ssd_scan_bwd", carried)
    g_dt_bias, g_a_log, g_ssd_d = ssd_vjp((g_dtb, g_ap, g_dpar))
    grads["l0_ssd_dt_bias"], grads["l0_ssd_a_log"], grads["l0_ssd_d"] = g_dt_bias, g_a_log, g_ssd_d
    dxc = jnp.concatenate([dxs, db_ssd, dc_ssd], axis=1)
    dpre, grads["l0_ssd_conv_w"], grads["l0_ssd_conv_b"] = _conv_bwd_pre(dxc, xbc, conv_w, conv_b, "ssd_conv_bwd_pre")
    dxbc = _conv_bwd_in(dpre, conv_w, "ssd_conv_bwd_in")

    def s5_out_bwd(yt, ut, tt, gt, dt_, dv_, bg):
        _, vjp = jax.vjp(_f_s5_out, yt, ut, tt, gt, dv_, bg)
        return vjp(dt_)

    dyc_a, du_a, dt_glu, dgate, g_dvec_a, g_bglu = _rowwise(
        s5_out_bwd, [yc, u_win, t_glu, gate_win, ds5o], [dvec, bglu],
        [(s5w, F32), (s5w, F32), (s5w, BF16), (s5w, BF16)], [s5w, s5w], tl=128, name="s5_out_bwd")
    grads["l0_s5_b_glu"] = g_bglu
    grads["l0_s5_w_glu"] = _matmul(gl, dt_glu, mode="tn", name="s5_glu_dw")
    dgl = _matmul(dt_glu, wb["w_glu"], mode="nt", name="s5_glu_dx")

    def s5_gelu_bwd(yt, ut, dg, dya, dua, dv_):
        _, vjp = jax.vjp(_f_s5_gelu, yt, ut, dv_)
        dy_, du_, ddv = vjp(dg)
        return dy_ + dya, du_ + dua, ddv

    dyc, du_ab, g_dvec_b = _rowwise(s5_gelu_bwd, [yc, u_win, dgl, dyc_a, du_a], [dvec],
                                    [(s5w, F32), (s5w, F32)], [s5w], tl=128, name="s5_gelu_bwd")
    ds_re = _bdmm([dyc], [tr(cd_re_b)], name="s5_ds_re").reshape(shape3)
    ds_im = _bdmm([dyc], [tr(cd_imn_b)], name="s5_ds_im").reshape(shape3)
    kin_s, kin_u = s5s // nj, s5w // nj
    g_cd_re = _bdmm_tn_sized(s_re, dyc, nj, kin_s, kin_u, 0, "s5_dcd_re")
    g_cd_imn = _bdmm_tn_sized(s_im, dyc, nj, kin_s, kin_u, 0, "s5_dcd_im")
    g_re3, g_im3, g_a_re, g_a_im = _s5_scan_bwd(ds_re, ds_im, s_re3, s_im3, a_re, a_im, "s5_scan_bwd")
    g_re, g_im = g_re3.reshape(length, s5s), g_im3.reshape(length, s5s)
    du = _bdmm([g_re, g_im], [tr(bd_re_b), tr(bd_im_b)], addend=du_ab, out_dtype=BF16, name="s5_du")
    g_bd_re = _bdmm_tn_sized(ug, g_re, nj, kin_u, kin_s, 0, "s5_dbd_re")
    g_bd_im = _bdmm_tn_sized(ug, g_im, nj, kin_u, kin_s, 0, "s5_dbd_im")
    s5_g = s5_vjp((g_a_re, g_a_im, g_bd_re, g_bd_im, g_cd_re, g_cd_imn))
    for nm, g in zip(("lambda_re", "lambda_im", "log_step", "b_re", "b_im", "c_re", "c_im"), s5_g):
        grads["l0_s5_" + nm] = g
    grads["l0_s5_d"] = (g_dvec_a + g_dvec_b).reshape(sm["l0_s5_d"].shape)

    g0, n0 = None, w0t.shape[0]
    for nm, s, off in (("u", du, 0), ("g", dgate, s5w), ("z", dz, o1), ("xbc", dxbc, o2)):
        g0 = _matmul(s, h0, mode="tn", rows_of=(n0, off, g0), name="l0_in_dw_" + nm)
    g0_dt = _unpad_head_rows(_matmul(ddtraw, h0, mode="tn", name="l0_in_dw_dt"), hpg)
    grads["l0_w_inT"] = lax.dynamic_update_slice(g0, g0_dt, (o3, 0))
    dh0 = _matmul(ddtraw, w0_dt, mode="nn", name="l0_in_dx_dt")
    for nm, s, off in (("u", du, 0), ("g", dgate, s5w), ("z", dz, o1), ("xbc", dxbc, o2)):
        dh0 = reducing("l0_in_dx_" + nm, s, w0t, mode="nn", dims=(length, d, s.shape[1]), b_off=(off, 0),
                       addend=dh0)
    dx, _, grads["l0_norm_w"] = _norm_bwd(x, dh0, dx1, nw0, "l0_norm_bwd")
    return loss_lanes, dx, grads


_ANY = pl.BlockSpec(memory_space=pl.ANY)


def _place():
    x, y, c = lax.axis_index("x"), lax.axis_index("y"), lax.axis_index("c")
    return x, y, c, [(1 - x, y), (x, 1 - y), (1 - x, 1 - y)]


def _remote(src, dst, send_sem, recv_sem, to):
    return pltpu.make_async_remote_copy(src_ref=src, dst_ref=dst, send_sem=send_sem, recv_sem=recv_sem,
                                        device_id=to, device_id_type=MESH)


def _comm_call(body, n_in, out_shape, n_sems, name):
    return pl.pallas_call(
        body, name=name, in_specs=[_ANY] * n_in, out_specs=[_ANY] * len(out_shape), out_shape=out_shape,
        scratch_shapes=[pltpu.SemaphoreType.DMA((k,)) for k in n_sems],
        compiler_params=pltpu.CompilerParams(has_side_effects=True),
    )


def _half(ref_or_shape, c):
    ch = ref_or_shape.shape[-1] // 2
    return pl.ds(pl.multiple_of(c * ch, LANES), ch)


def _gather_rider(shards):
    n = len(shards)

    def sends(ins, outs, sems):
        send, recv = sems[:2]
        x, y, c, chips = _place()
        me = 2 * x + y
        return [_remote(ins[a].at[:, _half(ins[a], c)], outs[a].at[me, :, _half(ins[a], c)],
                        send.at[3 * a + k], recv.at[3 * a + k], (px, py, c))
                for a in range(n) for k, (px, py) in enumerate(chips)]

    def start(ins, outs, sems):
        for cp in sends(ins, outs, sems):
            cp.start()

    def finish(ins, outs, sems):
        send, recv, fsend, frecv = sems
        x, y, c, chips = _place()
        passed = []
        for a in range(n):
            for k, (px, py) in enumerate(chips):
                got = outs[a].at[2 * px + py, :, _half(ins[a], c)]
                _remote(got, got, send.at[3 * a + k], recv.at[3 * a + k], (px, py, c)).wait_recv()
                cp = _remote(got, got, fsend.at[3 * a + k], frecv.at[3 * a + k], (x, y, 1 - c))
                cp.start()
                passed.append(cp)
        for a in range(n):
            for k, (px, py) in enumerate(chips):
                got = outs[a].at[2 * px + py, :, _half(ins[a], 1 - c)]
                _remote(got, got, fsend.at[3 * a + k], frecv.at[3 * a + k], (x, y, 1 - c)).wait_recv()
        for cp in sends(ins, outs, sems) + passed:
            cp.wait_send()

    out_shape = [jax.ShapeDtypeStruct((N_SHARD,) + s.shape, s.dtype) for s in shards]
    return _Rider(shards, out_shape, [3 * n] * 4, start, finish)


def _chip_rider(parts):
    n = len(parts)

    def copies(ins, outs, sems):
        send, recv = sems
        x, y, c, chips = _place()
        return [_remote(ins[a].at[2 * px + py], outs[a].at[k], send.at[3 * a + k], recv.at[3 * a + k], (px, py, c))
                for a in range(n) for k, (px, py) in enumerate(chips)]

    def start(ins, outs, sems):
        for cp in copies(ins, outs, sems):
            cp.start()

    def finish(ins, outs, sems):
        for cp in copies(ins, outs, sems):
            cp.wait()

    out_shape = [jax.ShapeDtypeStruct((3,) + p.shape[1:], p.dtype) for p in parts]
    return _Rider(parts, out_shape, [3 * n] * 2, start, finish)


def _run_rider(rider, name):
    n_in, n_out = len(rider.inputs), len(rider.out_shape)

    def body(*refs):
        ins, outs, sems = refs[:n_in], refs[n_in:n_in + n_out], refs[n_in + n_out:]
        rider.start(ins, outs, sems)
        rider.finish(ins, outs, sems)

    return _comm_call(body, n_in, rider.out_shape, rider.sems, name)(*rider.inputs)


def _sibling_halves(grads, name):
    n = len(grads)

    def body(*refs):
        ins, outs = refs[:n], refs[n:2 * n]
        send, recv = refs[2 * n:]
        x, y, c, _ = _place()
        copies = [_remote(ins[a].at[:, :, _half(ins[a], 1 - c)], outs[a], send.at[a], recv.at[a], (x, y, 1 - c))
                  for a in range(n)]
        for cp in copies:
            cp.start()
        for cp in copies:
            cp.wait()

    out_shape = [jax.ShapeDtypeStruct(g.shape[:2] + (g.shape[2] // 2,), g.dtype) for g in grads]
    return _comm_call(body, n, out_shape, [n, n], name)(*grads)


def _join_halves(halves, name):
    n = len(halves)

    def body(*refs):
        outs = refs[n:2 * n]
        send, recv = refs[2 * n:]
        x, y, c, _ = _place()
        mine = [outs[a].at[:, _half(outs[a], c)] for a in range(n)]
        copies = [_remote(mine[a], mine[a], send.at[a], recv.at[a], (x, y, 1 - c)) for a in range(n)]
        for cp in copies:
            cp.start()
        for a in range(n):
            copies[a].wait_send()
            got = outs[a].at[:, _half(outs[a], 1 - c)]
            _remote(got, got, send.at[a], recv.at[a], (x, y, 1 - c)).wait_recv()

    return pl.pallas_call(
        body, name=name, in_specs=[_ANY] * n, out_specs=[_ANY] * n,
        out_shape=[jax.ShapeDtypeStruct(h.shape, h.dtype) for h in halves],
        input_output_aliases={a: a for a in range(n)},
        scratch_shapes=[pltpu.SemaphoreType.DMA((n,)), pltpu.SemaphoreType.DMA((n,))],
        compiler_params=pltpu.CompilerParams(has_side_effects=True),
    )(*halves)


def _gather_all(buf, name):
    def body(in_ref, out_ref, send, recv, lsem):
        x, y, c, _ = _place()
        me = 4 * x + 2 * y + c
        local = pltpu.make_async_copy(in_ref, out_ref.at[me], lsem.at[0])
        local.start()
        copies = []
        for k in range(1, N_DEV):
            fx, fy, fc = (k >> 2) & 1, (k >> 1) & 1, k & 1
            peer = (x + fx - 2 * x * fx, y + fy - 2 * y * fy, c + fc - 2 * c * fc)
            cp = _remote(in_ref, out_ref.at[me], send.at[k - 1], recv.at[k - 1], peer)
            cp.start()
            copies.append((cp, 4 * peer[0] + 2 * peer[1] + peer[2]))
        for k, (cp, slot) in enumerate(copies):
            cp.wait_send()
            got = out_ref.at[slot]
            _remote(got, got, send.at[k], recv.at[k], (x, y, c)).wait_recv()
        local.wait()

    out_shape = [jax.ShapeDtypeStruct((N_DEV,) + buf.shape, buf.dtype)]
    return _comm_call(body, 1, out_shape, [N_DEV - 1, N_DEV - 1, 1], name)(buf)[0]


def _sum_slots(buf, name):
    slots, rows, _ = buf.shape
    tr = _pick(rows, (512, 256, 128, 64, 32, 16, 8))

    def body(b_ref, o_ref):
        acc = b_ref[0]
        for s in range(1, slots):
            acc = acc + b_ref[s]
        o_ref[...] = acc

    return pl.pallas_call(
        body, name=name, grid=(rows // tr,),
        in_specs=[pl.BlockSpec((slots, tr, LANES), lambda i: (0, i, 0))],
        out_specs=pl.BlockSpec((tr, LANES), lambda i: (i, 0)),
        out_shape=jax.ShapeDtypeStruct((rows, LANES), F32),
        compiler_params=_params(("parallel",)),
    )(buf)


def _tile2(rows, cols, n_bufs):
    tr = max(t for t in range(8, min(rows, 2048) + 1, 8) if rows % t == 0) if rows % 8 == 0 else rows
    budget = 24 * 1024 * 1024 // (8 * n_bufs * tr)
    tc = max([t for t in range(LANES, cols + 1, LANES) if cols % t == 0 and t <= budget] or [LANES])
    return tr, tc


def _elementwise(fn, ins, out_dtypes, name):
    rows, cols = ins[0].shape
    tr, tc = _tile2(rows, cols, len(ins) + len(out_dtypes))
    n_in = len(ins)

    def body(*refs):
        outs = fn(*[r[...] for r in refs[:n_in]])
        for r, v in zip(refs[n_in:], outs if isinstance(outs, (tuple, list)) else (outs,)):
            r[...] = v.astype(r.dtype)

    blk = pl.BlockSpec((tr, tc), lambda i, j: (i, j))
    return pl.pallas_call(
        body, name=name, grid=(rows // tr, cols // tc), in_specs=[blk] * n_in, out_specs=[blk] * len(out_dtypes),
        out_shape=[jax.ShapeDtypeStruct((rows, cols), dt) for dt in out_dtypes],
        compiler_params=_params(("parallel", "parallel")),
    )(*ins)


def _presum(grad, sib, name):
    ns, rows, ch = sib.shape
    tr, tc = _tile2(rows, ch, 3)
    nct = ch // tc

    def body(g_ref, r_ref, o_ref):
        o_ref[...] = (g_ref[...] + r_ref[...]).astype(o_ref.dtype)

    blk = pl.BlockSpec((1, tr, tc), lambda j, i, k: (j, i, k))
    return pl.pallas_call(
        body, name=name, grid=(ns, rows // tr, nct),
        in_specs=[pl.BlockSpec((1, tr, tc), lambda j, i, k: (j, i, lax.axis_index("c") * nct + k)), blk],
        out_specs=blk, out_shape=jax.ShapeDtypeStruct((ns, rows, ch), BF16),
        compiler_params=_params(("parallel", "parallel", "parallel")),
    )(grad, sib)


def _finish_half(grad, sib, others, name):
    _, rows, ch = sib.shape
    tr, tc = _tile2(rows, ch, 6)
    nct = ch // tc

    def body(g_ref, r_ref, q_ref, o_ref):
        acc = g_ref[0] + r_ref[0]
        for k in range(3):
            acc = acc + q_ref[k].astype(F32)
        o_ref[...] = acc

    core = lambda: lax.axis_index("c")
    chip = lambda: 2 * lax.axis_index("x") + lax.axis_index("y")
    return pl.pallas_call(
        body, name=name, grid=(rows // tr, nct),
        in_specs=[pl.BlockSpec((1, tr, tc), lambda i, k: (chip(), i, core() * nct + k)),
                  pl.BlockSpec((1, tr, tc), lambda i, k: (chip(), i, k)),
                  pl.BlockSpec((3, tr, tc), lambda i, k: (0, i, k))],
        out_specs=pl.BlockSpec((tr, tc), lambda i, k: (i, core() * nct + k)),
        out_shape=jax.ShapeDtypeStruct((rows, 2 * ch), F32),
        compiler_params=_params(("parallel", "parallel")),
    )(grad, sib, others)


def _cast_bf16(w, name):
    return _elementwise(lambda t: t, [w], [BF16], name)[0]


_WEIGHTS = ("l0_norm_w", "l0_w_in", "l0_s5_lambda_re", "l0_s5_lambda_im", "l0_s5_log_step", "l0_s5_b_re",
            "l0_s5_b_im", "l0_s5_c_re", "l0_s5_c_im", "l0_s5_d", "l0_s5_w_glu", "l0_s5_b_glu", "l0_ssd_conv_w",
            "l0_ssd_conv_b", "l0_ssd_dt_bias", "l0_ssd_a_log", "l0_ssd_d", "l0_ssd_norm_w", "l0_w_out",
            "l1_norm_w", "l1_w_in", "l1_fox_b_f", "l1_w_out", "final_norm_w")
_COL_SHARDED = ("l0_w_in", "l1_w_in")
_ROW_SHARDED = ("l0_s5_w_glu", "l0_w_out", "l1_w_out")
_BIG = ("l0_w_in", "l0_s5_w_glu", "l0_w_out", "l1_w_in", "l1_w_out")
_CONV = "l0_ssd_conv_w"
_SMALL = tuple(n for n in _WEIGHTS if n not in _BIG and n != _CONV)


def _pack(arrays):
    flat = jnp.concatenate([a.reshape(-1).astype(F32) for a in arrays])
    size = flat.shape[0]
    padded = -(-size // (512 * LANES)) * (512 * LANES)
    return jnp.pad(flat, (0, padded - size)).reshape(-1, LANES)


def _unpack(buf, like):
    flat = buf.reshape(-1)
    out, pos = [], 0
    for a in like:
        out.append(flat[pos:pos + a.size].reshape(a.shape))
        pos += a.size
    return out


def _step(p):
    x, tgt = p["x"][0], p["loss_target"][0]
    d = x.shape[1]
    chip = 2 * lax.axis_index("x") + lax.axis_index("y")

    def rows_first(a, n):
        return a.T if n in _COL_SHARDED else a

    shard = {n: _cast_bf16(rows_first(p[n], n), "cast_" + n) for n in _BIG}
    shard[_CONV] = p[_CONV]

    def with_own(n, g):
        return lax.dynamic_update_index_in_dim(g, shard[n][None], chip, 0)

    def whole(n, g):
        g = with_own(n, g)
        return g.reshape(N_SHARD * g.shape[1], g.shape[2])

    now = ("l0_w_in", _CONV)
    got = dict(zip(now, _run_rider(_gather_rider([shard[n] for n in now]), "gather_first")))
    wb = {"w0T": whole("l0_w_in", got["l0_w_in"])}
    sm = {n: p[n] for n in _SMALL}
    taps, ccols = p[_CONV].shape
    conv_all = lax.dynamic_update_index_in_dim(got[_CONV], p[_CONV][None], chip, 0)
    sm[_CONV] = conv_all.transpose(1, 0, 2).reshape(taps, N_SHARD * ccols)
    cut = shard["l1_w_in"].shape[0] // 3 // 8 * 8
    shard["l1_w_in#0"], shard["l1_w_in#1"] = shard["l1_w_in"][:cut], shard["l1_w_in"][cut:]
    later = {"l0_in_ug": ("l1_w_out",), "l0_in_z": ("l0_s5_w_glu", "l1_w_in#0"), "l0_in_xbc": ("l0_w_out",),
             "ssd_scan": ("l1_w_in#1",)}
    early = {"fox_bwd_q": ("l1_w_out",), "l0_out_dx_ssd": ("l0_w_out",), "ssd_scan_bwd": ("l1_w_in",),
             "l0_in_dx_z": ("l0_w_in@0",), "l0_in_dx_xbc": ("l0_w_in@1", "l0_s5_w_glu")}
    grad_key = {n: n + "T" if n in _COL_SHARDED else n for n in _BIG}

    big, sib, others, pieces, part = {}, {}, {}, {}, {}

    def presummed(names, grads, tag):
        for n in names:
            g = grads[grad_key[n]]
            big[n] = g.reshape(N_SHARD, g.shape[0] // N_SHARD, g.shape[1])
        sib.update(zip(names, _sibling_halves([big[n] for n in names], "reduce_sibling_" + tag)))
        return [_presum(big[n], sib[n], "presum_" + n) for n in names]

    class Plan:
        def gather_rider(self, host):
            return _gather_rider([shard[n] for n in later[host]])

        def gathered(self, host, carried):
            got = dict(zip(later[host], carried))
            pieces.update({n: with_own(n, g) for n, g in got.items() if "#" in n})
            w = {n: whole(n, g) for n, g in got.items() if "#" not in n}
            if host == "ssd_scan":
                both = jnp.concatenate([pieces["l1_w_in#0"], pieces["l1_w_in#1"]], axis=1)
                w["l1_w_in"] = both.reshape(N_SHARD * both.shape[1], both.shape[2])
            names = {"l0_s5_w_glu": "w_glu", "l0_w_out": "w0_out", "l1_w_in": "w1T", "l1_w_out": "w1_out"}
            return {names[n]: v for n, v in w.items()}

        def reduce_rider(self, host, grads):
            if host not in early:
                return None
            new = tuple(dict.fromkeys(n.partition("@")[0] for n in early[host]))
            new = tuple(n for n in new if n not in part)
            part.update(zip(new, presummed(new, grads, host)))
            sent = []
            for n in early[host]:
                base, _, k = n.partition("@")
                a = part[base]
                if k:
                    w = a.shape[2] // 2
                    a = a[:, :, int(k) * w:(int(k) + 1) * w]
                sent.append(a)
            return _chip_rider(sent)

        def reduced(self, host, carried):
            for n, got in zip(early[host], carried):
                base, _, k = n.partition("@")
                if not k:
                    others[base] = got
                    continue
                pieces[n] = got
                if base + "@0" in pieces and base + "@1" in pieces:
                    others[base] = jnp.concatenate([pieces[base + "@0"], pieces[base + "@1"]], axis=2)

    loss_lanes, dx, grads = _local_step(x, tgt, wb, sm, Plan())

    small_like = [p[n] for n in _SMALL] + [sm[_CONV], jnp.zeros((1,), F32)]
    small_sum = _sum_slots(_gather_all(_pack([grads[n] for n in _SMALL] + [grads[_CONV], jnp.sum(loss_lanes)]),
                                       "gather_small"), "sum_small")
    *small_grads, conv_grad, loss = _unpack(small_sum, small_like)
    conv_grad = lax.dynamic_slice(conv_grad, (0, chip * ccols), (taps, ccols))
    final = dict(zip(_SMALL, small_grads))
    final[_CONV] = conv_grad

    late = tuple(n for n in _BIG if n not in others)
    if late:
        others.update(zip(late, _run_rider(_chip_rider(presummed(late, grads, "late")), "reduce_chips_late")))
    done = [_finish_half(big[n], sib[n], others[n], "finish_" + n) for n in _BIG]

    delta, new_m, new_v = {}, {}, {}
    for n, full in zip(_BIG, _join_halves(done, "join_halves")):
        upd = _adamw(rows_first(p[n], n), full, rows_first(p["m_" + n], n), rows_first(p["v_" + n], n),
                     "adamw_" + n)
        final[n], delta[n], new_m[n], new_v[n] = (rows_first(t, n) for t in (full, *upd))
    rest = _SMALL + (_CONV,)
    packed = [_pack([t[n] for n in rest]) for t in
              ({n: p[n] for n in rest}, final, {n: p["m_" + n] for n in rest}, {n: p["v_" + n] for n in rest})]
    for dst, buf in zip((delta, new_m, new_v), _adamw(*packed, "adamw_small")):
        dst.update(zip(rest, _unpack(buf, [p[n] for n in rest])))

    outs = [loss.reshape(()), dx[None]]
    for group in (final, delta, new_m, new_v):
        outs += [group[n].reshape(p[n].shape) for n in _WEIGHTS]
    return tuple(outs)


_INPUTS = ("x",) + _WEIGHTS + ("loss_target",) + tuple("m_" + n for n in _WEIGHTS) + tuple("v_" + n for n in _WEIGHTS)


def kernel(x, l0_norm_w, l0_w_in, l0_s5_lambda_re, l0_s5_lambda_im, l0_s5_log_step, l0_s5_b_re, l0_s5_b_im, l0_s5_c_re,
           l0_s5_c_im, l0_s5_d, l0_s5_w_glu, l0_s5_b_glu, l0_ssd_conv_w, l0_ssd_conv_b, l0_ssd_dt_bias,
           l0_ssd_a_log, l0_ssd_d, l0_ssd_norm_w, l0_w_out, l1_norm_w, l1_w_in, l1_fox_b_f, l1_w_out,
           final_norm_w, loss_target, m_l0_norm_w, m_l0_w_in, m_l0_s5_lambda_re, m_l0_s5_lambda_im,
           m_l0_s5_log_step, m_l0_s5_b_re, m_l0_s5_b_im, m_l0_s5_c_re, m_l0_s5_c_im, m_l0_s5_d,
           m_l0_s5_w_glu, m_l0_s5_b_glu, m_l0_ssd_conv_w, m_l0_ssd_conv_b, m_l0_ssd_dt_bias, m_l0_ssd_a_log,
           m_l0_ssd_d, m_l0_ssd_norm_w, m_l0_w_out, m_l1_norm_w, m_l1_w_in, m_l1_fox_b_f, m_l1_w_out,
           m_final_norm_w, v_l0_norm_w, v_l0_w_in, v_l0_s5_lambda_re, v_l0_s5_lambda_im, v_l0_s5_log_step,
           v_l0_s5_b_re, v_l0_s5_b_im, v_l0_s5_c_re, v_l0_s5_c_im, v_l0_s5_d, v_l0_s5_w_glu, v_l0_s5_b_glu,
           v_l0_ssd_conv_w, v_l0_ssd_conv_b, v_l0_ssd_dt_bias, v_l0_ssd_a_log, v_l0_ssd_d, v_l0_ssd_norm_w,
           v_l0_w_out, v_l1_norm_w, v_l1_w_in, v_l1_fox_b_f, v_l1_w_out, v_final_norm_w):
    values = (x, l0_norm_w, l0_w_in, l0_s5_lambda_re, l0_s5_lambda_im, l0_s5_log_step, l0_s5_b_re, l0_s5_b_im,
              l0_s5_c_re, l0_s5_c_im, l0_s5_d, l0_s5_w_glu, l0_s5_b_glu, l0_ssd_conv_w, l0_ssd_conv_b,
              l0_ssd_dt_bias, l0_ssd_a_log, l0_ssd_d, l0_ssd_norm_w, l0_w_out, l1_norm_w, l1_w_in,
              l1_fox_b_f, l1_w_out, final_norm_w, loss_target, m_l0_norm_w, m_l0_w_in,
              m_l0_s5_lambda_re, m_l0_s5_lambda_im, m_l0_s5_log_step, m_l0_s5_b_re, m_l0_s5_b_im,
              m_l0_s5_c_re, m_l0_s5_c_im, m_l0_s5_d, m_l0_s5_w_glu, m_l0_s5_b_glu, m_l0_ssd_conv_w,
              m_l0_ssd_conv_b, m_l0_ssd_dt_bias, m_l0_ssd_a_log, m_l0_ssd_d, m_l0_ssd_norm_w,
              m_l0_w_out, m_l1_norm_w, m_l1_w_in, m_l1_fox_b_f, m_l1_w_out, m_final_norm_w, v_l0_norm_w,
              v_l0_w_in, v_l0_s5_lambda_re, v_l0_s5_lambda_im, v_l0_s5_log_step, v_l0_s5_b_re,
              v_l0_s5_b_im, v_l0_s5_c_re, v_l0_s5_c_im, v_l0_s5_d, v_l0_s5_w_glu, v_l0_s5_b_glu,
              v_l0_ssd_conv_w, v_l0_ssd_conv_b, v_l0_ssd_dt_bias, v_l0_ssd_a_log, v_l0_ssd_d,
              v_l0_ssd_norm_w, v_l0_w_out, v_l1_norm_w, v_l1_w_in, v_l1_fox_b_f, v_l1_w_out,
              v_final_norm_w)
    return _step(dict(zip(_INPUTS, values)))
```

```python
import functools
import math

import jax
import jax.numpy as jnp
from jax import lax
from jax.experimental import pallas as pl
from jax.experimental.pallas import tpu as pltpu

F32 = jnp.float32
BF16 = jnp.bfloat16

S5_GROUP = 16
S5_STATE = 64
S5_EIG_CLIP = -1e-4
SSD_HEAD_DIM = 64
SSD_GROUPS = 8
SSD_STATE = 128
SSD_CONV = 4
SSD_CHUNK = 128
FOX_HEAD_DIM = 128
FOX_TILE = 1024
NORM_EPS = 1e-5
ADAM_LR = 0.001
ADAM_B1 = 0.9
ADAM_B2 = 0.999
ADAM_EPS = 1e-08
ADAM_WD = 0.01
ADAM_STEP = 10

N_SHARD = 4
N_DEV = 8
LANES = 128
VMEM_LIMIT = 56 * 1024 * 1024
MESH = pl.DeviceIdType.MESH


def _pick(dim, prefs, offs=()):
    for p in prefs:
        if dim % p == 0 and all(o % p == 0 for o in offs):
            return p
    return dim


def _params(sem=None, vmem=VMEM_LIMIT):
    return pltpu.CompilerParams(dimension_semantics=sem, vmem_limit_bytes=vmem)


class _Rider:
    def __init__(self, inputs, out_shape, sems, start, finish):
        self.inputs, self.out_shape, self.sems = list(inputs), list(out_shape), list(sems)
        self.start, self.finish = start, finish


def _hosted_call(body, *, name, grid, in_specs, out_specs, out_shape, scratch_shapes, sem, args, rider=None):
    single = not isinstance(out_shape, (list, tuple))
    out_specs = [out_specs] if single else list(out_specs)
    out_shape = [out_shape] if single else list(out_shape)
    if rider is None:
        res = pl.pallas_call(body, name=name, grid=grid, in_specs=in_specs, out_specs=out_specs,
                             out_shape=out_shape, scratch_shapes=scratch_shapes,
                             compiler_params=_params(sem))(*args)
        return (res[0] if single else res), []
    n_in, n_out, n_scr = len(in_specs), len(out_shape), len(scratch_shapes)
    n_rin, n_rout = len(rider.inputs), len(rider.out_shape)

    def carried(*refs):
        ins, refs = refs[:n_in], refs[n_in:]
        rin, refs = refs[:n_rin], refs[n_rin:]
        outs, refs = refs[:n_out], refs[n_out:]
        rout, refs = refs[:n_rout], refs[n_rout:]
        scr, rsem = refs[:n_scr], refs[n_scr:]
        ids = [pl.program_id(k) for k in range(len(grid))]
        first = functools.reduce(jnp.logical_and, [i == 0 for i in ids])
        last = functools.reduce(jnp.logical_and, [i == g - 1 for i, g in zip(ids, grid)])

        @pl.when(first)
        def _():
            rider.start(rin, rout, rsem)

        body(*ins, *outs, *scr)

        @pl.when(last)
        def _():
            rider.finish(rin, rout, rsem)

    res = pl.pallas_call(
        carried, name=name, grid=grid,
        in_specs=list(in_specs) + [_ANY] * n_rin, out_specs=out_specs + [_ANY] * n_rout,
        out_shape=out_shape + rider.out_shape,
        scratch_shapes=list(scratch_shapes) + [pltpu.SemaphoreType.DMA((k,)) for k in rider.sems],
        compiler_params=pltpu.CompilerParams(dimension_semantics=("arbitrary",) * len(grid),
                                             vmem_limit_bytes=VMEM_LIMIT, has_side_effects=True),
    )(*args, *rider.inputs)
    outs = res[:n_out]
    return (outs[0] if single else outs), list(res[n_out:])


def _matmul(a, b, *, mode="nn", dims=None, a_off=(0, 0), b_off=(0, 0), addend=None,
            out_dtype=F32, rider=None, rows_of=None, name):
    if dims is None:
        if mode == "nn":
            dims = (a.shape[0], b.shape[1], a.shape[1])
        elif mode == "nt":
            dims = (a.shape[0], b.shape[0], a.shape[1])
        else:
            dims = (a.shape[1], b.shape[1], a.shape[0])
    m, n, k = dims
    if mode == "nn":
        om, on, ok = (a_off[0],), (b_off[1],), (a_off[1], b_off[0])
    elif mode == "nt":
        om, on, ok = (a_off[0],), (b_off[0],), (a_off[1], b_off[1])
    else:
        om, on, ok = (a_off[1],), (b_off[1],), (a_off[0], b_off[0])
    tm = _pick(m, (1024, 512, 256, 128), om)
    tn = _pick(n, (1024, 768, 512, 384, 256, 128), on)
    tk = _pick(k, (2048, 1024, 512, 256, 128), ok)
    nk = k // tk
    if mode == "nn":
        a_blk, a_div = (tm, tk), (tm, tk)
        b_blk, b_div = (tk, tn), (tk, tn)
        a_map = lambda i, j, kk: (i + a_off[0] // tm, kk + a_off[1] // tk)
        b_map = lambda i, j, kk: (kk + b_off[0] // tk, j + b_off[1] // tn)
        dn = (((1,), (0,)), ((), ()))
    elif mode == "nt":
        a_blk, a_div = (tm, tk), (tm, tk)
        b_blk, b_div = (tn, tk), (tn, tk)
        a_map = lambda i, j, kk: (i + a_off[0] // tm, kk + a_off[1] // tk)
        b_map = lambda i, j, kk: (j + b_off[0] // tn, kk + b_off[1] // tk)
        dn = (((1,), (1,)), ((), ()))
    else:
        a_blk, a_div = (tk, tm), (tk, tm)
        b_blk, b_div = (tk, tn), (tk, tn)
        a_map = lambda i, j, kk: (kk + a_off[0] // tk, i + a_off[1] // tm)
        b_map = lambda i, j, kk: (kk + b_off[0] // tk, j + b_off[1] // tn)
        dn = (((0,), (0,)), ((), ()))
    assert a_off[0] % a_div[0] == 0 and a_off[1] % a_div[1] == 0, (name, a_off, a_div)
    assert b_off[0] % b_div[0] == 0 and b_off[1] % b_div[1] == 0, (name, b_off, b_div)
    has_add = addend is not None

    def body(*refs):
        if has_add:
            a_ref, b_ref, c_ref, o_ref, acc_ref = refs
        else:
            a_ref, b_ref, o_ref, acc_ref = refs
        kk = pl.program_id(2)

        @pl.when(kk == 0)
        def _():
            acc_ref[...] = jnp.zeros_like(acc_ref)

        acc_ref[...] += lax.dot_general(a_ref[...].astype(BF16), b_ref[...].astype(BF16), dn,
                                        preferred_element_type=F32)

        @pl.when(kk == nk - 1)
        def _():
            r = acc_ref[...]
            if has_add:
                r = r + c_ref[...].astype(F32)
            o_ref[...] = r.astype(o_ref.dtype)

    in_specs = [pl.BlockSpec(a_blk, a_map), pl.BlockSpec(b_blk, b_map)]
    args = [a, b]
    if has_add:
        in_specs.append(pl.BlockSpec((tm, tn), lambda i, j, kk: (i, j)))
        args.append(addend)
    if rows_of is not None:
        total, row_off, buf = rows_of
        assert row_off % tm == 0 and rider is None, (name, row_off, tm)
        if buf is not None:
            def body_into(*refs):
                body(*refs[:len(args)], *refs[len(args) + 1:])

            return pl.pallas_call(
                body_into, name=name, grid=(m // tm, n // tn, nk),
                in_specs=in_specs + [pl.BlockSpec(memory_space=pl.ANY)],
                out_specs=pl.BlockSpec((tm, tn), lambda i, j, kk: (i + row_off // tm, j)),
                out_shape=jax.ShapeDtypeStruct((total, n), out_dtype),
                input_output_aliases={len(args): 0},
                scratch_shapes=[pltpu.VMEM((tm, tn), F32)],
                compiler_params=_params(("parallel", "parallel", "arbitrary")),
            )(*args, buf)
        return pl.pallas_call(
            body, name=name, grid=(m // tm, n // tn, nk), in_specs=in_specs,
            out_specs=pl.BlockSpec((tm, tn), lambda i, j, kk: (i + row_off // tm, j)),
            out_shape=jax.ShapeDtypeStruct((total, n), out_dtype),
            scratch_shapes=[pltpu.VMEM((tm, tn), F32)],
            compiler_params=_params(("parallel", "parallel", "arbitrary")),
        )(*args)
    out, carried = _hosted_call(
        body, name=name, grid=(m // tm, n // tn, nk),
        in_specs=in_specs, out_specs=pl.BlockSpec((tm, tn), lambda i, j, kk: (i, j)),
        out_shape=jax.ShapeDtypeStruct((m, n), out_dtype),
        scratch_shapes=[pltpu.VMEM((tm, tn), F32)],
        sem=("parallel", "parallel", "arbitrary"), args=args, rider=rider)
    return out if rider is None else (out, carried)


def _rowwise(fn, rows, params, out_rows, out_accs=(), *, tl, ncol=1, name):
    rows = [r if isinstance(r, tuple) else (r, r.shape[1] // ncol, 0) for r in rows]
    n_rows, n_par, n_or, n_oa = len(rows), len(params), len(out_rows), len(out_accs)
    length = rows[0][0].shape[0]
    tl = _pick(length, [t for t in (1024, 512, 256, 128, 64, 32, 16, 8) if t <= tl])

    def body(*refs):
        row_refs = refs[:n_rows]
        par_refs = refs[n_rows:n_rows + n_par]
        or_refs = refs[n_rows + n_par:n_rows + n_par + n_or]
        oa_refs = refs[n_rows + n_par + n_or:]
        outs = fn(*[r[...] for r in row_refs], *[p[...] for p in par_refs])
        if not isinstance(outs, (tuple, list)):
            outs = (outs,)
        for r, v in zip(or_refs, outs[:n_or]):
            r[...] = v.astype(r.dtype)
        if n_oa:
            @pl.when(pl.program_id(1) == 0)
            def _():
                for r in oa_refs:
                    r[...] = jnp.zeros_like(r)

            for r, v in zip(oa_refs, outs[n_or:]):
                r[...] += v.astype(F32)

    in_specs = [pl.BlockSpec((tl, w), functools.partial(lambda j, i, b0: (i, b0 + j), b0=b0))
                for (_, w, b0) in rows]
    in_specs += [pl.BlockSpec((p.shape[0], p.shape[1] // ncol), lambda j, i: (0, j)) for p in params]
    out_specs = [pl.BlockSpec((tl, w), lambda j, i: (i, j)) for (w, _) in out_rows]
    out_specs += [pl.BlockSpec((1, w), lambda j, i: (0, j)) for w in out_accs]
    out_shape = [jax.ShapeDtypeStruct((length, ncol * w), dt) for (w, dt) in out_rows]
    out_shape += [jax.ShapeDtypeStruct((1, ncol * w), F32) for w in out_accs]
    res = pl.pallas_call(
        body, name=name, grid=(ncol, length // tl),
        in_specs=in_specs, out_specs=out_specs, out_shape=out_shape,
        compiler_params=_params(("parallel", "arbitrary" if n_oa else "parallel")),
    )(*[r[0] for r in rows], *params)
    return res


def _bdmm(xs, ws, *, addend=None, out_dtype=F32, name):
    nj, kin, kout = ws[0].shape
    xs = [x if isinstance(x, tuple) else (x, 0) for x in xs]
    length = xs[0][0].shape[0]
    tl = _pick(length, (2048, 1024, 512, 256, 128))
    n_x = len(xs)
    has_add = addend is not None

    def body(*refs):
        x_refs = refs[:n_x]
        w_refs = refs[n_x:2 * n_x]
        o_ref = refs[-1]
        acc = None
        for xr, wr in zip(x_refs, w_refs):
            t = jnp.dot(xr[...].astype(BF16), wr[0], preferred_element_type=F32)
            acc = t if acc is None else acc + t
        if has_add:
            acc = acc + refs[2 * n_x][...].astype(F32)
        o_ref[...] = acc.astype(o_ref.dtype)

    in_specs = [pl.BlockSpec((tl, kin), functools.partial(lambda i, j, b0: (i, b0 + j), b0=b0)) for (_, b0) in xs]
    in_specs += [pl.BlockSpec((1, kin, kout), lambda i, j: (j, 0, 0)) for _ in ws]
    args = [x[0] for x in xs] + list(ws)
    if has_add:
        in_specs.append(pl.BlockSpec((tl, kout), lambda i, j: (i, j)))
        args.append(addend)
    return pl.pallas_call(
        body, name=name, grid=(length // tl, nj),
        in_specs=in_specs, out_specs=pl.BlockSpec((tl, kout), lambda i, j: (i, j)),
        out_shape=jax.ShapeDtypeStruct((length, nj * kout), out_dtype),
        compiler_params=_params(("parallel", "parallel")),
    )(*args)


def _bdmm_tn_sized(x, g, nj, kin, kout, x_first, name):
    length = x.shape[0]
    tl = _pick(length, (512, 256, 128))
    nt = length // tl

    def body(x_ref, g_ref, o_ref):
        @pl.when(pl.program_id(1) == 0)
        def _():
            o_ref[...] = jnp.zeros_like(o_ref)

        o_ref[0] += lax.dot_general(x_ref[...].astype(BF16), g_ref[...].astype(BF16),
                                    (((0,), (0,)), ((), ())), preferred_element_type=F32)

    return pl.pallas_call(
        body, name=name, grid=(nj, nt),
        in_specs=[pl.BlockSpec((tl, kin), lambda j, t: (t, x_first + j)),
                  pl.BlockSpec((tl, kout), lambda j, t: (t, j))],
        out_specs=pl.BlockSpec((1, kin, kout), lambda j, t: (j, 0, 0)),
        out_shape=jax.ShapeDtypeStruct((nj, kin, kout), F32),
        compiler_params=_params(("parallel", "arbitrary")),
    )(x, g)


def _f_norm(x, w):
    return x * lax.rsqrt(jnp.mean(x * x, axis=-1, keepdims=True) + NORM_EPS) * w


def _gelu(y):
    return 0.5 * y * (1.0 + jnp.tanh(math.sqrt(2.0 / math.pi) * (y + 0.044715 * (y * y * y))))


def _sigmoid(x):
    return 1.0 / (1.0 + jnp.exp(-x))


def _silu(x):
    return x * _sigmoid(x)


def _softplus(x):
    return jnp.maximum(x, 0.0) + jnp.log(1.0 + jnp.exp(-jnp.abs(x)))


def _f_s5_gelu(yc, u, dvec):
    return _gelu(yc + dvec * u)


def _f_s5_out(yc, u, t, gate, dvec, bglu):
    gl = _gelu(yc + dvec * u)
    return gl * _sigmoid(t + bglu) * _silu(gate)


def _f_ssd_out(y, xs, z, dpar, nw):
    v = (y + dpar * xs) * _silu(z)
    return v * lax.rsqrt(jnp.mean(v * v, axis=-1, keepdims=True) + NORM_EPS) * nw


def _f_fox_out(att, gate):
    return att * _silu(gate)


def _norm_fwd(x, w, name):
    return _rowwise(lambda xt, wt: _f_norm(xt, wt), [x], [w], [(x.shape[1], BF16)], tl=256, name=name)[0]


def _norm_bwd(x, dh, dres, w, name):
    d = x.shape[1]

    def fn(xt, dht, drt, wt):
        _, vjp = jax.vjp(_f_norm, xt, wt)
        dx, dw = vjp(dht)
        dx = dx + drt
        return dx, dx, dw

    return _rowwise(fn, [x, dh, dres], [w], [(d, F32), (d, BF16)], [d], tl=128, name=name)


def _adamw_math(w, g, m, v):
    m = ADAM_B1 * m + (1.0 - ADAM_B1) * g
    v = ADAM_B2 * v + (1.0 - ADAM_B2) * jnp.square(g)
    m_hat = m / (1.0 - ADAM_B1 ** ADAM_STEP)
    v_hat = v / (1.0 - ADAM_B2 ** ADAM_STEP)
    delta = -ADAM_LR * (m_hat / (jnp.sqrt(v_hat) + ADAM_EPS) + ADAM_WD * w)
    return delta, m, v


def _adamw(w, g, m, v, name):
    return _elementwise(_adamw_math, [w, g, m, v], [F32] * 3, name)


def _s5_scan_fwd(bu_re, bu_im, a_re, a_im, name):
    length, rows, _ = bu_re.shape
    rb = _pick(rows, (32, 16, 8))
    tl = _pick(length, (64, 32, 16, 8))

    def body(bur_ref, bui_ref, ar_ref, ai_ref, sr_ref, si_ref, st_ref):
        @pl.when(pl.program_id(1) == 0)
        def _():
            st_ref[...] = jnp.zeros_like(st_ref)

        ar = ar_ref[...]
        ai = ai_ref[...]

        def step(l, carry):
            sr, si = carry
            nr = ar * sr - ai * si + bur_ref[l]
            ni = ar * si + ai * sr + bui_ref[l]
            sr_ref[l] = nr
            si_ref[l] = ni
            return nr, ni

        sr, si = lax.fori_loop(0, tl, step, (st_ref[0], st_ref[1]))
        st_ref[0] = sr
        st_ref[1] = si

    blk = pl.BlockSpec((tl, rb, LANES), lambda cb, t: (t, cb, 0))
    ablk = pl.BlockSpec((rb, LANES), lambda cb, t: (cb, 0))
    return pl.pallas_call(
        body, name=name, grid=(rows // rb, length // tl),
        in_specs=[blk, blk, ablk, ablk], out_specs=[blk, blk],
        out_shape=[jax.ShapeDtypeStruct(bu_re.shape, F32)] * 2,
        scratch_shapes=[pltpu.VMEM((2, rb, LANES), F32)],
        compiler_params=_params(("parallel", "arbitrary")),
    )(bu_re, bu_im, a_re, a_im)


def _s5_scan_bwd(ds_re, ds_im, s_re, s_im, a_re, a_im, name):
    length, rows, _ = ds_re.shape
    rb = _pick(rows, (32, 16, 8))
    tl = _pick(length, (64, 32, 16, 8))
    nt = length // tl

    def body(dsr_ref, dsi_ref, sr_ref, si_ref, pr_ref, pi_ref, ar_ref, ai_ref,
             gr_ref, gi_ref, dar_ref, dai_ref, st_ref):
        t = pl.program_id(1)

        @pl.when(t == 0)
        def _():
            st_ref[...] = jnp.zeros_like(st_ref)
            dar_ref[...] = jnp.zeros_like(dar_ref)
            dai_ref[...] = jnp.zeros_like(dai_ref)

        ar = ar_ref[...]
        ai = ai_ref[...]

        def adj(l, gr, gi):
            ngr = dsr_ref[l] + ar * gr + ai * gi
            ngi = dsi_ref[l] + ar * gi - ai * gr
            gr_ref[l] = ngr
            gi_ref[l] = ngi
            return ngr, ngi

        def step(idx, carry):
            gr, gi, dar, dai = carry
            l = tl - 1 - idx
            gr, gi = adj(l, gr, gi)
            pr = sr_ref[l - 1]
            pi = si_ref[l - 1]
            dar = dar + gr * pr + gi * pi
            dai = dai + gi * pr - gr * pi
            return gr, gi, dar, dai

        zero = jnp.zeros((rb, LANES), F32)
        gr, gi, dar, dai = lax.fori_loop(0, tl - 1, step, (st_ref[0], st_ref[1], zero, zero))
        gr, gi = adj(0, gr, gi)
        first = (t == nt - 1)
        pr = jnp.where(first, 0.0, pr_ref[0])
        pi = jnp.where(first, 0.0, pi_ref[0])
        dar = dar + gr * pr + gi * pi
        dai = dai + gi * pr - gr * pi
        st_ref[0] = gr
        st_ref[1] = gi
        dar_ref[...] += dar
        dai_ref[...] += dai

    blk = pl.BlockSpec((tl, rb, LANES), lambda cb, t: (nt - 1 - t, cb, 0))
    prev = pl.BlockSpec((1, rb, LANES), lambda cb, t: (jnp.maximum((nt - 1 - t) * tl - 1, 0), cb, 0))
    ablk = pl.BlockSpec((rb, LANES), lambda cb, t: (cb, 0))
    return pl.pallas_call(
        body, name=name, grid=(rows // rb, nt),
        in_specs=[blk, blk, blk, blk, prev, prev, ablk, ablk],
        out_specs=[blk, blk, ablk, ablk],
        out_shape=[jax.ShapeDtypeStruct(ds_re.shape, F32)] * 2 + [jax.ShapeDtypeStruct(a_re.shape, F32)] * 2,
        scratch_shapes=[pltpu.VMEM((2, rb, LANES), F32)],
        compiler_params=_params(("parallel", "arbitrary")),
    )(ds_re, ds_im, s_re, s_im, s_re, s_im, a_re, a_im)


def _s5_prepare(lam_re, lam_im, log_step, b_re, b_im, c_re, c_im):
    groups, state = lam_re.shape
    lr = jnp.minimum(lam_re, S5_EIG_CLIP)
    li = lam_im
    step = jnp.exp(log_step)[:, None]
    mag = jnp.exp(lr * step)
    ab_re = mag * jnp.cos(li * step)
    ab_im = mag * jnp.sin(li * step)
    denom = lr * lr + li * li
    nr = ab_re - 1.0
    ni = ab_im
    coef_re = (nr * lr + ni * li) / denom
    coef_im = (ni * lr - nr * li) / denom
    bb_re = coef_re[..., None] * b_re - coef_im[..., None] * b_im
    bb_im = coef_re[..., None] * b_im + coef_im[..., None] * b_re
    per = LANES // S5_GROUP
    nj = groups // per
    eye = jnp.eye(per, dtype=F32)

    def in_map(bb):
        return jnp.einsum('jgph,gk->jghkp', bb.reshape(nj, per, state, S5_GROUP), eye).reshape(
            nj, per * S5_GROUP, per * state)

    def out_map(cc):
        return jnp.einsum('jghp,gk->jgpkh', cc.reshape(nj, per, S5_GROUP, state), eye).reshape(
            nj, per * state, per * S5_GROUP)

    shape2 = (groups * state // LANES, LANES)
    return (ab_re.reshape(shape2), ab_im.reshape(shape2), in_map(bb_re), in_map(bb_im),
            out_map(c_re), -out_map(c_im))


def _shift_down(cur, prev8, j):
    rolled = pltpu.roll(cur, j, 0)
    pr = pltpu.roll(prev8, j, 0)
    row = lax.broadcasted_iota(jnp.int32, cur.shape, 0)
    return jnp.where(row < j, jnp.tile(pr, (cur.shape[0] // 8, 1)), rolled)


def _shift_up(cur, next8, j):
    tl = cur.shape[0]
    rolled = pltpu.roll(cur, tl - j, 0)
    nx = pltpu.roll(next8, 8 - j, 0)
    row = lax.broadcasted_iota(jnp.int32, cur.shape, 0)
    return jnp.where(row >= tl - j, jnp.tile(nx, (tl // 8, 1)), rolled)


def _conv_tiles(length, ch):
    return _pick(length, (256, 128, 64, 32, 16, 8)), _pick(ch, (1024, 512, 256, 128))


def _conv_fwd(xbc, w, b, name):
    length, ch = xbc.shape
    tl, tc = _conv_tiles(length, ch)

    def body(x_ref, p_ref, w_ref, b_ref, o_ref):
        cur = x_ref[...]
        prev8 = jnp.where(pl.program_id(1) == 0, 0.0, p_ref[...])
        pre = b_ref[...] + w_ref[SSD_CONV - 1:SSD_CONV, :] * cur
        for j in range(1, SSD_CONV):
            pre = pre + w_ref[SSD_CONV - 1 - j:SSD_CONV - j, :] * _shift_down(cur, prev8, j)
        o_ref[...] = _silu(pre)

    return pl.pallas_call(
        body, name=name, grid=(ch // tc, length // tl),
        in_specs=[pl.BlockSpec((tl, tc), lambda c, i: (i, c)),
                  pl.BlockSpec((8, tc), lambda c, i: (jnp.maximum(i * (tl // 8) - 1, 0), c)),
                  pl.BlockSpec((SSD_CONV, tc), lambda c, i: (0, c)),
                  pl.BlockSpec((1, tc), lambda c, i: (0, c))],
        out_specs=pl.BlockSpec((tl, tc), lambda c, i: (i, c)),
        out_shape=jax.ShapeDtypeStruct((length, ch), F32),
        compiler_params=_params(("parallel", "parallel")),
    )(xbc, xbc, w, b)


def _conv_bwd_pre(dxc, xbc, w, b, name):
    length, ch = xbc.shape
    tl, tc = _conv_tiles(length, ch)

    def body(d_ref, x_ref, p_ref, w_ref, b_ref, o_ref, dw_ref, db_ref):
        @pl.when(pl.program_id(1) == 0)
        def _():
            dw_ref[...] = jnp.zeros_like(dw_ref)
            db_ref[...] = jnp.zeros_like(db_ref)

        cur = x_ref[...]
        prev8 = jnp.where(pl.program_id(1) == 0, 0.0, p_ref[...])
        shifted = [cur] + [_shift_down(cur, prev8, j) for j in range(1, SSD_CONV)]
        pre = b_ref[...]
        for j in range(SSD_CONV):
            pre = pre + w_ref[SSD_CONV - 1 - j:SSD_CONV - j, :] * shifted[j]
        sg = _sigmoid(pre)
        dpre = d_ref[...] * (sg * (1.0 + pre * (1.0 - sg)))
        o_ref[...] = dpre
        db_ref[...] += jnp.sum(dpre, axis=0, keepdims=True)
        row = lax.broadcasted_iota(jnp.int32, (SSD_CONV, tc), 0)
        dw = jnp.zeros((SSD_CONV, tc), F32)
        for j in range(SSD_CONV):
            dw = dw + jnp.where(row == SSD_CONV - 1 - j, jnp.sum(dpre * shifted[j], axis=0, keepdims=True), 0.0)
        dw_ref[...] += dw

    return pl.pallas_call(
        body, name=name, grid=(ch // tc, length // tl),
        in_specs=[pl.BlockSpec((tl, tc), lambda c, i: (i, c)),
                  pl.BlockSpec((tl, tc), lambda c, i: (i, c)),
                  pl.BlockSpec((8, tc), lambda c, i: (jnp.maximum(i * (tl // 8) - 1, 0), c)),
                  pl.BlockSpec((SSD_CONV, tc), lambda c, i: (0, c)),
                  pl.BlockSpec((1, tc), lambda c, i: (0, c))],
        out_specs=[pl.BlockSpec((tl, tc), lambda c, i: (i, c)),
                   pl.BlockSpec((SSD_CONV, tc), lambda c, i: (0, c)),
                   pl.BlockSpec((1, tc), lambda c, i: (0, c))],
        out_shape=[jax.ShapeDtypeStruct((length, ch), F32), jax.ShapeDtypeStruct((SSD_CONV, ch), F32),
                   jax.ShapeDtypeStruct((1, ch), F32)],
        compiler_params=_params(("parallel", "arbitrary")),
    )(dxc, xbc, xbc, w, b)


def _conv_bwd_in(dpre, w, name):
    length, ch = dpre.shape
    tl, tc = _conv_tiles(length, ch)
    nt = length // tl

    def body(d_ref, n_ref, w_ref, o_ref):
        cur = d_ref[...]
        next8 = jnp.where(pl.program_id(1) == nt - 1, 0.0, n_ref[...])
        acc = w_ref[SSD_CONV - 1:SSD_CONV, :] * cur
        for j in range(1, SSD_CONV):
            acc = acc + w_ref[SSD_CONV - 1 - j:SSD_CONV - j, :] * _shift_up(cur, next8, j)
        o_ref[...] = acc.astype(o_ref.dtype)

    return pl.pallas_call(
        body, name=name, grid=(ch // tc, nt),
        in_specs=[pl.BlockSpec((tl, tc), lambda c, i: (i, c)),
                  pl.BlockSpec((8, tc), lambda c, i: (jnp.minimum((i + 1) * (tl // 8), length // 8 - 1), c)),
                  pl.BlockSpec((SSD_CONV, tc), lambda c, i: (0, c))],
        out_specs=pl.BlockSpec((tl, tc), lambda c, i: (i, c)),
        out_shape=jax.ShapeDtypeStruct((length, ch), BF16),
        compiler_params=_params(("parallel", "parallel")),
    )(dpre, dpre, w)


def _split(x, terms):
    parts = []
    for _ in range(terms):
        part = x.astype(BF16)
        parts.append(part)
        x = x - part.astype(F32)
    return parts


def _dot(a, b, dn=(((1,), (0,)), ((), ()))):
    return lax.dot_general(a, b, dn, preferred_element_type=F32)


_NN = (((1,), (0,)), ((), ()))
_NT = (((1,), (1,)), ((), ()))
_TN = (((0,), (0,)), ((), ()))


def _pdot(parts, sel, dn=_NN):
    return functools.reduce(lambda a, b: a + b, [_dot(part, sel, dn) for part in parts])


def _pdotr(sel, parts, dn=_NN):
    return functools.reduce(lambda a, b: a + b, [_dot(sel, part, dn) for part in parts])


def _dot3(x, sel, dn=_NN):
    return _pdot(_split(x, 3), sel, dn)


def _dot3r(sel, x, dn=_NN):
    return _pdotr(sel, _split(x, 3), dn)


def _dot2(x, sel, dn=_NN):
    return _pdot(_split(x, 2), sel, dn)


def _iota2(shape, axis):
    return lax.broadcasted_iota(jnp.int32, shape, axis)


def _ssd_masks():
    q = SSD_CHUNK
    r, c = _iota2((q, q), 0), _iota2((q, q), 1)
    tril = (c <= r)
    return r, c, tril


def _head_of_lane(wg):
    return (_iota2((SSD_CHUNK, wg), 0) == _iota2((SSD_CHUNK, wg), 1) // SSD_HEAD_DIM).astype(BF16)


def _head_of_row(wg, dtype):
    return (_iota2((wg, SSD_CHUNK), 1) == _iota2((wg, SSD_CHUNK), 0) // SSD_HEAD_DIM).astype(dtype)


class _SsdChunk:
    def __init__(self, dtraw, dtb, ap, b_t, c_t, wg):
        q = SSD_CHUNK
        hpg = wg // SSD_HEAD_DIM
        r, c, self.tril = _ssd_masks()
        self.upper = (c > r)
        self.dt = _softplus(dtraw + dtb)
        self.la = self.dt * ap
        self.cum = _dot3r(self.tril.astype(BF16), self.la)
        rem = _dot3r(self.upper.astype(BF16), self.la)
        total = _dot3(self.la, jnp.ones((q, q), BF16), _TN)
        self.scores = _dot(c_t, b_t, _NT)
        cum2 = _split(self.cum, 2)
        stack = jnp.concatenate(_split(self.dt, 2) + cum2 + _split(rem, 2), axis=0)
        lanes = _dot(stack, _head_of_lane(wg))
        self.dt_l = lanes[0:q] + lanes[q:2 * q]
        self.cum_l = lanes[2 * q:3 * q] + lanes[3 * q:4 * q]
        self.rem_l = lanes[4 * q:5 * q] + lanes[5 * q:6 * q]
        self.grow = jnp.exp(_pdotr(_head_of_row(wg, BF16), _split(total, 2)))
        every_lane = (_iota2((q, hpg * q), 0) == _iota2((q, hpg * q), 1) // q).astype(BF16)
        cq = _dot(jnp.concatenate(cum2, axis=0), every_lane)
        self.cq = cq[0:q] + cq[q:2 * q]
        every_row = (_iota2((hpg * q, q), 1) == _iota2((hpg * q, q), 0) // q).astype(BF16)
        self.ck = _pdotr(every_row, cum2, _NT)

    def decay(self, h):
        q = SSD_CHUNK
        seg = self.cq[:, h * q:(h + 1) * q] - self.ck[h * q:(h + 1) * q, :]
        return jnp.exp(jnp.where(self.tril, seg, -jnp.inf))


def _by_head(x_b, lane):
    first = (lane // SSD_HEAD_DIM) == 0
    return jnp.concatenate([jnp.where(first, x_b, 0), jnp.where(first, 0, x_b)], axis=0)


def _ssd_tiles(xc, n_heads):
    hpg = n_heads // SSD_GROUPS
    wg = hpg * SSD_HEAD_DIM
    xw = n_heads * SSD_HEAD_DIM
    return hpg, wg, xw // wg, xw // SSD_STATE


def _ssd_fwd(xc, dtraw, dtb, ap, n_heads, name, rider=None):
    length = xc.shape[0]
    q = SSD_CHUNK
    nc = length // q
    hpg, wg, _, b_blk0 = _ssd_tiles(xc, n_heads)
    c_blk0 = b_blk0 + SSD_GROUPS
    npair = hpg // 2

    def body(x_ref, b_ref, c_ref, dt_ref, dtb_ref, ap_ref, y_ref, st_ref, s_ref):
        @pl.when(pl.program_id(1) == 0)
        def _():
            s_ref[...] = jnp.zeros_like(s_ref)

        st_ref[0, 0] = s_ref[...]
        b_t = b_ref[...].astype(BF16)
        c_t = c_ref[...].astype(BF16)
        ck = _SsdChunk(dt_ref[...], dtb_ref[...], ap_ref[...], b_t, c_t, wg)
        lane = _iota2((q, q), 1)
        xd = x_ref[...] * ck.dt_l
        xd_b = xd.astype(BF16)
        s_prev = s_ref[...]
        y_state = _dot(c_t, s_prev.astype(BF16), _NT) * jnp.exp(ck.cum_l)
        for i in range(npair):
            sl = slice(i * LANES, (i + 1) * LANES)
            wm = jnp.concatenate([(ck.scores * ck.decay(2 * i + hh)).astype(BF16) for hh in range(2)], axis=1)
            y_ref[:, sl] = y_state[:, sl] + _dot(wm, _by_head(xd_b[:, sl], lane))
        xw_b = (xd * jnp.exp(ck.rem_l)).astype(BF16)
        s_ref[...] = ck.grow * s_prev + _dot(xw_b, b_t, _TN)

    outs, carried = _hosted_call(
        body, name=name, grid=(SSD_GROUPS, nc),
        in_specs=[pl.BlockSpec((q, wg), lambda g, c: (c, g)),
                  pl.BlockSpec((q, SSD_STATE), lambda g, c: (c, b_blk0 + g)),
                  pl.BlockSpec((q, SSD_STATE), lambda g, c: (c, c_blk0 + g)),
                  pl.BlockSpec((q, LANES), lambda g, c: (c, g)),
                  pl.BlockSpec((1, LANES), lambda g, c: (0, g)),
                  pl.BlockSpec((1, LANES), lambda g, c: (0, g))],
        out_specs=[pl.BlockSpec((q, wg), lambda g, c: (c, g)),
                   pl.BlockSpec((1, 1, wg, SSD_STATE), lambda g, c: (c, g, 0, 0))],
        out_shape=[jax.ShapeDtypeStruct((length, SSD_GROUPS * wg), F32),
                   jax.ShapeDtypeStruct((nc, SSD_GROUPS, wg, SSD_STATE), F32)],
        scratch_shapes=[pltpu.VMEM((wg, SSD_STATE), F32)],
        sem=("parallel", "arbitrary"), args=(xc, xc, xc, dtraw, dtb, ap), rider=rider)
    return outs[0], outs[1], carried


def _ssd_bwd(dy, dxa, xc, dtraw, states, dtb, ap, n_heads, name, rider=None):
    length = xc.shape[0]
    q = SSD_CHUNK
    nc = length // q
    hpg, wg, _, b_blk0 = _ssd_tiles(xc, n_heads)
    c_blk0 = b_blk0 + SSD_GROUPS
    npair = hpg // 2

    def body(dy_ref, dxa_ref, x_ref, b_ref, c_ref, dt_ref, st_ref, dtb_ref, ap_ref,
             dx_ref, db_ref, dc_ref, ddt_ref, ddtb_ref, dap_ref, ds_ref, el_ref, er_ref):
        @pl.when(pl.program_id(1) == 0)
        def _():
            ds_ref[...] = jnp.zeros_like(ds_ref)
            ddtb_ref[...] = jnp.zeros_like(ddtb_ref)
            dap_ref[...] = jnp.zeros_like(dap_ref)

        b_t = b_ref[...].astype(BF16)
        c_t = c_ref[...].astype(BF16)
        dtraw_t = dt_ref[...]
        ck = _SsdChunk(dtraw_t, dtb_ref[...], ap_ref[...], b_t, c_t, wg)
        lane = _iota2((q, q), 1)
        to_head = _head_of_lane(wg)

        def per_head(v):
            return _pdot(_split(v, 2), to_head, _NT)

        x_all, dy_all = x_ref[...], dy_ref[...]
        dy_b = dy_all.astype(BF16)
        xd = x_all * ck.dt_l
        xd_b = xd.astype(BF16)
        s_prev = st_ref[0, 0]
        sp_b = s_prev.astype(BF16)
        ds1 = ds_ref[...]
        ds1_b = ds1.astype(BF16)
        ecum, wrem = jnp.exp(ck.cum_l), jnp.exp(ck.rem_l)
        dscores = jnp.zeros((q, q), F32)
        for i in range(npair):
            sl = slice(i * LANES, (i + 1) * LANES)
            dym = _by_head(dy_b[:, sl], lane)
            dwm2 = _dot(dym, xd_b[:, sl], _NT)
            wms = []
            for hh in range(2):
                h = 2 * i + hh
                decay = ck.decay(h)
                wm = ck.scores * decay
                dwm = dwm2[hh * q:(hh + 1) * q]
                dscores = dscores + dwm * decay
                e = (dwm * wm).astype(BF16)
                el_ref[:, h * q:(h + 1) * q] = e
                er_ref[h * q:(h + 1) * q, :] = e
                wms.append(wm.astype(BF16))
            dx_ref[:, sl] = _dot(jnp.concatenate(wms, axis=0), dym, _TN)
        put = (_iota2((hpg * q, q), 1) == _iota2((hpg * q, q), 0) // q).astype(BF16)
        dcum = _dot(el_ref[...], put) - _dot(er_ref[...], put, _TN)
        t_mat = _dot(c_t, sp_b, _NT)
        d_t = (dy_all * ecum).astype(BF16)
        dc_acc = _dot(d_t, sp_b)
        ds_prev = _dot(d_t, c_t, _TN)
        dcum = dcum + per_head(dy_all * t_mat * ecum)
        ds_prev = ds_prev + ck.grow * ds1
        zs = jnp.sum(ds1 * s_prev * ck.grow, axis=1, keepdims=True)
        dtot = _pdotr(jnp.ones((q, wg), BF16), _split(zs * _head_of_row(wg, F32), 2))
        xw = xd * wrem
        dxw = _dot(b_t, ds1_b, _NT)
        db_acc = _dot(xw.astype(BF16), ds1_b)
        dxd = dx_ref[...] + dxw * wrem
        drem = per_head(dxw * xw)
        dx_ref[...] = dxd * ck.dt_l + dxa_ref[...]
        ddt = per_head(dxd * x_all)
        ds_ref[...] = ds_prev
        ds_b = dscores.astype(BF16)
        dc_ref[...] = dc_acc + _dot(ds_b, b_t)
        db_ref[...] = db_acc + _dot(ds_b, c_t, _TN)
        dla = (_dot3r(ck.tril.astype(BF16), dcum, _TN) + _dot3r(ck.upper.astype(BF16), drem, _TN) + dtot)
        dt = ck.dt
        ddt = ddt + dla * ap_ref[...]
        dap_ref[...] += jnp.sum(dla * dt, axis=0, keepdims=True)
        ddtraw = ddt * _sigmoid(dtraw_t + dtb_ref[...])
        ddt_ref[...] = ddtraw.astype(ddt_ref.dtype)
        ddtb_ref[...] += jnp.sum(ddtraw, axis=0, keepdims=True)

    rev = lambda g, c: (nc - 1 - c, g)
    outs, carried = _hosted_call(
        body, name=name, grid=(SSD_GROUPS, nc),
        in_specs=[pl.BlockSpec((q, wg), rev),
                  pl.BlockSpec((q, wg), rev),
                  pl.BlockSpec((q, wg), rev),
                  pl.BlockSpec((q, SSD_STATE), lambda g, c: (nc - 1 - c, b_blk0 + g)),
                  pl.BlockSpec((q, SSD_STATE), lambda g, c: (nc - 1 - c, c_blk0 + g)),
                  pl.BlockSpec((q, LANES), rev),
                  pl.BlockSpec((1, 1, wg, SSD_STATE), lambda g, c: (nc - 1 - c, g, 0, 0)),
                  pl.BlockSpec((1, LANES), lambda g, c: (0, g)),
                  pl.BlockSpec((1, LANES), lambda g, c: (0, g))],
        out_specs=[pl.BlockSpec((q, wg), rev),
                   pl.BlockSpec((q, SSD_STATE), rev),
                   pl.BlockSpec((q, SSD_STATE), rev),
                   pl.BlockSpec((q, LANES), rev),
                   pl.BlockSpec((1, LANES), lambda g, c: (0, g)),
                   pl.BlockSpec((1, LANES), lambda g, c: (0, g))],
        out_shape=[jax.ShapeDtypeStruct((length, SSD_GROUPS * wg), F32),
                   jax.ShapeDtypeStruct((length, SSD_GROUPS * SSD_STATE), F32),
                   jax.ShapeDtypeStruct((length, SSD_GROUPS * SSD_STATE), F32),
                   jax.ShapeDtypeStruct((length, SSD_GROUPS * LANES), BF16),
                   jax.ShapeDtypeStruct((1, SSD_GROUPS * LANES), F32),
                   jax.ShapeDtypeStruct((1, SSD_GROUPS * LANES), F32)],
        scratch_shapes=[pltpu.VMEM((wg, SSD_STATE), F32), pltpu.VMEM((q, hpg * q), BF16),
                        pltpu.VMEM((hpg * q, q), BF16)],
        sem=("parallel", "arbitrary"), args=(dy, dxa, xc, xc, xc, dtraw, states, dtb, ap), rider=rider)
    return tuple(outs) + (carried,)


def _fox_cumsum(fraw, bf, name):
    length = fraw.shape[0]
    q = 128

    def body(f_ref, b_ref, o_ref, carry_ref):
        @pl.when(pl.program_id(0) == 0)
        def _():
            carry_ref[...] = jnp.zeros_like(carry_ref)

        lf = -_softplus(-(f_ref[...] + b_ref[...]))
        r, c = _iota2((q, q), 0), _iota2((q, q), 1)
        o_ref[...] = _dot3r((c <= r).astype(BF16), lf) + carry_ref[...]
        carry_ref[...] += jnp.sum(lf, axis=0, keepdims=True)

    return pl.pallas_call(
        body, name=name, grid=(length // q,),
        in_specs=[pl.BlockSpec((q, LANES), lambda i: (i, 0)), pl.BlockSpec((1, LANES), lambda i: (0, 0))],
        out_specs=pl.BlockSpec((q, LANES), lambda i: (i, 0)),
        out_shape=jax.ShapeDtypeStruct((length, LANES), F32),
        scratch_shapes=[pltpu.VMEM((1, LANES), F32)],
        compiler_params=_params(("arbitrary",)),
    )(fraw, bf)


def _fox_cumsum_bwd(dc, fraw, bf, name):
    length = fraw.shape[0]
    heads = dc.shape[0]
    q = 128
    nt = length // q

    def body(d_ref, f_ref, b_ref, o_ref, db_ref, carry_ref):
        @pl.when(pl.program_id(0) == 0)
        def _():
            carry_ref[...] = jnp.zeros_like(carry_ref)
            db_ref[...] = jnp.zeros_like(db_ref)

        d = jnp.concatenate([d_ref[...], jnp.zeros((LANES - heads, q), F32)], axis=0).T
        r, c = _iota2((q, q), 0), _iota2((q, q), 1)
        dlf = _dot3r((c >= r).astype(BF16), d) + carry_ref[...]
        carry_ref[...] += jnp.sum(d, axis=0, keepdims=True)
        df = dlf * _sigmoid(-(f_ref[...] + b_ref[...]))
        o_ref[...] = df.astype(o_ref.dtype)
        db_ref[...] += jnp.sum(df, axis=0, keepdims=True)

    rev = lambda i: (nt - 1 - i, 0)
    return pl.pallas_call(
        body, name=name, grid=(nt,),
        in_specs=[pl.BlockSpec((heads, q), lambda i: (0, nt - 1 - i)), pl.BlockSpec((q, LANES), rev),
                  pl.BlockSpec((1, LANES), lambda i: (0, 0))],
        out_specs=[pl.BlockSpec((q, LANES), rev), pl.BlockSpec((1, LANES), lambda i: (0, 0))],
        out_shape=[jax.ShapeDtypeStruct((length, LANES), BF16), jax.ShapeDtypeStruct((1, LANES), F32)],
        scratch_shapes=[pltpu.VMEM((1, LANES), F32)],
        compiler_params=_params(("arbitrary",)),
    )(dc, fraw, bf)


def _fox_scores(q_ref, k_ref, cq_ref, ck_ref, diagonal):
    scale = 1.0 / math.sqrt(FOX_HEAD_DIM)
    s = _dot(q_ref[...].astype(BF16), k_ref[...].astype(BF16), _NT) * scale + (cq_ref[0] - ck_ref[0])
    if diagonal:
        s = jnp.where(_iota2(s.shape, 1) <= _iota2(s.shape, 0), s, -jnp.inf)
    return s


def _fox_tiles(i, j, step):
    @pl.when(j < i)
    def _():
        step(False)

    @pl.when(j == i)
    def _():
        step(True)


def _fox_fwd(qkvg, c_col, c_row, n_heads, name):
    length = qkvg.shape[0]
    hd = FOX_HEAD_DIM
    tq = _pick(length, (FOX_TILE, 512, 256, 128))
    nq = length // tq
    sr = _pick(tq, (256, 128))

    def body(q_ref, k_ref, v_ref, cq_ref, ck_ref, o_ref, lse_ref, m_ref, l_ref, acc_ref):
        i, j = pl.program_id(1), pl.program_id(2)

        @pl.when(j == 0)
        def _():
            m_ref[...] = jnp.full_like(m_ref, -jnp.inf)
            l_ref[...] = jnp.zeros_like(l_ref)
            acc_ref[...] = jnp.zeros_like(acc_ref)

        def step(diagonal):
            k_b, v_b, ck_row = k_ref[...].astype(BF16), v_ref[...].astype(BF16), ck_ref[0]
            scale = 1.0 / math.sqrt(FOX_HEAD_DIM)

            def strip(t, carry):
                rows = pl.ds(pl.multiple_of(t * sr, sr), sr)
                s = _dot(q_ref[rows, :].astype(BF16), k_b, _NT) * scale + (cq_ref[0, rows, :] - ck_row)
                if diagonal:
                    s = jnp.where(_iota2(s.shape, 1) <= _iota2(s.shape, 0) + t * sr, s, -jnp.inf)
                m_old = m_ref[rows, :]
                m_new = jnp.maximum(m_old, jnp.max(s, axis=1, keepdims=True))
                alpha = jnp.exp(m_old - m_new)
                p = jnp.exp(s - m_new)
                l_ref[rows, :] = alpha * l_ref[rows, :] + jnp.sum(p, axis=1, keepdims=True)
                acc_ref[rows, :] = alpha * acc_ref[rows, :] + _dot(p.astype(BF16), v_b)
                m_ref[rows, :] = m_new
                return carry

            lax.fori_loop(0, tq // sr, strip, 0)

        _fox_tiles(i, j, step)

        @pl.when(j == nq - 1)
        def _():
            o_ref[...] = acc_ref[...] / l_ref[...]
            lse_ref[0] = m_ref[...] + jnp.log(l_ref[...])

    kmap = lambda off: (lambda h, i, j: (jnp.minimum(j, i), off * n_heads + h))
    return pl.pallas_call(
        body, name=name, grid=(n_heads, nq, nq),
        in_specs=[pl.BlockSpec((tq, hd), lambda h, i, j: (i, h)),
                  pl.BlockSpec((tq, hd), kmap(1)),
                  pl.BlockSpec((tq, hd), kmap(2)),
                  pl.BlockSpec((1, tq, 1), lambda h, i, j: (h, i, 0)),
                  pl.BlockSpec((1, 1, tq), lambda h, i, j: (h, 0, jnp.minimum(j, i)))],
        out_specs=[pl.BlockSpec((tq, hd), lambda h, i, j: (i, h)),
                   pl.BlockSpec((1, tq, 1), lambda h, i, j: (h, i, 0))],
        out_shape=[jax.ShapeDtypeStruct((length, n_heads * hd), F32),
                   jax.ShapeDtypeStruct((n_heads, length, 1), F32)],
        scratch_shapes=[pltpu.VMEM((tq, 1), F32), pltpu.VMEM((tq, 1), F32), pltpu.VMEM((tq, hd), F32)],
        compiler_params=_params(("parallel", "parallel", "arbitrary")),
    )(qkvg, qkvg, qkvg, c_col, c_row)


def _fox_bwd_q(qkvg, datt, lse, c_col, c_row, n_heads, name, rider=None):
    length = qkvg.shape[0]
    hd = FOX_HEAD_DIM
    tq = _pick(length, (FOX_TILE, 512, 256, 128))
    nq = length // tq
    scale = 1.0 / math.sqrt(hd)

    def body(q_ref, k_ref, v_ref, do_ref, lse_ref, cq_ref, ck_ref, dq_ref, dsum_ref, a1_ref, a2_ref, d_ref):
        i, j = pl.program_id(1), pl.program_id(2)

        @pl.when(j == 0)
        def _():
            a1_ref[...] = jnp.zeros_like(a1_ref)
            a2_ref[...] = jnp.zeros_like(a2_ref)
            d_ref[...] = jnp.zeros_like(d_ref)

        def step(diagonal):
            s = _fox_scores(q_ref, k_ref, cq_ref, ck_ref, diagonal)
            p = jnp.exp(s - lse_ref[0])
            pdp = p * _dot(do_ref[...].astype(BF16), v_ref[...].astype(BF16), _NT)
            d_ref[...] += jnp.sum(pdp, axis=1, keepdims=True)
            k_b = k_ref[...].astype(BF16)
            a1_ref[...] += _dot(pdp.astype(BF16), k_b)
            a2_ref[...] += _dot(p.astype(BF16), k_b)

        _fox_tiles(i, j, step)

        @pl.when(j == nq - 1)
        def _():
            dq_ref[...] = ((a1_ref[...] - d_ref[...] * a2_ref[...]) * scale).astype(dq_ref.dtype)
            dsum_ref[0] = d_ref[...]

    kmap = lambda off: (lambda h, i, j: (jnp.minimum(j, i), off * n_heads + h))
    qmap = lambda h, i, j: (i, h)
    col = pl.BlockSpec((1, tq, 1), lambda h, i, j: (h, i, 0))
    outs, carried = _hosted_call(
        body, name=name, grid=(n_heads, nq, nq),
        in_specs=[pl.BlockSpec((tq, hd), qmap), pl.BlockSpec((tq, hd), kmap(1)), pl.BlockSpec((tq, hd), kmap(2)),
                  pl.BlockSpec((tq, hd), qmap), col, col,
                  pl.BlockSpec((1, 1, tq), lambda h, i, j: (h, 0, jnp.minimum(j, i)))],
        out_specs=[pl.BlockSpec((tq, hd), qmap), col],
        out_shape=[jax.ShapeDtypeStruct((length, n_heads * hd), BF16),
                   jax.ShapeDtypeStruct((n_heads, length, 1), F32)],
        scratch_shapes=[pltpu.VMEM((tq, hd), F32), pltpu.VMEM((tq, hd), F32), pltpu.VMEM((tq, 1), F32)],
        sem=("parallel", "parallel", "arbitrary"), args=(qkvg, qkvg, qkvg, datt, lse, c_col, c_row), rider=rider)
    return outs[0], outs[1], carried


def _fox_bwd_kv(qkvg, datt, lse, dsum, c_col, c_row, n_heads, name):
    length = qkvg.shape[0]
    hd = FOX_HEAD_DIM
    tq = _pick(length, (FOX_TILE, 512, 256, 128))
    nq = length // tq
    scale = 1.0 / math.sqrt(hd)

    def body(q_ref, k_ref, v_ref, do_ref, lse_ref, dsum_ref, cq_ref, ck_ref, dk_ref, dv_ref, dck_ref,
             dk_acc, dv_acc, dc_acc):
        j, i = pl.program_id(1), pl.program_id(2)

        @pl.when(i == 0)
        def _():
            dk_acc[...] = jnp.zeros_like(dk_acc)
            dv_acc[...] = jnp.zeros_like(dv_acc)
            dc_acc[...] = jnp.zeros_like(dc_acc)

        def step(diagonal):
            s = _fox_scores(q_ref, k_ref, cq_ref, ck_ref, diagonal)
            p = jnp.exp(s - lse_ref[0])
            do_b = do_ref[...].astype(BF16)
            dv_acc[...] += _dot(p.astype(BF16), do_b, _TN)
            dp = _dot(do_b, v_ref[...].astype(BF16), _NT)
            ds = p * (dp - dsum_ref[0])
            dk_acc[...] += _dot(ds.astype(BF16), q_ref[...].astype(BF16), _TN)
            dc_acc[...] -= jnp.sum(ds, axis=0, keepdims=True)

        _fox_tiles(i, j, step)

        @pl.when(i == nq - 1)
        def _():
            dk_ref[...] = (dk_acc[...] * scale).astype(dk_ref.dtype)
            dv_ref[...] = dv_acc[...].astype(dv_ref.dtype)
            dck_ref[0] = dc_acc[...]

    qmap = lambda h, j, i: (jnp.maximum(i, j), h)
    kmap = lambda off: (lambda h, j, i: (j, off * n_heads + h))
    col = pl.BlockSpec((1, tq, 1), lambda h, j, i: (h, jnp.maximum(i, j), 0))
    return pl.pallas_call(
        body, name=name, grid=(n_heads, nq, nq),
        in_specs=[pl.BlockSpec((tq, hd), qmap), pl.BlockSpec((tq, hd), kmap(1)), pl.BlockSpec((tq, hd), kmap(2)),
                  pl.BlockSpec((tq, hd), qmap), col, col, col,
                  pl.BlockSpec((1, 1, tq), lambda h, j, i: (h, 0, j))],
        out_specs=[pl.BlockSpec((tq, hd), lambda h, j, i: (j, h)), pl.BlockSpec((tq, hd), lambda h, j, i: (j, h)),
                   pl.BlockSpec((1, 1, tq), lambda h, j, i: (h, 0, j))],
        out_shape=[jax.ShapeDtypeStruct((length, n_heads * hd), BF16)] * 2
        + [jax.ShapeDtypeStruct((n_heads, 1, length), F32)],
        scratch_shapes=[pltpu.VMEM((tq, hd), F32), pltpu.VMEM((tq, hd), F32), pltpu.VMEM((1, tq), F32)],
        compiler_params=_params(("parallel", "parallel", "arbitrary")),
    )(qkvg, qkvg, qkvg, datt, lse, dsum, c_col, c_row)


def _row(v):
    return v.reshape(1, -1).astype(F32)


def _pad_heads(v, per_group):
    lead = v.shape[:-1]
    v = v.reshape(lead + (SSD_GROUPS, per_group))
    v = jnp.pad(v, [(0, 0)] * len(lead) + [(0, 0), (0, LANES - per_group)])
    return v.reshape(lead + (SSD_GROUPS * LANES,))


def _unpad_heads(v, per_group):
    lead = v.shape[:-1]
    return v.reshape(lead + (SSD_GROUPS, LANES))[..., :per_group].reshape(lead + (SSD_GROUPS * per_group,))


class _NoOverlap:
    def gather_rider(self, host):
        return None

    def gathered(self, host, carried):
        return {}

    def reduce_rider(self, host, grads):
        return None

    def reduced(self, host, carried):
        pass


def _pad_head_rows(w, per_group):
    w = w.reshape(SSD_GROUPS, per_group, w.shape[1])
    return jnp.pad(w, ((0, 0), (0, LANES - per_group), (0, 0))).reshape(SSD_GROUPS * LANES, w.shape[2])


def _unpad_head_rows(w, per_group):
    return w.reshape(SSD_GROUPS, LANES, w.shape[1])[:, :per_group].reshape(SSD_GROUPS * per_group, w.shape[1])


def _local_step(x, tgt, wb, sm, plan=None):
    plan = plan or _NoOverlap()
    wb = dict(wb)
    length, d = x.shape
    mix = 2 * d
    s5w = mix // 4
    ssdw = mix - s5w
    xbcw = ssdw + 2 * SSD_GROUPS * SSD_STATE
    n_ssd = ssdw // SSD_HEAD_DIM
    hpg = n_ssd // SSD_GROUPS
    fw = d
    o1, o2, o3 = 2 * s5w, 2 * s5w + ssdw, 2 * s5w + ssdw + xbcw
    w0t = wb["w0T"]
    w0_dt = _pad_head_rows(w0t[o3:], hpg)
    n_fox = fw // FOX_HEAD_DIM
    s5g = s5w // S5_GROUP
    s5s = s5g * S5_STATE
    grads = {}

    s5_in = (sm["l0_s5_lambda_re"], sm["l0_s5_lambda_im"], sm["l0_s5_log_step"], sm["l0_s5_b_re"],
             sm["l0_s5_b_im"], sm["l0_s5_c_re"], sm["l0_s5_c_im"])
    (a_re, a_im, bd_re, bd_im, cd_re, cd_imn), s5_vjp = jax.vjp(_s5_prepare, *s5_in)
    nj = bd_re.shape[0]
    bd_re_b, bd_im_b, cd_re_b, cd_imn_b = (t.astype(BF16) for t in (bd_re, bd_im, cd_re, cd_imn))
    tr = lambda t: jnp.swapaxes(t, 1, 2)
    dvec = _row(sm["l0_s5_d"])
    bglu = _row(sm["l0_s5_b_glu"])
    conv_w = sm["l0_ssd_conv_w"]
    conv_b = _row(sm["l0_ssd_conv_b"])

    def ssd_prepare(dt_bias, a_log, dd):
        return (_pad_heads(_row(dt_bias), hpg), _pad_heads(_row(-jnp.exp(a_log)), hpg),
                jnp.repeat(_row(dd), SSD_HEAD_DIM, axis=1))

    (dtb, ap, dpar), ssd_vjp = jax.vjp(ssd_prepare, sm["l0_ssd_dt_bias"], sm["l0_ssd_a_log"], sm["l0_ssd_d"])
    ssd_nw = _row(sm["l0_ssd_norm_w"])
    nw0, nw1, fnw = _row(sm["l0_norm_w"]), _row(sm["l1_norm_w"]), _row(sm["final_norm_w"])
    bf = jnp.pad(_row(sm["l1_fox_b_f"]), ((0, 0), (0, LANES - n_fox)))

    h0 = _norm_fwd(x, nw0, "l0_norm")
    def gathering(host, *args, **kw):
        rider = plan.gather_rider(host)
        out = _matmul(*args, name=host, rider=rider, **kw)
        if rider is None:
            return out
        wb.update(plan.gathered(host, out[1]))
        return out[0]

    def reducing(host, *args, **kw):
        rider = plan.reduce_rider(host, grads)
        out = _matmul(*args, name=host, rider=rider, **kw)
        if rider is None:
            return out
        plan.reduced(host, out[1])
        return out[0]

    ug = gathering("l0_in_ug", h0, w0t, mode="nt", dims=(length, o1, d))
    z = gathering("l0_in_z", h0, w0t, mode="nt", dims=(length, ssdw, d), b_off=(o1, 0))
    xbc = gathering("l0_in_xbc", h0, w0t, mode="nt", dims=(length, xbcw, d), b_off=(o2, 0))
    dtraw = _matmul(h0, w0_dt, mode="nt", name="l0_in_dt")
    u_win, gate_win = (ug, s5w, 0), (ug, s5w, 1)

    shape3 = (length, s5s // LANES, LANES)
    bu_re = _bdmm([(ug, 0)], [bd_re_b], name="s5_bu_re").reshape(shape3)
    bu_im = _bdmm([(ug, 0)], [bd_im_b], name="s5_bu_im").reshape(shape3)
    s_re3, s_im3 = _s5_scan_fwd(bu_re, bu_im, a_re, a_im, "s5_scan")
    s_re, s_im = s_re3.reshape(length, s5s), s_im3.reshape(length, s5s)
    yc = _bdmm([s_re, s_im], [cd_re_b, cd_imn_b], name="s5_y")
    gl = _rowwise(_f_s5_gelu, [yc, u_win], [dvec], [(s5w, BF16)], tl=256, name="s5_gelu")[0]
    t_glu = _matmul(gl, wb["w_glu"], name="s5_glu")
    s5o = _rowwise(_f_s5_out, [yc, u_win, t_glu, gate_win], [dvec, bglu], [(s5w, BF16)], tl=256,
                   name="s5_out")[0]

    xc = _conv_fwd(xbc, conv_w, conv_b, "ssd_conv")
    y_ssd, states, carried = _ssd_fwd(xc, dtraw, dtb, ap, n_ssd, "ssd_scan", rider=plan.gather_rider("ssd_scan"))
    wb.update(plan.gathered("ssd_scan", carried))
    wg = ssdw // SSD_GROUPS
    ssdo = _rowwise(_f_ssd_out, [y_ssd, (xc, wg, 0), z], [dpar, ssd_nw], [(wg, BF16)], tl=256,
                    ncol=SSD_GROUPS, name="ssd_out")[0]
    x1 = _matmul(s5o, wb["w0_out"], dims=(length, d, s5w), addend=x, name="l0_out_s5")
    x1 = _matmul(ssdo, wb["w0_out"], dims=(length, d, ssdw), b_off=(s5w, 0), addend=x1, name="l0_out_ssd")

    h1 = _norm_fwd(x1, nw1, "l1_norm")
    w1t = wb["w1T"]
    w1_f = jnp.pad(w1t[4 * fw:], ((0, LANES - n_fox), (0, 0)))
    qkvg = _matmul(h1, w1t, mode="nt", dims=(length, 4 * fw, d), name="l1_in")
    fraw = _matmul(h1, w1_f, mode="nt", name="l1_in_f")
    cc = _fox_cumsum(fraw, bf, "fox_cumsum")
    c_t = cc[:, :n_fox].T
    c_col, c_row = c_t[:, :, None], c_t[:, None, :]
    att, lse = _fox_fwd(qkvg, c_col, c_row, n_fox, "fox_fwd")
    gate1_win = (qkvg, fw, 3)
    fox_o = _rowwise(_f_fox_out, [att, gate1_win], [], [(fw, BF16)], tl=256, name="fox_out")[0]
    x2 = _matmul(fox_o, wb["w1_out"], addend=x1, name="l1_out")

    def loss_fn(xt, tt, wt):
        def f(xx, ww):
            err = _f_norm(xx, ww) - tt
            return (0.5 / d) * err * err
        lanes, vjp = jax.vjp(f, xt, wt)
        dx, dw = vjp(jnp.ones_like(lanes))
        return dx, dx, jnp.sum(lanes, axis=0, keepdims=True), dw

    dx2, dx2b, loss_lanes, g_fnw = _rowwise(loss_fn, [x2, tgt], [fnw], [(d, F32), (d, BF16)], [d, d],
                                            tl=128, name="loss_head")
    grads["final_norm_w"] = g_fnw

    grads["l1_w_out"] = _matmul(fox_o, dx2b, mode="tn", name="l1_out_dw")
    do1 = _matmul(dx2b, wb["w1_out"], mode="nt", name="l1_out_dx")

    def fox_out_bwd(at, gt, dt_):
        _, vjp = jax.vjp(_f_fox_out, at, gt)
        return vjp(dt_)

    datt, dgate1 = _rowwise(fox_out_bwd, [att, gate1_win, do1], [], [(fw, F32), (fw, BF16)], tl=256,
                            name="fox_out_bwd")
    dq, dsum, carried = _fox_bwd_q(qkvg, datt, lse, c_col, c_row, n_fox, "fox_bwd_q",
                                   rider=plan.reduce_rider("fox_bwd_q", grads))
    plan.reduced("fox_bwd_q", carried)
    dk, dv, dck = _fox_bwd_kv(qkvg, datt, lse, dsum, c_col, c_row, n_fox, "fox_bwd_kv")
    dfraw, g_bf = _fox_cumsum_bwd(dck.reshape(n_fox, length), fraw, bf, "fox_cumsum_bwd")
    grads["l1_fox_b_f"] = g_bf[:, :n_fox]
    dsegs = [dq, dk, dv, dgate1]
    g1, n1 = None, w1t.shape[0]
    for i, s in enumerate(dsegs):
        g1 = _matmul(s, h1, mode="tn", rows_of=(n1, i * fw, g1), name=f"l1_in_dw{i}")
    g1_f = _matmul(dfraw, h1, mode="tn", name="l1_in_dwf")[:n_fox]
    grads["l1_w_inT"] = lax.dynamic_update_slice(g1, g1_f, (4 * fw, 0))
    dh1 = _matmul(dfraw, w1_f, mode="nn", name="l1_in_dxf")
    for i, s in enumerate(dsegs):
        dh1 = _matmul(s, w1t, mode="nn", dims=(length, d, fw), b_off=(i * fw, 0), addend=dh1,
                      name=f"l1_in_dx{i}")
    dx1, dx1b, grads["l1_norm_w"] = _norm_bwd(x1, dh1, dx2, nw1, "l1_norm_bwd")

    g_out = _matmul(s5o, dx1b, mode="tn", rows_of=(mix, 0, None), name="l0_out_dw_s5")
    grads["l0_w_out"] = _matmul(ssdo, dx1b, mode="tn", rows_of=(mix, s5w, g_out), name="l0_out_dw_ssd")
    ds5o = _matmul(dx1b, wb["w0_out"], mode="nt", dims=(length, s5w, d), name="l0_out_dx_s5")
    dssdo = reducing("l0_out_dx_ssd", dx1b, wb["w0_out"], mode="nt", dims=(length, ssdw, d), b_off=(s5w, 0))

    def ssd_out_bwd(yt, xt, zt, dt_, dp, nw):
        _, vjp = jax.vjp(_f_ssd_out, yt, xt, zt, dp, nw)
        return vjp(dt_)

    dy_ssd, dxa, dz, g_dpar, g_ssd_nw = _rowwise(
        ssd_out_bwd, [y_ssd, (xc, wg, 0), z, dssdo], [dpar, ssd_nw],
        [(wg, F32), (wg, F32), (wg, BF16)], [wg, wg], tl=128, ncol=SSD_GROUPS, name="ssd_out_bwd")
    grads["l0_ssd_norm_w"] = g_ssd_nw
    dxs, db_ssd, dc_ssd, ddtraw, g_dtb, g_ap, carried = _ssd_bwd(
        dy_ssd, dxa, xc, dtraw, states, dtb, ap, n_ssd, "ssd_scan_bwd",
        rider=plan.reduce_rider("ssd_scan_bwd", grads))
    plan.reduced("ssd_scan_bwd", carried)
    g_dt_bias, g_a_log, g_ssd_d = ssd_vjp((g_dtb, g_ap, g_dpar))
    grads["l0_ssd_dt_bias"], grads["l0_ssd_a_log"], grads["l0_ssd_d"] = g_dt_bias, g_a_log, g_ssd_d
    dxc = jnp.concatenate([dxs, db_ssd, dc_ssd], axis=1)
    dpre, grads["l0_ssd_conv_w"], grads["l0_ssd_conv_b"] = _conv_bwd_pre(dxc, xbc, conv_w, conv_b, "ssd_conv_bwd_pre")
    dxbc = _conv_bwd_in(dpre, conv_w, "ssd_conv_bwd_in")

    def s5_out_bwd(yt, ut, tt, gt, dt_, dv_, bg):
        _, vjp = jax.vjp(_f_s5_out, yt, ut, tt, gt, dv_, bg)
        return vjp(dt_)

    dyc_a, du_a, dt_glu, dgate, g_dvec_a, g_bglu = _rowwise(
        s5_out_bwd, [yc, u_win, t_glu, gate_win, ds5o], [dvec, bglu],
        [(s5w, F32), (s5w, F32), (s5w, BF16), (s5w, BF16)], [s5w, s5w], tl=128, name="s5_out_bwd")
    grads["l0_s5_b_glu"] = g_bglu
    grads["l0_s5_w_glu"] = _matmul(gl, dt_glu, mode="tn", name="s5_glu_dw")
    dgl = _matmul(dt_glu, wb["w_glu"], mode="nt", name="s5_glu_dx")

    def s5_gelu_bwd(yt, ut, dg, dya, dua, dv_):
        _, vjp = jax.vjp(_f_s5_gelu, yt, ut, dv_)
        dy_, du_, ddv = vjp(dg)
        return dy_ + dya, du_ + dua, ddv

    dyc, du_ab, g_dvec_b = _rowwise(s5_gelu_bwd, [yc, u_win, dgl, dyc_a, du_a], [dvec],
                                    [(s5w, F32), (s5w, F32)], [s5w], tl=128, name="s5_gelu_bwd")
    ds_re = _bdmm([dyc], [tr(cd_re_b)], name="s5_ds_re").reshape(shape3)
    ds_im = _bdmm([dyc], [tr(cd_imn_b)], name="s5_ds_im").reshape(shape3)
    kin_s, kin_u = s5s // nj, s5w // nj
    g_cd_re = _bdmm_tn_sized(s_re, dyc, nj, kin_s, kin_u, 0, "s5_dcd_re")
    g_cd_imn = _bdmm_tn_sized(s_im, dyc, nj, kin_s, kin_u, 0, "s5_dcd_im")
    g_re3, g_im3, g_a_re, g_a_im = _s5_scan_bwd(ds_re, ds_im, s_re3, s_im3, a_re, a_im, "s5_scan_bwd")
    g_re, g_im = g_re3.reshape(length, s5s), g_im3.reshape(length, s5s)
    du = _bdmm([g_re, g_im], [tr(bd_re_b), tr(bd_im_b)], addend=du_ab, out_dtype=BF16, name="s5_du")
    g_bd_re = _bdmm_tn_sized(ug, g_re, nj, kin_u, kin_s, 0, "s5_dbd_re")
    g_bd_im = _bdmm_tn_sized(ug, g_im, nj, kin_u, kin_s, 0, "s5_dbd_im")
    s5_g = s5_vjp((g_a_re, g_a_im, g_bd_re, g_bd_im, g_cd_re, g_cd_imn))
    for nm, g in zip(("lambda_re", "lambda_im", "log_step", "b_re", "b_im", "c_re", "c_im"), s5_g):
        grads["l0_s5_" + nm] = g
    grads["l0_s5_d"] = (g_dvec_a + g_dvec_b).reshape(sm["l0_s5_d"].shape)

    g0, n0 = None, w0t.shape[0]
    for nm, s, off in (("u", du, 0), ("g", dgate, s5w), ("z", dz, o1), ("xbc", dxbc, o2)):
        g0 = _matmul(s, h0, mode="tn", rows_of=(n0, off, g0), name="l0_in_dw_" + nm)
    g0_dt = _unpad_head_rows(_matmul(ddtraw, h0, mode="tn", name="l0_in_dw_dt"), hpg)
    grads["l0_w_inT"] = lax.dynamic_update_slice(g0, g0_dt, (o3, 0))
    dh0 = _matmul(ddtraw, w0_dt, mode="nn", name="l0_in_dx_dt")
    for nm, s, off in (("u", du, 0), ("g", dgate, s5w), ("z", dz, o1), ("xbc", dxbc, o2)):
        dh0 = reducing("l0_in_dx_" + nm, s, w0t, mode="nn", dims=(length, d, s.shape[1]), b_off=(off, 0),
                       addend=dh0)
    dx, _, grads["l0_norm_w"] = _norm_bwd(x, dh0, dx1, nw0, "l0_norm_bwd")
    return loss_lanes, dx, grads


_ANY = pl.BlockSpec(memory_space=pl.ANY)


def _place():
    x, y, c = lax.axis_index("x"), lax.axis_index("y"), lax.axis_index("c")
    return x, y, c, [(1 - x, y), (x, 1 - y), (1 - x, 1 - y)]


def _remote(src, dst, send_sem, recv_sem, to):
    return pltpu.make_async_remote_copy(src_ref=src, dst_ref=dst, send_sem=send_sem, recv_sem=recv_sem,
                                        device_id=to, device_id_type=MESH)


def _comm_call(body, n_in, out_shape, n_sems, name):
    return pl.pallas_call(
        body, name=name, in_specs=[_ANY] * n_in, out_specs=[_ANY] * len(out_shape), out_shape=out_shape,
        scratch_shapes=[pltpu.SemaphoreType.DMA((k,)) for k in n_sems],
        compiler_params=pltpu.CompilerParams(has_side_effects=True),
    )


def _half(ref_or_shape, c):
    ch = ref_or_shape.shape[-1] // 2
    return pl.ds(pl.multiple_of(c * ch, LANES), ch)


def _gather_rider(shards):
    n = len(shards)

    def sends(ins, outs, sems):
        send, recv = sems[:2]
        x, y, c, chips = _place()
        me = 2 * x + y
        return [_remote(ins[a].at[:, _half(ins[a], c)], outs[a].at[me, :, _half(ins[a], c)],
                        send.at[3 * a + k], recv.at[3 * a + k], (px, py, c))
                for a in range(n) for k, (px, py) in enumerate(chips)]

    def start(ins, outs, sems):
        for cp in sends(ins, outs, sems):
            cp.start()

    def finish(ins, outs, sems):
        send, recv, fsend, frecv = sems
        x, y, c, chips = _place()
        passed = []
        for a in range(n):
            for k, (px, py) in enumerate(chips):
                got = outs[a].at[2 * px + py, :, _half(ins[a], c)]
                _remote(got, got, send.at[3 * a + k], recv.at[3 * a + k], (px, py, c)).wait_recv()
                cp = _remote(got, got, fsend.at[3 * a + k], frecv.at[3 * a + k], (x, y, 1 - c))
                cp.start()
                passed.append(cp)
        for a in range(n):
            for k, (px, py) in enumerate(chips):
                got = outs[a].at[2 * px + py, :, _half(ins[a], 1 - c)]
                _remote(got, got, fsend.at[3 * a + k], frecv.at[3 * a + k], (x, y, 1 - c)).wait_recv()
        for cp in sends(ins, outs, sems) + passed:
            cp.wait_send()

    out_shape = [jax.ShapeDtypeStruct((N_SHARD,) + s.shape, s.dtype) for s in shards]
    return _Rider(shards, out_shape, [3 * n] * 4, start, finish)


def _chip_rider(parts):
    n = len(parts)

    def copies(ins, outs, sems):
        send, recv = sems
        x, y, c, chips = _place()
        return [_remote(ins[a].at[2 * px + py], outs[a].at[k], send.at[3 * a + k], recv.at[3 * a + k], (px, py, c))
                for a in range(n) for k, (px, py) in enumerate(chips)]

    def start(ins, outs, sems):
        for cp in copies(ins, outs, sems):
            cp.start()

    def finish(ins, outs, sems):
        for cp in copies(ins, outs, sems):
            cp.wait()

    out_shape = [jax.ShapeDtypeStruct((3,) + p.shape[1:], p.dtype) for p in parts]
    return _Rider(parts, out_shape, [3 * n] * 2, start, finish)


def _run_rider(rider, name):
    n_in, n_out = len(rider.inputs), len(rider.out_shape)

    def body(*refs):
        ins, outs, sems = refs[:n_in], refs[n_in:n_in + n_out], refs[n_in + n_out:]
        rider.start(ins, outs, sems)
        rider.finish(ins, outs, sems)

    return _comm_call(body, n_in, rider.out_shape, rider.sems, name)(*rider.inputs)


def _sibling_halves(grads, name):
    n = len(grads)

    def body(*refs):
        ins, outs = refs[:n], refs[n:2 * n]
        send, recv = refs[2 * n:]
        x, y, c, _ = _place()
        copies = [_remote(ins[a].at[:, :, _half(ins[a], 1 - c)], outs[a], send.at[a], recv.at[a], (x, y, 1 - c))
                  for a in range(n)]
        for cp in copies:
            cp.start()
        for cp in copies:
            cp.wait()

    out_shape = [jax.ShapeDtypeStruct(g.shape[:2] + (g.shape[2] // 2,), g.dtype) for g in grads]
    return _comm_call(body, n, out_shape, [n, n], name)(*grads)


def _join_halves(halves, name):
    n = len(halves)

    def body(*refs):
        outs = refs[n:2 * n]
        send, recv = refs[2 * n:]
        x, y, c, _ = _place()
        mine = [outs[a].at[:, _half(outs[a], c)] for a in range(n)]
        copies = [_remote(mine[a], mine[a], send.at[a], recv.at[a], (x, y, 1 - c)) for a in range(n)]
        for cp in copies:
            cp.start()
        for a in range(n):
            copies[a].wait_send()
            got = outs[a].at[:, _half(outs[a], 1 - c)]
            _remote(got, got, send.at[a], recv.at[a], (x, y, 1 - c)).wait_recv()

    return pl.pallas_call(
        body, name=name, in_specs=[_ANY] * n, out_specs=[_ANY] * n,
        out_shape=[jax.ShapeDtypeStruct(h.shape, h.dtype) for h in halves],
        input_output_aliases={a: a for a in range(n)},
        scratch_shapes=[pltpu.SemaphoreType.DMA((n,)), pltpu.SemaphoreType.DMA((n,))],
        compiler_params=pltpu.CompilerParams(has_side_effects=True),
    )(*halves)


def _gather_all(buf, name):
    def body(in_ref, out_ref, send, recv, lsem):
        x, y, c, _ = _place()
        me = 4 * x + 2 * y + c
        local = pltpu.make_async_copy(in_ref, out_ref.at[me], lsem.at[0])
        local.start()
        copies = []
        for k in range(1, N_DEV):
            fx, fy, fc = (k >> 2) & 1, (k >> 1) & 1, k & 1
            peer = (x + fx - 2 * x * fx, y + fy - 2 * y * fy, c + fc - 2 * c * fc)
            cp = _remote(in_ref, out_ref.at[me], send.at[k - 1], recv.at[k - 1], peer)
            cp.start()
            copies.append((cp, 4 * peer[0] + 2 * peer[1] + peer[2]))
        for k, (cp, slot) in enumerate(copies):
            cp.wait_send()
            got = out_ref.at[slot]
            _remote(got, got, send.at[k], recv.at[k], (x, y, c)).wait_recv()
        local.wait()

    out_shape = [jax.ShapeDtypeStruct((N_DEV,) + buf.shape, buf.dtype)]
    return _comm_call(body, 1, out_shape, [N_DEV - 1, N_DEV - 1, 1], name)(buf)[0]


def _sum_slots(buf, name):
    slots, rows, _ = buf.shape
    tr = _pick(rows, (512, 256, 128, 64, 32, 16, 8))

    def body(b_ref, o_ref):
        acc = b_ref[0]
        for s in range(1, slots):
            acc = acc + b_ref[s]
        o_ref[...] = acc

    return pl.pallas_call(
        body, name=name, grid=(rows // tr,),
        in_specs=[pl.BlockSpec((slots, tr, LANES), lambda i: (0, i, 0))],
        out_specs=pl.BlockSpec((tr, LANES), lambda i: (i, 0)),
        out_shape=jax.ShapeDtypeStruct((rows, LANES), F32),
        compiler_params=_params(("parallel",)),
    )(buf)


def _tile2(rows, cols, n_bufs):
    tr = max(t for t in range(8, min(rows, 2048) + 1, 8) if rows % t == 0) if rows % 8 == 0 else rows
    budget = 24 * 1024 * 1024 // (8 * n_bufs * tr)
    tc = max([t for t in range(LANES, cols + 1, LANES) if cols % t == 0 and t <= budget] or [LANES])
    return tr, tc


def _elementwise(fn, ins, out_dtypes, name):
    rows, cols = ins[0].shape
    tr, tc = _tile2(rows, cols, len(ins) + len(out_dtypes))
    n_in = len(ins)

    def body(*refs):
        outs = fn(*[r[...] for r in refs[:n_in]])
        for r, v in zip(refs[n_in:], outs if isinstance(outs, (tuple, list)) else (outs,)):
            r[...] = v.astype(r.dtype)

    blk = pl.BlockSpec((tr, tc), lambda i, j: (i, j))
    return pl.pallas_call(
        body, name=name, grid=(rows // tr, cols // tc), in_specs=[blk] * n_in, out_specs=[blk] * len(out_dtypes),
        out_shape=[jax.ShapeDtypeStruct((rows, cols), dt) for dt in out_dtypes],
        compiler_params=_params(("parallel", "parallel")),
    )(*ins)


def _presum(grad, sib, name):
    ns, rows, ch = sib.shape
    tr, tc = _tile2(rows, ch, 3)
    nct = ch // tc

    def body(g_ref, r_ref, o_ref):
        o_ref[...] = (g_ref[...] + r_ref[...]).astype(o_ref.dtype)

    blk = pl.BlockSpec((1, tr, tc), lambda j, i, k: (j, i, k))
    return pl.pallas_call(
        body, name=name, grid=(ns, rows // tr, nct),
        in_specs=[pl.BlockSpec((1, tr, tc), lambda j, i, k: (j, i, lax.axis_index("c") * nct + k)), blk],
        out_specs=blk, out_shape=jax.ShapeDtypeStruct((ns, rows, ch), BF16),
        compiler_params=_params(("parallel", "parallel", "parallel")),
    )(grad, sib)


def _finish_half(grad, sib, others, name):
    _, rows, ch = sib.shape
    tr, tc = _tile2(rows, ch, 6)
    nct = ch // tc

    def body(g_ref, r_ref, q_ref, o_ref):
        acc = g_ref[0] + r_ref[0]
        for k in range(3):
            acc = acc + q_ref[k].astype(F32)
        o_ref[...] = acc

    core = lambda: lax.axis_index("c")
    chip = lambda: 2 * lax.axis_index("x") + lax.axis_index("y")
    return pl.pallas_call(
        body, name=name, grid=(rows // tr, nct),
        in_specs=[pl.BlockSpec((1, tr, tc), lambda i, k: (chip(), i, core() * nct + k)),
                  pl.BlockSpec((1, tr, tc), lambda i, k: (chip(), i, k)),
                  pl.BlockSpec((3, tr, tc), lambda i, k: (0, i, k))],
        out_specs=pl.BlockSpec((tr, tc), lambda i, k: (i, core() * nct + k)),
        out_shape=jax.ShapeDtypeStruct((rows, 2 * ch), F32),
        compiler_params=_params(("parallel", "parallel")),
    )(grad, sib, others)


def _cast_bf16(w, name):
    return _elementwise(lambda t: t, [w], [BF16], name)[0]


_WEIGHTS = ("l0_norm_w", "l0_w_in", "l0_s5_lambda_re", "l0_s5_lambda_im", "l0_s5_log_step", "l0_s5_b_re",
            "l0_s5_b_im", "l0_s5_c_re", "l0_s5_c_im", "l0_s5_d", "l0_s5_w_glu", "l0_s5_b_glu", "l0_ssd_conv_w",
            "l0_ssd_conv_b", "l0_ssd_dt_bias", "l0_ssd_a_log", "l0_ssd_d", "l0_ssd_norm_w", "l0_w_out",
            "l1_norm_w", "l1_w_in", "l1_fox_b_f", "l1_w_out", "final_norm_w")
_COL_SHARDED = ("l0_w_in", "l1_w_in")
_ROW_SHARDED = ("l0_s5_w_glu", "l0_w_out", "l1_w_out")
_BIG = ("l0_w_in", "l0_s5_w_glu", "l0_w_out", "l1_w_in", "l1_w_out")
_CONV = "l0_ssd_conv_w"
_SMALL = tuple(n for n in _WEIGHTS if n not in _BIG and n != _CONV)


def _pack(arrays):
    flat = jnp.concatenate([a.reshape(-1).astype(F32) for a in arrays])
    size = flat.shape[0]
    padded = -(-size // (512 * LANES)) * (512 * LANES)
    return jnp.pad(flat, (0, padded - size)).reshape(-1, LANES)


def _unpack(buf, like):
    flat = buf.reshape(-1)
    out, pos = [], 0
    for a in like:
        out.append(flat[pos:pos + a.size].reshape(a.shape))
        pos += a.size
    return out


def _step(p):
    x, tgt = p["x"][0], p["loss_target"][0]
    d = x.shape[1]
    chip = 2 * lax.axis_index("x") + lax.axis_index("y")

    def rows_first(a, n):
        return a.T if n in _COL_SHARDED else a

    shard = {n: _cast_bf16(rows_first(p[n], n), "cast_" + n) for n in _BIG}
    shard[_CONV] = p[_CONV]

    def with_own(n, g):
        return lax.dynamic_update_index_in_dim(g, shard[n][None], chip, 0)

    def whole(n, g):
        g = with_own(n, g)
        return g.reshape(N_SHARD * g.shape[1], g.shape[2])

    now = ("l0_w_in", _CONV)
    got = dict(zip(now, _run_rider(_gather_rider([shard[n] for n in now]), "gather_first")))
    wb = {"w0T": whole("l0_w_in", got["l0_w_in"])}
    sm = {n: p[n] for n in _SMALL}
    taps, ccols = p[_CONV].shape
    conv_all = lax.dynamic_update_index_in_dim(got[_CONV], p[_CONV][None], chip, 0)
    sm[_CONV] = conv_all.transpose(1, 0, 2).reshape(taps, N_SHARD * ccols)
    cut = shard["l1_w_in"].shape[0] // 3 // 8 * 8
    shard["l1_w_in#0"], shard["l1_w_in#1"] = shard["l1_w_in"][:cut], shard["l1_w_in"][cut:]
    later = {"l0_in_ug": ("l1_w_out",), "l0_in_z": ("l0_s5_w_glu", "l1_w_in#0"), "l0_in_xbc": ("l0_w_out",),
             "ssd_scan": ("l1_w_in#1",)}
    early = {"fox_bwd_q": ("l1_w_out",), "l0_out_dx_ssd": ("l0_w_out",), "ssd_scan_bwd": ("l1_w_in",),
             "l0_in_dx_z": ("l0_w_in@0",), "l0_in_dx_xbc": ("l0_w_in@1", "l0_s5_w_glu")}
    grad_key = {n: n + "T" if n in _COL_SHARDED else n for n in _BIG}

    big, sib, others, pieces, part = {}, {}, {}, {}, {}

    def presummed(names, grads, tag):
        for n in names:
            g = grads[grad_key[n]]
            big[n] = g.reshape(N_SHARD, g.shape[0] // N_SHARD, g.shape[1])
        sib.update(zip(names, _sibling_halves([big[n] for n in names], "reduce_sibling_" + tag)))
        return [_presum(big[n], sib[n], "presum_" + n) for n in names]

    class Plan:
        def gather_rider(self, host):
            return _gather_rider([shard[n] for n in later[host]])

        def gathered(self, host, carried):
            got = dict(zip(later[host], carried))
            pieces.update({n: with_own(n, g) for n, g in got.items() if "#" in n})
            w = {n: whole(n, g) for n, g in got.items() if "#" not in n}
            if host == "ssd_scan":
                both = jnp.concatenate([pieces["l1_w_in#0"], pieces["l1_w_in#1"]], axis=1)
                w["l1_w_in"] = both.reshape(N_SHARD * both.shape[1], both.shape[2])
            names = {"l0_s5_w_glu": "w_glu", "l0_w_out": "w0_out", "l1_w_in": "w1T", "l1_w_out": "w1_out"}
            return {names[n]: v for n, v in w.items()}

        def reduce_rider(self, host, grads):
            if host not in early:
                return None
            new = tuple(dict.fromkeys(n.partition("@")[0] for n in early[host]))
            new = tuple(n for n in new if n not in part)
            part.update(zip(new, presummed(new, grads, host)))
            sent = []
            for n in early[host]:
                base, _, k = n.partition("@")
                a = part[base]
                if k:
                    w = a.shape[2] // 2
                    a = a[:, :, int(k) * w:(int(k) + 1) * w]
                sent.append(a)
            return _chip_rider(sent)

        def reduced(self, host, carried):
            for n, got in zip(early[host], carried):
                base, _, k = n.partition("@")
                if not k:
                    others[base] = got
                    continue
                pieces[n] = got
                if base + "@0" in pieces and base + "@1" in pieces:
                    others[base] = jnp.concatenate([pieces[base + "@0"], pieces[base + "@1"]], axis=2)

    loss_lanes, dx, grads = _local_step(x, tgt, wb, sm, Plan())

    small_like = [p[n] for n in _SMALL] + [sm[_CONV], jnp.zeros((1,), F32)]
    small_sum = _sum_slots(_gather_all(_pack([grads[n] for n in _SMALL] + [grads[_CONV], jnp.sum(loss_lanes)]),
                                       "gather_small"), "sum_small")
    *small_grads, conv_grad, loss = _unpack(small_sum, small_like)
    conv_grad = lax.dynamic_slice(conv_grad, (0, chip * ccols), (taps, ccols))
    final = dict(zip(_SMALL, small_grads))
    final[_CONV] = conv_grad

    late = tuple(n for n in _BIG if n not in others)
    if late:
        others.update(zip(late, _run_rider(_chip_rider(presummed(late, grads, "late")), "reduce_chips_late")))
    done = [_finish_half(big[n], sib[n], others[n], "finish_" + n) for n in _BIG]

    delta, new_m, new_v = {}, {}, {}
    for n, full in zip(_BIG, _join_halves(done, "join_halves")):
        upd = _adamw(rows_first(p[n], n), full, rows_first(p["m_" + n], n), rows_first(p["v_" + n], n),
                     "adamw_" + n)
        final[n], delta[n], new_m[n], new_v[n] = (rows_first(t, n) for t in (full, *upd))
    rest = _SMALL + (_CONV,)
    packed = [_pack([t[n] for n in rest]) for t in
              ({n: p[n] for n in rest}, final, {n: p["m_" + n] for n in rest}, {n: p["v_" + n] for n in rest})]
    for dst, buf in zip((delta, new_m, new_v), _adamw(*packed, "adamw_small")):
        dst.update(zip(rest, _unpack(buf, [p[n] for n in rest])))

    outs = [loss.reshape(()), dx[None]]
    for group in (final, delta, new_m, new_v):
        outs += [group[n].reshape(p[n].shape) for n in _WEIGHTS]
    return tuple(outs)


_INPUTS = ("x",) + _WEIGHTS + ("loss_target",) + tuple("m_" + n for n in _WEIGHTS) + tuple("v_" + n for n in _WEIGHTS)


def kernel(x, l0_norm_w, l0_w_in, l0_s5_lambda_re, l0_s5_lambda_im, l0_s5_log_step, l0_s5_b_re, l0_s5_b_im, l0_s5_c_re,
           l0_s5_c_im, l0_s5_d, l0_s5_w_glu, l0_s5_b_glu, l0_ssd_conv_w, l0_ssd_conv_b, l0_ssd_dt_bias,
           l0_ssd_a_log, l0_ssd_d, l0_ssd_norm_w, l0_w_out, l1_norm_w, l1_w_in, l1_fox_b_f, l1_w_out,
           final_norm_w, loss_target, m_l0_norm_w, m_l0_w_in, m_l0_s5_lambda_re, m_l0_s5_lambda_im,
           m_l0_s5_log_step, m_l0_s5_b_re, m_l0_s5_b_im, m_l0_s5_c_re, m_l0_s5_c_im, m_l0_s5_d,
           m_l0_s5_w_glu, m_l0_s5_b_glu, m_l0_ssd_conv_w, m_l0_ssd_conv_b, m_l0_ssd_dt_bias, m_l0_ssd_a_log,
           m_l0_ssd_d, m_l0_ssd_norm_w, m_l0_w_out, m_l1_norm_w, m_l1_w_in, m_l1_fox_b_f, m_l1_w_out,
           m_final_norm_w, v_l0_norm_w, v_l0_w_in, v_l0_s5_lambda_re, v_l0_s5_lambda_im, v_l0_s5_log_step,
           v_l0_s5_b_re, v_l0_s5_b_im, v_l0_s5_c_re, v_l0_s5_c_im, v_l0_s5_d, v_l0_s5_w_glu, v_l0_s5_b_glu,
           v_l0_ssd_conv_w, v_l0_ssd_conv_b, v_l0_ssd_dt_bias, v_l0_ssd_a_log, v_l0_ssd_d, v_l0_ssd_norm_w,
           v_l0_w_out, v_l1_norm_w, v_l1_w_in, v_l1_fox_b_f, v_l1_w_out, v_final_norm_w):
    values = (x, l0_norm_w, l0_w_in, l0_s5_lambda_re, l0_s5_lambda_im, l0_s5_log_step, l0_s5_b_re, l0_s5_b_im,
              l0_s5_c_re, l0_s5_c_im, l0_s5_d, l0_s5_w_glu, l0_s5_b_glu, l0_ssd_conv_w, l0_ssd_conv_b,
              l0_ssd_dt_bias, l0_ssd_a_log, l0_ssd_d, l0_ssd_norm_w, l0_w_out, l1_norm_w, l1_w_in,
              l1_fox_b_f, l1_w_out, final_norm_w, loss_target, m_l0_norm_w, m_l0_w_in,
              m_l0_s5_lambda_re, m_l0_s5_lambda_im, m_l0_s5_log_step, m_l0_s5_b_re, m_l0_s5_b_im,
              m_l0_s5_c_re, m_l0_s5_c_im, m_l0_s5_d, m_l0_s5_w_glu, m_l0_s5_b_glu, m_l0_ssd_conv_w,
              m_l0_ssd_conv_b, m_l0_ssd_dt_bias, m_l0_ssd_a_log, m_l0_ssd_d, m_l0_ssd_norm_w,
              m_l0_w_out, m_l1_norm_w, m_l1_w_in, m_l1_fox_b_f, m_l1_w_out, m_final_norm_w, v_l0_norm_w,
              v_l0_w_in, v_l0_s5_lambda_re, v_l0_s5_lambda_im, v_l0_s5_log_step, v_l0_s5_b_re,
              v_l0_s5_b_im, v_l0_s5_c_re, v_l0_s5_c_im, v_l0_s5_d, v_l0_s5_w_glu, v_l0_s5_b_glu,
              v_l0_ssd_conv_w, v_l0_ssd_conv_b, v_l0_ssd_dt_bias, v_l0_ssd_a_log, v_l0_ssd_d,
              v_l0_ssd_norm_w, v_l0_w_out, v_l1_norm_w, v_l1_w_in, v_l1_fox_b_f, v_l1_w_out,
              v_final_norm_w)
    return _step(dict(zip(_INPUTS, values)))
```
